```python
import math
import functools
import jax
import jax.numpy as jnp
from jax import lax
import numpy as np

D_MODEL = 1024
BATCH = 8
SEQ = 2048
DEPTH = 2

HEAD_DIM = 64
D_FF = 2816
MACARON_WEIGHT = 0.5
NORM_EPS = 1e-6
QBLK = 128
GATHER_QBLK = 16
NEG_INF = -1e30
FORCE_SCORE = 1e9

REL_BUCKETS = 32
REL_MAX_DIST = 1024
A_SLOT = 0
C_SLOT = 8
D_SLOT = 16
REL_SLOTS = 24

A_HEADS = 8
A_KV_HEADS = 2
A_WINDOW = 128

B_HEADS = 8
MLA_Q_RANK = 256
MLA_KV_RANK = 128
MLA_NOPE = 64
MLA_ROPE = 32
MLA_V = 64
ROPE_THETA = 10000.0

C_HEADS = 8
C_KV_HEADS = 2
CMP_BLOCK = 32
CMP_STRIDE = 16
CMP_HIDDEN = 256
SEL_BLOCK = 64
SEL_TOPK = 8
NSA_WINDOW = 256

D_HEADS = 8
MOBA_BLOCK = 256
MOBA_TOPK = 3

AB_COLS = (A_HEADS * HEAD_DIM, A_KV_HEADS * HEAD_DIM, A_KV_HEADS * HEAD_DIM, MLA_Q_RANK, MLA_KV_RANK, MLA_ROPE)
CD_COLS = (C_HEADS * HEAD_DIM,) + (C_KV_HEADS * HEAD_DIM,) * 6 + (3 * C_HEADS,) + (D_HEADS * HEAD_DIM,) * 3
AB_OUT = A_HEADS * HEAD_DIM + B_HEADS * MLA_V
CD_OUT = C_HEADS * HEAD_DIM + D_HEADS * HEAD_DIM

kernel_name = 'hybrid_swa_mla_nsa_moba_macaron'


def split_cols(z, sizes):
    return jnp.split(z, np.cumsum(sizes)[:-1].tolist(), axis=-1)


def rms_norm(x, g):
    xf = x.astype(jnp.float32)
    y = xf * lax.rsqrt(jnp.mean(xf * xf, axis=-1, keepdims=True) + NORM_EPS)
    return (y * g.astype(jnp.float32)).astype(x.dtype)


def swiglu(h, wg, wu, wd):
    return (jax.nn.silu(h @ wg) * (h @ wu)) @ wd


def sandwich(x, pre, post, fn, weight):
    return x + weight * rms_norm(fn(rms_norm(x, pre)), post)


def t5_bucket(dist):
    n = jnp.maximum(dist, 0)
    exact = REL_BUCKETS // 2
    nf = jnp.maximum(n, 1).astype(jnp.float32)
    large = exact + (jnp.log(nf / exact) / math.log(REL_MAX_DIST / exact) * (REL_BUCKETS - exact)).astype(jnp.int32)
    return jnp.where(n < exact, n, jnp.minimum(large, REL_BUCKETS - 1))


def rope_tables(S):
    inv = ROPE_THETA ** (-jnp.arange(0, MLA_ROPE, 2, dtype=jnp.float32) / MLA_ROPE)
    ang = jnp.arange(S, dtype=jnp.float32)[:, None] * inv[None, :]
    return jnp.cos(ang), jnp.sin(ang)


def apply_rope(x, cos, sin):
    x1, x2 = jnp.split(x, 2, axis=-1)
    return jnp.concatenate([x1 * cos - x2 * sin, x2 * cos + x1 * sin], axis=-1).astype(x.dtype)


def band_blocks(t, window):
    B, S = t.shape[:2]
    nb = S // QBLK
    tp = jnp.pad(t, [(0, 0), (window, 0)] + [(0, 0)] * (t.ndim - 2))
    parts = [tp[:, j * QBLK:j * QBLK + S].reshape((B, nb, QBLK) + t.shape[2:]) for j in range(window // QBLK + 1)]
    return jnp.concatenate(parts, axis=2)


def band_attention(q, k, v, window, bias_cols, sinks=None):
    B, S, Hkv, G, Dh = q.shape
    nb = S // QBLK
    span = window + QBLK
    qb = q.reshape(B, nb, QBLK, Hkv, G, Dh)
    kb = band_blocks(k, window)
    vb = band_blocks(v, window)
    kk = jnp.arange(span)[None, :]
    dist = jnp.arange(QBLK)[:, None] + window - kk
    kpos = jnp.arange(nb)[:, None, None] * QBLK - window + kk[None]
    valid = (dist >= 0) & (dist < window) & (kpos >= 0)
    bias = bias_cols.astype(jnp.float32)[t5_bucket(dist)].reshape(QBLK, span, Hkv, G).transpose(2, 3, 0, 1)
    s = jnp.einsum('bnqhgd,bnkhd->bnhgqk', qb, kb).astype(jnp.float32) * Dh ** -0.5 + bias
    s = jnp.where(valid[None, :, None, None], s, NEG_INF)
    if sinks is not None:
        sink = jnp.broadcast_to(sinks.astype(jnp.float32).reshape(Hkv, G, 1, 1), s.shape[:-1] + (1,))
        p = jax.nn.softmax(jnp.concatenate([s, sink], axis=-1), axis=-1)[..., :-1]
    else:
        p = jax.nn.softmax(s, axis=-1)
    o = jnp.einsum('bnhgqk,bnkhd->bnqhgd', p.astype(v.dtype), vb)
    return o.reshape(B, S, Hkv, G, -1)


def causal_block_attention(q, k, v):
    B, S, H, Dk = q.shape
    nb = S // QBLK
    kpos = jnp.arange(S)

    def block(args):
        qb, i = args
        qpos = i * QBLK + jnp.arange(QBLK)
        s = jnp.einsum('bqhd,bkhd->bhqk', qb, k).astype(jnp.float32) * Dk ** -0.5
        s = jnp.where(kpos[None, :] <= qpos[:, None], s, NEG_INF)
        p = jax.nn.softmax(s, axis=-1).astype(v.dtype)
        return jnp.einsum('bhqk,bkhd->bqhd', p, v)

    qs = q.reshape(B, nb, QBLK, H, Dk).swapaxes(0, 1)
    o = lax.map(block, (qs, jnp.arange(nb)))
    return o.swapaxes(0, 1).reshape(B, S, H, -1)


def gather_blocks(blocks, idx):
    return jax.vmap(jax.vmap(lambda b, i: b[i]))(blocks, idx)


def nsa_compressed(q, k, v, pe_k, pe_v, wk1, wk2, wv1, wv2):
    B, S, Hkv, G, Dh = q.shape
    n_c = (S - CMP_BLOCK) // CMP_STRIDE + 1
    starts = jnp.arange(n_c) * CMP_STRIDE
    cidx = starts[:, None] + jnp.arange(CMP_BLOCK)[None, :]

    def compress(t, pe, w1, w2):
        blk = t[:, cidx] + pe[None, None, :, None, :]
        blk = blk.transpose(0, 1, 3, 2, 4).reshape(B, n_c, Hkv, CMP_BLOCK * Dh)
        return jax.nn.silu(blk @ w1) @ w2

    k_cmp = compress(k, pe_k, wk1, wk2)
    v_cmp = compress(v, pe_v, wv1, wv2)
    valid = (starts + CMP_BLOCK - 1)[None, :] <= jnp.arange(S)[:, None]
    s = jnp.einsum('bshgd,bchd->bhgsc', q, k_cmp).astype(jnp.float32) * Dh ** -0.5
    p = jax.nn.softmax(jnp.where(valid, s, NEG_INF), axis=-1) * valid
    o = jnp.einsum('bhgsc,bchd->bshgd', p.astype(v_cmp.dtype), v_cmp)
    return o, p


def nsa_select(p_cmp, S):
    n_c = p_cmp.shape[-1]
    n_s = S // SEL_BLOCK
    c_start = jnp.arange(n_c)[:, None] * CMP_STRIDE
    j_start = jnp.arange(n_s)[None, :] * SEL_BLOCK
    overlap = ((c_start < j_start + SEL_BLOCK) & (c_start + CMP_BLOCK > j_start)).astype(jnp.float32)
    imp = jnp.einsum('bhgsc,cj->bhsj', p_cmp, overlap)
    cur = (jnp.arange(S) // SEL_BLOCK)[:, None]
    j = jnp.arange(n_s)[None, :]
    forced = (j == 0) | (j == cur) | (j == cur - 1)
    imp = jnp.where(forced, FORCE_SCORE, jnp.where(j <= cur, imp, NEG_INF))
    _, idx = lax.top_k(imp, min(SEL_TOPK, n_s))
    return idx


def nsa_selected(q, k, v, idx, bias_cols):
    B, S, Hkv, G, Dh = q.shape
    n_s = S // SEL_BLOCK
    top = idx.shape[-1]
    nq = S // GATHER_QBLK
    k_blk = k.reshape(B, n_s, SEL_BLOCK, Hkv, Dh).transpose(0, 3, 1, 2, 4)
    v_blk = v.reshape(B, n_s, SEL_BLOCK, Hkv, Dh).transpose(0, 3, 1, 2, 4)
    tab = bias_cols.astype(jnp.float32).reshape(REL_BUCKETS, Hkv, G).transpose(1, 0, 2)
    h_ix = jnp.arange(Hkv)[None, :, None, None, None]

    def block(args):
        qb, ib, i = args
        qpos = i * GATHER_QBLK + jnp.arange(GATHER_QBLK)
        kg = gather_blocks(k_blk, ib)
        vg = gather_blocks(v_blk, ib)
        dist = qpos[None, None, :, None, None] - (ib[..., None] * SEL_BLOCK + jnp.arange(SEL_BLOCK))
        bias = jnp.moveaxis(tab[h_ix, t5_bucket(dist)], -1, 2)
        s = jnp.einsum('bqhgd,bhqkld->bhgqkl', qb, kg).astype(jnp.float32) * Dh ** -0.5 + bias
        s = jnp.where(dist[:, :, None] >= 0, s, NEG_INF).reshape(B, Hkv, G, GATHER_QBLK, top * SEL_BLOCK)
        p = jax.nn.softmax(s, axis=-1).astype(vg.dtype).reshape(B, Hkv, G, GATHER_QBLK, top, SEL_BLOCK)
        return jnp.einsum('bhgqkl,bhqkld->bqhgd', p, vg)

    qs = q.reshape(B, nq, GATHER_QBLK, Hkv, G, Dh).swapaxes(0, 1)
    ids = idx.reshape(B, Hkv, nq, GATHER_QBLK, top).transpose(2, 0, 1, 3, 4)
    o = lax.map(block, (qs, ids, jnp.arange(nq)))
    return o.swapaxes(0, 1).reshape(B, S, Hkv, G, Dh)


def moba_attention(q, k, v, bias_cols):
    B, S, H, Dh = q.shape
    n_b = -(-S // MOBA_BLOCK)
    pad = n_b * MOBA_BLOCK - S
    kp = jnp.pad(k, ((0, 0), (0, pad), (0, 0), (0, 0)))
    vp = jnp.pad(v, ((0, 0), (0, pad), (0, 0), (0, 0)))
    k_blk = kp.reshape(B, n_b, MOBA_BLOCK, H, Dh)
    k_mean = jnp.mean(k_blk.astype(jnp.float32), axis=2)
    cur = (jnp.arange(S) // MOBA_BLOCK)[:, None]
    past = jnp.arange(n_b)[None, :] < cur
    gate = jnp.einsum('bshd,bnhd->bhsn', q.astype(jnp.float32), k_mean)
    top = min(MOBA_TOPK, n_b)
    _, idx = lax.top_k(jnp.where(past, gate, NEG_INF), top)
    k_bh = k_blk.transpose(0, 3, 1, 2, 4)
    v_bh = vp.reshape(B, n_b, MOBA_BLOCK, H, Dh).transpose(0, 3, 1, 2, 4)
    tab = bias_cols.astype(jnp.float32)
    tab_h = tab.T
    h_ix = jnp.arange(H)[None, :, None, None, None]
    nq = S // GATHER_QBLK
    scale = Dh ** -0.5

    def block(args):
        qb, ib, i = args
        qpos = i * GATHER_QBLK + jnp.arange(GATHER_QBLK)
        own = (i * GATHER_QBLK) // MOBA_BLOCK
        kg = gather_blocks(k_bh, ib)
        vg = gather_blocks(v_bh, ib)
        dist_sel = qpos[None, None, :, None, None] - (ib[..., None] * MOBA_BLOCK + jnp.arange(MOBA_BLOCK))
        s_sel = jnp.einsum('bqhd,bhqjld->bhqjl', qb, kg).astype(jnp.float32) * scale + tab_h[h_ix, t5_bucket(dist_sel)]
        s_sel = jnp.where((ib < own)[..., None], s_sel, NEG_INF).reshape(B, H, GATHER_QBLK, top * MOBA_BLOCK)
        k_own = lax.dynamic_slice_in_dim(kp, own * MOBA_BLOCK, MOBA_BLOCK, axis=1)
        v_own = lax.dynamic_slice_in_dim(vp, own * MOBA_BLOCK, MOBA_BLOCK, axis=1)
        dist_own = qpos[:, None] - (own * MOBA_BLOCK + jnp.arange(MOBA_BLOCK))[None, :]
        s_own = jnp.einsum('bqhd,bkhd->bhqk', qb, k_own).astype(jnp.float32) * scale + jnp.moveaxis(tab[t5_bucket(dist_own)], -1, 0)
        s_own = jnp.where(dist_own >= 0, s_own, NEG_INF)
        p = jax.nn.softmax(jnp.concatenate([s_sel, s_own], axis=-1), axis=-1).astype(v.dtype)
        p_sel = p[..., :top * MOBA_BLOCK].reshape(B, H, GATHER_QBLK, top, MOBA_BLOCK)
        p_own = p[..., top * MOBA_BLOCK:]
        return jnp.einsum('bhqjl,bhqjld->bqhd', p_sel, vg) + jnp.einsum('bhqk,bkhd->bqhd', p_own, v_own)

    qs = q.reshape(B, nq, GATHER_QBLK, H, Dh).swapaxes(0, 1)
    ids = idx.reshape(B, H, nq, GATHER_QBLK, top).transpose(2, 0, 1, 3, 4)
    o = lax.map(block, (qs, ids, jnp.arange(nq)))
    return o.swapaxes(0, 1).reshape(B, S, H, Dh)


def mixer_swa_mla(h, w_in, sinks, q_norm, w_uq, kv_norm, w_ukv, w_out, rel_table):
    B, S, _ = h.shape
    qa, ka, va, cq, ckv, kr = split_cols(h @ w_in, AB_COLS)
    kv_shape = (B, S, A_KV_HEADS, HEAD_DIM)
    qa = qa.reshape(B, S, A_KV_HEADS, A_HEADS // A_KV_HEADS, HEAD_DIM)
    o_a = band_attention(qa, ka.reshape(kv_shape), va.reshape(kv_shape), A_WINDOW,
                         rel_table[:, A_SLOT:A_SLOT + A_HEADS], sinks)
    q = (rms_norm(cq, q_norm) @ w_uq).reshape(B, S, B_HEADS, MLA_NOPE + MLA_ROPE)
    kv = (rms_norm(ckv, kv_norm) @ w_ukv).reshape(B, S, B_HEADS, MLA_NOPE + MLA_V)
    cos, sin = rope_tables(S)
    q = jnp.concatenate([q[..., :MLA_NOPE], apply_rope(q[..., MLA_NOPE:], cos[:, None], sin[:, None])], axis=-1)
    kr = apply_rope(kr, cos, sin)
    k = jnp.concatenate([kv[..., :MLA_NOPE], jnp.broadcast_to(kr[:, :, None, :], (B, S, B_HEADS, MLA_ROPE))], axis=-1)
    o_b = causal_block_attention(q, k, kv[..., MLA_NOPE:])
    o = jnp.concatenate([o_a.reshape(B, S, -1), o_b.reshape(B, S, -1)], axis=-1)
    return o @ w_out


def mixer_nsa_moba(h, w_in, pe_k, pe_v, wk1, wk2, wv1, wv2, w_out, rel_table):
    B, S, _ = h.shape
    G = C_HEADS // C_KV_HEADS
    qc, kc, vc, ks, vs, kw, vw, gl, qd, kd, vd = split_cols(h @ w_in, CD_COLS)
    kv_shape = (B, S, C_KV_HEADS, HEAD_DIM)
    qc = qc.reshape(B, S, C_KV_HEADS, G, HEAD_DIM)
    c_cols = rel_table[:, C_SLOT:C_SLOT + C_HEADS]
    o_cmp, p_cmp = nsa_compressed(qc, kc.reshape(kv_shape), vc.reshape(kv_shape), pe_k, pe_v, wk1, wk2, wv1, wv2)
    sel_idx = nsa_select(p_cmp, S)
    o_slc = nsa_selected(qc, ks.reshape(kv_shape), vs.reshape(kv_shape), sel_idx, c_cols)
    o_win = band_attention(qc, kw.reshape(kv_shape), vw.reshape(kv_shape), NSA_WINDOW, c_cols)
    g = jax.nn.sigmoid(gl).reshape(B, S, 3, C_KV_HEADS, G, 1)
    o_c = g[:, :, 0] * o_cmp + g[:, :, 1] * o_slc + g[:, :, 2] * o_win
    d_shape = (B, S, D_HEADS, HEAD_DIM)
    o_d = moba_attention(qd.reshape(d_shape), kd.reshape(d_shape), vd.reshape(d_shape),
                         rel_table[:, D_SLOT:D_SLOT + D_HEADS])
    o = jnp.concatenate([o_c.reshape(B, S, -1), o_d.reshape(B, S, -1)], axis=-1)
    return o @ w_out


def setup_inputs(seed: int = 0) -> dict:
    key = jax.random.key(seed)
    keys = iter(jax.random.split(key, 64))

    def normal(shape, scale):
        return scale * jax.random.normal(next(keys), shape, jnp.float32)

    def dense(fan_in, fan_out):
        return normal((fan_in, fan_out), fan_in ** -0.5)

    def gain(n):
        return 1.0 + normal((n,), 0.05)

    def add_ffn(d, prefix):
        d[prefix + '_pre'] = gain(D_MODEL)
        d[prefix + '_post'] = gain(D_MODEL)
        d[prefix + '_wg'] = dense(D_MODEL, D_FF)
        d[prefix + '_wu'] = dense(D_MODEL, D_FF)
        d[prefix + '_wd'] = dense(D_FF, D_MODEL)

    d = {}
    d['x'] = normal((BATCH, SEQ, D_MODEL), 1.0)
    d['rel_bias_table'] = normal((REL_BUCKETS, REL_SLOTS), 0.3)
    add_ffn(d, 'l0_ffn1')
    d['l0_mix_pre'] = gain(D_MODEL)
    d['l0_mix_post'] = gain(D_MODEL)
    d['l0_w_in'] = dense(D_MODEL, sum(AB_COLS))
    d['l0_sinks'] = normal((A_HEADS,), 0.5)
    d['l0_mla_q_norm'] = gain(MLA_Q_RANK)
    d['l0_mla_w_uq'] = dense(MLA_Q_RANK, B_HEADS * (MLA_NOPE + MLA_ROPE))
    d['l0_mla_kv_norm'] = gain(MLA_KV_RANK)
    d['l0_mla_w_ukv'] = dense(MLA_KV_RANK, B_HEADS * (MLA_NOPE + MLA_V))
    d['l0_w_out'] = dense(AB_OUT, D_MODEL)
    add_ffn(d, 'l0_ffn2')
    add_ffn(d, 'l1_ffn1')
    d['l1_mix_pre'] = gain(D_MODEL)
    d['l1_mix_post'] = gain(D_MODEL)
    d['l1_w_in'] = dense(D_MODEL, sum(CD_COLS))
    d['l1_nsa_pe_k'] = normal((CMP_BLOCK, HEAD_DIM), 0.3)
    d['l1_nsa_pe_v'] = normal((CMP_BLOCK, HEAD_DIM), 0.3)
    d['l1_nsa_wk1'] = dense(CMP_BLOCK * HEAD_DIM, CMP_HIDDEN)
    d['l1_nsa_wk2'] = dense(CMP_HIDDEN, HEAD_DIM)
    d['l1_nsa_wv1'] = dense(CMP_BLOCK * HEAD_DIM, CMP_HIDDEN)
    d['l1_nsa_wv2'] = dense(CMP_HIDDEN, HEAD_DIM)
    d['l1_w_out'] = dense(CD_OUT, D_MODEL)
    add_ffn(d, 'l1_ffn2')
    return d


def reference(x, rel_bias_table,
              l0_ffn1_pre, l0_ffn1_post, l0_ffn1_wg, l0_ffn1_wu, l0_ffn1_wd,
              l0_mix_pre, l0_mix_post, l0_w_in, l0_sinks, l0_mla_q_norm, l0_mla_w_uq,
              l0_mla_kv_norm, l0_mla_w_ukv, l0_w_out,
              l0_ffn2_pre, l0_ffn2_post, l0_ffn2_wg, l0_ffn2_wu, l0_ffn2_wd,
              l1_ffn1_pre, l1_ffn1_post, l1_ffn1_wg, l1_ffn1_wu, l1_ffn1_wd,
              l1_mix_pre, l1_mix_post, l1_w_in, l1_nsa_pe_k, l1_nsa_pe_v,
              l1_nsa_wk1, l1_nsa_wk2, l1_nsa_wv1, l1_nsa_wv2, l1_w_out,
              l1_ffn2_pre, l1_ffn2_post, l1_ffn2_wg, l1_ffn2_wu, l1_ffn2_wd):
    mixer0 = functools.partial(mixer_swa_mla, w_in=l0_w_in, sinks=l0_sinks, q_norm=l0_mla_q_norm,
                               w_uq=l0_mla_w_uq, kv_norm=l0_mla_kv_norm, w_ukv=l0_mla_w_ukv,
                               w_out=l0_w_out, rel_table=rel_bias_table)
    mixer1 = functools.partial(mixer_nsa_moba, w_in=l1_w_in, pe_k=l1_nsa_pe_k, pe_v=l1_nsa_pe_v,
                               wk1=l1_nsa_wk1, wk2=l1_nsa_wk2, wv1=l1_nsa_wv1, wv2=l1_nsa_wv2,
                               w_out=l1_w_out, rel_table=rel_bias_table)
    layers = (
        ((l0_ffn1_pre, l0_ffn1_post, l0_ffn1_wg, l0_ffn1_wu, l0_ffn1_wd), (l0_mix_pre, l0_mix_post), mixer0,
         (l0_ffn2_pre, l0_ffn2_post, l0_ffn2_wg, l0_ffn2_wu, l0_ffn2_wd)),
        ((l1_ffn1_pre, l1_ffn1_post, l1_ffn1_wg, l1_ffn1_wu, l1_ffn1_wd), (l1_mix_pre, l1_mix_post), mixer1,
         (l1_ffn2_pre, l1_ffn2_post, l1_ffn2_wg, l1_ffn2_wu, l1_ffn2_wd)),
    )
    for layer in range(DEPTH):
        ffn1, mix_norms, mixer, ffn2 = layers[layer]
        x = sandwich(x, ffn1[0], ffn1[1], functools.partial(swiglu, wg=ffn1[2], wu=ffn1[3], wd=ffn1[4]), MACARON_WEIGHT)
        x = sandwich(x, mix_norms[0], mix_norms[1], mixer, 1.0)
        x = sandwich(x, ffn2[0], ffn2[1], functools.partial(swiglu, wg=ffn2[2], wu=ffn2[3], wd=ffn2[4]), MACARON_WEIGHT)
    return x
```

```python
import functools
import math

import numpy as np
import jax
import jax.numpy as jnp
from jax import lax
from jax.experimental import pallas as pl
from jax.experimental.pallas import tpu as pltpu

F32 = jnp.float32
BF16 = jnp.bfloat16

D_MODEL = 1024
HEAD_DIM = 64
D_FF = 2816
NORM_EPS = 1e-6
NEG_INF = -1e30
PAD_SCORE = -3e38
FORCE_SCORE = 1e9
REL_BUCKETS = 32
REL_MAX_DIST = 1024
REL_SLOTS = 24
A_SLOT, C_SLOT, D_SLOT = 0, 8, 16
MLA_NOPE, MLA_ROPE, MLA_V = 64, 32, 64
MLA_HEADS = 8
ROPE_THETA = 10000.0
CMP_BLOCK, CMP_STRIDE = 32, 16
SEL_BLOCK, SEL_TOPK = 64, 8
MOBA_BLOCK, MOBA_TOPK = 256, 3

LANES = 128
TILE = 128
BIAS_TILES = 9
TOKEN_TILE = 512
FF_CHUNK = 256
VMEM_LIMIT = 56 * 1024 * 1024


def _cparams(n_axes):
    return pltpu.CompilerParams(dimension_semantics=("arbitrary",) * n_axes, vmem_limit_bytes=VMEM_LIMIT)


def _dot(a, b):
    return jnp.dot(a, b, preferred_element_type=F32)


def _dot_nt(a, b):
    return lax.dot_general(a, b, (((1,), (1,)), ((), ())), preferred_element_type=F32)


def _rms(x, g):
    return x * lax.rsqrt(jnp.mean(x * x, axis=-1, keepdims=True) + NORM_EPS) * g


def _silu(x):
    return x / (1.0 + jnp.exp(-x))


def _const_spec(shape):
    nd = len(shape)
    return pl.BlockSpec(shape, lambda *_: (0,) * nd)


def _ffn_body(x_ref, pre_ref, post_ref, wg_ref, wu_ref, wd_ref, o_ref, acc_ref, *, n_chunks, weight):
    x = x_ref[...]
    h = _rms(x, pre_ref[...]).astype(BF16)
    acc_ref[...] = jnp.zeros_like(acc_ref)

    def chunk(c, carry):
        g = _dot(h, wg_ref[c])
        u = _dot(h, wu_ref[c])
        a = (_silu(g) * u).astype(BF16)
        acc_ref[...] += _dot(a, wd_ref[c])
        return carry

    lax.fori_loop(0, n_chunks, chunk, 0)
    o_ref[...] = x + weight * _rms(acc_ref[...], post_ref[...])


def _ffn(x2, pre, post, wg, wu, wd, weight):
    m, d = x2.shape
    n_chunks = D_FF // FF_CHUNK
    wg3 = wg.astype(BF16).reshape(d, n_chunks, FF_CHUNK).transpose(1, 0, 2)
    wu3 = wu.astype(BF16).reshape(d, n_chunks, FF_CHUNK).transpose(1, 0, 2)
    wd3 = wd.astype(BF16).reshape(n_chunks, FF_CHUNK, d)
    tm = TOKEN_TILE
    return pl.pallas_call(
        functools.partial(_ffn_body, n_chunks=n_chunks, weight=weight),
        grid=(m // tm,),
        in_specs=[
            pl.BlockSpec((tm, d), lambda i: (i, 0)),
            _const_spec((1, d)), _const_spec((1, d)),
            _const_spec((n_chunks, d, FF_CHUNK)), _const_spec((n_chunks, d, FF_CHUNK)),
            _const_spec((n_chunks, FF_CHUNK, d)),
        ],
        out_specs=pl.BlockSpec((tm, d), lambda i: (i, 0)),
        out_shape=jax.ShapeDtypeStruct((m, d), F32),
        scratch_shapes=[pltpu.VMEM((tm, d), F32)],
        compiler_params=_cparams(1),
        name="ffn",
    )(x2, pre.reshape(1, d), post.reshape(1, d), wg3, wu3, wd3)


def _outproj_body(x_ref, o1_ref, o2_ref, w1_ref, w2_ref, post_ref, o_ref):
    y = _dot(o1_ref[...], w1_ref[...]) + _dot(o2_ref[...], w2_ref[...])
    o_ref[...] = x_ref[...] + _rms(y, post_ref[...])


def _outproj(x2, o1, o2, w_out, post):
    m, d = x2.shape
    n1 = o1.shape[1]
    n2 = o2.shape[1]
    w = w_out.astype(BF16)
    tm = TOKEN_TILE
    return pl.pallas_call(
        _outproj_body,
        grid=(m // tm,),
        in_specs=[
            pl.BlockSpec((tm, d), lambda i: (i, 0)),
            pl.BlockSpec((tm, n1), lambda i: (i, 0)),
            pl.BlockSpec((tm, n2), lambda i: (i, 0)),
            _const_spec((n1, d)), _const_spec((n2, d)), _const_spec((1, d)),
        ],
        out_specs=pl.BlockSpec((tm, d), lambda i: (i, 0)),
        out_shape=jax.ShapeDtypeStruct((m, d), F32),
        compiler_params=_cparams(1),
        name="outproj",
    )(x2, o1, o2, w[:n1], w[n1:], post.reshape(1, d))


def _t5_bucket(dist):
    n = jnp.maximum(dist, 0)
    exact = REL_BUCKETS // 2
    nf = jnp.maximum(n, 1).astype(jnp.float32)
    large = exact + (jnp.log(nf / exact) / math.log(REL_MAX_DIST / exact) * (REL_BUCKETS - exact)).astype(jnp.int32)
    return jnp.where(n < exact, n, jnp.minimum(large, REL_BUCKETS - 1))


def _bias_body(tab_ref, idx_ref, o_ref):
    slot = pl.program_id(0)

    def tile(d, carry):
        idx = idx_ref[d]
        v = jnp.full((TILE, TILE), tab_ref[slot], F32)
        for b in range(1, REL_BUCKETS):
            v = jnp.where(idx == b, tab_ref[b * REL_SLOTS + slot], v)
        o_ref[0, d] = v
        return carry

    lax.fori_loop(0, BIAS_TILES, tile, 0)


def _bias_tiles(rel_table):
    d = jnp.arange(BIAS_TILES)[:, None, None]
    r = jnp.arange(TILE)[None, :, None]
    c = jnp.arange(TILE)[None, None, :]
    idx = _t5_bucket(d * TILE + r - c).astype(jnp.int32)
    return pl.pallas_call(
        _bias_body,
        grid=(REL_SLOTS,),
        in_specs=[
            pl.BlockSpec(memory_space=pltpu.SMEM),
            _const_spec((BIAS_TILES, TILE, TILE)),
        ],
        out_specs=pl.BlockSpec((1, BIAS_TILES, TILE, TILE), lambda s: (s, 0, 0, 0)),
        out_shape=jax.ShapeDtypeStruct((REL_SLOTS, BIAS_TILES, TILE, TILE), F32),
        compiler_params=_cparams(1),
        name="bias_tiles",
    )(rel_table.astype(F32).reshape(-1), idx)


def _stack_group_bias(tiles, n_tiles):
    t = tiles[:, :n_tiles].reshape(2, 4, n_tiles, TILE, TILE)
    return t.transpose(0, 2, 1, 3, 4).reshape(2, n_tiles, 4 * TILE, TILE)


def _proj0_body(x_ref, pre_ref, w_ref, qn_ref, wq1_ref, wq2_ref, kvn_ref, wk_ref, wv_ref, cos_ref, sin_ref,
                qa_ref, ka_ref, va_ref, qb_ref, kb_ref, vb_ref):
    h = _rms(x_ref[...], pre_ref[...]).astype(BF16)
    y = _dot(h, w_ref[...])
    qa_ref[...] = (y[:, 0:512] * (HEAD_DIM ** -0.5)).astype(BF16)
    ka_ref[...] = y[:, 512:640].astype(BF16)
    va_ref[...] = y[:, 640:768].astype(BF16)
    cq = _rms(y[:, 768:1024], qn_ref[...]).astype(BF16)
    ckv = _rms(y[:, 1024:1152], kvn_ref[...]).astype(BF16)
    cos_t = cos_ref[...]
    sin_t = sin_ref[...]
    lane = lax.broadcasted_iota(jnp.int32, cos_t.shape, 1)
    scale = (MLA_NOPE + MLA_ROPE) ** -0.5
    qmul = scale * (cos_t + jnp.where(lane < MLA_NOPE, 1.0, 0.0))
    qsin = scale * sin_t
    kr = y[:, 1152:1280] * cos_t + y[:, 1280:1408] * sin_t
    q1 = _dot(cq, wq1_ref[...])
    q2 = _dot(cq, wq2_ref[...])
    k1 = _dot(ckv, wk_ref[...])
    for hd in range(MLA_HEADS):
        sl = slice(hd * LANES, (hd + 1) * LANES)
        qb_ref[:, sl] = (q1[:, sl] * qmul + q2[:, sl] * qsin).astype(BF16)
        kb_ref[:, sl] = (k1[:, sl] + kr).astype(BF16)
    vb_ref[...] = _dot(ckv, wv_ref[...]).astype(BF16)


def _rope_lane_tables(s):
    inv = ROPE_THETA ** (-jnp.arange(0, MLA_ROPE, 2, dtype=jnp.float32) / MLA_ROPE)
    ang = jnp.arange(s, dtype=jnp.float32)[:, None] * inv[None, :]
    cos, sin = jnp.cos(ang), jnp.sin(ang)
    z_lo = jnp.zeros((s, MLA_NOPE), F32)
    z_hi = jnp.zeros((s, LANES - MLA_NOPE - MLA_ROPE), F32)
    return (jnp.concatenate([z_lo, cos, cos, z_hi], axis=1), jnp.concatenate([z_lo, sin, sin, z_hi], axis=1))


def _rot_cols(w):
    half = w.shape[-1] // 2
    return jnp.concatenate([-w[..., half:], w[..., :half]], axis=-1)


def _proj0(x2, seq, pre, w_in, q_norm, w_uq, kv_norm, w_ukv):
    m, d = x2.shape
    tm = TOKEN_TILE
    kr_w = w_in[:, 1152:1184]
    z64 = jnp.zeros((d, 64), F32)
    z32 = jnp.zeros((d, 32), F32)
    w0 = jnp.concatenate([w_in[:, :1152], z64, kr_w, z32, z64, _rot_cols(kr_w), z32], axis=1).astype(BF16)
    wq = w_uq.reshape(-1, MLA_HEADS, MLA_NOPE + MLA_ROPE)
    rq = wq.shape[0]
    zq = jnp.zeros((rq, MLA_HEADS, 32), F32)
    wq1 = jnp.concatenate([wq, zq], axis=2).reshape(rq, -1).astype(BF16)
    wq2 = jnp.concatenate([jnp.zeros((rq, MLA_HEADS, 64), F32), _rot_cols(wq[:, :, MLA_NOPE:]), zq],
                          axis=2).reshape(rq, -1).astype(BF16)
    wkv = w_ukv.reshape(-1, MLA_HEADS, MLA_NOPE + MLA_V)
    rk = wkv.shape[0]
    wk = jnp.concatenate([wkv[:, :, :MLA_NOPE], jnp.zeros((rk, MLA_HEADS, 64), F32)], axis=2).reshape(rk, -1).astype(BF16)
    wv = wkv[:, :, MLA_NOPE:].reshape(rk, -1).astype(BF16)
    cos_t, sin_t = _rope_lane_tables(seq)
    n_st = seq // tm
    tok = lambda n: pl.BlockSpec((tm, n), lambda i: (i, 0))
    outs = pl.pallas_call(
        _proj0_body,
        grid=(m // tm,),
        in_specs=[
            tok(d), _const_spec((1, d)), _const_spec(w0.shape),
            _const_spec((1, rq)), _const_spec(wq1.shape), _const_spec(wq2.shape),
            _const_spec((1, rk)), _const_spec(wk.shape), _const_spec(wv.shape),
            pl.BlockSpec((tm, LANES), lambda i: (i % n_st, 0)),
            pl.BlockSpec((tm, LANES), lambda i: (i % n_st, 0)),
        ],
        out_specs=[tok(512), tok(128), tok(128), tok(1024), tok(1024), tok(512)],
        out_shape=[jax.ShapeDtypeStruct((m, n), BF16) for n in (512, 128, 128, 1024, 1024, 512)],
        compiler_params=_cparams(1),
        name="proj0",
    )(x2, pre.reshape(1, d), w0, q_norm.reshape(1, rq), wq1, wq2, kv_norm.reshape(1, rk), wk, wv, cos_t, sin_t)
    return outs


def _stack_heads(q_ref, hkv):
    return jnp.concatenate(
        [q_ref[0, :, (hkv * 4 + g) * HEAD_DIM:(hkv * 4 + g + 1) * HEAD_DIM] for g in range(4)], axis=0)


def _swa_body(q_ref, kp_ref, kc_ref, vp_ref, vc_ref, bias_ref, sink_ref, o_ref):
    n = pl.program_id(1)
    row = lax.broadcasted_iota(jnp.int32, (4 * TILE, TILE), 0) & (TILE - 1)
    col = lax.broadcasted_iota(jnp.int32, (4 * TILE, TILE), 1)
    no_prev = jnp.where(n >= 1, 0, TILE)
    outs = []
    for hkv in range(2):
        sl = slice(hkv * HEAD_DIM, (hkv + 1) * HEAD_DIM)
        qs = _stack_heads(q_ref, hkv)
        s_p = _dot_nt(qs, kp_ref[0, :, sl]) + bias_ref[hkv, 1]
        s_c = _dot_nt(qs, kc_ref[0, :, sl]) + bias_ref[hkv, 0]
        s_p = jnp.where(col > row + no_prev, s_p, NEG_INF)
        s_c = jnp.where(col <= row, s_c, NEG_INF)
        sink = sink_ref[hkv]
        mx = jnp.maximum(jnp.maximum(jnp.max(s_p, -1, keepdims=True), jnp.max(s_c, -1, keepdims=True)), sink)
        p_p = jnp.exp(s_p - mx)
        p_c = jnp.exp(s_c - mx)
        den = jnp.sum(p_p, -1, keepdims=True) + jnp.sum(p_c, -1, keepdims=True) + jnp.exp(sink - mx)
        o = (_dot(p_p.astype(BF16), vp_ref[0, :, sl]) + _dot(p_c.astype(BF16), vc_ref[0, :, sl])) / den
        outs.extend(o[g * TILE:(g + 1) * TILE] for g in range(4))
    o_ref[0] = jnp.concatenate(outs, axis=1).astype(BF16)


def _swa(qa, ka, va, bias_a, sinks):
    b, s, _ = qa.shape
    nb = s // TILE
    sink_col = jnp.broadcast_to(sinks.astype(F32).reshape(2, 4, 1, 1), (2, 4, TILE, 1)).reshape(2, 4 * TILE, 1)
    prev = lambda bi, n: (bi, jnp.maximum(n - 1, 0), 0)
    cur = lambda bi, n: (bi, n, 0)
    return pl.pallas_call(
        _swa_body,
        grid=(b, nb),
        in_specs=[
            pl.BlockSpec((1, TILE, 512), cur),
            pl.BlockSpec((1, TILE, 128), prev), pl.BlockSpec((1, TILE, 128), cur),
            pl.BlockSpec((1, TILE, 128), prev), pl.BlockSpec((1, TILE, 128), cur),
            _const_spec(bias_a.shape), _const_spec(sink_col.shape),
        ],
        out_specs=pl.BlockSpec((1, TILE, 512), cur),
        out_shape=jax.ShapeDtypeStruct((b, s, 512), BF16),
        compiler_params=_cparams(2),
        name="swa",
    )(qa, ka, ka, va, va, bias_a, sink_col)


def _flash_first(s, v):
    m = jnp.max(s, -1, keepdims=True)
    p = jnp.exp(s - m)
    return m, jnp.sum(p, -1, keepdims=True), _dot(p.astype(BF16), v)


def _flash_next(s, v, m, l, acc):
    m_new = jnp.maximum(m, jnp.max(s, -1, keepdims=True))
    alpha = jnp.exp(m - m_new)
    p = jnp.exp(s - m_new)
    return m_new, alpha * l + jnp.sum(p, -1, keepdims=True), alpha * acc + _dot(p.astype(BF16), v)


def _mla_body(q_ref, k_ref, v_ref, o_ref, *, tq):
    qi = pl.program_id(2)
    row = lax.broadcasted_iota(jnp.int32, (tq, tq), 0)
    col = lax.broadcasted_iota(jnp.int32, (tq, tq), 1)
    diag = pl.multiple_of(qi * tq, tq)
    outs = []
    for hd in range(2):
        ksl = slice(hd * LANES, (hd + 1) * LANES)
        vsl = slice(hd * MLA_V, (hd + 1) * MLA_V)
        q = q_ref[0, :, ksl]
        s = jnp.where(col <= row, _dot_nt(q, k_ref[0, pl.ds(diag, tq), ksl]), NEG_INF)
        carry = _flash_first(s, v_ref[0, pl.ds(diag, tq), vsl])

        def body(kt, c, q=q, ksl=ksl, vsl=vsl):
            st = pl.multiple_of(kt * tq, tq)
            return _flash_next(_dot_nt(q, k_ref[0, pl.ds(st, tq), ksl]), v_ref[0, pl.ds(st, tq), vsl], *c)

        m, l, acc = lax.fori_loop(0, qi, body, carry)
        outs.append(acc / l)
    o_ref[0] = jnp.concatenate(outs, axis=1).astype(BF16)


def _mla(qb, kb, vb, tq=256):
    b, s, _ = qb.shape
    return pl.pallas_call(
        functools.partial(_mla_body, tq=tq),
        grid=(b, MLA_HEADS // 2, s // tq),
        in_specs=[
            pl.BlockSpec((1, tq, 2 * LANES), lambda bi, hp, qi: (bi, qi, hp)),
            pl.BlockSpec((1, s, 2 * LANES), lambda bi, hp, qi: (bi, 0, hp)),
            pl.BlockSpec((1, s, 2 * MLA_V), lambda bi, hp, qi: (bi, 0, hp)),
        ],
        out_specs=pl.BlockSpec((1, tq, 2 * MLA_V), lambda bi, hp, qi: (bi, qi, hp)),
        out_shape=jax.ShapeDtypeStruct((b, s, MLA_HEADS * MLA_V), BF16),
        compiler_params=_cparams(3),
        name="mla",
    )(qb, kb, vb)


def _proj1_body(x_ref, pre_ref, w_ref, qc_ref, kc_ref, vc_ref, ks_ref, vs_ref, kw_ref, vw_ref, g_ref,
                qd_ref, kd_ref, vd_ref, km_ref, *, tm):
    h = _rms(x_ref[...], pre_ref[...]).astype(BF16)
    y = _dot(h, w_ref[...])
    scale = HEAD_DIM ** -0.5
    qc_ref[...] = (y[:, 0:512] * scale).astype(BF16)
    for hkv in range(2):
        kc_ref[0, hkv] = y[:, 512 + hkv * 64:576 + hkv * 64].astype(BF16)
        vc_ref[0, hkv] = y[:, 640 + hkv * 64:704 + hkv * 64].astype(BF16)
    ks_ref[...] = y[:, 768:896].astype(BF16)
    vs_ref[...] = y[:, 896:1024].astype(BF16)
    kw_ref[...] = y[:, 1024:1152].astype(BF16)
    vw_ref[...] = y[:, 1152:1280].astype(BF16)
    gl = y[:, 1280:1408]
    g_ref[...] = 1.0 / (1.0 + jnp.exp(-gl))
    qd_ref[...] = (y[:, 1408:1920] * scale).astype(BF16)
    kd = y[:, 1920:2432]
    kd_ref[...] = kd.astype(BF16)
    vd_ref[...] = y[:, 2432:2944].astype(BF16)
    for j in range(tm // MOBA_BLOCK):
        km_ref[j] = jnp.mean(kd[j * MOBA_BLOCK:(j + 1) * MOBA_BLOCK], axis=0, keepdims=True)


def _proj1(x2, batch, seq, pre, w_in):
    m, d = x2.shape
    tm = TOKEN_TILE
    n_st = seq // tm
    pad = jnp.zeros((d, LANES - 24), F32)
    w1 = jnp.concatenate([w_in[:, :1304], pad, w_in[:, 1304:]], axis=1).astype(BF16)
    tok = lambda n: pl.BlockSpec((tm, n), lambda i: (i, 0))
    hm = pl.BlockSpec((1, 2, tm, HEAD_DIM), lambda i: (i // n_st, 0, i % n_st, 0))
    nblk = tm // MOBA_BLOCK
    return pl.pallas_call(
        functools.partial(_proj1_body, tm=tm),
        grid=(m // tm,),
        in_specs=[tok(d), _const_spec((1, d)), _const_spec(w1.shape)],
        out_specs=[tok(512), hm, hm, tok(128), tok(128), tok(128), tok(128), tok(128), tok(512), tok(512), tok(512),
                   pl.BlockSpec((nblk, 1, 512), lambda i: (i, 0, 0))],
        out_shape=[
            jax.ShapeDtypeStruct((m, 512), BF16),
            jax.ShapeDtypeStruct((batch, 2, seq, HEAD_DIM), BF16),
            jax.ShapeDtypeStruct((batch, 2, seq, HEAD_DIM), BF16),
            jax.ShapeDtypeStruct((m, 128), BF16), jax.ShapeDtypeStruct((m, 128), BF16),
            jax.ShapeDtypeStruct((m, 128), BF16), jax.ShapeDtypeStruct((m, 128), BF16),
            jax.ShapeDtypeStruct((m, 128), F32),
            jax.ShapeDtypeStruct((m, 512), BF16), jax.ShapeDtypeStruct((m, 512), BF16),
            jax.ShapeDtypeStruct((m, 512), BF16),
            jax.ShapeDtypeStruct((m // MOBA_BLOCK, 1, 512), F32),
        ],
        compiler_params=_cparams(1),
        name="proj1",
    )(x2, pre.reshape(1, d), w1)


def _compress_body(kx_ref, vx_ref, pek_ref, pev_ref, wk1_ref, wk2_ref, wv1_ref, wv2_ref, ko_ref, vo_ref):
    half = CMP_STRIDE * HEAD_DIM
    for x_ref, pe_ref, w1_ref, w2_ref, o_ref in ((kx_ref, pek_ref, wk1_ref, wk2_ref, ko_ref),
                                                 (vx_ref, pev_ref, wv1_ref, wv2_ref, vo_ref)):
        pe8 = jnp.broadcast_to(pe_ref[...], (8, 2 * half)).astype(BF16)
        pe_term = _dot(pe8, w1_ref[...])[0:1]
        outs = []
        for hkv in range(2):
            x = x_ref[0, hkv]
            lo = _dot(x, w1_ref[0:half])
            hi = _dot(x, w1_ref[half:2 * half])
            hid = lo + pltpu.roll(hi, x.shape[0] - 1, 0) + pe_term
            outs.append(_dot(_silu(hid).astype(BF16), w2_ref[...]))
        o_ref[0] = jnp.concatenate(outs, axis=1).astype(BF16)


def _compress(kc_hm, vc_hm, pe_k, pe_v, wk1, wk2, wv1, wv2):
    b, _, s, dh = kc_hm.shape
    rows = s // CMP_STRIDE
    feat = CMP_STRIDE * dh
    kx = kc_hm.reshape(b, 2, rows, feat)
    vx = vc_hm.reshape(b, 2, rows, feat)
    xspec = pl.BlockSpec((1, 2, rows, feat), lambda bi: (bi, 0, 0, 0))
    ospec = pl.BlockSpec((1, rows, 2 * dh), lambda bi: (bi, 0, 0))
    w1s, w2s = wk1.shape, wk2.shape
    return pl.pallas_call(
        _compress_body,
        grid=(b,),
        in_specs=[xspec, xspec, _const_spec((1, 2 * feat)), _const_spec((1, 2 * feat)),
                  _const_spec(w1s), _const_spec(w2s), _const_spec(w1s), _const_spec(w2s)],
        out_specs=[ospec, ospec],
        out_shape=[jax.ShapeDtypeStruct((b, rows, 2 * dh), BF16)] * 2,
        compiler_params=_cparams(1),
        name="nsa_compress",
    )(kx, vx, pe_k.astype(F32).reshape(1, -1), pe_v.astype(F32).reshape(1, -1),
      wk1.astype(BF16), wk2.astype(BF16), wv1.astype(BF16), wv2.astype(BF16))


def _topk_mask(val, lane, n_real, k):
    rank = jnp.zeros(val.shape, F32)
    for j in range(n_real):
        vj = val[:, j:j + 1]
        beats = (vj > val) | ((vj == val) & (lane > j))
        rank = rank + jnp.where(beats, 1.0, 0.0)
    return jnp.where((rank < k) & (lane < n_real), 1.0, 0.0)


def _nsa_body(q_ref, kc_ref, vc_ref, ks_ref, vs_ref, kw_ref, vw_ref, g_ref, bias_ref, ov_ref, e_ref, o_ref):
    qi = pl.program_id(1)
    rows = 4 * TILE
    row = lax.broadcasted_iota(jnp.int32, (rows, TILE), 0) & (TILE - 1)
    col = lax.broadcasted_iota(jnp.int32, (rows, TILE), 1)
    rowq = lax.broadcasted_iota(jnp.int32, (TILE, TILE), 0)
    lane = lax.broadcasted_iota(jnp.int32, (TILE, TILE), 1)
    n_sel = 2048 // SEL_BLOCK
    cur = qi * (TILE // SEL_BLOCK) + rowq // SEL_BLOCK
    forced = (lane == 0) | (lane == cur) | (lane == cur - 1)
    diag = pl.multiple_of(qi * TILE, TILE)
    gates = g_ref[0]
    outs = []
    for hkv in range(2):
        sl = slice(hkv * HEAD_DIM, (hkv + 1) * HEAD_DIM)
        qs = _stack_heads(q_ref, hkv)

        valid = CMP_STRIDE * col + (CMP_BLOCK - 1) <= qi * TILE + row
        sc = jnp.where(valid, _dot_nt(qs, kc_ref[0, :, sl]), NEG_INF)
        mc = jnp.max(sc, -1, keepdims=True)
        ec = jnp.where(valid, jnp.exp(sc - mc), 0.0)
        lc = jnp.sum(ec, -1, keepdims=True)
        p = ec / jnp.where(lc > 0.0, lc, 1.0)
        o_cmp = _dot(p.astype(BF16), vc_ref[0, :, sl])

        ps = p[0:TILE] + p[TILE:2 * TILE] + p[2 * TILE:3 * TILE] + p[3 * TILE:4 * TILE]
        ps_hi = ps.astype(BF16)
        ps_lo = (ps - ps_hi.astype(F32)).astype(BF16)
        imp = _dot(ps_hi, ov_ref[...]) + _dot(ps_lo, ov_ref[...])
        val = jnp.where(forced, FORCE_SCORE, jnp.where(lane <= cur, imp, NEG_INF))
        val = jnp.where(lane < n_sel, val, PAD_SCORE)
        sel = _topk_mask(val, lane, n_sel, SEL_TOPK).astype(BF16)

        def sel_tile(kt, d, sl=sl, qs=qs, sel=sel, hkv=hkv):
            st = pl.multiple_of(kt * TILE, TILE)
            s = _dot_nt(qs, ks_ref[0, pl.ds(st, TILE), sl]) + bias_ref[hkv, d]
            keep = _dot(sel, e_ref[kt])
            keep = jnp.concatenate([keep] * 4, axis=0)
            return s, keep, vs_ref[0, pl.ds(st, TILE), sl]

        s, keep, v = sel_tile(qi, 0)
        s = jnp.where((keep > 0.5) & (col <= row), s, NEG_INF)
        carry = _flash_first(s, v)

        def body(kt, c, sel_tile=sel_tile):
            s, keep, v = sel_tile(kt, jnp.minimum(qi - kt, BIAS_TILES - 1))
            return _flash_next(jnp.where(keep > 0.5, s, NEG_INF), v, *c)

        m, l, acc = lax.fori_loop(0, qi, body, carry)
        o_slc = acc / l

        t1 = pl.multiple_of(jnp.maximum(qi - 1, 0) * TILE, TILE)
        t2 = pl.multiple_of(jnp.maximum(qi - 2, 0) * TILE, TILE)
        s0 = _dot_nt(qs, kw_ref[0, pl.ds(diag, TILE), sl]) + bias_ref[hkv, 0]
        s1 = _dot_nt(qs, kw_ref[0, pl.ds(t1, TILE), sl]) + bias_ref[hkv, 1]
        s2 = _dot_nt(qs, kw_ref[0, pl.ds(t2, TILE), sl]) + bias_ref[hkv, 2]
        s0 = jnp.where(col <= row, s0, NEG_INF)
        s1 = jnp.where(col >= jnp.where(qi >= 1, 0, TILE), s1, NEG_INF)
        s2 = jnp.where(col > row + jnp.where(qi >= 2, 0, TILE), s2, NEG_INF)
        mw = jnp.maximum(jnp.maximum(jnp.max(s0, -1, keepdims=True), jnp.max(s1, -1, keepdims=True)),
                         jnp.max(s2, -1, keepdims=True))
        p0, p1, p2 = jnp.exp(s0 - mw), jnp.exp(s1 - mw), jnp.exp(s2 - mw)
        lw = jnp.sum(p0, -1, keepdims=True) + jnp.sum(p1, -1, keepdims=True) + jnp.sum(p2, -1, keepdims=True)
        o_win = (_dot(p0.astype(BF16), vw_ref[0, pl.ds(diag, TILE), sl])
                 + _dot(p1.astype(BF16), vw_ref[0, pl.ds(t1, TILE), sl])
                 + _dot(p2.astype(BF16), vw_ref[0, pl.ds(t2, TILE), sl])) / lw

        for g in range(4):
            hd = hkv * 4 + g
            rs = slice(g * TILE, (g + 1) * TILE)
            outs.append(gates[:, hd:hd + 1] * o_cmp[rs] + gates[:, 8 + hd:9 + hd] * o_slc[rs]
                        + gates[:, 16 + hd:17 + hd] * o_win[rs])
    o_ref[0] = jnp.concatenate(outs, axis=1).astype(BF16)


def _nsa(qc, kcmp, vcmp, ks, vs, kw, vw, gates, bias_c):
    b, s, _ = qc.shape
    nq = s // TILE
    n_c = (s - CMP_BLOCK) // CMP_STRIDE + 1
    c_start = np.arange(TILE)[:, None] * CMP_STRIDE
    j_start = np.arange(TILE)[None, :] * SEL_BLOCK
    overlap = ((c_start < j_start + SEL_BLOCK) & (c_start + CMP_BLOCK > j_start)
               & (np.arange(TILE)[:, None] < n_c) & (np.arange(TILE)[None, :] < s // SEL_BLOCK))
    key = np.arange(s).reshape(nq, 1, TILE)
    expand = (key // SEL_BLOCK) == np.arange(TILE).reshape(1, TILE, 1)
    ov = jnp.asarray(overlap, BF16)
    ex = jnp.asarray(expand, BF16)
    qtile = lambda n: pl.BlockSpec((1, TILE, n), lambda bi, qi: (bi, qi, 0))
    full = pl.BlockSpec((1, s, 128), lambda bi, qi: (bi, 0, 0))
    cmp_spec = pl.BlockSpec((1, TILE, 128), lambda bi, qi: (bi, 0, 0))
    return pl.pallas_call(
        _nsa_body,
        grid=(b, nq),
        in_specs=[qtile(512), cmp_spec, cmp_spec, full, full, full, full, qtile(128),
                  _const_spec(bias_c.shape), _const_spec(ov.shape), _const_spec(ex.shape)],
        out_specs=qtile(512),
        out_shape=jax.ShapeDtypeStruct((b, s, 512), BF16),
        compiler_params=_cparams(2),
        name="nsa",
    )(qc, kcmp, vcmp, ks, vs, kw, vw, gates, bias_c, ov, ex)


def _moba_body(q_ref, k_ref, v_ref, km_ref, bias_ref, e_ref, o_ref):
    qi = pl.program_id(2)
    tiles_per_block = MOBA_BLOCK // TILE
    cur = qi // tiles_per_block
    row = lax.broadcasted_iota(jnp.int32, (TILE, TILE), 0)
    lane = lax.broadcasted_iota(jnp.int32, (TILE, TILE), 1)
    n_blocks = 2048 // MOBA_BLOCK
    diag = pl.multiple_of(qi * TILE, TILE)
    outs = []
    for hd in range(2):
        sl = slice(hd * HEAD_DIM, (hd + 1) * HEAD_DIM)
        q = q_ref[0, :, sl]
        km = jnp.concatenate([km_ref[0, :, sl], jnp.zeros((TILE - n_blocks, HEAD_DIM), F32)], axis=0)
        gate = _dot_nt(q, km.astype(BF16))
        val = jnp.where(lane < cur, gate, NEG_INF)
        val = jnp.where(lane < n_blocks, val, PAD_SCORE)
        sel = _topk_mask(val, lane, n_blocks, MOBA_TOPK)
        sel = jnp.where(lane < cur, sel, 0.0).astype(BF16)

        s = _dot_nt(q, k_ref[0, pl.ds(diag, TILE), sl]) + bias_ref[hd, 0]
        carry = _flash_first(jnp.where(lane <= row, s, NEG_INF), v_ref[0, pl.ds(diag, TILE), sl])

        def body(kt, c, q=q, sl=sl, sel=sel, hd=hd):
            st = pl.multiple_of(kt * TILE, TILE)
            s = _dot_nt(q, k_ref[0, pl.ds(st, TILE), sl]) + bias_ref[hd, jnp.minimum(qi - kt, BIAS_TILES - 1)]
            own = jnp.where(kt // tiles_per_block == cur, 1.0, 0.0)
            keep = _dot(sel, e_ref[kt]) + own
            return _flash_next(jnp.where(keep > 0.5, s, NEG_INF), v_ref[0, pl.ds(st, TILE), sl], *c)

        m, l, acc = lax.fori_loop(0, qi, body, carry)
        outs.append(acc / l)
    o_ref[0] = jnp.concatenate(outs, axis=1).astype(BF16)


def _moba(qd, kd, vd, kmean, bias_d):
    b, s, _ = qd.shape
    nq = s // TILE
    key = np.arange(s).reshape(nq, 1, TILE)
    expand = (key // MOBA_BLOCK) == np.arange(TILE).reshape(1, TILE, 1)
    ex = jnp.asarray(expand, BF16)
    n_blocks = s // MOBA_BLOCK
    return pl.pallas_call(
        _moba_body,
        grid=(b, 4, nq),
        in_specs=[
            pl.BlockSpec((1, TILE, 128), lambda bi, hp, qi: (bi, qi, hp)),
            pl.BlockSpec((1, s, 128), lambda bi, hp, qi: (bi, 0, hp)),
            pl.BlockSpec((1, s, 128), lambda bi, hp, qi: (bi, 0, hp)),
            pl.BlockSpec((1, n_blocks, 128), lambda bi, hp, qi: (bi, 0, hp)),
            pl.BlockSpec((2, BIAS_TILES, TILE, TILE), lambda bi, hp, qi: (hp, 0, 0, 0)),
            _const_spec(ex.shape),
        ],
        out_specs=pl.BlockSpec((1, TILE, 128), lambda bi, hp, qi: (bi, qi, hp)),
        out_shape=jax.ShapeDtypeStruct((b, s, 512), BF16),
        compiler_params=_cparams(3),
        name="moba",
    )(qd, kd, vd, kmean.reshape(b, n_blocks, 512), bias_d, ex)


def kernel(x, rel_bias_table, l0_ffn1_pre, l0_ffn1_post, l0_ffn1_wg, l0_ffn1_wu, l0_ffn1_wd, l0_mix_pre, l0_mix_post, l0_w_in, l0_sinks, l0_mla_q_norm, l0_mla_w_uq, l0_mla_kv_norm, l0_mla_w_ukv, l0_w_out, l0_ffn2_pre, l0_ffn2_post, l0_ffn2_wg, l0_ffn2_wu, l0_ffn2_wd, l1_ffn1_pre, l1_ffn1_post, l1_ffn1_wg, l1_ffn1_wu, l1_ffn1_wd, l1_mix_pre, l1_mix_post, l1_w_in, l1_nsa_pe_k, l1_nsa_pe_v, l1_nsa_wk1, l1_nsa_wk2, l1_nsa_wv1, l1_nsa_wv2, l1_w_out, l1_ffn2_pre, l1_ffn2_post, l1_ffn2_wg, l1_ffn2_wu, l1_ffn2_wd):
    b, s, d = x.shape
    m = b * s
    x2 = x.reshape(m, d)
    tiles = _bias_tiles(rel_bias_table)
    bias_a = _stack_group_bias(tiles[A_SLOT:A_SLOT + 8], 2)
    bias_c = _stack_group_bias(tiles[C_SLOT:C_SLOT + 8], BIAS_TILES)
    bias_d = tiles[D_SLOT:D_SLOT + 8]

    x2 = _ffn(x2, l0_ffn1_pre, l0_ffn1_post, l0_ffn1_wg, l0_ffn1_wu, l0_ffn1_wd, 0.5)
    qa, ka, va, qb, kb, vb = _proj0(x2, s, l0_mix_pre, l0_w_in, l0_mla_q_norm, l0_mla_w_uq,
                                    l0_mla_kv_norm, l0_mla_w_ukv)
    shp = lambda t: t.reshape(b, s, t.shape[-1])
    o_a = _swa(shp(qa), shp(ka), shp(va), bias_a, l0_sinks)
    o_b = _mla(shp(qb), shp(kb), shp(vb))
    x2 = _outproj(x2, o_a.reshape(m, -1), o_b.reshape(m, -1), l0_w_out, l0_mix_post)
    x2 = _ffn(x2, l0_ffn2_pre, l0_ffn2_post, l0_ffn2_wg, l0_ffn2_wu, l0_ffn2_wd, 0.5)

    x2 = _ffn(x2, l1_ffn1_pre, l1_ffn1_post, l1_ffn1_wg, l1_ffn1_wu, l1_ffn1_wd, 0.5)
    qc, kc_hm, vc_hm, ks, vs, kw, vw, gates, qd, kd, vd, kmean = _proj1(x2, b, s, l1_mix_pre, l1_w_in)
    kcmp, vcmp = _compress(kc_hm, vc_hm, l1_nsa_pe_k, l1_nsa_pe_v, l1_nsa_wk1, l1_nsa_wk2, l1_nsa_wv1, l1_nsa_wv2)
    o_c = _nsa(shp(qc), kcmp, vcmp, shp(ks), shp(vs), shp(kw), shp(vw), shp(gates), bias_c)
    o_d = _moba(shp(qd), shp(kd), shp(vd), kmean, bias_d)
    x2 = _outproj(x2, o_c.reshape(m, -1), o_d.reshape(m, -1), l1_w_out, l1_mix_post)
    x2 = _ffn(x2, l1_ffn2_pre, l1_ffn2_post, l1_ffn2_wg, l1_ffn2_wu, l1_ffn2_wd, 0.5)
    return x2.reshape(b, s, d)
```

```python
import functools
import math

import numpy as np
import jax
import jax.numpy as jnp
from jax import lax
from jax.experimental import pallas as pl
from jax.experimental.pallas import tpu as pltpu

F32 = jnp.float32
BF16 = jnp.bfloat16

D_MODEL = 1024
HEAD_DIM = 64
D_FF = 2816
NORM_EPS = 1e-6
NEG_INF = -1e30
PAD_SCORE = -3e38
FORCE_SCORE = 1e9
REL_BUCKETS = 32
REL_MAX_DIST = 1024
REL_SLOTS = 24
A_SLOT, C_SLOT, D_SLOT = 0, 8, 16
MLA_NOPE, MLA_ROPE, MLA_V = 64, 32, 64
MLA_HEADS = 8
ROPE_THETA = 10000.0
CMP_BLOCK, CMP_STRIDE = 32, 16
SEL_BLOCK, SEL_TOPK = 64, 8
MOBA_BLOCK, MOBA_TOPK = 256, 3

LANES = 128
TILE = 128
BLK = 256
BIAS_TILES = 9
TOKEN_TILE = 512
FF_CHUNK = 256
VMEM_LIMIT = 56 * 1024 * 1024


def _cparams(n_axes):
    return pltpu.CompilerParams(dimension_semantics=("arbitrary",) * n_axes, vmem_limit_bytes=VMEM_LIMIT)


def _dot(a, b):
    return jnp.dot(a, b, preferred_element_type=F32)


def _dot_nt(a, b):
    return lax.dot_general(a, b, (((1,), (1,)), ((), ())), preferred_element_type=F32)


def _rms(x, g):
    return x * lax.rsqrt(jnp.mean(x * x, axis=-1, keepdims=True) + NORM_EPS) * g


def _silu(x):
    return x / (1.0 + jnp.exp(-x))


def _const_spec(shape):
    nd = len(shape)
    return pl.BlockSpec(shape, lambda *_: (0,) * nd)


def _ffn_body(x_ref, pre_ref, post_ref, wg_ref, wu_ref, wd_ref, o_ref, acc_ref, *, n_chunks, weight):
    x = x_ref[...]
    h = _rms(x, pre_ref[...]).astype(BF16)
    acc_ref[...] = jnp.zeros_like(acc_ref)

    def chunk(c, carry):
        g = _dot(h, wg_ref[c])
        u = _dot(h, wu_ref[c])
        a = (_silu(g) * u).astype(BF16)
        acc_ref[...] += _dot(a, wd_ref[c])
        return carry

    lax.fori_loop(0, n_chunks, chunk, 0)
    o_ref[...] = x + weight * _rms(acc_ref[...], post_ref[...])


def _ffn(x2, pre, post, wg, wu, wd, weight):
    m, d = x2.shape
    n_chunks = D_FF // FF_CHUNK
    wg3 = wg.astype(BF16).reshape(d, n_chunks, FF_CHUNK).transpose(1, 0, 2)
    wu3 = wu.astype(BF16).reshape(d, n_chunks, FF_CHUNK).transpose(1, 0, 2)
    wd3 = wd.astype(BF16).reshape(n_chunks, FF_CHUNK, d)
    tm = TOKEN_TILE
    return pl.pallas_call(
        functools.partial(_ffn_body, n_chunks=n_chunks, weight=weight),
        grid=(m // tm,),
        in_specs=[
            pl.BlockSpec((tm, d), lambda i: (i, 0)),
            _const_spec((1, d)), _const_spec((1, d)),
            _const_spec((n_chunks, d, FF_CHUNK)), _const_spec((n_chunks, d, FF_CHUNK)),
            _const_spec((n_chunks, FF_CHUNK, d)),
        ],
        out_specs=pl.BlockSpec((tm, d), lambda i: (i, 0)),
        out_shape=jax.ShapeDtypeStruct((m, d), F32),
        scratch_shapes=[pltpu.VMEM((tm, d), F32)],
        compiler_params=_cparams(1),
        name="ffn",
    )(x2, pre.reshape(1, d), post.reshape(1, d), wg3, wu3, wd3)


def _outproj_body(x_ref, o1_ref, o2_ref, w1_ref, w2_ref, post_ref, o_ref):
    y = _dot(o1_ref[...], w1_ref[...]) + _dot(o2_ref[...], w2_ref[...])
    o_ref[...] = x_ref[...] + _rms(y, post_ref[...])


def _outproj(x2, o1, o2, w_out, post):
    m, d = x2.shape
    n1 = o1.shape[1]
    n2 = o2.shape[1]
    w = w_out.astype(BF16)
    tm = TOKEN_TILE
    return pl.pallas_call(
        _outproj_body,
        grid=(m // tm,),
        in_specs=[
            pl.BlockSpec((tm, d), lambda i: (i, 0)),
            pl.BlockSpec((tm, n1), lambda i: (i, 0)),
            pl.BlockSpec((tm, n2), lambda i: (i, 0)),
            _const_spec((n1, d)), _const_spec((n2, d)), _const_spec((1, d)),
        ],
        out_specs=pl.BlockSpec((tm, d), lambda i: (i, 0)),
        out_shape=jax.ShapeDtypeStruct((m, d), F32),
        compiler_params=_cparams(1),
        name="outproj",
    )(x2, o1, o2, w[:n1], w[n1:], post.reshape(1, d))


def _t5_bucket(dist):
    n = jnp.maximum(dist, 0)
    exact = REL_BUCKETS // 2
    nf = jnp.maximum(n, 1).astype(jnp.float32)
    large = exact + (jnp.log(nf / exact) / math.log(REL_MAX_DIST / exact) * (REL_BUCKETS - exact)).astype(jnp.int32)
    return jnp.where(n < exact, n, jnp.minimum(large, REL_BUCKETS - 1))


def _bias_body(tab_ref, idx_ref, o_ref, *, first_slot):
    slot = pl.program_id(0) + first_slot

    def tile(d, carry):
        idx = idx_ref[d]
        v = jnp.full((TILE, TILE), tab_ref[slot], F32)
        for b in range(1, REL_BUCKETS):
            v = jnp.where(idx == b, tab_ref[b * REL_SLOTS + slot], v)
        o_ref[0, d] = v
        return carry

    lax.fori_loop(0, BIAS_TILES, tile, 0)


def _bias_tiles(rel_table, first_slot, n_slots, keys_on_rows):
    d = jnp.arange(BIAS_TILES)[:, None, None]
    r = jnp.arange(TILE)[None, :, None]
    c = jnp.arange(TILE)[None, None, :]
    idx = _t5_bucket(d * TILE + (c - r if keys_on_rows else r - c)).astype(jnp.int32)
    return pl.pallas_call(
        functools.partial(_bias_body, first_slot=first_slot),
        grid=(n_slots,),
        in_specs=[
            pl.BlockSpec(memory_space=pltpu.SMEM),
            _const_spec((BIAS_TILES, TILE, TILE)),
        ],
        out_specs=pl.BlockSpec((1, BIAS_TILES, TILE, TILE), lambda s: (s, 0, 0, 0)),
        out_shape=jax.ShapeDtypeStruct((n_slots, BIAS_TILES, TILE, TILE), F32),
        compiler_params=_cparams(1),
        name="bias_tiles",
    )(rel_table.astype(F32).reshape(-1), idx)


def _stack_group_bias(tiles, n_tiles):
    t = tiles[:, :n_tiles].reshape(2, 4, n_tiles, TILE, TILE)
    return t.transpose(0, 2, 1, 3, 4).reshape(2, n_tiles, 4 * TILE, TILE)


def _proj0_body(x_ref, pre_ref, w_ref, qn_ref, wq1_ref, wq2_ref, kvn_ref, wk_ref, wv_ref, cos_ref, sin_ref,
                qa_ref, ka_ref, va_ref, qb_ref, kb_ref, vbt_ref):
    h = _rms(x_ref[...], pre_ref[...]).astype(BF16)
    y = _dot(h, w_ref[...])
    qa_ref[...] = (y[:, 0:512] * (HEAD_DIM ** -0.5)).astype(BF16)
    ka_ref[...] = y[:, 512:640].astype(BF16)
    va_ref[...] = y[:, 640:768].astype(BF16)
    cq = _rms(y[:, 768:1024], qn_ref[...]).astype(BF16)
    ckv = _rms(y[:, 1024:1152], kvn_ref[...]).astype(BF16)
    cos_t = cos_ref[...]
    sin_t = sin_ref[...]
    lane = lax.broadcasted_iota(jnp.int32, cos_t.shape, 1)
    scale = (MLA_NOPE + MLA_ROPE) ** -0.5
    qmul = scale * (cos_t + jnp.where(lane < MLA_NOPE, 1.0, 0.0))
    qsin = scale * sin_t
    kr = y[:, 1152:1280] * cos_t + y[:, 1280:1408] * sin_t
    q1 = _dot(cq, wq1_ref[...])
    q2 = _dot(cq, wq2_ref[...])
    k1 = _dot(ckv, wk_ref[...])
    for hd in range(MLA_HEADS):
        sl = slice(hd * LANES, (hd + 1) * LANES)
        qb_ref[:, sl] = (q1[:, sl] * qmul + q2[:, sl] * qsin).astype(BF16)
        kb_ref[:, sl] = (k1[:, sl] + kr).astype(BF16)
    vb = _dot(ckv, wv_ref[...])
    for j in range(vb.shape[0] // BLK):
        _store_vt_ones(vbt_ref, j, vb[j * BLK:(j + 1) * BLK], MLA_HEADS)


def _rope_lane_tables(s):
    inv = ROPE_THETA ** (-jnp.arange(0, MLA_ROPE, 2, dtype=jnp.float32) / MLA_ROPE)
    ang = jnp.arange(s, dtype=jnp.float32)[:, None] * inv[None, :]
    cos, sin = jnp.cos(ang), jnp.sin(ang)
    z_lo = jnp.zeros((s, MLA_NOPE), F32)
    z_hi = jnp.zeros((s, LANES - MLA_NOPE - MLA_ROPE), F32)
    return (jnp.concatenate([z_lo, cos, cos, z_hi], axis=1), jnp.concatenate([z_lo, sin, sin, z_hi], axis=1))


def _rot_cols(w):
    half = w.shape[-1] // 2
    return jnp.concatenate([-w[..., half:], w[..., :half]], axis=-1)


def _proj0(x2, seq, pre, w_in, q_norm, w_uq, kv_norm, w_ukv):
    m, d = x2.shape
    tm = TOKEN_TILE
    kr_w = w_in[:, 1152:1184]
    z64 = jnp.zeros((d, 64), F32)
    z32 = jnp.zeros((d, 32), F32)
    w0 = jnp.concatenate([w_in[:, :1152], z64, kr_w, z32, z64, _rot_cols(kr_w), z32], axis=1).astype(BF16)
    wq = w_uq.reshape(-1, MLA_HEADS, MLA_NOPE + MLA_ROPE)
    rq = wq.shape[0]
    zq = jnp.zeros((rq, MLA_HEADS, 32), F32)
    wq1 = jnp.concatenate([wq, zq], axis=2).reshape(rq, -1).astype(BF16)
    wq2 = jnp.concatenate([jnp.zeros((rq, MLA_HEADS, 64), F32), _rot_cols(wq[:, :, MLA_NOPE:]), zq],
                          axis=2).reshape(rq, -1).astype(BF16)
    wkv = w_ukv.reshape(-1, MLA_HEADS, MLA_NOPE + MLA_V)
    rk = wkv.shape[0]
    wk = jnp.concatenate([wkv[:, :, :MLA_NOPE], jnp.zeros((rk, MLA_HEADS, 64), F32)], axis=2).reshape(rk, -1).astype(BF16)
    wv = wkv[:, :, MLA_NOPE:].reshape(rk, -1).astype(BF16)
    cos_t, sin_t = _rope_lane_tables(seq)
    n_st = seq // tm
    tok = lambda n: pl.BlockSpec((tm, n), lambda i: (i, 0))
    outs = pl.pallas_call(
        _proj0_body,
        grid=(m // tm,),
        in_specs=[
            tok(d), _const_spec((1, d)), _const_spec(w0.shape),
            _const_spec((1, rq)), _const_spec(wq1.shape), _const_spec(wq2.shape),
            _const_spec((1, rk)), _const_spec(wk.shape), _const_spec(wv.shape),
            pl.BlockSpec((tm, LANES), lambda i: (i % n_st, 0)),
            pl.BlockSpec((tm, LANES), lambda i: (i % n_st, 0)),
        ],
        out_specs=[tok(512), tok(128), tok(128), tok(1024), tok(1024),
                   pl.BlockSpec((1, tm // BLK, 2 * MLA_HEADS * MLA_V, BLK), lambda i: (i // n_st, i % n_st, 0, 0))],
        out_shape=[jax.ShapeDtypeStruct((m, n), BF16) for n in (512, 128, 128, 1024, 1024)]
        + [jax.ShapeDtypeStruct((m // seq, seq // BLK, 2 * MLA_HEADS * MLA_V, BLK), BF16)],
        compiler_params=_cparams(1),
        name="proj0",
    )(x2, pre.reshape(1, d), w0, q_norm.reshape(1, rq), wq1, wq2, kv_norm.reshape(1, rk), wk, wv, cos_t, sin_t)
    return outs


def _stack_heads(q_ref, hkv):
    return jnp.concatenate(
        [q_ref[0, :, (hkv * 4 + g) * HEAD_DIM:(hkv * 4 + g + 1) * HEAD_DIM] for g in range(4)], axis=0)


def _swa_body(q_ref, kp_ref, kc_ref, vp_ref, vc_ref, bias_ref, sink_ref, o_ref):
    n = pl.program_id(1)
    row = lax.broadcasted_iota(jnp.int32, (4 * TILE, TILE), 0) & (TILE - 1)
    col = lax.broadcasted_iota(jnp.int32, (4 * TILE, TILE), 1)
    no_prev = jnp.where(n >= 1, 0, TILE)
    outs = []
    for hkv in range(2):
        sl = slice(hkv * HEAD_DIM, (hkv + 1) * HEAD_DIM)
        qs = _stack_heads(q_ref, hkv)
        s_p = _dot_nt(qs, kp_ref[0, :, sl]) + bias_ref[hkv, 1]
        s_c = _dot_nt(qs, kc_ref[0, :, sl]) + bias_ref[hkv, 0]
        s_p = jnp.where(col > row + no_prev, s_p, NEG_INF)
        s_c = jnp.where(col <= row, s_c, NEG_INF)
        sink = sink_ref[hkv]
        mx = jnp.maximum(jnp.maximum(jnp.max(s_p, -1, keepdims=True), jnp.max(s_c, -1, keepdims=True)), sink)
        p_p = jnp.exp(s_p - mx)
        p_c = jnp.exp(s_c - mx)
        den = jnp.sum(p_p, -1, keepdims=True) + jnp.sum(p_c, -1, keepdims=True) + jnp.exp(sink - mx)
        o = (_dot(p_p.astype(BF16), vp_ref[0, :, sl]) + _dot(p_c.astype(BF16), vc_ref[0, :, sl])) / den
        outs.extend(o[g * TILE:(g + 1) * TILE] for g in range(4))
    o_ref[0] = jnp.concatenate(outs, axis=1).astype(BF16)


def _swa(qa, ka, va, bias_a, sinks):
    b, s, _ = qa.shape
    nb = s // TILE
    sink_col = jnp.broadcast_to(sinks.astype(F32).reshape(2, 4, 1, 1), (2, 4, TILE, 1)).reshape(2, 4 * TILE, 1)
    prev = lambda bi, n: (bi, jnp.maximum(n - 1, 0), 0)
    cur = lambda bi, n: (bi, n, 0)
    return pl.pallas_call(
        _swa_body,
        grid=(b, nb),
        in_specs=[
            pl.BlockSpec((1, TILE, 512), cur),
            pl.BlockSpec((1, TILE, 128), prev), pl.BlockSpec((1, TILE, 128), cur),
            pl.BlockSpec((1, TILE, 128), prev), pl.BlockSpec((1, TILE, 128), cur),
            _const_spec(bias_a.shape), _const_spec(sink_col.shape),
        ],
        out_specs=pl.BlockSpec((1, TILE, 512), cur),
        out_shape=jax.ShapeDtypeStruct((b, s, 512), BF16),
        compiler_params=_cparams(2),
        name="swa",
    )(qa, ka, ka, va, va, bias_a, sink_col)


def _finish(acc, dv):
    return acc[0:dv] / acc[dv:dv + 1]


def _mla_body(q_ref, k_ref, vt_ref, o_ref, acc_ref, *, n_heads):
    qi = pl.program_id(1)
    lanes = n_heads * BLK
    krow = lax.broadcasted_iota(jnp.int32, (BLK, lanes), 0)
    qcol = lax.broadcasted_iota(jnp.int32, (BLK, lanes), 1) & (BLK - 1)

    def scores(n):
        st = pl.multiple_of(n * BLK, BLK)
        return jnp.concatenate(
            [_dot_nt(k_ref[0, pl.ds(st, BLK), hd * LANES:(hd + 1) * LANES], q_ref[0, :, hd * LANES:(hd + 1) * LANES])
             for hd in range(n_heads)], axis=1)

    def weighted_values(n, p):
        return jnp.concatenate(
            [_dot(vt_ref[0, n, hd * 2 * MLA_V:(hd + 1) * 2 * MLA_V, :], p[:, hd * BLK:(hd + 1) * BLK])
             for hd in range(n_heads)], axis=1)

    s = jnp.where(krow <= qcol, scores(qi), NEG_INF)
    m0 = jnp.max(s, axis=0, keepdims=True)
    acc_ref[...] = weighted_values(qi, jnp.exp(s - m0).astype(BF16))

    def body(n, m):
        s = scores(n)
        m_new = jnp.maximum(m, jnp.max(s, axis=0, keepdims=True))
        acc_ref[...] = jnp.exp(m - m_new) * acc_ref[...] + weighted_values(n, jnp.exp(s - m_new).astype(BF16))
        return m_new

    lax.fori_loop(0, qi, body, m0)
    o_t = _finish(acc_ref[...], MLA_V)
    o_ref[0] = jnp.concatenate([o_t[:, hd * BLK:(hd + 1) * BLK] for hd in range(n_heads)],
                               axis=0).T.astype(BF16)


def _mla(qb, kb, vbt):
    b, s, width = qb.shape
    n_heads = width // LANES
    n_blocks = s // BLK
    return pl.pallas_call(
        functools.partial(_mla_body, n_heads=n_heads),
        grid=(b, n_blocks),
        in_specs=[
            pl.BlockSpec((1, BLK, width), lambda bi, qi: (bi, qi, 0)),
            pl.BlockSpec((1, s, width), lambda bi, qi: (bi, 0, 0)),
            pl.BlockSpec((1, n_blocks, n_heads * 2 * MLA_V, BLK), lambda bi, qi: (bi, 0, 0, 0)),
        ],
        out_specs=pl.BlockSpec((1, BLK, n_heads * MLA_V), lambda bi, qi: (bi, qi, 0)),
        out_shape=jax.ShapeDtypeStruct((b, s, n_heads * MLA_V), BF16),
        scratch_shapes=[pltpu.VMEM((2 * MLA_V, n_heads * BLK), F32)],
        compiler_params=_cparams(2),
        name="mla",
    )(qb, kb, vbt)


def _store_vt_ones(vt_ref, j, v, n_heads):
    dv = v.shape[1] // n_heads
    vt = v.T.astype(BF16)
    ones = jnp.ones((dv, v.shape[0]), BF16)
    for hd in range(n_heads):
        vt_ref[0, j, 2 * hd * dv:(2 * hd + 1) * dv, :] = vt[hd * dv:(hd + 1) * dv]
        vt_ref[0, j, (2 * hd + 1) * dv:(2 * hd + 2) * dv, :] = ones


def _proj1_body(x_ref, pre_ref, w_ref, qc_ref, kc_ref, vc_ref, ks_ref, vst_ref, kw_ref, vwt_ref, g_ref,
                qd_ref, kd_ref, vdt_ref, km_ref, *, tm):
    h = _rms(x_ref[...], pre_ref[...]).astype(BF16)
    y = _dot(h, w_ref[...])
    scale = HEAD_DIM ** -0.5
    qc_ref[...] = (y[:, 0:512] * scale).astype(BF16)
    for hkv in range(2):
        kc_ref[0, hkv] = y[:, 512 + hkv * 64:576 + hkv * 64].astype(BF16)
        vc_ref[0, hkv] = y[:, 640 + hkv * 64:704 + hkv * 64].astype(BF16)
    ks_ref[...] = y[:, 768:896].astype(BF16)
    vs = y[:, 896:1024]
    kw_ref[...] = y[:, 1024:1152].astype(BF16)
    vw = y[:, 1152:1280]
    gl = y[:, 1280:1408]
    g_ref[...] = 1.0 / (1.0 + jnp.exp(-gl))
    qd_ref[...] = (y[:, 1408:1920] * scale).astype(BF16)
    kd = y[:, 1920:2432]
    kd_ref[...] = kd.astype(BF16)
    vd = y[:, 2432:2944]
    for j in range(tm // BLK):
        rows = slice(j * BLK, (j + 1) * BLK)
        km_ref[j] = jnp.mean(kd[rows], axis=0, keepdims=True)
        _store_vt_ones(vdt_ref, j, vd[rows], 8)
        _store_vt_ones(vst_ref, j, vs[rows], 2)
        _store_vt_ones(vwt_ref, j, vw[rows], 2)


def _proj1(x2, batch, seq, pre, w_in):
    m, d = x2.shape
    tm = TOKEN_TILE
    n_st = seq // tm
    pad = jnp.zeros((d, LANES - 24), F32)
    w1 = jnp.concatenate([w_in[:, :1304], pad, w_in[:, 1304:]], axis=1).astype(BF16)
    tok = lambda n: pl.BlockSpec((tm, n), lambda i: (i, 0))
    hm = pl.BlockSpec((1, 2, tm, HEAD_DIM), lambda i: (i // n_st, 0, i % n_st, 0))
    nblk = tm // BLK
    vt_spec = lambda rows: pl.BlockSpec((1, nblk, rows, BLK), lambda i: (i // n_st, i % n_st, 0, 0))
    vt_shape = lambda rows: jax.ShapeDtypeStruct((batch, seq // BLK, rows, BLK), BF16)
    return pl.pallas_call(
        functools.partial(_proj1_body, tm=tm),
        grid=(m // tm,),
        in_specs=[tok(d), _const_spec((1, d)), _const_spec(w1.shape)],
        out_specs=[tok(512), hm, hm, tok(128), vt_spec(256), tok(128), vt_spec(256), tok(128), tok(512), tok(512),
                   vt_spec(1024),
                   pl.BlockSpec((nblk, 1, 512), lambda i: (i, 0, 0))],
        out_shape=[
            jax.ShapeDtypeStruct((m, 512), BF16),
            jax.ShapeDtypeStruct((batch, 2, seq, HEAD_DIM), BF16),
            jax.ShapeDtypeStruct((batch, 2, seq, HEAD_DIM), BF16),
            jax.ShapeDtypeStruct((m, 128), BF16), vt_shape(256),
            jax.ShapeDtypeStruct((m, 128), BF16), vt_shape(256),
            jax.ShapeDtypeStruct((m, 128), F32),
            jax.ShapeDtypeStruct((m, 512), BF16), jax.ShapeDtypeStruct((m, 512), BF16),
            jax.ShapeDtypeStruct((batch, seq // BLK, 1024, BLK), BF16),
            jax.ShapeDtypeStruct((m // MOBA_BLOCK, 1, 512), F32),
        ],
        compiler_params=_cparams(1),
        name="proj1",
    )(x2, pre.reshape(1, d), w1)


def _compress_body(kx_ref, vx_ref, pek_ref, pev_ref, wk1_ref, wk2_ref, wv1_ref, wv2_ref, ko_ref, vo_ref):
    half = CMP_STRIDE * HEAD_DIM
    for x_ref, pe_ref, w1_ref, w2_ref, o_ref in ((kx_ref, pek_ref, wk1_ref, wk2_ref, ko_ref),
                                                 (vx_ref, pev_ref, wv1_ref, wv2_ref, vo_ref)):
        pe8 = jnp.broadcast_to(pe_ref[...], (8, 2 * half)).astype(BF16)
        pe_term = _dot(pe8, w1_ref[...])[0:1]
        outs = []
        for hkv in range(2):
            x = x_ref[0, hkv]
            lo = _dot(x, w1_ref[0:half])
            hi = _dot(x, w1_ref[half:2 * half])
            hid = lo + pltpu.roll(hi, x.shape[0] - 1, 0) + pe_term
            outs.append(_dot(_silu(hid).astype(BF16), w2_ref[...]))
        out = jnp.concatenate(outs, axis=1)
        o_ref[0] = (out.T if o_ref is vo_ref else out).astype(BF16)


def _compress(kc_hm, vc_hm, pe_k, pe_v, wk1, wk2, wv1, wv2):
    b, _, s, dh = kc_hm.shape
    rows = s // CMP_STRIDE
    feat = CMP_STRIDE * dh
    kx = kc_hm.reshape(b, 2, rows, feat)
    vx = vc_hm.reshape(b, 2, rows, feat)
    xspec = pl.BlockSpec((1, 2, rows, feat), lambda bi: (bi, 0, 0, 0))
    ospec = pl.BlockSpec((1, rows, 2 * dh), lambda bi: (bi, 0, 0))
    w1s, w2s = wk1.shape, wk2.shape
    return pl.pallas_call(
        _compress_body,
        grid=(b,),
        in_specs=[xspec, xspec, _const_spec((1, 2 * feat)), _const_spec((1, 2 * feat)),
                  _const_spec(w1s), _const_spec(w2s), _const_spec(w1s), _const_spec(w2s)],
        out_specs=[ospec, ospec],
        out_shape=[jax.ShapeDtypeStruct((b, rows, 2 * dh), BF16)] * 2,
        compiler_params=_cparams(1),
        name="nsa_compress",
    )(kx, vx, pe_k.astype(F32).reshape(1, -1), pe_v.astype(F32).reshape(1, -1),
      wk1.astype(BF16), wk2.astype(BF16), wv1.astype(BF16), wv2.astype(BF16))


def _bias_block(bias_ref, hd, dist_blocks):
    sub = BLK // TILE
    rows = []
    for a in range(sub):
        cols = []
        for bq in range(sub):
            e = sub * dist_blocks + bq - a
            e = max(e, 0) if isinstance(e, int) else jnp.clip(e, 0, BIAS_TILES - 1)
            cols.append(bias_ref[hd, e])
        rows.append(jnp.concatenate(cols, axis=1))
    return jnp.concatenate(rows, axis=0)


def _rank_rows(val, ridx, n_real):
    rank = jnp.zeros(val.shape, F32)
    for j in range(n_real):
        vj = val[j:j + 1, :]
        rank = rank + jnp.where((vj > val) | ((vj == val) & (ridx > j)), 1.0, 0.0)
    return rank


def _nsa_body(q_ref, kc_ref, vct_ref, ks_ref, vst_ref, kw_ref, vwt_ref, g_ref, bias_ref, ovt_ref, o_ref,
              pen_ref, acc_ref, *, n_sel):
    qi = pl.program_id(1)
    grp = 4
    lanes = grp * BLK
    sub = BLK // SEL_BLOCK
    own = pl.multiple_of(qi * BLK, BLK)
    prev = pl.multiple_of(jnp.maximum(qi - 1, 0) * BLK, BLK)
    prev_blk = jnp.maximum(qi - 1, 0)
    krow = lax.broadcasted_iota(jnp.int32, (BLK, lanes), 0)
    qcol = lax.broadcasted_iota(jnp.int32, (BLK, lanes), 1) & (BLK - 1)
    causal = krow <= qcol
    crow = lax.broadcasted_iota(jnp.int32, (TILE, lanes), 0)
    cq = lax.broadcasted_iota(jnp.int32, (TILE, lanes), 1) & (BLK - 1)
    visible = CMP_STRIDE * crow + (CMP_BLOCK - 1) <= qi * BLK + cq
    jrow = lax.broadcasted_iota(jnp.int32, (n_sel, BLK), 0)
    cur = qi * sub + (lax.broadcasted_iota(jnp.int32, (n_sel, BLK), 1) >> 6)
    forced = (jrow == 0) | (jrow == cur) | (jrow == cur - 1)
    gates_t = g_ref[0].T
    has_prev = jnp.where(qi >= 1, 0, BLK)

    def stack_q(hkv):
        return jnp.concatenate(
            [q_ref[0, :, (hkv * grp + g) * HEAD_DIM:(hkv * grp + g + 1) * HEAD_DIM] for g in range(grp)], axis=0)

    def group_bias(hkv, dist_blocks):
        return jnp.concatenate([_bias_block(bias_ref, hkv * grp + g, dist_blocks) for g in range(grp)], axis=1)

    def sel_block(hkv, n, dist_blocks):
        st = pl.multiple_of(n * BLK, BLK)
        sl = slice(hkv * HEAD_DIM, (hkv + 1) * HEAD_DIM)
        s = _dot_nt(ks_ref[0, pl.ds(st, BLK), sl], stack_q(hkv)) + group_bias(hkv, dist_blocks)
        pens = pen_ref[hkv, n, 0:sub, :]
        return s, [jnp.concatenate([pens[j:j + 1]] * grp, axis=1) for j in range(sub)]

    def sel_update(s, pens, m_old):
        parts = [s[j * SEL_BLOCK:(j + 1) * SEL_BLOCK] for j in range(sub)]
        m_new = m_old
        for j in range(sub):
            mj = jnp.max(parts[j], axis=0, keepdims=True) + pens[j]
            m_new = mj if m_new is None else jnp.maximum(m_new, mj)
        p = jnp.concatenate([jnp.exp(parts[j] - (m_new - pens[j])) for j in range(sub)], axis=0)
        return m_new, p.astype(BF16)

    o_cmp, o_win, ms = [], [], []
    for hkv in range(2):
        sl = slice(hkv * HEAD_DIM, (hkv + 1) * HEAD_DIM)
        vsl = slice(hkv * 2 * HEAD_DIM, (hkv + 1) * 2 * HEAD_DIM)
        qs = stack_q(hkv)

        sc = jnp.where(visible, _dot_nt(kc_ref[0, :, sl], qs), NEG_INF)
        mc = jnp.max(sc, axis=0, keepdims=True)
        ec = jnp.where(visible, jnp.exp(sc - mc), 0.0)
        lc = jnp.sum(ec, axis=0, keepdims=True)
        p = ec / jnp.where(lc > 0.0, lc, 1.0)
        o_cmp.append(_dot(vct_ref[0, sl, :], p.astype(BF16)))

        ps = p[:, 0:BLK] + p[:, BLK:2 * BLK] + p[:, 2 * BLK:3 * BLK] + p[:, 3 * BLK:4 * BLK]
        ps_hi = ps.astype(BF16)
        ps_lo = (ps - ps_hi.astype(F32)).astype(BF16)
        imp = _dot(ovt_ref[...], ps_hi) + _dot(ovt_ref[...], ps_lo)
        val = jnp.where(forced, FORCE_SCORE, jnp.where(jrow <= cur, imp, NEG_INF))
        pen = jnp.where(_rank_rows(val, jrow, n_sel) < SEL_TOPK, 0.0, NEG_INF)
        for t in range(n_sel // sub):
            pen_ref[hkv, t, 0:sub, :] = pen[t * sub:(t + 1) * sub]

        s, pens = sel_block(hkv, qi, 0)
        m, pb = sel_update(jnp.where(causal, s, NEG_INF), pens, None)
        acc_ref[hkv] = _dot(vst_ref[0, qi, vsl, :], pb)
        ms.append(m)

        s0 = _dot_nt(kw_ref[0, pl.ds(own, BLK), sl], qs) + group_bias(hkv, 0)
        s1 = _dot_nt(kw_ref[0, pl.ds(prev, BLK), sl], qs) + group_bias(hkv, 1)
        s0 = jnp.where(causal, s0, NEG_INF)
        s1 = jnp.where(krow > qcol + has_prev, s1, NEG_INF)
        mw = jnp.maximum(jnp.max(s0, axis=0, keepdims=True), jnp.max(s1, axis=0, keepdims=True))
        ow = (_dot(vwt_ref[0, qi, vsl, :], jnp.exp(s0 - mw).astype(BF16))
              + _dot(vwt_ref[0, prev_blk, vsl, :], jnp.exp(s1 - mw).astype(BF16)))
        o_win.append(_finish(ow, HEAD_DIM))

    def body(n, ms):
        new = []
        for hkv in range(2):
            vsl = slice(hkv * 2 * HEAD_DIM, (hkv + 1) * 2 * HEAD_DIM)
            s, pens = sel_block(hkv, n, qi - n)
            m_new, pb = sel_update(s, pens, ms[hkv])
            acc_ref[hkv] = jnp.exp(ms[hkv] - m_new) * acc_ref[hkv] + _dot(vst_ref[0, n, vsl, :], pb)
            new.append(m_new)
        return tuple(new)

    lax.fori_loop(0, qi, body, tuple(ms))
    outs = []
    for hkv in range(2):
        o_slc = _finish(acc_ref[hkv], HEAD_DIM)
        for g in range(grp):
            hd = hkv * grp + g
            ls = slice(g * BLK, (g + 1) * BLK)
            outs.append(gates_t[hd:hd + 1] * o_cmp[hkv][:, ls] + gates_t[8 + hd:9 + hd] * o_slc[:, ls]
                        + gates_t[16 + hd:17 + hd] * o_win[hkv][:, ls])
    o_ref[0] = jnp.concatenate(outs, axis=0).T.astype(BF16)


def _nsa(qc, kcmp, vcmpt, ks, vst, kw, vwt, gates, bias_c):
    b, s, width = qc.shape
    n_blocks = s // BLK
    n_sel = s // SEL_BLOCK
    n_c = (s - CMP_BLOCK) // CMP_STRIDE + 1
    j_start = np.arange(n_sel)[:, None] * SEL_BLOCK
    c_start = np.arange(TILE)[None, :] * CMP_STRIDE
    overlap_t = ((c_start < j_start + SEL_BLOCK) & (c_start + CMP_BLOCK > j_start)
                 & (np.arange(TILE)[None, :] < n_c))
    ovt = jnp.asarray(overlap_t, BF16)
    qtile = lambda n: pl.BlockSpec((1, BLK, n), lambda bi, qi: (bi, qi, 0))
    full = pl.BlockSpec((1, s, 128), lambda bi, qi: (bi, 0, 0))
    full_t = pl.BlockSpec((1, n_blocks, 256, BLK), lambda bi, qi: (bi, 0, 0, 0))
    cmp_spec = pl.BlockSpec((1, TILE, 128), lambda bi, qi: (bi, 0, 0))
    return pl.pallas_call(
        functools.partial(_nsa_body, n_sel=n_sel),
        grid=(b, n_blocks),
        in_specs=[qtile(width), cmp_spec, cmp_spec, full, full_t, full, full_t, qtile(128),
                  _const_spec(bias_c.shape), _const_spec(ovt.shape)],
        out_specs=qtile(width),
        out_shape=jax.ShapeDtypeStruct((b, s, width), BF16),
        scratch_shapes=[pltpu.VMEM((2, n_blocks, 8, BLK), F32), pltpu.VMEM((2, 2 * HEAD_DIM, 4 * BLK), F32)],
        compiler_params=_cparams(2),
        name="nsa",
    )(qc, kcmp, vcmpt, ks, vst, kw, vwt, gates, bias_c, ovt)


def _moba_body(q_ref, k_ref, vt_ref, km_ref, bias_ref, o_ref, pen_ref, acc_ref, *, n_heads, n_blocks):
    qi = pl.program_id(1)
    lanes = n_heads * BLK
    nidx = lax.broadcasted_iota(jnp.int32, (16, lanes), 0)
    krow = lax.broadcasted_iota(jnp.int32, (BLK, lanes), 0)
    qcol = lax.broadcasted_iota(jnp.int32, (BLK, lanes), 1) & (BLK - 1)

    def scores(n, dist_blocks):
        st = pl.multiple_of(n * BLK, BLK)
        parts = []
        for hd in range(n_heads):
            sl = slice(hd * HEAD_DIM, (hd + 1) * HEAD_DIM)
            parts.append(_dot_nt(k_ref[0, pl.ds(st, BLK), sl], q_ref[0, :, sl])
                         + _bias_block(bias_ref, hd, dist_blocks))
        return jnp.concatenate(parts, axis=1)

    def weighted_values(n, p):
        return jnp.concatenate(
            [_dot(vt_ref[0, n, hd * 2 * HEAD_DIM:(hd + 1) * 2 * HEAD_DIM, :], p[:, hd * BLK:(hd + 1) * BLK])
             for hd in range(n_heads)], axis=1)

    zpad = jnp.zeros((16 - n_blocks, HEAD_DIM), F32)
    gate = jnp.concatenate(
        [_dot_nt(jnp.concatenate([km_ref[0, :, hd * HEAD_DIM:(hd + 1) * HEAD_DIM], zpad], axis=0).astype(BF16),
                 q_ref[0, :, hd * HEAD_DIM:(hd + 1) * HEAD_DIM]) for hd in range(n_heads)], axis=1)
    val = jnp.where(nidx < qi, gate, NEG_INF)
    val = jnp.where(nidx < n_blocks, val, PAD_SCORE)
    keep = (_rank_rows(val, nidx, n_blocks) < MOBA_TOPK) & (nidx < qi)
    pen_ref[...] = jnp.where(keep, 0.0, NEG_INF)

    s = jnp.where(krow <= qcol, scores(qi, 0), NEG_INF)
    m0 = jnp.max(s, axis=0, keepdims=True)
    acc_ref[...] = weighted_values(qi, jnp.exp(s - m0).astype(BF16))

    def body(n, m):
        s = scores(n, qi - n)
        pen = pen_ref[pl.ds(n, 1), :]
        m_new = jnp.maximum(m, jnp.max(s, axis=0, keepdims=True) + pen)
        p = jnp.exp(s - (m_new - pen)).astype(BF16)
        acc_ref[...] = jnp.exp(m - m_new) * acc_ref[...] + weighted_values(n, p)
        return m_new

    lax.fori_loop(0, qi, body, m0)
    o_t = _finish(acc_ref[...], HEAD_DIM)
    o_ref[0] = jnp.concatenate([o_t[:, hd * BLK:(hd + 1) * BLK] for hd in range(n_heads)],
                               axis=0).T.astype(BF16)


def _moba(qd, kd, vdt, kmean, bias_d):
    b, s, width = qd.shape
    n_heads = width // HEAD_DIM
    n_blocks = s // BLK
    return pl.pallas_call(
        functools.partial(_moba_body, n_heads=n_heads, n_blocks=n_blocks),
        grid=(b, n_blocks),
        in_specs=[
            pl.BlockSpec((1, BLK, width), lambda bi, qi: (bi, qi, 0)),
            pl.BlockSpec((1, s, width), lambda bi, qi: (bi, 0, 0)),
            pl.BlockSpec((1, n_blocks, 2 * width, BLK), lambda bi, qi: (bi, 0, 0, 0)),
            pl.BlockSpec((1, n_blocks, width), lambda bi, qi: (bi, 0, 0)),
            _const_spec(bias_d.shape),
        ],
        out_specs=pl.BlockSpec((1, BLK, width), lambda bi, qi: (bi, qi, 0)),
        out_shape=jax.ShapeDtypeStruct((b, s, width), BF16),
        scratch_shapes=[pltpu.VMEM((16, n_heads * BLK), F32), pltpu.VMEM((2 * HEAD_DIM, n_heads * BLK), F32)],
        compiler_params=_cparams(2),
        name="moba",
    )(qd, kd, vdt, kmean.reshape(b, n_blocks, width), bias_d)


def kernel(x, rel_bias_table, l0_ffn1_pre, l0_ffn1_post, l0_ffn1_wg, l0_ffn1_wu, l0_ffn1_wd, l0_mix_pre, l0_mix_post, l0_w_in, l0_sinks, l0_mla_q_norm, l0_mla_w_uq, l0_mla_kv_norm, l0_mla_w_ukv, l0_w_out, l0_ffn2_pre, l0_ffn2_post, l0_ffn2_wg, l0_ffn2_wu, l0_ffn2_wd, l1_ffn1_pre, l1_ffn1_post, l1_ffn1_wg, l1_ffn1_wu, l1_ffn1_wd, l1_mix_pre, l1_mix_post, l1_w_in, l1_nsa_pe_k, l1_nsa_pe_v, l1_nsa_wk1, l1_nsa_wk2, l1_nsa_wv1, l1_nsa_wv2, l1_w_out, l1_ffn2_pre, l1_ffn2_post, l1_ffn2_wg, l1_ffn2_wu, l1_ffn2_wd):
    b, s, d = x.shape
    m = b * s
    x2 = x.reshape(m, d)
    bias_a = _stack_group_bias(_bias_tiles(rel_bias_table, A_SLOT, 8, False), 2)
    bias_c = _bias_tiles(rel_bias_table, C_SLOT, 8, True)
    bias_d = _bias_tiles(rel_bias_table, D_SLOT, 8, True)

    x2 = _ffn(x2, l0_ffn1_pre, l0_ffn1_post, l0_ffn1_wg, l0_ffn1_wu, l0_ffn1_wd, 0.5)
    qa, ka, va, qb, kb, vbt = _proj0(x2, s, l0_mix_pre, l0_w_in, l0_mla_q_norm, l0_mla_w_uq,
                                     l0_mla_kv_norm, l0_mla_w_ukv)
    shp = lambda t: t.reshape(b, s, t.shape[-1])
    o_a = _swa(shp(qa), shp(ka), shp(va), bias_a, l0_sinks)
    o_b = _mla(shp(qb), shp(kb), vbt)
    x2 = _outproj(x2, o_a.reshape(m, -1), o_b.reshape(m, -1), l0_w_out, l0_mix_post)
    x2 = _ffn(x2, l0_ffn2_pre, l0_ffn2_post, l0_ffn2_wg, l0_ffn2_wu, l0_ffn2_wd, 0.5)

    x2 = _ffn(x2, l1_ffn1_pre, l1_ffn1_post, l1_ffn1_wg, l1_ffn1_wu, l1_ffn1_wd, 0.5)
    qc, kc_hm, vc_hm, ks, vst, kw, vwt, gates, qd, kd, vdt, kmean = _proj1(x2, b, s, l1_mix_pre, l1_w_in)
    kcmp, vcmpt = _compress(kc_hm, vc_hm, l1_nsa_pe_k, l1_nsa_pe_v, l1_nsa_wk1, l1_nsa_wk2, l1_nsa_wv1, l1_nsa_wv2)
    o_c = _nsa(shp(qc), kcmp, vcmpt, shp(ks), vst, shp(kw), vwt, shp(gates), bias_c)
    o_d = _moba(shp(qd), shp(kd), vdt, kmean, bias_d)
    x2 = _outproj(x2, o_c.reshape(m, -1), o_d.reshape(m, -1), l1_w_out, l1_mix_post)
    x2 = _ffn(x2, l1_ffn2_pre, l1_ffn2_post, l1_ffn2_wg, l1_ffn2_wu, l1_ffn2_wd, 0.5)
    return x2.reshape(b, s, d)
```

```python
import functools
import math

import numpy as np
import jax
import jax.numpy as jnp
from jax import lax
from jax.experimental import pallas as pl
from jax.experimental.pallas import tpu as pltpu

F32 = jnp.float32
BF16 = jnp.bfloat16

D_MODEL = 1024
HEAD_DIM = 64
D_FF = 2816
NORM_EPS = 1e-6
NEG_INF = -1e30
PAD_SCORE = -3e38
FORCE_SCORE = 1e9
REL_BUCKETS = 32
REL_MAX_DIST = 1024
REL_SLOTS = 24
A_SLOT, C_SLOT, D_SLOT = 0, 8, 16
MLA_NOPE, MLA_ROPE, MLA_V = 64, 32, 64
MLA_HEADS = 8
ROPE_THETA = 10000.0
CMP_BLOCK, CMP_STRIDE = 32, 16
SEL_BLOCK, SEL_TOPK = 64, 8
MOBA_BLOCK, MOBA_TOPK = 256, 3

LANES = 128
TILE = 128
BLK = 256
BIAS_TILES = 9
TOKEN_TILE = 512
FF_CHUNK = 256
VMEM_LIMIT = 56 * 1024 * 1024


def _cparams(n_axes):
    return pltpu.CompilerParams(dimension_semantics=("arbitrary",) * n_axes, vmem_limit_bytes=VMEM_LIMIT)


def _dot(a, b):
    return jnp.dot(a, b, preferred_element_type=F32)


def _dot_nt(a, b):
    return lax.dot_general(a, b, (((1,), (1,)), ((), ())), preferred_element_type=F32)


def _rms(x, g):
    return x * lax.rsqrt(jnp.mean(x * x, axis=-1, keepdims=True) + NORM_EPS) * g


def _silu(x):
    return x / (1.0 + jnp.exp(-x))


def _const_spec(shape):
    nd = len(shape)
    return pl.BlockSpec(shape, lambda *_: (0,) * nd)


def _ffn_body(x_ref, pre_ref, post_ref, wg_ref, wu_ref, wd_ref, o_ref, acc_ref, *, n_chunks, weight):
    x = x_ref[...]
    h = _rms(x, pre_ref[...]).astype(BF16)
    for c in range(n_chunks):
        cols = slice(c * FF_CHUNK, (c + 1) * FF_CHUNK)
        g = _dot(h, wg_ref[:, cols])
        u = _dot(h, wu_ref[:, cols])
        a = (_silu(g) * u).astype(BF16)
        y = _dot(a, wd_ref[cols, :])
        if c == 0:
            acc_ref[...] = y
        else:
            acc_ref[...] += y
    o_ref[...] = x + weight * _rms(acc_ref[...], post_ref[...])


def _ffn(x2, pre, post, wg, wu, wd, weight):
    m, d = x2.shape
    d_ff = wg.shape[1]
    tm = TOKEN_TILE
    return pl.pallas_call(
        functools.partial(_ffn_body, n_chunks=d_ff // FF_CHUNK, weight=weight),
        grid=(m // tm,),
        in_specs=[
            pl.BlockSpec((tm, d), lambda i: (i, 0)),
            _const_spec((1, d)), _const_spec((1, d)),
            _const_spec((d, d_ff)), _const_spec((d, d_ff)), _const_spec((d_ff, d)),
        ],
        out_specs=pl.BlockSpec((tm, d), lambda i: (i, 0)),
        out_shape=jax.ShapeDtypeStruct((m, d), F32),
        scratch_shapes=[pltpu.VMEM((tm, d), F32)],
        compiler_params=_cparams(1),
        name="ffn",
    )(x2, pre.reshape(1, d), post.reshape(1, d), wg.astype(BF16), wu.astype(BF16), wd.astype(BF16))


def _outproj_body(x_ref, o1_ref, o2_ref, w1_ref, w2_ref, post_ref, o_ref):
    y = _dot(o1_ref[...], w1_ref[...]) + _dot(o2_ref[...], w2_ref[...])
    o_ref[...] = x_ref[...] + _rms(y, post_ref[...])


def _outproj(x2, o1, o2, w_out, post):
    m, d = x2.shape
    n1 = o1.shape[1]
    n2 = o2.shape[1]
    w = w_out.astype(BF16)
    tm = TOKEN_TILE
    return pl.pallas_call(
        _outproj_body,
        grid=(m // tm,),
        in_specs=[
            pl.BlockSpec((tm, d), lambda i: (i, 0)),
            pl.BlockSpec((tm, n1), lambda i: (i, 0)),
            pl.BlockSpec((tm, n2), lambda i: (i, 0)),
            _const_spec((n1, d)), _const_spec((n2, d)), _const_spec((1, d)),
        ],
        out_specs=pl.BlockSpec((tm, d), lambda i: (i, 0)),
        out_shape=jax.ShapeDtypeStruct((m, d), F32),
        compiler_params=_cparams(1),
        name="outproj",
    )(x2, o1, o2, w[:n1], w[n1:], post.reshape(1, d))


def _t5_bucket(dist):
    n = jnp.maximum(dist, 0)
    exact = REL_BUCKETS // 2
    nf = jnp.maximum(n, 1).astype(jnp.float32)
    large = exact + (jnp.log(nf / exact) / math.log(REL_MAX_DIST / exact) * (REL_BUCKETS - exact)).astype(jnp.int32)
    return jnp.where(n < exact, n, jnp.minimum(large, REL_BUCKETS - 1))


def _bias_body(tab_ref, idx_ref, o_ref):
    slot = pl.program_id(0)

    def tile(d, carry):
        idx = idx_ref[d]
        v = jnp.full((TILE, TILE), tab_ref[slot], F32)
        for b in range(1, REL_BUCKETS):
            v = jnp.where(idx == b, tab_ref[b * REL_SLOTS + slot], v)
        o_ref[0, d] = v
        return carry

    lax.fori_loop(0, BIAS_TILES, tile, 0)


def _bias_tiles(rel_table):
    n_slots = rel_table.shape[1]
    d = jnp.arange(BIAS_TILES)[:, None, None]
    key = jnp.arange(TILE)[None, :, None]
    query = jnp.arange(TILE)[None, None, :]
    idx = _t5_bucket(d * TILE + query - key).astype(jnp.int32)
    return pl.pallas_call(
        _bias_body,
        grid=(n_slots,),
        in_specs=[
            pl.BlockSpec(memory_space=pltpu.SMEM),
            _const_spec((BIAS_TILES, TILE, TILE)),
        ],
        out_specs=pl.BlockSpec((1, BIAS_TILES, TILE, TILE), lambda s: (s, 0, 0, 0)),
        out_shape=jax.ShapeDtypeStruct((n_slots, BIAS_TILES, TILE, TILE), F32),
        compiler_params=_cparams(1),
        name="bias_tiles",
    )(rel_table.astype(F32).reshape(-1), idx)


def _proj0_body(x_ref, pre_ref, w_ref, qn_ref, wq1_ref, wq2_ref, kvn_ref, wk_ref, wv_ref, cos_ref, sin_ref,
                qa_ref, ka_ref, vat_ref, qb_ref, kb_ref, vbt_ref):
    h = _rms(x_ref[...], pre_ref[...]).astype(BF16)
    y = _dot(h, w_ref[...])
    qa_ref[...] = (y[:, 0:512] * (HEAD_DIM ** -0.5)).astype(BF16)
    ka_ref[...] = y[:, 512:640].astype(BF16)
    for j in range(y.shape[0] // TILE):
        _store_vt_ones(vat_ref, j, y[j * TILE:(j + 1) * TILE, 640:768], 2)
    cq = _rms(y[:, 768:1024], qn_ref[...]).astype(BF16)
    ckv = _rms(y[:, 1024:1152], kvn_ref[...]).astype(BF16)
    cos_t = cos_ref[...]
    sin_t = sin_ref[...]
    lane = lax.broadcasted_iota(jnp.int32, cos_t.shape, 1)
    scale = (MLA_NOPE + MLA_ROPE) ** -0.5
    qmul = scale * (cos_t + jnp.where(lane < MLA_NOPE, 1.0, 0.0))
    qsin = scale * sin_t
    kr = y[:, 1152:1280] * cos_t + y[:, 1280:1408] * sin_t
    q1 = _dot(cq, wq1_ref[...])
    q2 = _dot(cq, wq2_ref[...])
    k1 = _dot(ckv, wk_ref[...])
    for hd in range(MLA_HEADS):
        sl = slice(hd * LANES, (hd + 1) * LANES)
        qb_ref[:, sl] = (q1[:, sl] * qmul + q2[:, sl] * qsin).astype(BF16)
        kb_ref[:, sl] = (k1[:, sl] + kr).astype(BF16)
    vb = _dot(ckv, wv_ref[...])
    for j in range(vb.shape[0] // BLK):
        _store_vt_ones(vbt_ref, j, vb[j * BLK:(j + 1) * BLK], MLA_HEADS)


def _rope_lane_tables(s):
    inv = ROPE_THETA ** (-jnp.arange(0, MLA_ROPE, 2, dtype=jnp.float32) / MLA_ROPE)
    ang = jnp.arange(s, dtype=jnp.float32)[:, None] * inv[None, :]
    cos, sin = jnp.cos(ang), jnp.sin(ang)
    z_lo = jnp.zeros((s, MLA_NOPE), F32)
    z_hi = jnp.zeros((s, LANES - MLA_NOPE - MLA_ROPE), F32)
    return (jnp.concatenate([z_lo, cos, cos, z_hi], axis=1), jnp.concatenate([z_lo, sin, sin, z_hi], axis=1))


def _rot_cols(w):
    half = w.shape[-1] // 2
    return jnp.concatenate([-w[..., half:], w[..., :half]], axis=-1)


def _proj0(x2, seq, pre, w_in, q_norm, w_uq, kv_norm, w_ukv):
    m, d = x2.shape
    tm = TOKEN_TILE
    kr_w = w_in[:, 1152:1184]
    z64 = jnp.zeros((d, 64), F32)
    z32 = jnp.zeros((d, 32), F32)
    w0 = jnp.concatenate([w_in[:, :1152], z64, kr_w, z32, z64, _rot_cols(kr_w), z32], axis=1).astype(BF16)
    wq = w_uq.reshape(-1, MLA_HEADS, MLA_NOPE + MLA_ROPE)
    rq = wq.shape[0]
    zq = jnp.zeros((rq, MLA_HEADS, 32), F32)
    wq1 = jnp.concatenate([wq, zq], axis=2).reshape(rq, -1).astype(BF16)
    wq2 = jnp.concatenate([jnp.zeros((rq, MLA_HEADS, 64), F32), _rot_cols(wq[:, :, MLA_NOPE:]), zq],
                          axis=2).reshape(rq, -1).astype(BF16)
    wkv = w_ukv.reshape(-1, MLA_HEADS, MLA_NOPE + MLA_V)
    rk = wkv.shape[0]
    wk = jnp.concatenate([wkv[:, :, :MLA_NOPE], jnp.zeros((rk, MLA_HEADS, 64), F32)], axis=2).reshape(rk, -1).astype(BF16)
    wv = wkv[:, :, MLA_NOPE:].reshape(rk, -1).astype(BF16)
    cos_t, sin_t = _rope_lane_tables(seq)
    n_st = seq // tm
    tok = lambda n: pl.BlockSpec((tm, n), lambda i: (i, 0))
    outs = pl.pallas_call(
        _proj0_body,
        grid=(m // tm,),
        in_specs=[
            tok(d), _const_spec((1, d)), _const_spec(w0.shape),
            _const_spec((1, rq)), _const_spec(wq1.shape), _const_spec(wq2.shape),
            _const_spec((1, rk)), _const_spec(wk.shape), _const_spec(wv.shape),
            pl.BlockSpec((tm, LANES), lambda i: (i % n_st, 0)),
            pl.BlockSpec((tm, LANES), lambda i: (i % n_st, 0)),
        ],
        out_specs=[tok(512), tok(128),
                   pl.BlockSpec((1, tm // TILE, 4 * HEAD_DIM, TILE), lambda i: (i // n_st, i % n_st, 0, 0)),
                   tok(1024), tok(1024),
                   pl.BlockSpec((1, tm // BLK, 2 * MLA_HEADS * MLA_V, BLK), lambda i: (i // n_st, i % n_st, 0, 0))],
        out_shape=[jax.ShapeDtypeStruct((m, 512), BF16), jax.ShapeDtypeStruct((m, 128), BF16),
                   jax.ShapeDtypeStruct((m // seq, seq // TILE, 4 * HEAD_DIM, TILE), BF16),
                   jax.ShapeDtypeStruct((m, 1024), BF16), jax.ShapeDtypeStruct((m, 1024), BF16),
                   jax.ShapeDtypeStruct((m // seq, seq // BLK, 2 * MLA_HEADS * MLA_V, BLK), BF16)],
        compiler_params=_cparams(1),
        name="proj0",
    )(x2, pre.reshape(1, d), w0, q_norm.reshape(1, rq), wq1, wq2, kv_norm.reshape(1, rk), wk, wv, cos_t, sin_t)
    return outs


def _swa_body(sink_ref, q_ref, k_ref, vt_ref, bias_ref, o_ref, *, n_kv, grp):
    qi = pl.program_id(1)
    halves = BLK // TILE
    items = [(hkv, half) for hkv in range(n_kv) for half in range(halves)]
    width = grp * TILE
    krow = lax.broadcasted_iota(jnp.int32, (TILE, len(items) * width), 0)
    qcol = lax.broadcasted_iota(jnp.int32, (TILE, len(items) * width), 1) & (TILE - 1)

    cur_s, prev_s, sink_parts, has_prev = [], [], [], []
    for hkv, half in items:
        sl = slice(hkv * HEAD_DIM, (hkv + 1) * HEAD_DIM)
        t = qi * halves + half
        tp = jnp.maximum(t - 1, 0)
        qs = jnp.concatenate([q_ref[0, half * TILE:(half + 1) * TILE, (hkv * grp + g) * HEAD_DIM:
                                    (hkv * grp + g + 1) * HEAD_DIM] for g in range(grp)], axis=0)
        bias_c = jnp.concatenate([bias_ref[hkv * grp + g, 0] for g in range(grp)], axis=1)
        bias_p = jnp.concatenate([bias_ref[hkv * grp + g, 1] for g in range(grp)], axis=1)
        cur_s.append(_dot_nt(k_ref[0, pl.ds(pl.multiple_of(t * TILE, TILE), TILE), sl], qs) + bias_c)
        prev_s.append(_dot_nt(k_ref[0, pl.ds(pl.multiple_of(tp * TILE, TILE), TILE), sl], qs) + bias_p)
        sink_parts.extend(jnp.full((1, TILE), sink_ref[hkv * grp + g], F32) for g in range(grp))
        has_prev.append(jnp.full((1, width), jnp.where(t >= 1, 0, TILE), jnp.int32))
    s_c = jnp.where(krow <= qcol, jnp.concatenate(cur_s, axis=1), NEG_INF)
    s_p = jnp.where(krow > qcol + jnp.concatenate(has_prev, axis=1),
                    jnp.concatenate(prev_s, axis=1), NEG_INF)
    sink = jnp.concatenate(sink_parts, axis=1)
    mx = jnp.maximum(jnp.maximum(jnp.max(s_c, axis=0, keepdims=True), jnp.max(s_p, axis=0, keepdims=True)), sink)
    p_c = jnp.exp(s_c - mx).astype(BF16)
    p_p = jnp.exp(s_p - mx).astype(BF16)
    accs = []
    for i, (hkv, half) in enumerate(items):
        vsl = slice(hkv * 2 * HEAD_DIM, (hkv + 1) * 2 * HEAD_DIM)
        ls = slice(i * width, (i + 1) * width)
        t = qi * halves + half
        accs.append(_dot(vt_ref[0, t, vsl, :], p_c[:, ls]) + _dot(vt_ref[0, jnp.maximum(t - 1, 0), vsl, :], p_p[:, ls]))
    acc = jnp.concatenate(accs, axis=1)
    o_t = acc[0:HEAD_DIM] / (acc[HEAD_DIM:HEAD_DIM + 1] + jnp.exp(sink - mx))
    heads = []
    for hkv in range(n_kv):
        for g in range(grp):
            heads.append(jnp.concatenate(
                [o_t[:, (hkv * halves + half) * width + g * TILE:(hkv * halves + half) * width + (g + 1) * TILE]
                 for half in range(halves)], axis=1))
    o_ref[0] = jnp.concatenate(heads, axis=0).T.astype(BF16)


def _swa(qa, ka, vat, bias_a, sinks):
    b, s, width = qa.shape
    n_kv = ka.shape[-1] // HEAD_DIM
    grp = width // HEAD_DIM // n_kv
    return pl.pallas_call(
        functools.partial(_swa_body, n_kv=n_kv, grp=grp),
        grid=(b, s // BLK),
        in_specs=[
            pl.BlockSpec(memory_space=pltpu.SMEM),
            pl.BlockSpec((1, BLK, width), lambda bi, qi: (bi, qi, 0)),
            pl.BlockSpec((1, s, n_kv * HEAD_DIM), lambda bi, qi: (bi, 0, 0)),
            pl.BlockSpec((1, s // TILE, n_kv * 2 * HEAD_DIM, TILE), lambda bi, qi: (bi, 0, 0, 0)),
            _const_spec(bias_a.shape),
        ],
        out_specs=pl.BlockSpec((1, BLK, width), lambda bi, qi: (bi, qi, 0)),
        out_shape=jax.ShapeDtypeStruct((b, s, width), BF16),
        compiler_params=_cparams(2),
        name="swa",
    )(sinks.astype(F32), qa, ka, vat, bias_a)


def _finish(acc, dv):
    return acc[0:dv] / acc[dv:dv + 1]


def _mla_body(q_ref, k_ref, vt_ref, o_ref, acc_ref, *, n_heads):
    qi = pl.program_id(1)
    lanes = n_heads * BLK
    krow = lax.broadcasted_iota(jnp.int32, (BLK, lanes), 0)
    qcol = lax.broadcasted_iota(jnp.int32, (BLK, lanes), 1) & (BLK - 1)

    def scores(n):
        st = pl.multiple_of(n * BLK, BLK)
        return jnp.concatenate(
            [_dot_nt(k_ref[0, pl.ds(st, BLK), hd * LANES:(hd + 1) * LANES], q_ref[0, :, hd * LANES:(hd + 1) * LANES])
             for hd in range(n_heads)], axis=1)

    def weighted_values(n, p):
        return jnp.concatenate(
            [_dot(vt_ref[0, n, hd * 2 * MLA_V:(hd + 1) * 2 * MLA_V, :], p[:, hd * BLK:(hd + 1) * BLK])
             for hd in range(n_heads)], axis=1)

    s = jnp.where(krow <= qcol, scores(qi), NEG_INF)
    m0 = jnp.max(s, axis=0, keepdims=True)
    acc_ref[...] = weighted_values(qi, jnp.exp(s - m0).astype(BF16))

    def body(n, m):
        s = scores(n)
        m_new = jnp.maximum(m, jnp.max(s, axis=0, keepdims=True))
        acc_ref[...] = jnp.exp(m - m_new) * acc_ref[...] + weighted_values(n, jnp.exp(s - m_new).astype(BF16))
        return m_new

    lax.fori_loop(0, qi, body, m0)
    o_t = _finish(acc_ref[...], MLA_V)
    o_ref[0] = jnp.concatenate([o_t[:, hd * BLK:(hd + 1) * BLK] for hd in range(n_heads)],
                               axis=0).T.astype(BF16)


def _mla(qb, kb, vbt):
    b, s, width = qb.shape
    n_heads = width // LANES
    n_blocks = s // BLK
    return pl.pallas_call(
        functools.partial(_mla_body, n_heads=n_heads),
        grid=(b, n_blocks),
        in_specs=[
            pl.BlockSpec((1, BLK, width), lambda bi, qi: (bi, qi, 0)),
            pl.BlockSpec((1, s, width), lambda bi, qi: (bi, 0, 0)),
            pl.BlockSpec((1, n_blocks, n_heads * 2 * MLA_V, BLK), lambda bi, qi: (bi, 0, 0, 0)),
        ],
        out_specs=pl.BlockSpec((1, BLK, n_heads * MLA_V), lambda bi, qi: (bi, qi, 0)),
        out_shape=jax.ShapeDtypeStruct((b, s, n_heads * MLA_V), BF16),
        scratch_shapes=[pltpu.VMEM((2 * MLA_V, n_heads * BLK), F32)],
        compiler_params=_cparams(2),
        name="mla",
    )(qb, kb, vbt)


def _store_vt_ones(vt_ref, j, v, n_heads):
    dv = v.shape[1] // n_heads
    vt = v.T.astype(BF16)
    ones = jnp.ones((dv, v.shape[0]), BF16)
    for hd in range(n_heads):
        vt_ref[0, j, 2 * hd * dv:(2 * hd + 1) * dv, :] = vt[hd * dv:(hd + 1) * dv]
        vt_ref[0, j, (2 * hd + 1) * dv:(2 * hd + 2) * dv, :] = ones


def _proj1_body(x_ref, pre_ref, w_ref, qc_ref, kc_ref, vc_ref, ks_ref, vst_ref, kw_ref, vwt_ref, g_ref,
                qd_ref, kd_ref, vdt_ref, km_ref, *, tm):
    h = _rms(x_ref[...], pre_ref[...]).astype(BF16)
    y = _dot(h, w_ref[...])
    scale = HEAD_DIM ** -0.5
    qc_ref[...] = (y[:, 0:512] * scale).astype(BF16)
    for hkv in range(2):
        kc_ref[0, hkv] = y[:, 512 + hkv * 64:576 + hkv * 64].astype(BF16)
        vc_ref[0, hkv] = y[:, 640 + hkv * 64:704 + hkv * 64].astype(BF16)
    ks_ref[...] = y[:, 768:896].astype(BF16)
    vs = y[:, 896:1024]
    kw_ref[...] = y[:, 1024:1152].astype(BF16)
    vw = y[:, 1152:1280]
    gl = y[:, 1280:1408]
    g_ref[...] = 1.0 / (1.0 + jnp.exp(-gl))
    qd_ref[...] = (y[:, 1408:1920] * scale).astype(BF16)
    kd = y[:, 1920:2432]
    kd_ref[...] = kd.astype(BF16)
    vd = y[:, 2432:2944]
    for j in range(tm // BLK):
        rows = slice(j * BLK, (j + 1) * BLK)
        km_ref[j] = jnp.mean(kd[rows], axis=0, keepdims=True)
        _store_vt_ones(vdt_ref, j, vd[rows], 8)
        _store_vt_ones(vst_ref, j, vs[rows], 2)
        _store_vt_ones(vwt_ref, j, vw[rows], 2)


def _proj1(x2, batch, seq, pre, w_in):
    m, d = x2.shape
    tm = TOKEN_TILE
    n_st = seq // tm
    pad = jnp.zeros((d, LANES - 24), F32)
    w1 = jnp.concatenate([w_in[:, :1304], pad, w_in[:, 1304:]], axis=1).astype(BF16)
    tok = lambda n: pl.BlockSpec((tm, n), lambda i: (i, 0))
    hm = pl.BlockSpec((1, 2, tm, HEAD_DIM), lambda i: (i // n_st, 0, i % n_st, 0))
    nblk = tm // BLK
    vt_spec = lambda rows: pl.BlockSpec((1, nblk, rows, BLK), lambda i: (i // n_st, i % n_st, 0, 0))
    vt_shape = lambda rows: jax.ShapeDtypeStruct((batch, seq // BLK, rows, BLK), BF16)
    return pl.pallas_call(
        functools.partial(_proj1_body, tm=tm),
        grid=(m // tm,),
        in_specs=[tok(d), _const_spec((1, d)), _const_spec(w1.shape)],
        out_specs=[tok(512), hm, hm, tok(128), vt_spec(256), tok(128), vt_spec(256), tok(128), tok(512), tok(512),
                   vt_spec(1024),
                   pl.BlockSpec((nblk, 1, 512), lambda i: (i, 0, 0))],
        out_shape=[
            jax.ShapeDtypeStruct((m, 512), BF16),
            jax.ShapeDtypeStruct((batch, 2, seq, HEAD_DIM), BF16),
            jax.ShapeDtypeStruct((batch, 2, seq, HEAD_DIM), BF16),
            jax.ShapeDtypeStruct((m, 128), BF16), vt_shape(256),
            jax.ShapeDtypeStruct((m, 128), BF16), vt_shape(256),
            jax.ShapeDtypeStruct((m, 128), F32),
            jax.ShapeDtypeStruct((m, 512), BF16), jax.ShapeDtypeStruct((m, 512), BF16),
            jax.ShapeDtypeStruct((batch, seq // BLK, 1024, BLK), BF16),
            jax.ShapeDtypeStruct((m // MOBA_BLOCK, 1, 512), F32),
        ],
        compiler_params=_cparams(1),
        name="proj1",
    )(x2, pre.reshape(1, d), w1)


def _compress_body(kx_ref, vx_ref, pek_ref, pev_ref, wk1_ref, wk2_ref, wv1_ref, wv2_ref, ko_ref, vo_ref):
    half = CMP_STRIDE * HEAD_DIM
    for x_ref, pe_ref, w1_ref, w2_ref, o_ref in ((kx_ref, pek_ref, wk1_ref, wk2_ref, ko_ref),
                                                 (vx_ref, pev_ref, wv1_ref, wv2_ref, vo_ref)):
        pe8 = jnp.broadcast_to(pe_ref[...], (8, 2 * half)).astype(BF16)
        pe_term = _dot(pe8, w1_ref[...])[0:1]
        outs = []
        for hkv in range(2):
            x = x_ref[0, hkv]
            lo = _dot(x, w1_ref[0:half])
            hi = _dot(x, w1_ref[half:2 * half])
            hid = lo + pltpu.roll(hi, x.shape[0] - 1, 0) + pe_term
            outs.append(_dot(_silu(hid).astype(BF16), w2_ref[...]))
        out = jnp.concatenate(outs, axis=1)
        o_ref[0] = (out.T if o_ref is vo_ref else out).astype(BF16)


def _compress(kc_hm, vc_hm, pe_k, pe_v, wk1, wk2, wv1, wv2):
    b, _, s, dh = kc_hm.shape
    rows = s // CMP_STRIDE
    feat = CMP_STRIDE * dh
    kx = kc_hm.reshape(b, 2, rows, feat)
    vx = vc_hm.reshape(b, 2, rows, feat)
    xspec = pl.BlockSpec((1, 2, rows, feat), lambda bi: (bi, 0, 0, 0))
    ospec = pl.BlockSpec((1, rows, 2 * dh), lambda bi: (bi, 0, 0))
    w1s, w2s = wk1.shape, wk2.shape
    return pl.pallas_call(
        _compress_body,
        grid=(b,),
        in_specs=[xspec, xspec, _const_spec((1, 2 * feat)), _const_spec((1, 2 * feat)),
                  _const_spec(w1s), _const_spec(w2s), _const_spec(w1s), _const_spec(w2s)],
        out_specs=[ospec, ospec],
        out_shape=[jax.ShapeDtypeStruct((b, rows, 2 * dh), BF16)] * 2,
        compiler_params=_cparams(1),
        name="nsa_compress",
    )(kx, vx, pe_k.astype(F32).reshape(1, -1), pe_v.astype(F32).reshape(1, -1),
      wk1.astype(BF16), wk2.astype(BF16), wv1.astype(BF16), wv2.astype(BF16))


def _bias_block(bias_ref, hd, dist_blocks):
    sub = BLK // TILE
    rows = []
    for a in range(sub):
        cols = []
        for bq in range(sub):
            e = sub * dist_blocks + bq - a
            e = max(e, 0) if isinstance(e, int) else jnp.clip(e, 0, BIAS_TILES - 1)
            cols.append(bias_ref[hd, e])
        rows.append(jnp.concatenate(cols, axis=1))
    return jnp.concatenate(rows, axis=0)


def _rank_rows(val, ridx, n_real):
    rank = jnp.zeros(val.shape, F32)
    for j in range(n_real):
        vj = val[j:j + 1, :]
        rank = rank + jnp.where((vj > val) | ((vj == val) & (ridx > j)), 1.0, 0.0)
    return rank


def _nsa_body(q_ref, kc_ref, vct_ref, ks_ref, vst_ref, kw_ref, vwt_ref, g_ref, bias_ref, ovt_ref, o_ref,
              pen_ref, acc_ref, *, n_sel):
    qi = pl.program_id(1)
    grp = 4
    lanes = grp * BLK
    sub = BLK // SEL_BLOCK
    own = pl.multiple_of(qi * BLK, BLK)
    prev = pl.multiple_of(jnp.maximum(qi - 1, 0) * BLK, BLK)
    prev_blk = jnp.maximum(qi - 1, 0)
    krow = lax.broadcasted_iota(jnp.int32, (BLK, lanes), 0)
    qcol = lax.broadcasted_iota(jnp.int32, (BLK, lanes), 1) & (BLK - 1)
    causal = krow <= qcol
    crow = lax.broadcasted_iota(jnp.int32, (TILE, lanes), 0)
    cq = lax.broadcasted_iota(jnp.int32, (TILE, lanes), 1) & (BLK - 1)
    visible = CMP_STRIDE * crow + (CMP_BLOCK - 1) <= qi * BLK + cq
    jrow = lax.broadcasted_iota(jnp.int32, (n_sel, BLK), 0)
    cur = qi * sub + (lax.broadcasted_iota(jnp.int32, (n_sel, BLK), 1) >> 6)
    forced = (jrow == 0) | (jrow == cur) | (jrow == cur - 1)
    gates_t = g_ref[0].T
    has_prev = jnp.where(qi >= 1, 0, BLK)

    def stack_q(hkv):
        return jnp.concatenate(
            [q_ref[0, :, (hkv * grp + g) * HEAD_DIM:(hkv * grp + g + 1) * HEAD_DIM] for g in range(grp)], axis=0)

    def group_bias(hkv, dist_blocks):
        return jnp.concatenate([_bias_block(bias_ref, hkv * grp + g, dist_blocks) for g in range(grp)], axis=1)

    def sel_block(hkv, n, dist_blocks):
        st = pl.multiple_of(n * BLK, BLK)
        sl = slice(hkv * HEAD_DIM, (hkv + 1) * HEAD_DIM)
        s = _dot_nt(ks_ref[0, pl.ds(st, BLK), sl], stack_q(hkv)) + group_bias(hkv, dist_blocks)
        pens = pen_ref[hkv, n, 0:sub, :]
        return s, [jnp.concatenate([pens[j:j + 1]] * grp, axis=1) for j in range(sub)]

    def sel_update(s, pens, m_old):
        parts = [s[j * SEL_BLOCK:(j + 1) * SEL_BLOCK] for j in range(sub)]
        m_new = m_old
        for j in range(sub):
            mj = jnp.max(parts[j], axis=0, keepdims=True) + pens[j]
            m_new = mj if m_new is None else jnp.maximum(m_new, mj)
        p = jnp.concatenate([jnp.exp(parts[j] - (m_new - pens[j])) for j in range(sub)], axis=0)
        return m_new, p.astype(BF16)

    o_cmp, o_win, ms = [], [], []
    for hkv in range(2):
        sl = slice(hkv * HEAD_DIM, (hkv + 1) * HEAD_DIM)
        vsl = slice(hkv * 2 * HEAD_DIM, (hkv + 1) * 2 * HEAD_DIM)
        qs = stack_q(hkv)

        sc = jnp.where(visible, _dot_nt(kc_ref[0, :, sl], qs), NEG_INF)
        mc = jnp.max(sc, axis=0, keepdims=True)
        ec = jnp.where(visible, jnp.exp(sc - mc), 0.0)
        lc = jnp.sum(ec, axis=0, keepdims=True)
        p = ec / jnp.where(lc > 0.0, lc, 1.0)
        o_cmp.append(_dot(vct_ref[0, sl, :], p.astype(BF16)))

        ps = p[:, 0:BLK] + p[:, BLK:2 * BLK] + p[:, 2 * BLK:3 * BLK] + p[:, 3 * BLK:4 * BLK]
        ps_hi = ps.astype(BF16)
        ps_lo = (ps - ps_hi.astype(F32)).astype(BF16)
        imp = _dot(ovt_ref[...], ps_hi) + _dot(ovt_ref[...], ps_lo)
        val = jnp.where(forced, FORCE_SCORE, jnp.where(jrow <= cur, imp, NEG_INF))
        pen = jnp.where(_rank_rows(val, jrow, n_sel) < SEL_TOPK, 0.0, NEG_INF)
        for t in range(n_sel // sub):
            pen_ref[hkv, t, 0:sub, :] = pen[t * sub:(t + 1) * sub]

        s, pens = sel_block(hkv, qi, 0)
        m, pb = sel_update(jnp.where(causal, s, NEG_INF), pens, None)
        acc_ref[hkv] = _dot(vst_ref[0, qi, vsl, :], pb)
        ms.append(m)

        s0 = _dot_nt(kw_ref[0, pl.ds(own, BLK), sl], qs) + group_bias(hkv, 0)
        s1 = _dot_nt(kw_ref[0, pl.ds(prev, BLK), sl], qs) + group_bias(hkv, 1)
        s0 = jnp.where(causal, s0, NEG_INF)
        s1 = jnp.where(krow > qcol + has_prev, s1, NEG_INF)
        mw = jnp.maximum(jnp.max(s0, axis=0, keepdims=True), jnp.max(s1, axis=0, keepdims=True))
        ow = (_dot(vwt_ref[0, qi, vsl, :], jnp.exp(s0 - mw).astype(BF16))
              + _dot(vwt_ref[0, prev_blk, vsl, :], jnp.exp(s1 - mw).astype(BF16)))
        o_win.append(_finish(ow, HEAD_DIM))

    def body(n, ms):
        new = []
        for hkv in range(2):
            vsl = slice(hkv * 2 * HEAD_DIM, (hkv + 1) * 2 * HEAD_DIM)
            s, pens = sel_block(hkv, n, qi - n)
            m_new, pb = sel_update(s, pens, ms[hkv])
            acc_ref[hkv] = jnp.exp(ms[hkv] - m_new) * acc_ref[hkv] + _dot(vst_ref[0, n, vsl, :], pb)
            new.append(m_new)
        return tuple(new)

    lax.fori_loop(0, qi, body, tuple(ms))
    outs = []
    for hkv in range(2):
        o_slc = _finish(acc_ref[hkv], HEAD_DIM)
        for g in range(grp):
            hd = hkv * grp + g
            ls = slice(g * BLK, (g + 1) * BLK)
            outs.append(gates_t[hd:hd + 1] * o_cmp[hkv][:, ls] + gates_t[8 + hd:9 + hd] * o_slc[:, ls]
                        + gates_t[16 + hd:17 + hd] * o_win[hkv][:, ls])
    o_ref[0] = jnp.concatenate(outs, axis=0).T.astype(BF16)


def _nsa(qc, kcmp, vcmpt, ks, vst, kw, vwt, gates, bias_c):
    b, s, width = qc.shape
    n_blocks = s // BLK
    n_sel = s // SEL_BLOCK
    n_c = (s - CMP_BLOCK) // CMP_STRIDE + 1
    j_start = np.arange(n_sel)[:, None] * SEL_BLOCK
    c_start = np.arange(TILE)[None, :] * CMP_STRIDE
    overlap_t = ((c_start < j_start + SEL_BLOCK) & (c_start + CMP_BLOCK > j_start)
                 & (np.arange(TILE)[None, :] < n_c))
    ovt = jnp.asarray(overlap_t, BF16)
    qtile = lambda n: pl.BlockSpec((1, BLK, n), lambda bi, qi: (bi, qi, 0))
    full = pl.BlockSpec((1, s, 128), lambda bi, qi: (bi, 0, 0))
    full_t = pl.BlockSpec((1, n_blocks, 256, BLK), lambda bi, qi: (bi, 0, 0, 0))
    cmp_spec = pl.BlockSpec((1, TILE, 128), lambda bi, qi: (bi, 0, 0))
    return pl.pallas_call(
        functools.partial(_nsa_body, n_sel=n_sel),
        grid=(b, n_blocks),
        in_specs=[qtile(width), cmp_spec, cmp_spec, full, full_t, full, full_t, qtile(128),
                  _const_spec(bias_c.shape), _const_spec(ovt.shape)],
        out_specs=qtile(width),
        out_shape=jax.ShapeDtypeStruct((b, s, width), BF16),
        scratch_shapes=[pltpu.VMEM((2, n_blocks, 8, BLK), F32), pltpu.VMEM((2, 2 * HEAD_DIM, 4 * BLK), F32)],
        compiler_params=_cparams(2),
        name="nsa",
    )(qc, kcmp, vcmpt, ks, vst, kw, vwt, gates, bias_c, ovt)


def _moba_body(q_ref, k_ref, vt_ref, km_ref, bias_ref, o_ref, pen_ref, acc_ref, *, n_heads, n_blocks):
    qi = pl.program_id(1)
    lanes = n_heads * BLK
    nidx = lax.broadcasted_iota(jnp.int32, (16, lanes), 0)
    krow = lax.broadcasted_iota(jnp.int32, (BLK, lanes), 0)
    qcol = lax.broadcasted_iota(jnp.int32, (BLK, lanes), 1) & (BLK - 1)

    def scores(n, dist_blocks):
        st = pl.multiple_of(n * BLK, BLK)
        parts = []
        for hd in range(n_heads):
            sl = slice(hd * HEAD_DIM, (hd + 1) * HEAD_DIM)
            parts.append(_dot_nt(k_ref[0, pl.ds(st, BLK), sl], q_ref[0, :, sl])
                         + _bias_block(bias_ref, hd, dist_blocks))
        return jnp.concatenate(parts, axis=1)

    def weighted_values(n, p):
        return jnp.concatenate(
            [_dot(vt_ref[0, n, hd * 2 * HEAD_DIM:(hd + 1) * 2 * HEAD_DIM, :], p[:, hd * BLK:(hd + 1) * BLK])
             for hd in range(n_heads)], axis=1)

    zpad = jnp.zeros((16 - n_blocks, HEAD_DIM), F32)
    gate = jnp.concatenate(
        [_dot_nt(jnp.concatenate([km_ref[0, :, hd * HEAD_DIM:(hd + 1) * HEAD_DIM], zpad], axis=0).astype(BF16),
                 q_ref[0, :, hd * HEAD_DIM:(hd + 1) * HEAD_DIM]) for hd in range(n_heads)], axis=1)
    val = jnp.where(nidx < qi, gate, NEG_INF)
    val = jnp.where(nidx < n_blocks, val, PAD_SCORE)
    keep = (_rank_rows(val, nidx, n_blocks) < MOBA_TOPK) & (nidx < qi)
    pen_ref[...] = jnp.where(keep, 0.0, NEG_INF)

    s = jnp.where(krow <= qcol, scores(qi, 0), NEG_INF)
    m0 = jnp.max(s, axis=0, keepdims=True)
    acc_ref[...] = weighted_values(qi, jnp.exp(s - m0).astype(BF16))

    def body(n, m):
        s = scores(n, qi - n)
        pen = pen_ref[pl.ds(n, 1), :]
        m_new = jnp.maximum(m, jnp.max(s, axis=0, keepdims=True) + pen)
        p = jnp.exp(s - (m_new - pen)).astype(BF16)
        acc_ref[...] = jnp.exp(m - m_new) * acc_ref[...] + weighted_values(n, p)
        return m_new

    lax.fori_loop(0, qi, body, m0)
    o_t = _finish(acc_ref[...], HEAD_DIM)
    o_ref[0] = jnp.concatenate([o_t[:, hd * BLK:(hd + 1) * BLK] for hd in range(n_heads)],
                               axis=0).T.astype(BF16)


def _moba(qd, kd, vdt, kmean, bias_d):
    b, s, width = qd.shape
    n_heads = width // HEAD_DIM
    n_blocks = s // BLK
    return pl.pallas_call(
        functools.partial(_moba_body, n_heads=n_heads, n_blocks=n_blocks),
        grid=(b, n_blocks),
        in_specs=[
            pl.BlockSpec((1, BLK, width), lambda bi, qi: (bi, qi, 0)),
            pl.BlockSpec((1, s, width), lambda bi, qi: (bi, 0, 0)),
            pl.BlockSpec((1, n_blocks, 2 * width, BLK), lambda bi, qi: (bi, 0, 0, 0)),
            pl.BlockSpec((1, n_blocks, width), lambda bi, qi: (bi, 0, 0)),
            _const_spec(bias_d.shape),
        ],
        out_specs=pl.BlockSpec((1, BLK, width), lambda bi, qi: (bi, qi, 0)),
        out_shape=jax.ShapeDtypeStruct((b, s, width), BF16),
        scratch_shapes=[pltpu.VMEM((16, n_heads * BLK), F32), pltpu.VMEM((2 * HEAD_DIM, n_heads * BLK), F32)],
        compiler_params=_cparams(2),
        name="moba",
    )(qd, kd, vdt, kmean.reshape(b, n_blocks, width), bias_d)


def kernel(x, rel_bias_table, l0_ffn1_pre, l0_ffn1_post, l0_ffn1_wg, l0_ffn1_wu, l0_ffn1_wd, l0_mix_pre, l0_mix_post, l0_w_in, l0_sinks, l0_mla_q_norm, l0_mla_w_uq, l0_mla_kv_norm, l0_mla_w_ukv, l0_w_out, l0_ffn2_pre, l0_ffn2_post, l0_ffn2_wg, l0_ffn2_wu, l0_ffn2_wd, l1_ffn1_pre, l1_ffn1_post, l1_ffn1_wg, l1_ffn1_wu, l1_ffn1_wd, l1_mix_pre, l1_mix_post, l1_w_in, l1_nsa_pe_k, l1_nsa_pe_v, l1_nsa_wk1, l1_nsa_wk2, l1_nsa_wv1, l1_nsa_wv2, l1_w_out, l1_ffn2_pre, l1_ffn2_post, l1_ffn2_wg, l1_ffn2_wu, l1_ffn2_wd):
    b, s, d = x.shape
    m = b * s
    x2 = x.reshape(m, d)
    tiles = _bias_tiles(rel_bias_table)
    bias_a = tiles[A_SLOT:A_SLOT + 8, 0:2]
    bias_c = tiles[C_SLOT:C_SLOT + 8]
    bias_d = tiles[D_SLOT:D_SLOT + 8]

    x2 = _ffn(x2, l0_ffn1_pre, l0_ffn1_post, l0_ffn1_wg, l0_ffn1_wu, l0_ffn1_wd, 0.5)
    qa, ka, vat, qb, kb, vbt = _proj0(x2, s, l0_mix_pre, l0_w_in, l0_mla_q_norm, l0_mla_w_uq,
                                      l0_mla_kv_norm, l0_mla_w_ukv)
    shp = lambda t: t.reshape(b, s, t.shape[-1])
    o_a = _swa(shp(qa), shp(ka), vat, bias_a, l0_sinks)
    o_b = _mla(shp(qb), shp(kb), vbt)
    x2 = _outproj(x2, o_a.reshape(m, -1), o_b.reshape(m, -1), l0_w_out, l0_mix_post)
    x2 = _ffn(x2, l0_ffn2_pre, l0_ffn2_post, l0_ffn2_wg, l0_ffn2_wu, l0_ffn2_wd, 0.5)

    x2 = _ffn(x2, l1_ffn1_pre, l1_ffn1_post, l1_ffn1_wg, l1_ffn1_wu, l1_ffn1_wd, 0.5)
    qc, kc_hm, vc_hm, ks, vst, kw, vwt, gates, qd, kd, vdt, kmean = _proj1(x2, b, s, l1_mix_pre, l1_w_in)
    kcmp, vcmpt = _compress(kc_hm, vc_hm, l1_nsa_pe_k, l1_nsa_pe_v, l1_nsa_wk1, l1_nsa_wk2, l1_nsa_wv1, l1_nsa_wv2)
    o_c = _nsa(shp(qc), kcmp, vcmpt, shp(ks), vst, shp(kw), vwt, shp(gates), bias_c)
    o_d = _moba(shp(qd), shp(kd), vdt, kmean, bias_d)
    x2 = _outproj(x2, o_c.reshape(m, -1), o_d.reshape(m, -1), l1_w_out, l1_mix_post)
    x2 = _ffn(x2, l1_ffn2_pre, l1_ffn2_post, l1_ffn2_wg, l1_ffn2_wu, l1_ffn2_wd, 0.5)
    return x2.reshape(b, s, d)
```

```python
import functools
import math

import numpy as np
import jax
import jax.numpy as jnp
from jax import lax
from jax.experimental import pallas as pl
from jax.experimental.pallas import tpu as pltpu

F32 = jnp.float32
BF16 = jnp.bfloat16

D_MODEL = 1024
HEAD_DIM = 64
D_FF = 2816
NORM_EPS = 1e-6
LOG2E = math.log2(math.e)
NEG_INF = -1e30
PAD_SCORE = -3e38
FORCE_SCORE = 1e9
REL_BUCKETS = 32
REL_MAX_DIST = 1024
REL_SLOTS = 24
A_SLOT, C_SLOT, D_SLOT = 0, 8, 16
MLA_NOPE, MLA_ROPE, MLA_V = 64, 32, 64
MLA_HEADS = 8
ROPE_THETA = 10000.0
CMP_BLOCK, CMP_STRIDE = 32, 16
SEL_BLOCK, SEL_TOPK = 64, 8
MOBA_BLOCK, MOBA_TOPK = 256, 3

LANES = 128
TILE = 128
BLK = 256
ONES_ROWS = 16
VT_ROWS = HEAD_DIM + ONES_ROWS
BIAS_TILES = 9
TOKEN_TILE = 512
FF_CHUNK = 256
VMEM_LIMIT = 56 * 1024 * 1024


def _cparams(n_axes):
    return pltpu.CompilerParams(dimension_semantics=("arbitrary",) * n_axes, vmem_limit_bytes=VMEM_LIMIT)


def _dot(a, b):
    return jnp.dot(a, b, preferred_element_type=F32)


def _dot_nt(a, b):
    return lax.dot_general(a, b, (((1,), (1,)), ((), ())), preferred_element_type=F32)


def _rms(x, g):
    return x * lax.rsqrt(jnp.mean(x * x, axis=-1, keepdims=True) + NORM_EPS) * g


def _silu(x):
    return x / (1.0 + jnp.exp(-x))


def _const_spec(shape):
    nd = len(shape)
    return pl.BlockSpec(shape, lambda *_: (0,) * nd)


def _ffn_body(x_ref, pre_ref, post_ref, wg_ref, wu_ref, wd_ref, o_ref, acc_ref, *, n_chunks, weight):
    x = x_ref[...]
    h = _rms(x, pre_ref[...]).astype(BF16)
    for c in range(n_chunks):
        cols = slice(c * FF_CHUNK, (c + 1) * FF_CHUNK)
        g = _dot(h, wg_ref[:, cols])
        u = _dot(h, wu_ref[:, cols])
        a = (_silu(g) * u).astype(BF16)
        y = _dot(a, wd_ref[cols, :])
        if c == 0:
            acc_ref[...] = y
        else:
            acc_ref[...] += y
    o_ref[...] = x + weight * _rms(acc_ref[...], post_ref[...])


def _ffn(x2, pre, post, wg, wu, wd, weight):
    m, d = x2.shape
    d_ff = wg.shape[1]
    tm = TOKEN_TILE
    return pl.pallas_call(
        functools.partial(_ffn_body, n_chunks=d_ff // FF_CHUNK, weight=weight),
        grid=(m // tm,),
        in_specs=[
            pl.BlockSpec((tm, d), lambda i: (i, 0)),
            _const_spec((1, d)), _const_spec((1, d)),
            _const_spec((d, d_ff)), _const_spec((d, d_ff)), _const_spec((d_ff, d)),
        ],
        out_specs=pl.BlockSpec((tm, d), lambda i: (i, 0)),
        out_shape=jax.ShapeDtypeStruct((m, d), F32),
        scratch_shapes=[pltpu.VMEM((tm, d), F32)],
        compiler_params=_cparams(1),
        name="ffn",
    )(x2, pre.reshape(1, d), post.reshape(1, d), wg.astype(BF16), wu.astype(BF16), wd.astype(BF16))


def _outproj_body(x_ref, o1_ref, o2_ref, w1_ref, w2_ref, post_ref, o_ref):
    y = _dot(o1_ref[...], w1_ref[...]) + _dot(o2_ref[...], w2_ref[...])
    o_ref[...] = x_ref[...] + _rms(y, post_ref[...])


def _outproj(x2, o1, o2, w_out, post):
    m, d = x2.shape
    n1 = o1.shape[1]
    n2 = o2.shape[1]
    w = w_out.astype(BF16)
    tm = TOKEN_TILE
    return pl.pallas_call(
        _outproj_body,
        grid=(m // tm,),
        in_specs=[
            pl.BlockSpec((tm, d), lambda i: (i, 0)),
            pl.BlockSpec((tm, n1), lambda i: (i, 0)),
            pl.BlockSpec((tm, n2), lambda i: (i, 0)),
            _const_spec((n1, d)), _const_spec((n2, d)), _const_spec((1, d)),
        ],
        out_specs=pl.BlockSpec((tm, d), lambda i: (i, 0)),
        out_shape=jax.ShapeDtypeStruct((m, d), F32),
        compiler_params=_cparams(1),
        name="outproj",
    )(x2, o1, o2, w[:n1], w[n1:], post.reshape(1, d))


def _t5_bucket(dist):
    n = jnp.maximum(dist, 0)
    exact = REL_BUCKETS // 2
    nf = jnp.maximum(n, 1).astype(jnp.float32)
    large = exact + (jnp.log(nf / exact) / math.log(REL_MAX_DIST / exact) * (REL_BUCKETS - exact)).astype(jnp.int32)
    return jnp.where(n < exact, n, jnp.minimum(large, REL_BUCKETS - 1))


def _bias_body(tab_ref, idx_ref, o_ref):
    slot = pl.program_id(0)

    def tile(d, carry):
        idx = idx_ref[d]
        v = jnp.full((TILE, TILE), LOG2E * tab_ref[slot], F32)
        for b in range(1, REL_BUCKETS):
            v = jnp.where(idx == b, LOG2E * tab_ref[b * REL_SLOTS + slot], v)
        o_ref[0, d] = v
        return carry

    lax.fori_loop(0, BIAS_TILES, tile, 0)


def _bias_tiles(rel_table):
    n_slots = rel_table.shape[1]
    d = jnp.arange(BIAS_TILES)[:, None, None]
    key = jnp.arange(TILE)[None, :, None]
    query = jnp.arange(TILE)[None, None, :]
    idx = _t5_bucket(d * TILE + query - key).astype(jnp.int32)
    return pl.pallas_call(
        _bias_body,
        grid=(n_slots,),
        in_specs=[
            pl.BlockSpec(memory_space=pltpu.SMEM),
            _const_spec((BIAS_TILES, TILE, TILE)),
        ],
        out_specs=pl.BlockSpec((1, BIAS_TILES, TILE, TILE), lambda s: (s, 0, 0, 0)),
        out_shape=jax.ShapeDtypeStruct((n_slots, BIAS_TILES, TILE, TILE), F32),
        compiler_params=_cparams(1),
        name="bias_tiles",
    )(rel_table.astype(F32).reshape(-1), idx)


def _proj0_body(x_ref, pre_ref, w_ref, qn_ref, wq1_ref, wq2_ref, kvn_ref, wk_ref, wv_ref, cos_ref, sin_ref,
                qa_ref, ka_ref, vat_ref, qb_ref, kb_ref, vbt_ref):
    h = _rms(x_ref[...], pre_ref[...]).astype(BF16)
    y = _dot(h, w_ref[...])
    qa_ref[...] = (y[:, 0:512] * (LOG2E * HEAD_DIM ** -0.5)).astype(BF16)
    ka_ref[...] = y[:, 512:640].astype(BF16)
    for j in range(y.shape[0] // TILE):
        _store_vt_ones(vat_ref, j, y[j * TILE:(j + 1) * TILE, 640:768], 2)
    cq = _rms(y[:, 768:1024], qn_ref[...]).astype(BF16)
    ckv = _rms(y[:, 1024:1152], kvn_ref[...]).astype(BF16)
    cos_t = cos_ref[...]
    sin_t = sin_ref[...]
    lane = lax.broadcasted_iota(jnp.int32, cos_t.shape, 1)
    scale = LOG2E * (MLA_NOPE + MLA_ROPE) ** -0.5
    qmul = scale * (cos_t + jnp.where(lane < MLA_NOPE, 1.0, 0.0))
    qsin = scale * sin_t
    kr = y[:, 1152:1280] * cos_t + y[:, 1280:1408] * sin_t
    q1 = _dot(cq, wq1_ref[...])
    q2 = _dot(cq, wq2_ref[...])
    k1 = _dot(ckv, wk_ref[...])
    for hd in range(MLA_HEADS):
        sl = slice(hd * LANES, (hd + 1) * LANES)
        qb_ref[:, sl] = (q1[:, sl] * qmul + q2[:, sl] * qsin).astype(BF16)
        kb_ref[:, sl] = (k1[:, sl] + kr).astype(BF16)
    vb = _dot(ckv, wv_ref[...])
    for j in range(vb.shape[0] // BLK):
        _store_vt_ones(vbt_ref, j, vb[j * BLK:(j + 1) * BLK], MLA_HEADS)


def _rope_lane_tables(s):
    inv = ROPE_THETA ** (-jnp.arange(0, MLA_ROPE, 2, dtype=jnp.float32) / MLA_ROPE)
    ang = jnp.arange(s, dtype=jnp.float32)[:, None] * inv[None, :]
    cos, sin = jnp.cos(ang), jnp.sin(ang)
    z_lo = jnp.zeros((s, MLA_NOPE), F32)
    z_hi = jnp.zeros((s, LANES - MLA_NOPE - MLA_ROPE), F32)
    return (jnp.concatenate([z_lo, cos, cos, z_hi], axis=1), jnp.concatenate([z_lo, sin, sin, z_hi], axis=1))


def _rot_cols(w):
    half = w.shape[-1] // 2
    return jnp.concatenate([-w[..., half:], w[..., :half]], axis=-1)


def _proj0(x2, seq, pre, w_in, q_norm, w_uq, kv_norm, w_ukv):
    m, d = x2.shape
    tm = TOKEN_TILE
    kr_w = w_in[:, 1152:1184]
    z64 = jnp.zeros((d, 64), F32)
    z32 = jnp.zeros((d, 32), F32)
    w0 = jnp.concatenate([w_in[:, :1152], z64, kr_w, z32, z64, _rot_cols(kr_w), z32], axis=1).astype(BF16)
    wq = w_uq.reshape(-1, MLA_HEADS, MLA_NOPE + MLA_ROPE)
    rq = wq.shape[0]
    zq = jnp.zeros((rq, MLA_HEADS, 32), F32)
    wq1 = jnp.concatenate([wq, zq], axis=2).reshape(rq, -1).astype(BF16)
    wq2 = jnp.concatenate([jnp.zeros((rq, MLA_HEADS, 64), F32), _rot_cols(wq[:, :, MLA_NOPE:]), zq],
                          axis=2).reshape(rq, -1).astype(BF16)
    wkv = w_ukv.reshape(-1, MLA_HEADS, MLA_NOPE + MLA_V)
    rk = wkv.shape[0]
    wk = jnp.concatenate([wkv[:, :, :MLA_NOPE], jnp.zeros((rk, MLA_HEADS, 64), F32)], axis=2).reshape(rk, -1).astype(BF16)
    wv = wkv[:, :, MLA_NOPE:].reshape(rk, -1).astype(BF16)
    cos_t, sin_t = _rope_lane_tables(seq)
    n_st = seq // tm
    tok = lambda n: pl.BlockSpec((tm, n), lambda i: (i, 0))
    outs = pl.pallas_call(
        _proj0_body,
        grid=(m // tm,),
        in_specs=[
            tok(d), _const_spec((1, d)), _const_spec(w0.shape),
            _const_spec((1, rq)), _const_spec(wq1.shape), _const_spec(wq2.shape),
            _const_spec((1, rk)), _const_spec(wk.shape), _const_spec(wv.shape),
            pl.BlockSpec((tm, LANES), lambda i: (i % n_st, 0)),
            pl.BlockSpec((tm, LANES), lambda i: (i % n_st, 0)),
        ],
        out_specs=[tok(512), tok(128),
                   pl.BlockSpec((1, tm // TILE, 2 * VT_ROWS, TILE), lambda i: (i // n_st, i % n_st, 0, 0)),
                   tok(1024), tok(1024),
                   pl.BlockSpec((1, tm // BLK, MLA_HEADS * VT_ROWS, BLK), lambda i: (i // n_st, i % n_st, 0, 0))],
        out_shape=[jax.ShapeDtypeStruct((m, 512), BF16), jax.ShapeDtypeStruct((m, 128), BF16),
                   jax.ShapeDtypeStruct((m // seq, seq // TILE, 2 * VT_ROWS, TILE), BF16),
                   jax.ShapeDtypeStruct((m, 1024), BF16), jax.ShapeDtypeStruct((m, 1024), BF16),
                   jax.ShapeDtypeStruct((m // seq, seq // BLK, MLA_HEADS * VT_ROWS, BLK), BF16)],
        compiler_params=_cparams(1),
        name="proj0",
    )(x2, pre.reshape(1, d), w0, q_norm.reshape(1, rq), wq1, wq2, kv_norm.reshape(1, rk), wk, wv, cos_t, sin_t)
    return outs


def _swa_body(sink_ref, q_ref, k_ref, vt_ref, bias_ref, o_ref, *, n_kv, grp):
    qi = pl.program_id(1)
    halves = BLK // TILE
    items = [(hkv, half) for hkv in range(n_kv) for half in range(halves)]
    width = grp * TILE
    krow = lax.broadcasted_iota(jnp.int32, (TILE, len(items) * width), 0)
    qcol = lax.broadcasted_iota(jnp.int32, (TILE, len(items) * width), 1) & (TILE - 1)

    cur_s, prev_s, sink_parts, has_prev = [], [], [], []
    for hkv, half in items:
        sl = slice(hkv * HEAD_DIM, (hkv + 1) * HEAD_DIM)
        t = qi * halves + half
        tp = jnp.maximum(t - 1, 0)
        qs = jnp.concatenate([q_ref[0, half * TILE:(half + 1) * TILE, (hkv * grp + g) * HEAD_DIM:
                                    (hkv * grp + g + 1) * HEAD_DIM] for g in range(grp)], axis=0)
        bias_c = jnp.concatenate([bias_ref[hkv * grp + g, 0] for g in range(grp)], axis=1)
        bias_p = jnp.concatenate([bias_ref[hkv * grp + g, 1] for g in range(grp)], axis=1)
        cur_s.append(_dot_nt(k_ref[0, pl.ds(pl.multiple_of(t * TILE, TILE), TILE), sl], qs) + bias_c)
        prev_s.append(_dot_nt(k_ref[0, pl.ds(pl.multiple_of(tp * TILE, TILE), TILE), sl], qs) + bias_p)
        sink_parts.extend(jnp.full((1, TILE), LOG2E * sink_ref[hkv * grp + g], F32) for g in range(grp))
        has_prev.append(jnp.full((1, width), jnp.where(t >= 1, 0, TILE), jnp.int32))
    s_c = jnp.where(krow <= qcol, jnp.concatenate(cur_s, axis=1), NEG_INF)
    s_p = jnp.where(krow > qcol + jnp.concatenate(has_prev, axis=1),
                    jnp.concatenate(prev_s, axis=1), NEG_INF)
    sink = jnp.concatenate(sink_parts, axis=1)
    mx = jnp.maximum(jnp.maximum(jnp.max(s_c, axis=0, keepdims=True), jnp.max(s_p, axis=0, keepdims=True)), sink)
    p_c = jnp.exp2(s_c - mx).astype(BF16)
    p_p = jnp.exp2(s_p - mx).astype(BF16)
    accs = []
    for i, (hkv, half) in enumerate(items):
        vsl = slice(hkv * VT_ROWS, (hkv + 1) * VT_ROWS)
        ls = slice(i * width, (i + 1) * width)
        t = qi * halves + half
        accs.append(_dot(vt_ref[0, t, vsl, :], p_c[:, ls]) + _dot(vt_ref[0, jnp.maximum(t - 1, 0), vsl, :], p_p[:, ls]))
    acc = jnp.concatenate(accs, axis=1)
    o_t = acc[0:HEAD_DIM] / (acc[HEAD_DIM:HEAD_DIM + 1] + jnp.exp2(sink - mx))
    heads = []
    for hkv in range(n_kv):
        for g in range(grp):
            heads.append(jnp.concatenate(
                [o_t[:, (hkv * halves + half) * width + g * TILE:(hkv * halves + half) * width + (g + 1) * TILE]
                 for half in range(halves)], axis=1))
    o_ref[0] = jnp.concatenate(heads, axis=0).T.astype(BF16)


def _swa(qa, ka, vat, bias_a, sinks):
    b, s, width = qa.shape
    n_kv = ka.shape[-1] // HEAD_DIM
    grp = width // HEAD_DIM // n_kv
    return pl.pallas_call(
        functools.partial(_swa_body, n_kv=n_kv, grp=grp),
        grid=(b, s // BLK),
        in_specs=[
            pl.BlockSpec(memory_space=pltpu.SMEM),
            pl.BlockSpec((1, BLK, width), lambda bi, qi: (bi, qi, 0)),
            pl.BlockSpec((1, s, n_kv * HEAD_DIM), lambda bi, qi: (bi, 0, 0)),
            pl.BlockSpec((1, s // TILE, n_kv * VT_ROWS, TILE), lambda bi, qi: (bi, 0, 0, 0)),
            _const_spec(bias_a.shape),
        ],
        out_specs=pl.BlockSpec((1, BLK, width), lambda bi, qi: (bi, qi, 0)),
        out_shape=jax.ShapeDtypeStruct((b, s, width), BF16),
        compiler_params=_cparams(2),
        name="swa",
    )(sinks.astype(F32), qa, ka, vat, bias_a)


def _finish(acc, dv):
    return acc[0:dv] / acc[dv:dv + 1]


def _mla_body(q_ref, k_ref, vt_ref, o_ref, acc_ref, *, n_heads):
    qi = pl.program_id(1)
    lanes = n_heads * BLK
    krow = lax.broadcasted_iota(jnp.int32, (BLK, lanes), 0)
    qcol = lax.broadcasted_iota(jnp.int32, (BLK, lanes), 1) & (BLK - 1)

    def scores(n):
        st = pl.multiple_of(n * BLK, BLK)
        return jnp.concatenate(
            [_dot_nt(k_ref[0, pl.ds(st, BLK), hd * LANES:(hd + 1) * LANES], q_ref[0, :, hd * LANES:(hd + 1) * LANES])
             for hd in range(n_heads)], axis=1)

    def weighted_values(n, p):
        return jnp.concatenate(
            [_dot(vt_ref[0, n, hd * VT_ROWS:(hd + 1) * VT_ROWS, :], p[:, hd * BLK:(hd + 1) * BLK])
             for hd in range(n_heads)], axis=1)

    s = jnp.where(krow <= qcol, scores(qi), NEG_INF)
    m0 = jnp.max(s, axis=0, keepdims=True)
    acc_ref[...] = weighted_values(qi, jnp.exp2(s - m0).astype(BF16))

    def body(n, m):
        s = scores(n)
        m_new = jnp.maximum(m, jnp.max(s, axis=0, keepdims=True))
        acc_ref[...] = jnp.exp2(m - m_new) * acc_ref[...] + weighted_values(n, jnp.exp2(s - m_new).astype(BF16))
        return m_new

    lax.fori_loop(0, qi, body, m0)
    o_t = _finish(acc_ref[...], MLA_V)
    o_ref[0] = jnp.concatenate([o_t[:, hd * BLK:(hd + 1) * BLK] for hd in range(n_heads)],
                               axis=0).T.astype(BF16)


def _mla(qb, kb, vbt):
    b, s, width = qb.shape
    n_heads = width // LANES
    n_blocks = s // BLK
    return pl.pallas_call(
        functools.partial(_mla_body, n_heads=n_heads),
        grid=(b, n_blocks),
        in_specs=[
            pl.BlockSpec((1, BLK, width), lambda bi, qi: (bi, qi, 0)),
            pl.BlockSpec((1, s, width), lambda bi, qi: (bi, 0, 0)),
            pl.BlockSpec((1, n_blocks, n_heads * VT_ROWS, BLK), lambda bi, qi: (bi, 0, 0, 0)),
        ],
        out_specs=pl.BlockSpec((1, BLK, n_heads * MLA_V), lambda bi, qi: (bi, qi, 0)),
        out_shape=jax.ShapeDtypeStruct((b, s, n_heads * MLA_V), BF16),
        scratch_shapes=[pltpu.VMEM((VT_ROWS, n_heads * BLK), F32)],
        compiler_params=_cparams(2),
        name="mla",
    )(qb, kb, vbt)


def _store_vt_ones(vt_ref, j, v, n_heads):
    vt = v.T.astype(BF16)
    ones = jnp.ones((ONES_ROWS, v.shape[0]), BF16)
    for hd in range(n_heads):
        vt_ref[0, j, hd * VT_ROWS:hd * VT_ROWS + HEAD_DIM, :] = vt[hd * HEAD_DIM:(hd + 1) * HEAD_DIM]
        vt_ref[0, j, hd * VT_ROWS + HEAD_DIM:(hd + 1) * VT_ROWS, :] = ones


def _proj1_body(x_ref, pre_ref, w_ref, qc_ref, kc_ref, vc_ref, ks_ref, vst_ref, kw_ref, vwt_ref, g_ref,
                qd_ref, kd_ref, vdt_ref, km_ref, *, tm):
    h = _rms(x_ref[...], pre_ref[...]).astype(BF16)
    y = _dot(h, w_ref[...])
    scale = LOG2E * HEAD_DIM ** -0.5
    qc_ref[...] = (y[:, 0:512] * scale).astype(BF16)
    for hkv in range(2):
        kc_ref[0, hkv] = y[:, 512 + hkv * 64:576 + hkv * 64].astype(BF16)
        vc_ref[0, hkv] = y[:, 640 + hkv * 64:704 + hkv * 64].astype(BF16)
    ks_ref[...] = y[:, 768:896].astype(BF16)
    vs = y[:, 896:1024]
    kw_ref[...] = y[:, 1024:1152].astype(BF16)
    vw = y[:, 1152:1280]
    gl = y[:, 1280:1408]
    g_ref[...] = 1.0 / (1.0 + jnp.exp(-gl))
    qd_ref[...] = (y[:, 1408:1920] * scale).astype(BF16)
    kd = y[:, 1920:2432]
    kd_ref[...] = kd.astype(BF16)
    vd = y[:, 2432:2944]
    for j in range(tm // BLK):
        rows = slice(j * BLK, (j + 1) * BLK)
        km_ref[j] = jnp.mean(kd[rows], axis=0, keepdims=True)
        _store_vt_ones(vdt_ref, j, vd[rows], 8)
        _store_vt_ones(vst_ref, j, vs[rows], 2)
        _store_vt_ones(vwt_ref, j, vw[rows], 2)


def _proj1(x2, batch, seq, pre, w_in):
    m, d = x2.shape
    tm = TOKEN_TILE
    n_st = seq // tm
    pad = jnp.zeros((d, LANES - 24), F32)
    w1 = jnp.concatenate([w_in[:, :1304], pad, w_in[:, 1304:]], axis=1).astype(BF16)
    tok = lambda n: pl.BlockSpec((tm, n), lambda i: (i, 0))
    hm = pl.BlockSpec((1, 2, tm, HEAD_DIM), lambda i: (i // n_st, 0, i % n_st, 0))
    nblk = tm // BLK
    vt_spec = lambda rows: pl.BlockSpec((1, nblk, rows, BLK), lambda i: (i // n_st, i % n_st, 0, 0))
    vt_shape = lambda rows: jax.ShapeDtypeStruct((batch, seq // BLK, rows, BLK), BF16)
    return pl.pallas_call(
        functools.partial(_proj1_body, tm=tm),
        grid=(m // tm,),
        in_specs=[tok(d), _const_spec((1, d)), _const_spec(w1.shape)],
        out_specs=[tok(512), hm, hm, tok(128), vt_spec(2 * VT_ROWS), tok(128), vt_spec(2 * VT_ROWS), tok(128), tok(512), tok(512),
                   vt_spec(8 * VT_ROWS),
                   pl.BlockSpec((nblk, 1, 512), lambda i: (i, 0, 0))],
        out_shape=[
            jax.ShapeDtypeStruct((m, 512), BF16),
            jax.ShapeDtypeStruct((batch, 2, seq, HEAD_DIM), BF16),
            jax.ShapeDtypeStruct((batch, 2, seq, HEAD_DIM), BF16),
            jax.ShapeDtypeStruct((m, 128), BF16), vt_shape(2 * VT_ROWS),
            jax.ShapeDtypeStruct((m, 128), BF16), vt_shape(2 * VT_ROWS),
            jax.ShapeDtypeStruct((m, 128), F32),
            jax.ShapeDtypeStruct((m, 512), BF16), jax.ShapeDtypeStruct((m, 512), BF16),
            vt_shape(8 * VT_ROWS),
            jax.ShapeDtypeStruct((m // MOBA_BLOCK, 1, 512), F32),
        ],
        compiler_params=_cparams(1),
        name="proj1",
    )(x2, pre.reshape(1, d), w1)


def _compress_body(kx_ref, vx_ref, pek_ref, pev_ref, wk1_ref, wk2_ref, wv1_ref, wv2_ref, ko_ref, vo_ref):
    half = CMP_STRIDE * HEAD_DIM
    for x_ref, pe_ref, w1_ref, w2_ref, o_ref in ((kx_ref, pek_ref, wk1_ref, wk2_ref, ko_ref),
                                                 (vx_ref, pev_ref, wv1_ref, wv2_ref, vo_ref)):
        pe8 = jnp.broadcast_to(pe_ref[...], (8, 2 * half)).astype(BF16)
        pe_term = _dot(pe8, w1_ref[...])[0:1]
        outs = []
        for hkv in range(2):
            x = x_ref[0, hkv]
            lo = _dot(x, w1_ref[0:half])
            hi = _dot(x, w1_ref[half:2 * half])
            hid = lo + pltpu.roll(hi, x.shape[0] - 1, 0) + pe_term
            outs.append(_dot(_silu(hid).astype(BF16), w2_ref[...]))
        out = jnp.concatenate(outs, axis=1)
        o_ref[0] = (out.T if o_ref is vo_ref else out).astype(BF16)


def _compress(kc_hm, vc_hm, pe_k, pe_v, wk1, wk2, wv1, wv2):
    b, _, s, dh = kc_hm.shape
    rows = s // CMP_STRIDE
    feat = CMP_STRIDE * dh
    kx = kc_hm.reshape(b, 2, rows, feat)
    vx = vc_hm.reshape(b, 2, rows, feat)
    xspec = pl.BlockSpec((1, 2, rows, feat), lambda bi: (bi, 0, 0, 0))
    ospec = pl.BlockSpec((1, rows, 2 * dh), lambda bi: (bi, 0, 0))
    w1s, w2s = wk1.shape, wk2.shape
    return pl.pallas_call(
        _compress_body,
        grid=(b,),
        in_specs=[xspec, xspec, _const_spec((1, 2 * feat)), _const_spec((1, 2 * feat)),
                  _const_spec(w1s), _const_spec(w2s), _const_spec(w1s), _const_spec(w2s)],
        out_specs=[ospec, ospec],
        out_shape=[jax.ShapeDtypeStruct((b, rows, 2 * dh), BF16)] * 2,
        compiler_params=_cparams(1),
        name="nsa_compress",
    )(kx, vx, pe_k.astype(F32).reshape(1, -1), pe_v.astype(F32).reshape(1, -1),
      wk1.astype(BF16), wk2.astype(BF16), wv1.astype(BF16), wv2.astype(BF16))


def _bias_block(bias_ref, hd, dist_blocks):
    sub = BLK // TILE
    rows = []
    for a in range(sub):
        cols = []
        for bq in range(sub):
            e = sub * dist_blocks + bq - a
            e = max(e, 0) if isinstance(e, int) else jnp.clip(e, 0, BIAS_TILES - 1)
            cols.append(bias_ref[hd, e])
        rows.append(jnp.concatenate(cols, axis=1))
    return jnp.concatenate(rows, axis=0)


def _rank_rows(val, ridx, n_real):
    rank = jnp.zeros(val.shape, F32)
    for j in range(n_real):
        vj = val[j:j + 1, :]
        rank = rank + jnp.where((vj > val) | ((vj == val) & (ridx > j)), 1.0, 0.0)
    return rank


def _nsa_body(q_ref, kc_ref, vct_ref, ks_ref, vst_ref, kw_ref, vwt_ref, g_ref, bias_ref, ovt_ref, o_ref,
              pen_ref, acc_ref, *, n_sel):
    qi = pl.program_id(1)
    grp = 4
    lanes = grp * BLK
    sub = BLK // SEL_BLOCK
    own = pl.multiple_of(qi * BLK, BLK)
    prev = pl.multiple_of(jnp.maximum(qi - 1, 0) * BLK, BLK)
    prev_blk = jnp.maximum(qi - 1, 0)
    krow = lax.broadcasted_iota(jnp.int32, (BLK, lanes), 0)
    qcol = lax.broadcasted_iota(jnp.int32, (BLK, lanes), 1) & (BLK - 1)
    causal = krow <= qcol
    crow = lax.broadcasted_iota(jnp.int32, (TILE, lanes), 0)
    cq = lax.broadcasted_iota(jnp.int32, (TILE, lanes), 1) & (BLK - 1)
    visible = CMP_STRIDE * crow + (CMP_BLOCK - 1) <= qi * BLK + cq
    jrow = lax.broadcasted_iota(jnp.int32, (n_sel, BLK), 0)
    cur = qi * sub + (lax.broadcasted_iota(jnp.int32, (n_sel, BLK), 1) >> 6)
    forced = (jrow == 0) | (jrow == cur) | (jrow == cur - 1)
    gates_t = g_ref[0].T
    has_prev = jnp.where(qi >= 1, 0, BLK)

    def stack_q(hkv):
        return jnp.concatenate(
            [q_ref[0, :, (hkv * grp + g) * HEAD_DIM:(hkv * grp + g + 1) * HEAD_DIM] for g in range(grp)], axis=0)

    def group_bias(hkv, dist_blocks):
        return jnp.concatenate([_bias_block(bias_ref, hkv * grp + g, dist_blocks) for g in range(grp)], axis=1)

    def sel_block(hkv, n, dist_blocks):
        st = pl.multiple_of(n * BLK, BLK)
        sl = slice(hkv * HEAD_DIM, (hkv + 1) * HEAD_DIM)
        s = _dot_nt(ks_ref[0, pl.ds(st, BLK), sl], stack_q(hkv)) + group_bias(hkv, dist_blocks)
        pens = pen_ref[hkv, n, 0:sub, :]
        return s, [jnp.concatenate([pens[j:j + 1]] * grp, axis=1) for j in range(sub)]

    def sel_update(s, pens, m_old):
        parts = [s[j * SEL_BLOCK:(j + 1) * SEL_BLOCK] for j in range(sub)]
        m_new = m_old
        for j in range(sub):
            mj = jnp.max(parts[j], axis=0, keepdims=True) + pens[j]
            m_new = mj if m_new is None else jnp.maximum(m_new, mj)
        p = jnp.concatenate([jnp.exp2(parts[j] - (m_new - pens[j])) for j in range(sub)], axis=0)
        return m_new, p.astype(BF16)

    o_cmp, o_win, ms = [], [], []
    for hkv in range(2):
        sl = slice(hkv * HEAD_DIM, (hkv + 1) * HEAD_DIM)
        vsl = slice(hkv * VT_ROWS, (hkv + 1) * VT_ROWS)
        qs = stack_q(hkv)

        sc = jnp.where(visible, _dot_nt(kc_ref[0, :, sl], qs), NEG_INF)
        mc = jnp.max(sc, axis=0, keepdims=True)
        ec = jnp.where(visible, jnp.exp2(sc - mc), 0.0)
        lc = jnp.sum(ec, axis=0, keepdims=True)
        p = ec / jnp.where(lc > 0.0, lc, 1.0)
        o_cmp.append(_dot(vct_ref[0, sl, :], p.astype(BF16)))

        ps = p[:, 0:BLK] + p[:, BLK:2 * BLK] + p[:, 2 * BLK:3 * BLK] + p[:, 3 * BLK:4 * BLK]
        ps_hi = ps.astype(BF16)
        ps_lo = (ps - ps_hi.astype(F32)).astype(BF16)
        imp = _dot(ovt_ref[...], ps_hi) + _dot(ovt_ref[...], ps_lo)
        val = jnp.where(forced, FORCE_SCORE, jnp.where(jrow <= cur, imp, NEG_INF))
        pen = jnp.where(_rank_rows(val, jrow, n_sel) < SEL_TOPK, 0.0, NEG_INF)
        for t in range(n_sel // sub):
            pen_ref[hkv, t, 0:sub, :] = pen[t * sub:(t + 1) * sub]

        s, pens = sel_block(hkv, qi, 0)
        m, pb = sel_update(jnp.where(causal, s, NEG_INF), pens, None)
        acc_ref[hkv] = _dot(vst_ref[0, qi, vsl, :], pb)
        ms.append(m)

        s0 = _dot_nt(kw_ref[0, pl.ds(own, BLK), sl], qs) + group_bias(hkv, 0)
        s1 = _dot_nt(kw_ref[0, pl.ds(prev, BLK), sl], qs) + group_bias(hkv, 1)
        s0 = jnp.where(causal, s0, NEG_INF)
        s1 = jnp.where(krow > qcol + has_prev, s1, NEG_INF)
        mw = jnp.maximum(jnp.max(s0, axis=0, keepdims=True), jnp.max(s1, axis=0, keepdims=True))
        ow = (_dot(vwt_ref[0, qi, vsl, :], jnp.exp2(s0 - mw).astype(BF16))
              + _dot(vwt_ref[0, prev_blk, vsl, :], jnp.exp2(s1 - mw).astype(BF16)))
        o_win.append(_finish(ow, HEAD_DIM))

    def body(n, ms):
        new = []
        for hkv in range(2):
            vsl = slice(hkv * VT_ROWS, (hkv + 1) * VT_ROWS)
            s, pens = sel_block(hkv, n, qi - n)
            m_new, pb = sel_update(s, pens, ms[hkv])
            acc_ref[hkv] = jnp.exp2(ms[hkv] - m_new) * acc_ref[hkv] + _dot(vst_ref[0, n, vsl, :], pb)
            new.append(m_new)
        return tuple(new)

    lax.fori_loop(0, qi, body, tuple(ms))
    outs = []
    for hkv in range(2):
        o_slc = _finish(acc_ref[hkv], HEAD_DIM)
        for g in range(grp):
            hd = hkv * grp + g
            ls = slice(g * BLK, (g + 1) * BLK)
            outs.append(gates_t[hd:hd + 1] * o_cmp[hkv][:, ls] + gates_t[8 + hd:9 + hd] * o_slc[:, ls]
                        + gates_t[16 + hd:17 + hd] * o_win[hkv][:, ls])
    o_ref[0] = jnp.concatenate(outs, axis=0).T.astype(BF16)


def _nsa(qc, kcmp, vcmpt, ks, vst, kw, vwt, gates, bias_c):
    b, s, width = qc.shape
    n_blocks = s // BLK
    n_sel = s // SEL_BLOCK
    n_c = (s - CMP_BLOCK) // CMP_STRIDE + 1
    j_start = np.arange(n_sel)[:, None] * SEL_BLOCK
    c_start = np.arange(TILE)[None, :] * CMP_STRIDE
    overlap_t = ((c_start < j_start + SEL_BLOCK) & (c_start + CMP_BLOCK > j_start)
                 & (np.arange(TILE)[None, :] < n_c))
    ovt = jnp.asarray(overlap_t, BF16)
    qtile = lambda n: pl.BlockSpec((1, BLK, n), lambda bi, qi: (bi, qi, 0))
    full = pl.BlockSpec((1, s, 128), lambda bi, qi: (bi, 0, 0))
    full_t = pl.BlockSpec((1, n_blocks, 2 * VT_ROWS, BLK), lambda bi, qi: (bi, 0, 0, 0))
    cmp_spec = pl.BlockSpec((1, TILE, 128), lambda bi, qi: (bi, 0, 0))
    return pl.pallas_call(
        functools.partial(_nsa_body, n_sel=n_sel),
        grid=(b, n_blocks),
        in_specs=[qtile(width), cmp_spec, cmp_spec, full, full_t, full, full_t, qtile(128),
                  _const_spec(bias_c.shape), _const_spec(ovt.shape)],
        out_specs=qtile(width),
        out_shape=jax.ShapeDtypeStruct((b, s, width), BF16),
        scratch_shapes=[pltpu.VMEM((2, n_blocks, 8, BLK), F32), pltpu.VMEM((2, VT_ROWS, 4 * BLK), F32)],
        compiler_params=_cparams(2),
        name="nsa",
    )(qc, kcmp, vcmpt, ks, vst, kw, vwt, gates, bias_c, ovt)


def _moba_body(q_ref, k_ref, vt_ref, km_ref, bias_ref, o_ref, pen_ref, acc_ref, *, n_heads, n_blocks):
    qi = pl.program_id(1)
    lanes = n_heads * BLK
    nidx = lax.broadcasted_iota(jnp.int32, (16, lanes), 0)
    krow = lax.broadcasted_iota(jnp.int32, (BLK, lanes), 0)
    qcol = lax.broadcasted_iota(jnp.int32, (BLK, lanes), 1) & (BLK - 1)

    def scores(n, dist_blocks):
        st = pl.multiple_of(n * BLK, BLK)
        parts = []
        for hd in range(n_heads):
            sl = slice(hd * HEAD_DIM, (hd + 1) * HEAD_DIM)
            parts.append(_dot_nt(k_ref[0, pl.ds(st, BLK), sl], q_ref[0, :, sl])
                         + _bias_block(bias_ref, hd, dist_blocks))
        return jnp.concatenate(parts, axis=1)

    def weighted_values(n, p):
        return jnp.concatenate(
            [_dot(vt_ref[0, n, hd * VT_ROWS:(hd + 1) * VT_ROWS, :], p[:, hd * BLK:(hd + 1) * BLK])
             for hd in range(n_heads)], axis=1)

    zpad = jnp.zeros((16 - n_blocks, HEAD_DIM), F32)
    gate = jnp.concatenate(
        [_dot_nt(jnp.concatenate([km_ref[0, :, hd * HEAD_DIM:(hd + 1) * HEAD_DIM], zpad], axis=0).astype(BF16),
                 q_ref[0, :, hd * HEAD_DIM:(hd + 1) * HEAD_DIM]) for hd in range(n_heads)], axis=1)
    val = jnp.where(nidx < qi, gate, NEG_INF)
    val = jnp.where(nidx < n_blocks, val, PAD_SCORE)
    keep = (_rank_rows(val, nidx, n_blocks) < MOBA_TOPK) & (nidx < qi)
    pen_ref[...] = jnp.where(keep, 0.0, NEG_INF)

    s = jnp.where(krow <= qcol, scores(qi, 0), NEG_INF)
    m0 = jnp.max(s, axis=0, keepdims=True)
    acc_ref[...] = weighted_values(qi, jnp.exp2(s - m0).astype(BF16))

    def body(n, m):
        s = scores(n, qi - n)
        pen = pen_ref[pl.ds(n, 1), :]
        m_new = jnp.maximum(m, jnp.max(s, axis=0, keepdims=True) + pen)
        p = jnp.exp2(s - (m_new - pen)).astype(BF16)
        acc_ref[...] = jnp.exp2(m - m_new) * acc_ref[...] + weighted_values(n, p)
        return m_new

    lax.fori_loop(0, qi, body, m0)
    o_t = _finish(acc_ref[...], HEAD_DIM)
    o_ref[0] = jnp.concatenate([o_t[:, hd * BLK:(hd + 1) * BLK] for hd in range(n_heads)],
                               axis=0).T.astype(BF16)


def _moba(qd, kd, vdt, kmean, bias_d):
    b, s, width = qd.shape
    n_heads = width // HEAD_DIM
    n_blocks = s // BLK
    return pl.pallas_call(
        functools.partial(_moba_body, n_heads=n_heads, n_blocks=n_blocks),
        grid=(b, n_blocks),
        in_specs=[
            pl.BlockSpec((1, BLK, width), lambda bi, qi: (bi, qi, 0)),
            pl.BlockSpec((1, s, width), lambda bi, qi: (bi, 0, 0)),
            pl.BlockSpec((1, n_blocks, 2 * width, BLK), lambda bi, qi: (bi, 0, 0, 0)),
            pl.BlockSpec((1, n_blocks, width), lambda bi, qi: (bi, 0, 0)),
            _const_spec(bias_d.shape),
        ],
        out_specs=pl.BlockSpec((1, BLK, width), lambda bi, qi: (bi, qi, 0)),
        out_shape=jax.ShapeDtypeStruct((b, s, width), BF16),
        scratch_shapes=[pltpu.VMEM((16, n_heads * BLK), F32), pltpu.VMEM((VT_ROWS, n_heads * BLK), F32)],
        compiler_params=_cparams(2),
        name="moba",
    )(qd, kd, vdt, kmean.reshape(b, n_blocks, width), bias_d)


def kernel(x, rel_bias_table, l0_ffn1_pre, l0_ffn1_post, l0_ffn1_wg, l0_ffn1_wu, l0_ffn1_wd, l0_mix_pre, l0_mix_post, l0_w_in, l0_sinks, l0_mla_q_norm, l0_mla_w_uq, l0_mla_kv_norm, l0_mla_w_ukv, l0_w_out, l0_ffn2_pre, l0_ffn2_post, l0_ffn2_wg, l0_ffn2_wu, l0_ffn2_wd, l1_ffn1_pre, l1_ffn1_post, l1_ffn1_wg, l1_ffn1_wu, l1_ffn1_wd, l1_mix_pre, l1_mix_post, l1_w_in, l1_nsa_pe_k, l1_nsa_pe_v, l1_nsa_wk1, l1_nsa_wk2, l1_nsa_wv1, l1_nsa_wv2, l1_w_out, l1_ffn2_pre, l1_ffn2_post, l1_ffn2_wg, l1_ffn2_wu, l1_ffn2_wd):
    b, s, d = x.shape
    m = b * s
    x2 = x.reshape(m, d)
    tiles = _bias_tiles(rel_bias_table)
    bias_a = tiles[A_SLOT:A_SLOT + 8, 0:2]
    bias_c = tiles[C_SLOT:C_SLOT + 8]
    bias_d = tiles[D_SLOT:D_SLOT + 8]

    x2 = _ffn(x2, l0_ffn1_pre, l0_ffn1_post, l0_ffn1_wg, l0_ffn1_wu, l0_ffn1_wd, 0.5)
    qa, ka, vat, qb, kb, vbt = _proj0(x2, s, l0_mix_pre, l0_w_in, l0_mla_q_norm, l0_mla_w_uq,
                                      l0_mla_kv_norm, l0_mla_w_ukv)
    shp = lambda t: t.reshape(b, s, t.shape[-1])
    o_a = _swa(shp(qa), shp(ka), vat, bias_a, l0_sinks)
    o_b = _mla(shp(qb), shp(kb), vbt)
    x2 = _outproj(x2, o_a.reshape(m, -1), o_b.reshape(m, -1), l0_w_out, l0_mix_post)
    x2 = _ffn(x2, l0_ffn2_pre, l0_ffn2_post, l0_ffn2_wg, l0_ffn2_wu, l0_ffn2_wd, 0.5)

    x2 = _ffn(x2, l1_ffn1_pre, l1_ffn1_post, l1_ffn1_wg, l1_ffn1_wu, l1_ffn1_wd, 0.5)
    qc, kc_hm, vc_hm, ks, vst, kw, vwt, gates, qd, kd, vdt, kmean = _proj1(x2, b, s, l1_mix_pre, l1_w_in)
    kcmp, vcmpt = _compress(kc_hm, vc_hm, l1_nsa_pe_k, l1_nsa_pe_v, l1_nsa_wk1, l1_nsa_wk2, l1_nsa_wv1, l1_nsa_wv2)
    o_c = _nsa(shp(qc), kcmp, vcmpt, shp(ks), vst, shp(kw), vwt, shp(gates), bias_c)
    o_d = _moba(shp(qd), shp(kd), vdt, kmean, bias_d)
    x2 = _outproj(x2, o_c.reshape(m, -1), o_d.reshape(m, -1), l1_w_out, l1_mix_post)
    x2 = _ffn(x2, l1_ffn2_pre, l1_ffn2_post, l1_ffn2_wg, l1_ffn2_wu, l1_ffn2_wd, 0.5)
    return x2.reshape(b, s, d)
```

```python
import functools
import math

import numpy as np
import jax
import jax.numpy as jnp
from jax import lax
from jax.experimental import pallas as pl
from jax.experimental.pallas import tpu as pltpu

F32 = jnp.float32
BF16 = jnp.bfloat16

D_MODEL = 1024
HEAD_DIM = 64
D_FF = 2816
NORM_EPS = 1e-6
LOG2E = math.log2(math.e)
NEG_INF = -1e30
PAD_SCORE = -3e38
FORCE_SCORE = 1e9
REL_BUCKETS = 32
REL_MAX_DIST = 1024
REL_SLOTS = 24
A_SLOT, C_SLOT, D_SLOT = 0, 8, 16
MLA_NOPE, MLA_ROPE, MLA_V = 64, 32, 64
MLA_HEADS = 8
ROPE_THETA = 10000.0
CMP_BLOCK, CMP_STRIDE = 32, 16
SEL_BLOCK, SEL_TOPK = 64, 8
MOBA_BLOCK, MOBA_TOPK = 256, 3

LANES = 128
TILE = 128
BLK = 256
ONES_ROWS = 16
VT_ROWS = HEAD_DIM + ONES_ROWS
BIAS_TILES = 9
TOKEN_TILE = 512
FF_CHUNK = 256
VMEM_LIMIT = 56 * 1024 * 1024


def _cparams(n_axes):
    return pltpu.CompilerParams(dimension_semantics=("arbitrary",) * n_axes, vmem_limit_bytes=VMEM_LIMIT)


def _dot(a, b):
    return jnp.dot(a, b, preferred_element_type=F32)


def _dot_nt(a, b):
    return lax.dot_general(a, b, (((1,), (1,)), ((), ())), preferred_element_type=F32)


def _rms(x, g):
    return x * lax.rsqrt(jnp.mean(x * x, axis=-1, keepdims=True) + NORM_EPS) * g


def _silu(x):
    return x / (1.0 + jnp.exp(-x))


def _const_spec(shape):
    nd = len(shape)
    return pl.BlockSpec(shape, lambda *_: (0,) * nd)


def _resident_spec(shape):
    nd = len(shape)
    return pl.BlockSpec(shape, lambda *_: (0,) * nd, pipeline_mode=pl.Buffered(1))


def _ffn_tile(x, pre_ref, post_ref, wg_ref, wu_ref, wd_ref, acc_ref, weight):
    h = _rms(x, pre_ref[...]).astype(BF16)
    for c in range(wg_ref.shape[1] // FF_CHUNK):
        cols = slice(c * FF_CHUNK, (c + 1) * FF_CHUNK)
        g = _dot(h, wg_ref[:, cols].astype(BF16))
        u = _dot(h, wu_ref[:, cols].astype(BF16))
        a = (_silu(g) * u).astype(BF16)
        y = _dot(a, wd_ref[cols, :].astype(BF16))
        if c == 0:
            acc_ref[...] = y
        else:
            acc_ref[...] += y
    return x + weight * _rms(acc_ref[...], post_ref[...])


def _ffn_body(x_ref, pre_ref, post_ref, wg_ref, wu_ref, wd_ref, o_ref, acc_ref, *, weight):
    o_ref[...] = _ffn_tile(x_ref[...], pre_ref, post_ref, wg_ref, wu_ref, wd_ref, acc_ref, weight)


def _ffn(x2, pre, post, wg, wu, wd, weight):
    m, d = x2.shape
    d_ff = wg.shape[1]
    tm = TOKEN_TILE
    return pl.pallas_call(
        functools.partial(_ffn_body, weight=weight),
        grid=(m // tm,),
        in_specs=[
            pl.BlockSpec((tm, d), lambda i: (i, 0)),
            _const_spec((1, d)), _const_spec((1, d)),
            _resident_spec((d, d_ff)), _resident_spec((d, d_ff)), _resident_spec((d_ff, d)),
        ],
        out_specs=pl.BlockSpec((tm, d), lambda i: (i, 0)),
        out_shape=jax.ShapeDtypeStruct((m, d), F32),
        scratch_shapes=[pltpu.VMEM((tm, d), F32)],
        compiler_params=_cparams(1),
        name="ffn",
    )(x2, pre.reshape(1, d), post.reshape(1, d), wg, wu, wd)


def _outproj_ffn_body(x_ref, o1_ref, o2_ref, wo_ref, mpost_ref, pre_ref, post_ref, wg_ref, wu_ref, wd_ref,
                      o_ref, acc_ref, *, weight):
    n1 = o1_ref.shape[1]
    y = (_dot(o1_ref[...], wo_ref[0:n1, :].astype(BF16))
         + _dot(o2_ref[...], wo_ref[n1:, :].astype(BF16)))
    x = x_ref[...] + _rms(y, mpost_ref[...])
    o_ref[...] = _ffn_tile(x, pre_ref, post_ref, wg_ref, wu_ref, wd_ref, acc_ref, weight)


def _outproj_ffn(x2, o1, o2, w_out, mix_post, pre, post, wg, wu, wd, weight):
    m, d = x2.shape
    d_ff = wg.shape[1]
    tm = TOKEN_TILE
    tok = lambda n: pl.BlockSpec((tm, n), lambda i: (i, 0))
    return pl.pallas_call(
        functools.partial(_outproj_ffn_body, weight=weight),
        grid=(m // tm,),
        in_specs=[
            tok(d), tok(o1.shape[1]), tok(o2.shape[1]), _resident_spec(w_out.shape), _const_spec((1, d)),
            _const_spec((1, d)), _const_spec((1, d)),
            _resident_spec((d, d_ff)), _resident_spec((d, d_ff)), _resident_spec((d_ff, d)),
        ],
        out_specs=tok(d),
        out_shape=jax.ShapeDtypeStruct((m, d), F32),
        scratch_shapes=[pltpu.VMEM((tm, d), F32)],
        compiler_params=_cparams(1),
        name="outproj_ffn",
    )(x2, o1, o2, w_out, mix_post.reshape(1, d), pre.reshape(1, d), post.reshape(1, d), wg, wu, wd)


def _t5_bucket(dist):
    n = jnp.maximum(dist, 0)
    exact = REL_BUCKETS // 2
    nf = jnp.maximum(n, 1).astype(jnp.float32)
    large = exact + (jnp.log(nf / exact) / math.log(REL_MAX_DIST / exact) * (REL_BUCKETS - exact)).astype(jnp.int32)
    return jnp.where(n < exact, n, jnp.minimum(large, REL_BUCKETS - 1))


def _bias_body(tab_ref, idx_ref, o_ref):
    slot = pl.program_id(0)

    def tile(d, carry):
        idx = idx_ref[d]
        v = jnp.full((TILE, TILE), LOG2E * tab_ref[slot], F32)
        for b in range(1, REL_BUCKETS):
            v = jnp.where(idx == b, LOG2E * tab_ref[b * REL_SLOTS + slot], v)
        o_ref[0, d] = v
        return carry

    lax.fori_loop(0, BIAS_TILES, tile, 0)


def _bias_tiles(rel_table):
    n_slots = rel_table.shape[1]
    d = jnp.arange(BIAS_TILES)[:, None, None]
    key = jnp.arange(TILE)[None, :, None]
    query = jnp.arange(TILE)[None, None, :]
    idx = _t5_bucket(d * TILE + query - key).astype(jnp.int32)
    return pl.pallas_call(
        _bias_body,
        grid=(n_slots,),
        in_specs=[
            pl.BlockSpec(memory_space=pltpu.SMEM),
            _const_spec((BIAS_TILES, TILE, TILE)),
        ],
        out_specs=pl.BlockSpec((1, BIAS_TILES, TILE, TILE), lambda s: (s, 0, 0, 0)),
        out_shape=jax.ShapeDtypeStruct((n_slots, BIAS_TILES, TILE, TILE), F32),
        compiler_params=_cparams(1),
        name="bias_tiles",
    )(rel_table.astype(F32).reshape(-1), idx)


def _proj0_body(x_ref, pre_ref, w_ref, qn_ref, wq1_ref, wq2_ref, kvn_ref, wk_ref, wv_ref, cos_ref, sin_ref,
                qa_ref, ka_ref, vat_ref, qb_ref, kb_ref, vbt_ref):
    h = _rms(x_ref[...], pre_ref[...]).astype(BF16)
    y = _dot(h, w_ref[...])
    qa_ref[...] = (y[:, 0:512] * (LOG2E * HEAD_DIM ** -0.5)).astype(BF16)
    ka_ref[...] = y[:, 512:640].astype(BF16)
    for j in range(y.shape[0] // TILE):
        _store_vt_ones(vat_ref, j, y[j * TILE:(j + 1) * TILE, 640:768], 2)
    cq = _rms(y[:, 768:1024], qn_ref[...]).astype(BF16)
    ckv = _rms(y[:, 1024:1152], kvn_ref[...]).astype(BF16)
    cos_t = cos_ref[...]
    sin_t = sin_ref[...]
    lane = lax.broadcasted_iota(jnp.int32, cos_t.shape, 1)
    scale = LOG2E * (MLA_NOPE + MLA_ROPE) ** -0.5
    qmul = scale * (cos_t + jnp.where(lane < MLA_NOPE, 1.0, 0.0))
    qsin = scale * sin_t
    kr = y[:, 1152:1280] * cos_t + y[:, 1280:1408] * sin_t
    q1 = _dot(cq, wq1_ref[...])
    q2 = _dot(cq, wq2_ref[...])
    k1 = _dot(ckv, wk_ref[...])
    for hd in range(MLA_HEADS):
        sl = slice(hd * LANES, (hd + 1) * LANES)
        qb_ref[:, sl] = (q1[:, sl] * qmul + q2[:, sl] * qsin).astype(BF16)
        kb_ref[:, sl] = (k1[:, sl] + kr).astype(BF16)
    vb = _dot(ckv, wv_ref[...])
    for j in range(vb.shape[0] // BLK):
        _store_vt_ones(vbt_ref, j, vb[j * BLK:(j + 1) * BLK], MLA_HEADS)


def _rope_lane_tables(s):
    inv = ROPE_THETA ** (-jnp.arange(0, MLA_ROPE, 2, dtype=jnp.float32) / MLA_ROPE)
    ang = jnp.arange(s, dtype=jnp.float32)[:, None] * inv[None, :]
    cos, sin = jnp.cos(ang), jnp.sin(ang)
    z_lo = jnp.zeros((s, MLA_NOPE), F32)
    z_hi = jnp.zeros((s, LANES - MLA_NOPE - MLA_ROPE), F32)
    return (jnp.concatenate([z_lo, cos, cos, z_hi], axis=1), jnp.concatenate([z_lo, sin, sin, z_hi], axis=1))


def _rot_cols(w):
    half = w.shape[-1] // 2
    return jnp.concatenate([-w[..., half:], w[..., :half]], axis=-1)


def _proj0(x2, seq, pre, w_in, q_norm, w_uq, kv_norm, w_ukv):
    m, d = x2.shape
    tm = TOKEN_TILE
    kr_w = w_in[:, 1152:1184]
    z64 = jnp.zeros((d, 64), F32)
    z32 = jnp.zeros((d, 32), F32)
    w0 = jnp.concatenate([w_in[:, :1152], z64, kr_w, z32, z64, _rot_cols(kr_w), z32], axis=1).astype(BF16)
    wq = w_uq.reshape(-1, MLA_HEADS, MLA_NOPE + MLA_ROPE)
    rq = wq.shape[0]
    zq = jnp.zeros((rq, MLA_HEADS, 32), F32)
    wq1 = jnp.concatenate([wq, zq], axis=2).reshape(rq, -1).astype(BF16)
    wq2 = jnp.concatenate([jnp.zeros((rq, MLA_HEADS, 64), F32), _rot_cols(wq[:, :, MLA_NOPE:]), zq],
                          axis=2).reshape(rq, -1).astype(BF16)
    wkv = w_ukv.reshape(-1, MLA_HEADS, MLA_NOPE + MLA_V)
    rk = wkv.shape[0]
    wk = jnp.concatenate([wkv[:, :, :MLA_NOPE], jnp.zeros((rk, MLA_HEADS, 64), F32)], axis=2).reshape(rk, -1).astype(BF16)
    wv = wkv[:, :, MLA_NOPE:].reshape(rk, -1).astype(BF16)
    cos_t, sin_t = _rope_lane_tables(seq)
    n_st = seq // tm
    tok = lambda n: pl.BlockSpec((tm, n), lambda i: (i, 0))
    outs = pl.pallas_call(
        _proj0_body,
        grid=(m // tm,),
        in_specs=[
            tok(d), _const_spec((1, d)), _const_spec(w0.shape),
            _const_spec((1, rq)), _const_spec(wq1.shape), _const_spec(wq2.shape),
            _const_spec((1, rk)), _const_spec(wk.shape), _const_spec(wv.shape),
            pl.BlockSpec((tm, LANES), lambda i: (i % n_st, 0)),
            pl.BlockSpec((tm, LANES), lambda i: (i % n_st, 0)),
        ],
        out_specs=[tok(512), tok(128),
                   pl.BlockSpec((1, tm // TILE, 2 * VT_ROWS, TILE), lambda i: (i // n_st, i % n_st, 0, 0)),
                   tok(1024), tok(1024),
                   pl.BlockSpec((1, tm // BLK, MLA_HEADS * VT_ROWS, BLK), lambda i: (i // n_st, i % n_st, 0, 0))],
        out_shape=[jax.ShapeDtypeStruct((m, 512), BF16), jax.ShapeDtypeStruct((m, 128), BF16),
                   jax.ShapeDtypeStruct((m // seq, seq // TILE, 2 * VT_ROWS, TILE), BF16),
                   jax.ShapeDtypeStruct((m, 1024), BF16), jax.ShapeDtypeStruct((m, 1024), BF16),
                   jax.ShapeDtypeStruct((m // seq, seq // BLK, MLA_HEADS * VT_ROWS, BLK), BF16)],
        compiler_params=_cparams(1),
        name="proj0",
    )(x2, pre.reshape(1, d), w0, q_norm.reshape(1, rq), wq1, wq2, kv_norm.reshape(1, rk), wk, wv, cos_t, sin_t)
    return outs


def _swa_body(sink_ref, q_ref, k_ref, vt_ref, bias_ref, o_ref, *, n_kv, grp):
    qi = pl.program_id(1)
    halves = BLK // TILE
    items = [(hkv, half) for hkv in range(n_kv) for half in range(halves)]
    width = grp * TILE
    krow = lax.broadcasted_iota(jnp.int32, (TILE, len(items) * width), 0)
    qcol = lax.broadcasted_iota(jnp.int32, (TILE, len(items) * width), 1) & (TILE - 1)

    cur_s, prev_s, sink_parts, has_prev = [], [], [], []
    for hkv, half in items:
        sl = slice(hkv * HEAD_DIM, (hkv + 1) * HEAD_DIM)
        t = qi * halves + half
        tp = jnp.maximum(t - 1, 0)
        qs = jnp.concatenate([q_ref[0, half * TILE:(half + 1) * TILE, (hkv * grp + g) * HEAD_DIM:
                                    (hkv * grp + g + 1) * HEAD_DIM] for g in range(grp)], axis=0)
        bias_c = jnp.concatenate([bias_ref[hkv * grp + g, 0] for g in range(grp)], axis=1)
        bias_p = jnp.concatenate([bias_ref[hkv * grp + g, 1] for g in range(grp)], axis=1)
        cur_s.append(_dot_nt(k_ref[0, pl.ds(pl.multiple_of(t * TILE, TILE), TILE), sl], qs) + bias_c)
        prev_s.append(_dot_nt(k_ref[0, pl.ds(pl.multiple_of(tp * TILE, TILE), TILE), sl], qs) + bias_p)
        sink_parts.extend(jnp.full((1, TILE), LOG2E * sink_ref[hkv * grp + g], F32) for g in range(grp))
        has_prev.append(jnp.full((1, width), jnp.where(t >= 1, 0, TILE), jnp.int32))
    s_c = jnp.where(krow <= qcol, jnp.concatenate(cur_s, axis=1), NEG_INF)
    s_p = jnp.where(krow > qcol + jnp.concatenate(has_prev, axis=1),
                    jnp.concatenate(prev_s, axis=1), NEG_INF)
    sink = jnp.concatenate(sink_parts, axis=1)
    mx = jnp.maximum(jnp.maximum(jnp.max(s_c, axis=0, keepdims=True), jnp.max(s_p, axis=0, keepdims=True)), sink)
    p_c = jnp.exp2(s_c - mx).astype(BF16)
    p_p = jnp.exp2(s_p - mx).astype(BF16)
    accs = []
    for i, (hkv, half) in enumerate(items):
        vsl = slice(hkv * VT_ROWS, (hkv + 1) * VT_ROWS)
        ls = slice(i * width, (i + 1) * width)
        t = qi * halves + half
        accs.append(_dot(vt_ref[0, t, vsl, :], p_c[:, ls]) + _dot(vt_ref[0, jnp.maximum(t - 1, 0), vsl, :], p_p[:, ls]))
    acc = jnp.concatenate(accs, axis=1)
    o_t = acc[0:HEAD_DIM] / (acc[HEAD_DIM:HEAD_DIM + 1] + jnp.exp2(sink - mx))
    heads = []
    for hkv in range(n_kv):
        for g in range(grp):
            heads.append(jnp.concatenate(
                [o_t[:, (hkv * halves + half) * width + g * TILE:(hkv * halves + half) * width + (g + 1) * TILE]
                 for half in range(halves)], axis=1))
    o_ref[0] = jnp.concatenate(heads, axis=0).T.astype(BF16)


def _swa(qa, ka, vat, bias_a, sinks):
    b, s, width = qa.shape
    n_kv = ka.shape[-1] // HEAD_DIM
    grp = width // HEAD_DIM // n_kv
    return pl.pallas_call(
        functools.partial(_swa_body, n_kv=n_kv, grp=grp),
        grid=(b, s // BLK),
        in_specs=[
            pl.BlockSpec(memory_space=pltpu.SMEM),
            pl.BlockSpec((1, BLK, width), lambda bi, qi: (bi, qi, 0)),
            pl.BlockSpec((1, s, n_kv * HEAD_DIM), lambda bi, qi: (bi, 0, 0)),
            pl.BlockSpec((1, s // TILE, n_kv * VT_ROWS, TILE), lambda bi, qi: (bi, 0, 0, 0)),
            _const_spec(bias_a.shape),
        ],
        out_specs=pl.BlockSpec((1, BLK, width), lambda bi, qi: (bi, qi, 0)),
        out_shape=jax.ShapeDtypeStruct((b, s, width), BF16),
        compiler_params=_cparams(2),
        name="swa",
    )(sinks.astype(F32), qa, ka, vat, bias_a)


def _finish(acc, dv):
    return acc[0:dv] / acc[dv:dv + 1]


def _mla_body(q_ref, k_ref, vt_ref, o_ref, acc_ref, *, n_heads):
    qi = pl.program_id(1)
    lanes = n_heads * BLK
    krow = lax.broadcasted_iota(jnp.int32, (BLK, lanes), 0)
    qcol = lax.broadcasted_iota(jnp.int32, (BLK, lanes), 1) & (BLK - 1)

    def scores(n):
        st = pl.multiple_of(n * BLK, BLK)
        return jnp.concatenate(
            [_dot_nt(k_ref[0, pl.ds(st, BLK), hd * LANES:(hd + 1) * LANES], q_ref[0, :, hd * LANES:(hd + 1) * LANES])
             for hd in range(n_heads)], axis=1)

    def weighted_values(n, p):
        return jnp.concatenate(
            [_dot(vt_ref[0, n, hd * VT_ROWS:(hd + 1) * VT_ROWS, :], p[:, hd * BLK:(hd + 1) * BLK])
             for hd in range(n_heads)], axis=1)

    s = jnp.where(krow <= qcol, scores(qi), NEG_INF)
    m0 = jnp.max(s, axis=0, keepdims=True)
    acc_ref[...] = weighted_values(qi, jnp.exp2(s - m0).astype(BF16))

    def body(n, m):
        s = scores(n)
        m_new = jnp.maximum(m, jnp.max(s, axis=0, keepdims=True))
        acc_ref[...] = jnp.exp2(m - m_new) * acc_ref[...] + weighted_values(n, jnp.exp2(s - m_new).astype(BF16))
        return m_new

    lax.fori_loop(0, qi, body, m0)
    o_t = _finish(acc_ref[...], MLA_V)
    o_ref[0] = jnp.concatenate([o_t[:, hd * BLK:(hd + 1) * BLK] for hd in range(n_heads)],
                               axis=0).T.astype(BF16)


def _mla(qb, kb, vbt):
    b, s, width = qb.shape
    n_heads = width // LANES
    n_blocks = s // BLK
    return pl.pallas_call(
        functools.partial(_mla_body, n_heads=n_heads),
        grid=(b, n_blocks),
        in_specs=[
            pl.BlockSpec((1, BLK, width), lambda bi, qi: (bi, qi, 0)),
            pl.BlockSpec((1, s, width), lambda bi, qi: (bi, 0, 0)),
            pl.BlockSpec((1, n_blocks, n_heads * VT_ROWS, BLK), lambda bi, qi: (bi, 0, 0, 0)),
        ],
        out_specs=pl.BlockSpec((1, BLK, n_heads * MLA_V), lambda bi, qi: (bi, qi, 0)),
        out_shape=jax.ShapeDtypeStruct((b, s, n_heads * MLA_V), BF16),
        scratch_shapes=[pltpu.VMEM((VT_ROWS, n_heads * BLK), F32)],
        compiler_params=_cparams(2),
        name="mla",
    )(qb, kb, vbt)


def _store_vt_ones(vt_ref, j, v, n_heads):
    vt = v.T.astype(BF16)
    ones = jnp.ones((ONES_ROWS, v.shape[0]), BF16)
    for hd in range(n_heads):
        vt_ref[0, j, hd * VT_ROWS:hd * VT_ROWS + HEAD_DIM, :] = vt[hd * HEAD_DIM:(hd + 1) * HEAD_DIM]
        vt_ref[0, j, hd * VT_ROWS + HEAD_DIM:(hd + 1) * VT_ROWS, :] = ones


def _proj1_body(x_ref, pre_ref, w_ref, qc_ref, kc_ref, vc_ref, ks_ref, vst_ref, kw_ref, vwt_ref, g_ref,
                qd_ref, kd_ref, vdt_ref, km_ref, *, tm):
    h = _rms(x_ref[...], pre_ref[...]).astype(BF16)
    y = _dot(h, w_ref[...])
    scale = LOG2E * HEAD_DIM ** -0.5
    qc_ref[...] = (y[:, 0:512] * scale).astype(BF16)
    for hkv in range(2):
        kc_ref[0, hkv] = y[:, 512 + hkv * 64:576 + hkv * 64].astype(BF16)
        vc_ref[0, hkv] = y[:, 640 + hkv * 64:704 + hkv * 64].astype(BF16)
    ks_ref[...] = y[:, 768:896].astype(BF16)
    vs = y[:, 896:1024]
    kw_ref[...] = y[:, 1024:1152].astype(BF16)
    vw = y[:, 1152:1280]
    gl = y[:, 1280:1408]
    g_ref[...] = 1.0 / (1.0 + jnp.exp(-gl))
    qd_ref[...] = (y[:, 1408:1920] * scale).astype(BF16)
    kd = y[:, 1920:2432]
    kd_ref[...] = kd.astype(BF16)
    vd = y[:, 2432:2944]
    for j in range(tm // BLK):
        rows = slice(j * BLK, (j + 1) * BLK)
        km_ref[j] = jnp.mean(kd[rows], axis=0, keepdims=True)
        _store_vt_ones(vdt_ref, j, vd[rows], 8)
        _store_vt_ones(vst_ref, j, vs[rows], 2)
        _store_vt_ones(vwt_ref, j, vw[rows], 2)


def _proj1(x2, batch, seq, pre, w_in):
    m, d = x2.shape
    tm = TOKEN_TILE
    n_st = seq // tm
    pad = jnp.zeros((d, LANES - 24), F32)
    w1 = jnp.concatenate([w_in[:, :1304], pad, w_in[:, 1304:]], axis=1).astype(BF16)
    tok = lambda n: pl.BlockSpec((tm, n), lambda i: (i, 0))
    hm = pl.BlockSpec((1, 2, tm, HEAD_DIM), lambda i: (i // n_st, 0, i % n_st, 0))
    nblk = tm // BLK
    vt_spec = lambda rows: pl.BlockSpec((1, nblk, rows, BLK), lambda i: (i // n_st, i % n_st, 0, 0))
    vt_shape = lambda rows: jax.ShapeDtypeStruct((batch, seq // BLK, rows, BLK), BF16)
    return pl.pallas_call(
        functools.partial(_proj1_body, tm=tm),
        grid=(m // tm,),
        in_specs=[tok(d), _const_spec((1, d)), _const_spec(w1.shape)],
        out_specs=[tok(512), hm, hm, tok(128), vt_spec(2 * VT_ROWS), tok(128), vt_spec(2 * VT_ROWS), tok(128), tok(512), tok(512),
                   vt_spec(8 * VT_ROWS),
                   pl.BlockSpec((nblk, 1, 512), lambda i: (i, 0, 0))],
        out_shape=[
            jax.ShapeDtypeStruct((m, 512), BF16),
            jax.ShapeDtypeStruct((batch, 2, seq, HEAD_DIM), BF16),
            jax.ShapeDtypeStruct((batch, 2, seq, HEAD_DIM), BF16),
            jax.ShapeDtypeStruct((m, 128), BF16), vt_shape(2 * VT_ROWS),
            jax.ShapeDtypeStruct((m, 128), BF16), vt_shape(2 * VT_ROWS),
            jax.ShapeDtypeStruct((m, 128), F32),
            jax.ShapeDtypeStruct((m, 512), BF16), jax.ShapeDtypeStruct((m, 512), BF16),
            vt_shape(8 * VT_ROWS),
            jax.ShapeDtypeStruct((m // MOBA_BLOCK, 1, 512), F32),
        ],
        compiler_params=_cparams(1),
        name="proj1",
    )(x2, pre.reshape(1, d), w1)


def _compress_body(kx_ref, vx_ref, pek_ref, pev_ref, wk1_ref, wk2_ref, wv1_ref, wv2_ref, ko_ref, vo_ref):
    half = CMP_STRIDE * HEAD_DIM
    for x_ref, pe_ref, w1_ref, w2_ref, o_ref in ((kx_ref, pek_ref, wk1_ref, wk2_ref, ko_ref),
                                                 (vx_ref, pev_ref, wv1_ref, wv2_ref, vo_ref)):
        pe8 = jnp.broadcast_to(pe_ref[...], (8, 2 * half)).astype(BF16)
        pe_term = _dot(pe8, w1_ref[...])[0:1]
        outs = []
        for hkv in range(2):
            x = x_ref[0, hkv]
            lo = _dot(x, w1_ref[0:half])
            hi = _dot(x, w1_ref[half:2 * half])
            hid = lo + pltpu.roll(hi, x.shape[0] - 1, 0) + pe_term
            outs.append(_dot(_silu(hid).astype(BF16), w2_ref[...]))
        out = jnp.concatenate(outs, axis=1)
        o_ref[0] = (out.T if o_ref is vo_ref else out).astype(BF16)


def _compress(kc_hm, vc_hm, pe_k, pe_v, wk1, wk2, wv1, wv2):
    b, _, s, dh = kc_hm.shape
    rows = s // CMP_STRIDE
    feat = CMP_STRIDE * dh
    kx = kc_hm.reshape(b, 2, rows, feat)
    vx = vc_hm.reshape(b, 2, rows, feat)
    xspec = pl.BlockSpec((1, 2, rows, feat), lambda bi: (bi, 0, 0, 0))
    ospec = pl.BlockSpec((1, rows, 2 * dh), lambda bi: (bi, 0, 0))
    w1s, w2s = wk1.shape, wk2.shape
    return pl.pallas_call(
        _compress_body,
        grid=(b,),
        in_specs=[xspec, xspec, _const_spec((1, 2 * feat)), _const_spec((1, 2 * feat)),
                  _const_spec(w1s), _const_spec(w2s), _const_spec(w1s), _const_spec(w2s)],
        out_specs=[ospec, ospec],
        out_shape=[jax.ShapeDtypeStruct((b, rows, 2 * dh), BF16)] * 2,
        compiler_params=_cparams(1),
        name="nsa_compress",
    )(kx, vx, pe_k.astype(F32).reshape(1, -1), pe_v.astype(F32).reshape(1, -1),
      wk1.astype(BF16), wk2.astype(BF16), wv1.astype(BF16), wv2.astype(BF16))


def _bias_block(bias_ref, hd, dist_blocks):
    sub = BLK // TILE
    rows = []
    for a in range(sub):
        cols = []
        for bq in range(sub):
            e = sub * dist_blocks + bq - a
            e = max(e, 0) if isinstance(e, int) else jnp.clip(e, 0, BIAS_TILES - 1)
            cols.append(bias_ref[hd, e])
        rows.append(jnp.concatenate(cols, axis=1))
    return jnp.concatenate(rows, axis=0)


def _rank_rows(val, ridx, n_real):
    rank = jnp.zeros(val.shape, F32)
    for j in range(n_real):
        vj = val[j:j + 1, :]
        rank = rank + jnp.where((vj > val) | ((vj == val) & (ridx > j)), 1.0, 0.0)
    return rank


def _nsa_body(q_ref, kc_ref, vct_ref, ks_ref, vst_ref, kw_ref, vwt_ref, g_ref, bias_ref, ovt_ref, o_ref,
              pen_ref, acc_ref, *, n_sel):
    qi = pl.program_id(1)
    grp = 4
    lanes = grp * BLK
    sub = BLK // SEL_BLOCK
    own = pl.multiple_of(qi * BLK, BLK)
    prev = pl.multiple_of(jnp.maximum(qi - 1, 0) * BLK, BLK)
    prev_blk = jnp.maximum(qi - 1, 0)
    krow = lax.broadcasted_iota(jnp.int32, (BLK, lanes), 0)
    qcol = lax.broadcasted_iota(jnp.int32, (BLK, lanes), 1) & (BLK - 1)
    causal = krow <= qcol
    crow = lax.broadcasted_iota(jnp.int32, (TILE, lanes), 0)
    cq = lax.broadcasted_iota(jnp.int32, (TILE, lanes), 1) & (BLK - 1)
    visible = CMP_STRIDE * crow + (CMP_BLOCK - 1) <= qi * BLK + cq
    jrow = lax.broadcasted_iota(jnp.int32, (n_sel, BLK), 0)
    cur = qi * sub + (lax.broadcasted_iota(jnp.int32, (n_sel, BLK), 1) >> 6)
    forced = (jrow == 0) | (jrow == cur) | (jrow == cur - 1)
    gates_t = g_ref[0].T
    has_prev = jnp.where(qi >= 1, 0, BLK)

    def stack_q(hkv):
        return jnp.concatenate(
            [q_ref[0, :, (hkv * grp + g) * HEAD_DIM:(hkv * grp + g + 1) * HEAD_DIM] for g in range(grp)], axis=0)

    def group_bias(hkv, dist_blocks):
        return jnp.concatenate([_bias_block(bias_ref, hkv * grp + g, dist_blocks) for g in range(grp)], axis=1)

    def sel_block(hkv, n, dist_blocks):
        st = pl.multiple_of(n * BLK, BLK)
        sl = slice(hkv * HEAD_DIM, (hkv + 1) * HEAD_DIM)
        s = _dot_nt(ks_ref[0, pl.ds(st, BLK), sl], stack_q(hkv)) + group_bias(hkv, dist_blocks)
        pens = pen_ref[hkv, n, 0:sub, :]
        return s, [jnp.concatenate([pens[j:j + 1]] * grp, axis=1) for j in range(sub)]

    def sel_update(s, pens, m_old):
        parts = [s[j * SEL_BLOCK:(j + 1) * SEL_BLOCK] for j in range(sub)]
        m_new = m_old
        for j in range(sub):
            mj = jnp.max(parts[j], axis=0, keepdims=True) + pens[j]
            m_new = mj if m_new is None else jnp.maximum(m_new, mj)
        p = jnp.concatenate([jnp.exp2(parts[j] - (m_new - pens[j])) for j in range(sub)], axis=0)
        return m_new, p.astype(BF16)

    o_cmp, o_win, ms = [], [], []
    for hkv in range(2):
        sl = slice(hkv * HEAD_DIM, (hkv + 1) * HEAD_DIM)
        vsl = slice(hkv * VT_ROWS, (hkv + 1) * VT_ROWS)
        qs = stack_q(hkv)

        sc = jnp.where(visible, _dot_nt(kc_ref[0, :, sl], qs), NEG_INF)
        mc = jnp.max(sc, axis=0, keepdims=True)
        ec = jnp.where(visible, jnp.exp2(sc - mc), 0.0)
        lc = jnp.sum(ec, axis=0, keepdims=True)
        p = ec / jnp.where(lc > 0.0, lc, 1.0)
        o_cmp.append(_dot(vct_ref[0, sl, :], p.astype(BF16)))

        ps = p[:, 0:BLK] + p[:, BLK:2 * BLK] + p[:, 2 * BLK:3 * BLK] + p[:, 3 * BLK:4 * BLK]
        ps_hi = ps.astype(BF16)
        ps_lo = (ps - ps_hi.astype(F32)).astype(BF16)
        imp = _dot(ovt_ref[...], ps_hi) + _dot(ovt_ref[...], ps_lo)
        val = jnp.where(forced, FORCE_SCORE, jnp.where(jrow <= cur, imp, NEG_INF))
        pen = jnp.where(_rank_rows(val, jrow, n_sel) < SEL_TOPK, 0.0, NEG_INF)
        for t in range(n_sel // sub):
            pen_ref[hkv, t, 0:sub, :] = pen[t * sub:(t + 1) * sub]

        s, pens = sel_block(hkv, qi, 0)
        m, pb = sel_update(jnp.where(causal, s, NEG_INF), pens, None)
        acc_ref[hkv] = _dot(vst_ref[0, qi, vsl, :], pb)
        ms.append(m)

        s0 = _dot_nt(kw_ref[0, pl.ds(own, BLK), sl], qs) + group_bias(hkv, 0)
        s1 = _dot_nt(kw_ref[0, pl.ds(prev, BLK), sl], qs) + group_bias(hkv, 1)
        s0 = jnp.where(causal, s0, NEG_INF)
        s1 = jnp.where(krow > qcol + has_prev, s1, NEG_INF)
        mw = jnp.maximum(jnp.max(s0, axis=0, keepdims=True), jnp.max(s1, axis=0, keepdims=True))
        ow = (_dot(vwt_ref[0, qi, vsl, :], jnp.exp2(s0 - mw).astype(BF16))
              + _dot(vwt_ref[0, prev_blk, vsl, :], jnp.exp2(s1 - mw).astype(BF16)))
        o_win.append(_finish(ow, HEAD_DIM))

    def body(n, ms):
        new = []
        for hkv in range(2):
            vsl = slice(hkv * VT_ROWS, (hkv + 1) * VT_ROWS)
            s, pens = sel_block(hkv, n, qi - n)
            m_new, pb = sel_update(s, pens, ms[hkv])
            acc_ref[hkv] = jnp.exp2(ms[hkv] - m_new) * acc_ref[hkv] + _dot(vst_ref[0, n, vsl, :], pb)
            new.append(m_new)
        return tuple(new)

    lax.fori_loop(0, qi, body, tuple(ms))
    outs = []
    for hkv in range(2):
        o_slc = _finish(acc_ref[hkv], HEAD_DIM)
        for g in range(grp):
            hd = hkv * grp + g
            ls = slice(g * BLK, (g + 1) * BLK)
            outs.append(gates_t[hd:hd + 1] * o_cmp[hkv][:, ls] + gates_t[8 + hd:9 + hd] * o_slc[:, ls]
                        + gates_t[16 + hd:17 + hd] * o_win[hkv][:, ls])
    o_ref[0] = jnp.concatenate(outs, axis=0).T.astype(BF16)


def _nsa(qc, kcmp, vcmpt, ks, vst, kw, vwt, gates, bias_c):
    b, s, width = qc.shape
    n_blocks = s // BLK
    n_sel = s // SEL_BLOCK
    n_c = (s - CMP_BLOCK) // CMP_STRIDE + 1
    j_start = np.arange(n_sel)[:, None] * SEL_BLOCK
    c_start = np.arange(TILE)[None, :] * CMP_STRIDE
    overlap_t = ((c_start < j_start + SEL_BLOCK) & (c_start + CMP_BLOCK > j_start)
                 & (np.arange(TILE)[None, :] < n_c))
    ovt = jnp.asarray(overlap_t, BF16)
    qtile = lambda n: pl.BlockSpec((1, BLK, n), lambda bi, qi: (bi, qi, 0))
    full = pl.BlockSpec((1, s, 128), lambda bi, qi: (bi, 0, 0))
    full_t = pl.BlockSpec((1, n_blocks, 2 * VT_ROWS, BLK), lambda bi, qi: (bi, 0, 0, 0))
    cmp_spec = pl.BlockSpec((1, TILE, 128), lambda bi, qi: (bi, 0, 0))
    return pl.pallas_call(
        functools.partial(_nsa_body, n_sel=n_sel),
        grid=(b, n_blocks),
        in_specs=[qtile(width), cmp_spec, cmp_spec, full, full_t, full, full_t, qtile(128),
                  _const_spec(bias_c.shape), _const_spec(ovt.shape)],
        out_specs=qtile(width),
        out_shape=jax.ShapeDtypeStruct((b, s, width), BF16),
        scratch_shapes=[pltpu.VMEM((2, n_blocks, 8, BLK), F32), pltpu.VMEM((2, VT_ROWS, 4 * BLK), F32)],
        compiler_params=_cparams(2),
        name="nsa",
    )(qc, kcmp, vcmpt, ks, vst, kw, vwt, gates, bias_c, ovt)


def _moba_body(q_ref, k_ref, vt_ref, km_ref, bias_ref, o_ref, pen_ref, acc_ref, *, n_heads, n_blocks):
    qi = pl.program_id(1)
    lanes = n_heads * BLK
    nidx = lax.broadcasted_iota(jnp.int32, (16, lanes), 0)
    krow = lax.broadcasted_iota(jnp.int32, (BLK, lanes), 0)
    qcol = lax.broadcasted_iota(jnp.int32, (BLK, lanes), 1) & (BLK - 1)

    def scores(n, dist_blocks):
        st = pl.multiple_of(n * BLK, BLK)
        parts = []
        for hd in range(n_heads):
            sl = slice(hd * HEAD_DIM, (hd + 1) * HEAD_DIM)
            parts.append(_dot_nt(k_ref[0, pl.ds(st, BLK), sl], q_ref[0, :, sl])
                         + _bias_block(bias_ref, hd, dist_blocks))
        return jnp.concatenate(parts, axis=1)

    def weighted_values(n, p):
        return jnp.concatenate(
            [_dot(vt_ref[0, n, hd * VT_ROWS:(hd + 1) * VT_ROWS, :], p[:, hd * BLK:(hd + 1) * BLK])
             for hd in range(n_heads)], axis=1)

    zpad = jnp.zeros((16 - n_blocks, HEAD_DIM), F32)
    gate = jnp.concatenate(
        [_dot_nt(jnp.concatenate([km_ref[0, :, hd * HEAD_DIM:(hd + 1) * HEAD_DIM], zpad], axis=0).astype(BF16),
                 q_ref[0, :, hd * HEAD_DIM:(hd + 1) * HEAD_DIM]) for hd in range(n_heads)], axis=1)
    val = jnp.where(nidx < qi, gate, NEG_INF)
    val = jnp.where(nidx < n_blocks, val, PAD_SCORE)
    keep = (_rank_rows(val, nidx, n_blocks) < MOBA_TOPK) & (nidx < qi)
    pen_ref[...] = jnp.where(keep, 0.0, NEG_INF)

    s = jnp.where(krow <= qcol, scores(qi, 0), NEG_INF)
    m0 = jnp.max(s, axis=0, keepdims=True)
    acc_ref[...] = weighted_values(qi, jnp.exp2(s - m0).astype(BF16))

    def body(n, m):
        s = scores(n, qi - n)
        pen = pen_ref[pl.ds(n, 1), :]
        m_new = jnp.maximum(m, jnp.max(s, axis=0, keepdims=True) + pen)
        p = jnp.exp2(s - (m_new - pen)).astype(BF16)
        acc_ref[...] = jnp.exp2(m - m_new) * acc_ref[...] + weighted_values(n, p)
        return m_new

    lax.fori_loop(0, qi, body, m0)
    o_t = _finish(acc_ref[...], HEAD_DIM)
    o_ref[0] = jnp.concatenate([o_t[:, hd * BLK:(hd + 1) * BLK] for hd in range(n_heads)],
                               axis=0).T.astype(BF16)


def _moba(qd, kd, vdt, kmean, bias_d):
    b, s, width = qd.shape
    n_heads = width // HEAD_DIM
    n_blocks = s // BLK
    return pl.pallas_call(
        functools.partial(_moba_body, n_heads=n_heads, n_blocks=n_blocks),
        grid=(b, n_blocks),
        in_specs=[
            pl.BlockSpec((1, BLK, width), lambda bi, qi: (bi, qi, 0)),
            pl.BlockSpec((1, s, width), lambda bi, qi: (bi, 0, 0)),
            pl.BlockSpec((1, n_blocks, 2 * width, BLK), lambda bi, qi: (bi, 0, 0, 0)),
            pl.BlockSpec((1, n_blocks, width), lambda bi, qi: (bi, 0, 0)),
            _const_spec(bias_d.shape),
        ],
        out_specs=pl.BlockSpec((1, BLK, width), lambda bi, qi: (bi, qi, 0)),
        out_shape=jax.ShapeDtypeStruct((b, s, width), BF16),
        scratch_shapes=[pltpu.VMEM((16, n_heads * BLK), F32), pltpu.VMEM((VT_ROWS, n_heads * BLK), F32)],
        compiler_params=_cparams(2),
        name="moba",
    )(qd, kd, vdt, kmean.reshape(b, n_blocks, width), bias_d)


def kernel(x, rel_bias_table, l0_ffn1_pre, l0_ffn1_post, l0_ffn1_wg, l0_ffn1_wu, l0_ffn1_wd, l0_mix_pre, l0_mix_post, l0_w_in, l0_sinks, l0_mla_q_norm, l0_mla_w_uq, l0_mla_kv_norm, l0_mla_w_ukv, l0_w_out, l0_ffn2_pre, l0_ffn2_post, l0_ffn2_wg, l0_ffn2_wu, l0_ffn2_wd, l1_ffn1_pre, l1_ffn1_post, l1_ffn1_wg, l1_ffn1_wu, l1_ffn1_wd, l1_mix_pre, l1_mix_post, l1_w_in, l1_nsa_pe_k, l1_nsa_pe_v, l1_nsa_wk1, l1_nsa_wk2, l1_nsa_wv1, l1_nsa_wv2, l1_w_out, l1_ffn2_pre, l1_ffn2_post, l1_ffn2_wg, l1_ffn2_wu, l1_ffn2_wd):
    b, s, d = x.shape
    m = b * s
    x2 = x.reshape(m, d)
    tiles = _bias_tiles(rel_bias_table)
    bias_a = tiles[A_SLOT:A_SLOT + 8, 0:2]
    bias_c = tiles[C_SLOT:C_SLOT + 8]
    bias_d = tiles[D_SLOT:D_SLOT + 8]

    x2 = _ffn(x2, l0_ffn1_pre, l0_ffn1_post, l0_ffn1_wg, l0_ffn1_wu, l0_ffn1_wd, 0.5)
    qa, ka, vat, qb, kb, vbt = _proj0(x2, s, l0_mix_pre, l0_w_in, l0_mla_q_norm, l0_mla_w_uq,
                                      l0_mla_kv_norm, l0_mla_w_ukv)
    shp = lambda t: t.reshape(b, s, t.shape[-1])
    o_a = _swa(shp(qa), shp(ka), vat, bias_a, l0_sinks)
    o_b = _mla(shp(qb), shp(kb), vbt)
    x2 = _outproj_ffn(x2, o_a.reshape(m, -1), o_b.reshape(m, -1), l0_w_out, l0_mix_post,
                      l0_ffn2_pre, l0_ffn2_post, l0_ffn2_wg, l0_ffn2_wu, l0_ffn2_wd, 0.5)

    x2 = _ffn(x2, l1_ffn1_pre, l1_ffn1_post, l1_ffn1_wg, l1_ffn1_wu, l1_ffn1_wd, 0.5)
    qc, kc_hm, vc_hm, ks, vst, kw, vwt, gates, qd, kd, vdt, kmean = _proj1(x2, b, s, l1_mix_pre, l1_w_in)
    kcmp, vcmpt = _compress(kc_hm, vc_hm, l1_nsa_pe_k, l1_nsa_pe_v, l1_nsa_wk1, l1_nsa_wk2, l1_nsa_wv1, l1_nsa_wv2)
    o_c = _nsa(shp(qc), kcmp, vcmpt, shp(ks), vst, shp(kw), vwt, shp(gates), bias_c)
    o_d = _moba(shp(qd), shp(kd), vdt, kmean, bias_d)
    x2 = _outproj_ffn(x2, o_c.reshape(m, -1), o_d.reshape(m, -1), l1_w_out, l1_mix_post,
                      l1_ffn2_pre, l1_ffn2_post, l1_ffn2_wg, l1_ffn2_wu, l1_ffn2_wd, 0.5)
    return x2.reshape(b, s, d)
```

```python
import functools
import math

import numpy as np
import jax
import jax.numpy as jnp
from jax import lax
from jax.experimental import pallas as pl
from jax.experimental.pallas import tpu as pltpu

F32 = jnp.float32
BF16 = jnp.bfloat16

D_MODEL = 1024
HEAD_DIM = 64
D_FF = 2816
NORM_EPS = 1e-6
LOG2E = math.log2(math.e)
NEG_INF = -1e30
PAD_SCORE = -3e38
FORCE_SCORE = 1e9
REL_BUCKETS = 32
REL_MAX_DIST = 1024
REL_SLOTS = 24
A_SLOT, C_SLOT, D_SLOT = 0, 8, 16
MLA_NOPE, MLA_ROPE, MLA_V = 64, 32, 64
MLA_HEADS = 8
ROPE_THETA = 10000.0
CMP_BLOCK, CMP_STRIDE = 32, 16
SEL_BLOCK, SEL_TOPK = 64, 8
MOBA_BLOCK, MOBA_TOPK = 256, 3

LANES = 128
TILE = 128
BLK = 256
ONES_ROWS = 16
VT_ROWS = HEAD_DIM + ONES_ROWS
BIAS_TILES = 9
TOKEN_TILE = 512
FF_CHUNK = 256
VMEM_LIMIT = 56 * 1024 * 1024


def _cparams(n_axes):
    return pltpu.CompilerParams(dimension_semantics=("arbitrary",) * n_axes, vmem_limit_bytes=VMEM_LIMIT)


def _dot(a, b):
    return jnp.dot(a, b, preferred_element_type=F32)


def _dot_nt(a, b):
    return lax.dot_general(a, b, (((1,), (1,)), ((), ())), preferred_element_type=F32)


def _rms(x, g):
    return x * lax.rsqrt(jnp.mean(x * x, axis=-1, keepdims=True) + NORM_EPS) * g


def _silu(x):
    return x / (1.0 + jnp.exp(-x))


def _const_spec(shape):
    nd = len(shape)
    return pl.BlockSpec(shape, lambda *_: (0,) * nd)


def _loop_in_pairs(count, body, carry):
    carry = lax.fori_loop(0, count // 2, lambda i, c: body(2 * i + 1, body(2 * i, c)), carry)
    return lax.cond(count % 2 == 1, lambda c: body(count - 1, c), lambda c: c, carry)


def _resident_spec(shape):
    nd = len(shape)
    return pl.BlockSpec(shape, lambda *_: (0,) * nd, pipeline_mode=pl.Buffered(1))


def _ffn_tile(x, pre_ref, post_ref, wg_ref, wu_ref, wd_ref, acc_ref, weight):
    h = _rms(x, pre_ref[...]).astype(BF16)
    for c in range(wg_ref.shape[1] // FF_CHUNK):
        cols = slice(c * FF_CHUNK, (c + 1) * FF_CHUNK)
        g = _dot(h, wg_ref[:, cols].astype(BF16))
        u = _dot(h, wu_ref[:, cols].astype(BF16))
        a = (_silu(g) * u).astype(BF16)
        y = _dot(a, wd_ref[cols, :].astype(BF16))
        if c == 0:
            acc_ref[...] = y
        else:
            acc_ref[...] += y
    return x + weight * _rms(acc_ref[...], post_ref[...])


def _ffn_body(x_ref, pre_ref, post_ref, wg_ref, wu_ref, wd_ref, o_ref, acc_ref, *, weight):
    o_ref[...] = _ffn_tile(x_ref[...], pre_ref, post_ref, wg_ref, wu_ref, wd_ref, acc_ref, weight)


def _ffn(x2, pre, post, wg, wu, wd, weight):
    m, d = x2.shape
    d_ff = wg.shape[1]
    tm = TOKEN_TILE
    return pl.pallas_call(
        functools.partial(_ffn_body, weight=weight),
        grid=(m // tm,),
        in_specs=[
            pl.BlockSpec((tm, d), lambda i: (i, 0)),
            _const_spec((1, d)), _const_spec((1, d)),
            _resident_spec((d, d_ff)), _resident_spec((d, d_ff)), _resident_spec((d_ff, d)),
        ],
        out_specs=pl.BlockSpec((tm, d), lambda i: (i, 0)),
        out_shape=jax.ShapeDtypeStruct((m, d), F32),
        scratch_shapes=[pltpu.VMEM((tm, d), F32)],
        compiler_params=_cparams(1),
        name="ffn",
    )(x2, pre.reshape(1, d), post.reshape(1, d), wg, wu, wd)


def _outproj_ffn_body(x_ref, o1_ref, o2_ref, wo_ref, mpost_ref, pre_ref, post_ref, wg_ref, wu_ref, wd_ref,
                      o_ref, acc_ref, *, weight):
    n1 = o1_ref.shape[1]
    y = (_dot(o1_ref[...], wo_ref[0:n1, :].astype(BF16))
         + _dot(o2_ref[...], wo_ref[n1:, :].astype(BF16)))
    x = x_ref[...] + _rms(y, mpost_ref[...])
    o_ref[...] = _ffn_tile(x, pre_ref, post_ref, wg_ref, wu_ref, wd_ref, acc_ref, weight)


def _outproj_ffn(x2, o1, o2, w_out, mix_post, pre, post, wg, wu, wd, weight):
    m, d = x2.shape
    d_ff = wg.shape[1]
    tm = TOKEN_TILE
    tok = lambda n: pl.BlockSpec((tm, n), lambda i: (i, 0))
    return pl.pallas_call(
        functools.partial(_outproj_ffn_body, weight=weight),
        grid=(m // tm,),
        in_specs=[
            tok(d), tok(o1.shape[1]), tok(o2.shape[1]), _resident_spec(w_out.shape), _const_spec((1, d)),
            _const_spec((1, d)), _const_spec((1, d)),
            _resident_spec((d, d_ff)), _resident_spec((d, d_ff)), _resident_spec((d_ff, d)),
        ],
        out_specs=tok(d),
        out_shape=jax.ShapeDtypeStruct((m, d), F32),
        scratch_shapes=[pltpu.VMEM((tm, d), F32)],
        compiler_params=_cparams(1),
        name="outproj_ffn",
    )(x2, o1, o2, w_out, mix_post.reshape(1, d), pre.reshape(1, d), post.reshape(1, d), wg, wu, wd)


def _t5_bucket(dist):
    n = jnp.maximum(dist, 0)
    exact = REL_BUCKETS // 2
    nf = jnp.maximum(n, 1).astype(jnp.float32)
    large = exact + (jnp.log(nf / exact) / math.log(REL_MAX_DIST / exact) * (REL_BUCKETS - exact)).astype(jnp.int32)
    return jnp.where(n < exact, n, jnp.minimum(large, REL_BUCKETS - 1))


def _tile_bucket_range(d):
    dist = np.arange(max(d * TILE - (TILE - 1), 0), d * TILE + TILE)
    exact = REL_BUCKETS // 2
    large = exact + np.log(np.maximum(dist, 1) / exact) / math.log(REL_MAX_DIST / exact) * (REL_BUCKETS - exact)
    bucket = np.where(dist < exact, dist, np.minimum(large.astype(np.int64), REL_BUCKETS - 1))
    return max(int(bucket.min()) - 1, 0), min(int(bucket.max()) + 1, REL_BUCKETS - 1)


def _bias_body(tab_ref, idx_ref, o_ref):
    slot = pl.program_id(0)
    for d in range(BIAS_TILES):
        idx = idx_ref[d]
        lo, hi = _tile_bucket_range(d)
        v = jnp.full((TILE, TILE), LOG2E * tab_ref[lo * REL_SLOTS + slot], F32)
        for b in range(lo + 1, hi + 1):
            v = jnp.where(idx == b, LOG2E * tab_ref[b * REL_SLOTS + slot], v)
        o_ref[0, d] = v


def _bias_tiles(rel_table):
    n_slots = rel_table.shape[1]
    d = jnp.arange(BIAS_TILES)[:, None, None]
    key = jnp.arange(TILE)[None, :, None]
    query = jnp.arange(TILE)[None, None, :]
    idx = _t5_bucket(d * TILE + query - key).astype(jnp.int32)
    return pl.pallas_call(
        _bias_body,
        grid=(n_slots,),
        in_specs=[
            pl.BlockSpec(memory_space=pltpu.SMEM),
            _const_spec((BIAS_TILES, TILE, TILE)),
        ],
        out_specs=pl.BlockSpec((1, BIAS_TILES, TILE, TILE), lambda s: (s, 0, 0, 0)),
        out_shape=jax.ShapeDtypeStruct((n_slots, BIAS_TILES, TILE, TILE), F32),
        compiler_params=_cparams(1),
        name="bias_tiles",
    )(rel_table.astype(F32).reshape(-1), idx)


def _proj0_body(x_ref, pre_ref, w_ref, qn_ref, wq1_ref, wq2_ref, kvn_ref, wk_ref, wv_ref, cos_ref, sin_ref,
                qa_ref, ka_ref, vat_ref, qb_ref, kb_ref, vbt_ref):
    h = _rms(x_ref[...], pre_ref[...]).astype(BF16)
    y = _dot(h, w_ref[...])
    qa_ref[...] = (y[:, 0:512] * (LOG2E * HEAD_DIM ** -0.5)).astype(BF16)
    ka_ref[...] = y[:, 512:640].astype(BF16)
    for j in range(y.shape[0] // TILE):
        _store_vt_ones(vat_ref, j, y[j * TILE:(j + 1) * TILE, 640:768], 2)
    cq = _rms(y[:, 768:1024], qn_ref[...]).astype(BF16)
    ckv = _rms(y[:, 1024:1152], kvn_ref[...]).astype(BF16)
    cos_t = cos_ref[...]
    sin_t = sin_ref[...]
    lane = lax.broadcasted_iota(jnp.int32, cos_t.shape, 1)
    scale = LOG2E * (MLA_NOPE + MLA_ROPE) ** -0.5
    qmul = scale * (cos_t + jnp.where(lane < MLA_NOPE, 1.0, 0.0))
    qsin = scale * sin_t
    kr = y[:, 1152:1280] * cos_t + y[:, 1280:1408] * sin_t
    q1 = _dot(cq, wq1_ref[...])
    q2 = _dot(cq, wq2_ref[...])
    k1 = _dot(ckv, wk_ref[...])
    for hd in range(MLA_HEADS):
        sl = slice(hd * LANES, (hd + 1) * LANES)
        qb_ref[:, sl] = (q1[:, sl] * qmul + q2[:, sl] * qsin).astype(BF16)
        kb_ref[:, sl] = (k1[:, sl] + kr).astype(BF16)
    vb = _dot(ckv, wv_ref[...])
    for j in range(vb.shape[0] // BLK):
        _store_vt_ones(vbt_ref, j, vb[j * BLK:(j + 1) * BLK], MLA_HEADS)


def _rope_lane_tables(s):
    inv = ROPE_THETA ** (-jnp.arange(0, MLA_ROPE, 2, dtype=jnp.float32) / MLA_ROPE)
    ang = jnp.arange(s, dtype=jnp.float32)[:, None] * inv[None, :]
    cos, sin = jnp.cos(ang), jnp.sin(ang)
    z_lo = jnp.zeros((s, MLA_NOPE), F32)
    z_hi = jnp.zeros((s, LANES - MLA_NOPE - MLA_ROPE), F32)
    return (jnp.concatenate([z_lo, cos, cos, z_hi], axis=1), jnp.concatenate([z_lo, sin, sin, z_hi], axis=1))


def _rot_cols(w):
    half = w.shape[-1] // 2
    return jnp.concatenate([-w[..., half:], w[..., :half]], axis=-1)


def _proj0(x2, seq, pre, w_in, q_norm, w_uq, kv_norm, w_ukv):
    m, d = x2.shape
    tm = TOKEN_TILE
    kr_w = w_in[:, 1152:1184]
    z64 = jnp.zeros((d, 64), F32)
    z32 = jnp.zeros((d, 32), F32)
    w0 = jnp.concatenate([w_in[:, :1152], z64, kr_w, z32, z64, _rot_cols(kr_w), z32], axis=1).astype(BF16)
    wq = w_uq.reshape(-1, MLA_HEADS, MLA_NOPE + MLA_ROPE)
    rq = wq.shape[0]
    zq = jnp.zeros((rq, MLA_HEADS, 32), F32)
    wq1 = jnp.concatenate([wq, zq], axis=2).reshape(rq, -1).astype(BF16)
    wq2 = jnp.concatenate([jnp.zeros((rq, MLA_HEADS, 64), F32), _rot_cols(wq[:, :, MLA_NOPE:]), zq],
                          axis=2).reshape(rq, -1).astype(BF16)
    wkv = w_ukv.reshape(-1, MLA_HEADS, MLA_NOPE + MLA_V)
    rk = wkv.shape[0]
    wk = jnp.concatenate([wkv[:, :, :MLA_NOPE], jnp.zeros((rk, MLA_HEADS, 64), F32)], axis=2).reshape(rk, -1).astype(BF16)
    wv = wkv[:, :, MLA_NOPE:].reshape(rk, -1).astype(BF16)
    cos_t, sin_t = _rope_lane_tables(seq)
    n_st = seq // tm
    tok = lambda n: pl.BlockSpec((tm, n), lambda i: (i, 0))
    outs = pl.pallas_call(
        _proj0_body,
        grid=(m // tm,),
        in_specs=[
            tok(d), _const_spec((1, d)), _const_spec(w0.shape),
            _const_spec((1, rq)), _const_spec(wq1.shape), _const_spec(wq2.shape),
            _const_spec((1, rk)), _const_spec(wk.shape), _const_spec(wv.shape),
            pl.BlockSpec((tm, LANES), lambda i: (i % n_st, 0)),
            pl.BlockSpec((tm, LANES), lambda i: (i % n_st, 0)),
        ],
        out_specs=[tok(512), tok(128),
                   pl.BlockSpec((1, tm // TILE, 2 * VT_ROWS, TILE), lambda i: (i // n_st, i % n_st, 0, 0)),
                   tok(1024), tok(1024),
                   pl.BlockSpec((1, tm // BLK, MLA_HEADS * VT_ROWS, BLK), lambda i: (i // n_st, i % n_st, 0, 0))],
        out_shape=[jax.ShapeDtypeStruct((m, 512), BF16), jax.ShapeDtypeStruct((m, 128), BF16),
                   jax.ShapeDtypeStruct((m // seq, seq // TILE, 2 * VT_ROWS, TILE), BF16),
                   jax.ShapeDtypeStruct((m, 1024), BF16), jax.ShapeDtypeStruct((m, 1024), BF16),
                   jax.ShapeDtypeStruct((m // seq, seq // BLK, MLA_HEADS * VT_ROWS, BLK), BF16)],
        compiler_params=_cparams(1),
        name="proj0",
    )(x2, pre.reshape(1, d), w0, q_norm.reshape(1, rq), wq1, wq2, kv_norm.reshape(1, rk), wk, wv, cos_t, sin_t)
    return outs


def _swa_body(sink_ref, q_ref, k_ref, vt_ref, bias_ref, o_ref, *, n_kv, grp):
    qi = pl.program_id(1)
    halves = BLK // TILE
    items = [(hkv, half) for hkv in range(n_kv) for half in range(halves)]
    width = grp * TILE
    krow = lax.broadcasted_iota(jnp.int32, (TILE, len(items) * width), 0)
    qcol = lax.broadcasted_iota(jnp.int32, (TILE, len(items) * width), 1) & (TILE - 1)

    cur_s, prev_s, sink_parts, has_prev = [], [], [], []
    for hkv, half in items:
        sl = slice(hkv * HEAD_DIM, (hkv + 1) * HEAD_DIM)
        t = qi * halves + half
        tp = jnp.maximum(t - 1, 0)
        qs = jnp.concatenate([q_ref[0, half * TILE:(half + 1) * TILE, (hkv * grp + g) * HEAD_DIM:
                                    (hkv * grp + g + 1) * HEAD_DIM] for g in range(grp)], axis=0)
        bias_c = jnp.concatenate([bias_ref[hkv * grp + g, 0] for g in range(grp)], axis=1)
        bias_p = jnp.concatenate([bias_ref[hkv * grp + g, 1] for g in range(grp)], axis=1)
        cur_s.append(_dot_nt(k_ref[0, pl.ds(pl.multiple_of(t * TILE, TILE), TILE), sl], qs) + bias_c)
        prev_s.append(_dot_nt(k_ref[0, pl.ds(pl.multiple_of(tp * TILE, TILE), TILE), sl], qs) + bias_p)
        sink_parts.extend(jnp.full((1, TILE), LOG2E * sink_ref[hkv * grp + g], F32) for g in range(grp))
        has_prev.append(jnp.full((1, width), jnp.where(t >= 1, 0, TILE), jnp.int32))
    s_c = jnp.where(krow <= qcol, jnp.concatenate(cur_s, axis=1), NEG_INF)
    s_p = jnp.where(krow > qcol + jnp.concatenate(has_prev, axis=1),
                    jnp.concatenate(prev_s, axis=1), NEG_INF)
    sink = jnp.concatenate(sink_parts, axis=1)
    mx = jnp.maximum(jnp.maximum(jnp.max(s_c, axis=0, keepdims=True), jnp.max(s_p, axis=0, keepdims=True)), sink)
    p_c = jnp.exp2(s_c - mx).astype(BF16)
    p_p = jnp.exp2(s_p - mx).astype(BF16)
    accs = []
    for i, (hkv, half) in enumerate(items):
        vsl = slice(hkv * VT_ROWS, (hkv + 1) * VT_ROWS)
        ls = slice(i * width, (i + 1) * width)
        t = qi * halves + half
        accs.append(_dot(vt_ref[0, t, vsl, :], p_c[:, ls]) + _dot(vt_ref[0, jnp.maximum(t - 1, 0), vsl, :], p_p[:, ls]))
    acc = jnp.concatenate(accs, axis=1)
    o_t = acc[0:HEAD_DIM] / (acc[HEAD_DIM:HEAD_DIM + 1] + jnp.exp2(sink - mx))
    heads = []
    for hkv in range(n_kv):
        for g in range(grp):
            heads.append(jnp.concatenate(
                [o_t[:, (hkv * halves + half) * width + g * TILE:(hkv * halves + half) * width + (g + 1) * TILE]
                 for half in range(halves)], axis=1))
    o_ref[0] = jnp.concatenate(heads, axis=0).T.astype(BF16)


def _swa(qa, ka, vat, bias_a, sinks):
    b, s, width = qa.shape
    n_kv = ka.shape[-1] // HEAD_DIM
    grp = width // HEAD_DIM // n_kv
    return pl.pallas_call(
        functools.partial(_swa_body, n_kv=n_kv, grp=grp),
        grid=(b, s // BLK),
        in_specs=[
            pl.BlockSpec(memory_space=pltpu.SMEM),
            pl.BlockSpec((1, BLK, width), lambda bi, qi: (bi, qi, 0)),
            pl.BlockSpec((1, s, n_kv * HEAD_DIM), lambda bi, qi: (bi, 0, 0)),
            pl.BlockSpec((1, s // TILE, n_kv * VT_ROWS, TILE), lambda bi, qi: (bi, 0, 0, 0)),
            _const_spec(bias_a.shape),
        ],
        out_specs=pl.BlockSpec((1, BLK, width), lambda bi, qi: (bi, qi, 0)),
        out_shape=jax.ShapeDtypeStruct((b, s, width), BF16),
        compiler_params=_cparams(2),
        name="swa",
    )(sinks.astype(F32), qa, ka, vat, bias_a)


def _finish(acc, dv):
    return acc[0:dv] / acc[dv:dv + 1]


def _mla_body(q_ref, k_ref, vt_ref, o_ref, acc_ref, *, n_heads):
    qi = pl.program_id(1)
    lanes = n_heads * BLK
    krow = lax.broadcasted_iota(jnp.int32, (BLK, lanes), 0)
    qcol = lax.broadcasted_iota(jnp.int32, (BLK, lanes), 1) & (BLK - 1)

    def scores(n):
        st = pl.multiple_of(n * BLK, BLK)
        return jnp.concatenate(
            [_dot_nt(k_ref[0, pl.ds(st, BLK), hd * LANES:(hd + 1) * LANES], q_ref[0, :, hd * LANES:(hd + 1) * LANES])
             for hd in range(n_heads)], axis=1)

    def weighted_values(n, p):
        return jnp.concatenate(
            [_dot(vt_ref[0, n, hd * VT_ROWS:(hd + 1) * VT_ROWS, :], p[:, hd * BLK:(hd + 1) * BLK])
             for hd in range(n_heads)], axis=1)

    s = jnp.where(krow <= qcol, scores(qi), NEG_INF)
    m0 = jnp.max(s, axis=0, keepdims=True)
    acc_ref[...] = weighted_values(qi, jnp.exp2(s - m0).astype(BF16))

    def body(n, m):
        s = scores(n)
        m_new = jnp.maximum(m, jnp.max(s, axis=0, keepdims=True))
        acc_ref[...] = jnp.exp2(m - m_new) * acc_ref[...] + weighted_values(n, jnp.exp2(s - m_new).astype(BF16))
        return m_new

    _loop_in_pairs(qi, body, m0)
    o_t = _finish(acc_ref[...], MLA_V)
    o_ref[0] = jnp.concatenate([o_t[:, hd * BLK:(hd + 1) * BLK] for hd in range(n_heads)],
                               axis=0).T.astype(BF16)


def _mla(qb, kb, vbt):
    b, s, width = qb.shape
    n_heads = width // LANES
    n_blocks = s // BLK
    return pl.pallas_call(
        functools.partial(_mla_body, n_heads=n_heads),
        grid=(b, n_blocks),
        in_specs=[
            pl.BlockSpec((1, BLK, width), lambda bi, qi: (bi, qi, 0)),
            pl.BlockSpec((1, s, width), lambda bi, qi: (bi, 0, 0)),
            pl.BlockSpec((1, n_blocks, n_heads * VT_ROWS, BLK), lambda bi, qi: (bi, 0, 0, 0)),
        ],
        out_specs=pl.BlockSpec((1, BLK, n_heads * MLA_V), lambda bi, qi: (bi, qi, 0)),
        out_shape=jax.ShapeDtypeStruct((b, s, n_heads * MLA_V), BF16),
        scratch_shapes=[pltpu.VMEM((VT_ROWS, n_heads * BLK), F32)],
        compiler_params=_cparams(2),
        name="mla",
    )(qb, kb, vbt)


def _store_vt_ones(vt_ref, j, v, n_heads):
    vt = v.T.astype(BF16)
    ones = jnp.ones((ONES_ROWS, v.shape[0]), BF16)
    for hd in range(n_heads):
        vt_ref[0, j, hd * VT_ROWS:hd * VT_ROWS + HEAD_DIM, :] = vt[hd * HEAD_DIM:(hd + 1) * HEAD_DIM]
        vt_ref[0, j, hd * VT_ROWS + HEAD_DIM:(hd + 1) * VT_ROWS, :] = ones


def _proj1_body(x_ref, pre_ref, w_ref, qc_ref, kc_ref, vc_ref, ks_ref, vst_ref, kw_ref, vwt_ref, g_ref,
                qd_ref, kd_ref, vdt_ref, km_ref, *, tm):
    h = _rms(x_ref[...], pre_ref[...]).astype(BF16)
    y = _dot(h, w_ref[...])
    scale = LOG2E * HEAD_DIM ** -0.5
    qc_ref[...] = (y[:, 0:512] * scale).astype(BF16)
    for hkv in range(2):
        kc_ref[0, hkv] = y[:, 512 + hkv * 64:576 + hkv * 64].astype(BF16)
        vc_ref[0, hkv] = y[:, 640 + hkv * 64:704 + hkv * 64].astype(BF16)
    ks_ref[...] = y[:, 768:896].astype(BF16)
    vs = y[:, 896:1024]
    kw_ref[...] = y[:, 1024:1152].astype(BF16)
    vw = y[:, 1152:1280]
    gl = y[:, 1280:1408]
    g_ref[...] = 1.0 / (1.0 + jnp.exp(-gl))
    qd_ref[...] = (y[:, 1408:1920] * scale).astype(BF16)
    kd = y[:, 1920:2432]
    kd_ref[...] = kd.astype(BF16)
    vd = y[:, 2432:2944]
    for j in range(tm // BLK):
        rows = slice(j * BLK, (j + 1) * BLK)
        km_ref[j] = jnp.mean(kd[rows], axis=0, keepdims=True)
        _store_vt_ones(vdt_ref, j, vd[rows], 8)
        _store_vt_ones(vst_ref, j, vs[rows], 2)
        _store_vt_ones(vwt_ref, j, vw[rows], 2)


def _proj1(x2, batch, seq, pre, w_in):
    m, d = x2.shape
    tm = TOKEN_TILE
    n_st = seq // tm
    pad = jnp.zeros((d, LANES - 24), F32)
    w1 = jnp.concatenate([w_in[:, :1304], pad, w_in[:, 1304:]], axis=1).astype(BF16)
    tok = lambda n: pl.BlockSpec((tm, n), lambda i: (i, 0))
    hm = pl.BlockSpec((1, 2, tm, HEAD_DIM), lambda i: (i // n_st, 0, i % n_st, 0))
    nblk = tm // BLK
    vt_spec = lambda rows: pl.BlockSpec((1, nblk, rows, BLK), lambda i: (i // n_st, i % n_st, 0, 0))
    vt_shape = lambda rows: jax.ShapeDtypeStruct((batch, seq // BLK, rows, BLK), BF16)
    return pl.pallas_call(
        functools.partial(_proj1_body, tm=tm),
        grid=(m // tm,),
        in_specs=[tok(d), _const_spec((1, d)), _const_spec(w1.shape)],
        out_specs=[tok(512), hm, hm, tok(128), vt_spec(2 * VT_ROWS), tok(128), vt_spec(2 * VT_ROWS), tok(128), tok(512), tok(512),
                   vt_spec(8 * VT_ROWS),
                   pl.BlockSpec((nblk, 1, 512), lambda i: (i, 0, 0))],
        out_shape=[
            jax.ShapeDtypeStruct((m, 512), BF16),
            jax.ShapeDtypeStruct((batch, 2, seq, HEAD_DIM), BF16),
            jax.ShapeDtypeStruct((batch, 2, seq, HEAD_DIM), BF16),
            jax.ShapeDtypeStruct((m, 128), BF16), vt_shape(2 * VT_ROWS),
            jax.ShapeDtypeStruct((m, 128), BF16), vt_shape(2 * VT_ROWS),
            jax.ShapeDtypeStruct((m, 128), F32),
            jax.ShapeDtypeStruct((m, 512), BF16), jax.ShapeDtypeStruct((m, 512), BF16),
            vt_shape(8 * VT_ROWS),
            jax.ShapeDtypeStruct((m // MOBA_BLOCK, 1, 512), F32),
        ],
        compiler_params=_cparams(1),
        name="proj1",
    )(x2, pre.reshape(1, d), w1)


def _compress_body(kx_ref, vx_ref, pek_ref, pev_ref, wk1_ref, wk2_ref, wv1_ref, wv2_ref, ko_ref, vo_ref):
    half = CMP_STRIDE * HEAD_DIM
    for x_ref, pe_ref, w1_ref, w2_ref, o_ref in ((kx_ref, pek_ref, wk1_ref, wk2_ref, ko_ref),
                                                 (vx_ref, pev_ref, wv1_ref, wv2_ref, vo_ref)):
        pe8 = jnp.broadcast_to(pe_ref[...], (8, 2 * half)).astype(BF16)
        pe_term = _dot(pe8, w1_ref[...])[0:1]
        outs = []
        for hkv in range(2):
            x = x_ref[0, hkv]
            lo = _dot(x, w1_ref[0:half])
            hi = _dot(x, w1_ref[half:2 * half])
            hid = lo + pltpu.roll(hi, x.shape[0] - 1, 0) + pe_term
            outs.append(_dot(_silu(hid).astype(BF16), w2_ref[...]))
        out = jnp.concatenate(outs, axis=1)
        o_ref[0] = (out.T if o_ref is vo_ref else out).astype(BF16)


def _compress(kc_hm, vc_hm, pe_k, pe_v, wk1, wk2, wv1, wv2):
    b, _, s, dh = kc_hm.shape
    rows = s // CMP_STRIDE
    feat = CMP_STRIDE * dh
    kx = kc_hm.reshape(b, 2, rows, feat)
    vx = vc_hm.reshape(b, 2, rows, feat)
    xspec = pl.BlockSpec((1, 2, rows, feat), lambda bi: (bi, 0, 0, 0))
    ospec = pl.BlockSpec((1, rows, 2 * dh), lambda bi: (bi, 0, 0))
    w1s, w2s = wk1.shape, wk2.shape
    return pl.pallas_call(
        _compress_body,
        grid=(b,),
        in_specs=[xspec, xspec, _const_spec((1, 2 * feat)), _const_spec((1, 2 * feat)),
                  _const_spec(w1s), _const_spec(w2s), _const_spec(w1s), _const_spec(w2s)],
        out_specs=[ospec, ospec],
        out_shape=[jax.ShapeDtypeStruct((b, rows, 2 * dh), BF16)] * 2,
        compiler_params=_cparams(1),
        name="nsa_compress",
    )(kx, vx, pe_k.astype(F32).reshape(1, -1), pe_v.astype(F32).reshape(1, -1),
      wk1.astype(BF16), wk2.astype(BF16), wv1.astype(BF16), wv2.astype(BF16))


def _bias_block(bias_ref, hd, dist_blocks):
    sub = BLK // TILE
    rows = []
    for a in range(sub):
        cols = []
        for bq in range(sub):
            e = sub * dist_blocks + bq - a
            e = max(e, 0) if isinstance(e, int) else jnp.clip(e, 0, BIAS_TILES - 1)
            cols.append(bias_ref[hd, e])
        rows.append(jnp.concatenate(cols, axis=1))
    return jnp.concatenate(rows, axis=0)


def _rank_rows(val, ridx, n_real):
    rank = jnp.zeros(val.shape, F32)
    for j in range(n_real):
        vj = val[j:j + 1, :]
        rank = rank + jnp.where((vj > val) | ((vj == val) & (ridx > j)), 1.0, 0.0)
    return rank


def _nsa_body(q_ref, kc_ref, vct_ref, ks_ref, vst_ref, kw_ref, vwt_ref, g_ref, bias_ref, ovt_ref, o_ref,
              pen_ref, acc_ref, *, n_sel):
    qi = pl.program_id(1)
    grp = 4
    lanes = grp * BLK
    sub = BLK // SEL_BLOCK
    own = pl.multiple_of(qi * BLK, BLK)
    prev = pl.multiple_of(jnp.maximum(qi - 1, 0) * BLK, BLK)
    prev_blk = jnp.maximum(qi - 1, 0)
    krow = lax.broadcasted_iota(jnp.int32, (BLK, lanes), 0)
    qcol = lax.broadcasted_iota(jnp.int32, (BLK, lanes), 1) & (BLK - 1)
    causal = krow <= qcol
    crow = lax.broadcasted_iota(jnp.int32, (TILE, lanes), 0)
    cq = lax.broadcasted_iota(jnp.int32, (TILE, lanes), 1) & (BLK - 1)
    visible = CMP_STRIDE * crow + (CMP_BLOCK - 1) <= qi * BLK + cq
    jrow = lax.broadcasted_iota(jnp.int32, (n_sel, BLK), 0)
    cur = qi * sub + (lax.broadcasted_iota(jnp.int32, (n_sel, BLK), 1) >> 6)
    forced = (jrow == 0) | (jrow == cur) | (jrow == cur - 1)
    gates_t = g_ref[0].T
    has_prev = jnp.where(qi >= 1, 0, BLK)

    def stack_q(hkv):
        return jnp.concatenate(
            [q_ref[0, :, (hkv * grp + g) * HEAD_DIM:(hkv * grp + g + 1) * HEAD_DIM] for g in range(grp)], axis=0)

    def group_bias(hkv, dist_blocks):
        return jnp.concatenate([_bias_block(bias_ref, hkv * grp + g, dist_blocks) for g in range(grp)], axis=1)

    def sel_block(hkv, n, dist_blocks):
        st = pl.multiple_of(n * BLK, BLK)
        sl = slice(hkv * HEAD_DIM, (hkv + 1) * HEAD_DIM)
        s = _dot_nt(ks_ref[0, pl.ds(st, BLK), sl], stack_q(hkv)) + group_bias(hkv, dist_blocks)
        pens = pen_ref[hkv, n, 0:sub, :]
        return s, [jnp.concatenate([pens[j:j + 1]] * grp, axis=1) for j in range(sub)]

    def sel_update(s, pens, m_old):
        parts = [s[j * SEL_BLOCK:(j + 1) * SEL_BLOCK] for j in range(sub)]
        m_new = m_old
        for j in range(sub):
            mj = jnp.max(parts[j], axis=0, keepdims=True) + pens[j]
            m_new = mj if m_new is None else jnp.maximum(m_new, mj)
        p = jnp.concatenate([jnp.exp2(parts[j] - (m_new - pens[j])) for j in range(sub)], axis=0)
        return m_new, p.astype(BF16)

    o_cmp, o_win, ms = [], [], []
    for hkv in range(2):
        sl = slice(hkv * HEAD_DIM, (hkv + 1) * HEAD_DIM)
        vsl = slice(hkv * VT_ROWS, (hkv + 1) * VT_ROWS)
        qs = stack_q(hkv)

        sc = jnp.where(visible, _dot_nt(kc_ref[0, :, sl], qs), NEG_INF)
        mc = jnp.max(sc, axis=0, keepdims=True)
        ec = jnp.where(visible, jnp.exp2(sc - mc), 0.0)
        lc = jnp.sum(ec, axis=0, keepdims=True)
        p = ec / jnp.where(lc > 0.0, lc, 1.0)
        o_cmp.append(_dot(vct_ref[0, sl, :], p.astype(BF16)))

        ps = p[:, 0:BLK] + p[:, BLK:2 * BLK] + p[:, 2 * BLK:3 * BLK] + p[:, 3 * BLK:4 * BLK]
        ps_hi = ps.astype(BF16)
        ps_lo = (ps - ps_hi.astype(F32)).astype(BF16)
        imp = _dot(ovt_ref[...], ps_hi) + _dot(ovt_ref[...], ps_lo)
        val = jnp.where(forced, FORCE_SCORE, jnp.where(jrow <= cur, imp, NEG_INF))
        pen = jnp.where(_rank_rows(val, jrow, n_sel) < SEL_TOPK, 0.0, NEG_INF)
        for t in range(n_sel // sub):
            pen_ref[hkv, t, 0:sub, :] = pen[t * sub:(t + 1) * sub]

        s, pens = sel_block(hkv, qi, 0)
        m, pb = sel_update(jnp.where(causal, s, NEG_INF), pens, None)
        acc_ref[hkv] = _dot(vst_ref[0, qi, vsl, :], pb)
        ms.append(m)

        s0 = _dot_nt(kw_ref[0, pl.ds(own, BLK), sl], qs) + group_bias(hkv, 0)
        s1 = _dot_nt(kw_ref[0, pl.ds(prev, BLK), sl], qs) + group_bias(hkv, 1)
        s0 = jnp.where(causal, s0, NEG_INF)
        s1 = jnp.where(krow > qcol + has_prev, s1, NEG_INF)
        mw = jnp.maximum(jnp.max(s0, axis=0, keepdims=True), jnp.max(s1, axis=0, keepdims=True))
        ow = (_dot(vwt_ref[0, qi, vsl, :], jnp.exp2(s0 - mw).astype(BF16))
              + _dot(vwt_ref[0, prev_blk, vsl, :], jnp.exp2(s1 - mw).astype(BF16)))
        o_win.append(_finish(ow, HEAD_DIM))

    def body(n, ms):
        new = []
        for hkv in range(2):
            vsl = slice(hkv * VT_ROWS, (hkv + 1) * VT_ROWS)
            s, pens = sel_block(hkv, n, qi - n)
            m_new, pb = sel_update(s, pens, ms[hkv])
            acc_ref[hkv] = jnp.exp2(ms[hkv] - m_new) * acc_ref[hkv] + _dot(vst_ref[0, n, vsl, :], pb)
            new.append(m_new)
        return tuple(new)

    _loop_in_pairs(qi, body, tuple(ms))
    outs = []
    for hkv in range(2):
        o_slc = _finish(acc_ref[hkv], HEAD_DIM)
        for g in range(grp):
            hd = hkv * grp + g
            ls = slice(g * BLK, (g + 1) * BLK)
            outs.append(gates_t[hd:hd + 1] * o_cmp[hkv][:, ls] + gates_t[8 + hd:9 + hd] * o_slc[:, ls]
                        + gates_t[16 + hd:17 + hd] * o_win[hkv][:, ls])
    o_ref[0] = jnp.concatenate(outs, axis=0).T.astype(BF16)


def _nsa(qc, kcmp, vcmpt, ks, vst, kw, vwt, gates, bias_c):
    b, s, width = qc.shape
    n_blocks = s // BLK
    n_sel = s // SEL_BLOCK
    n_c = (s - CMP_BLOCK) // CMP_STRIDE + 1
    j_start = np.arange(n_sel)[:, None] * SEL_BLOCK
    c_start = np.arange(TILE)[None, :] * CMP_STRIDE
    overlap_t = ((c_start < j_start + SEL_BLOCK) & (c_start + CMP_BLOCK > j_start)
                 & (np.arange(TILE)[None, :] < n_c))
    ovt = jnp.asarray(overlap_t, BF16)
    qtile = lambda n: pl.BlockSpec((1, BLK, n), lambda bi, qi: (bi, qi, 0))
    full = pl.BlockSpec((1, s, 128), lambda bi, qi: (bi, 0, 0))
    full_t = pl.BlockSpec((1, n_blocks, 2 * VT_ROWS, BLK), lambda bi, qi: (bi, 0, 0, 0))
    cmp_spec = pl.BlockSpec((1, TILE, 128), lambda bi, qi: (bi, 0, 0))
    return pl.pallas_call(
        functools.partial(_nsa_body, n_sel=n_sel),
        grid=(b, n_blocks),
        in_specs=[qtile(width), cmp_spec, cmp_spec, full, full_t, full, full_t, qtile(128),
                  _const_spec(bias_c.shape), _const_spec(ovt.shape)],
        out_specs=qtile(width),
        out_shape=jax.ShapeDtypeStruct((b, s, width), BF16),
        scratch_shapes=[pltpu.VMEM((2, n_blocks, 8, BLK), F32), pltpu.VMEM((2, VT_ROWS, 4 * BLK), F32)],
        compiler_params=_cparams(2),
        name="nsa",
    )(qc, kcmp, vcmpt, ks, vst, kw, vwt, gates, bias_c, ovt)


def _moba_body(q_ref, k_ref, vt_ref, km_ref, bias_ref, o_ref, pen_ref, acc_ref, *, n_heads, n_blocks):
    qi = pl.program_id(1)
    lanes = n_heads * BLK
    nidx = lax.broadcasted_iota(jnp.int32, (16, lanes), 0)
    krow = lax.broadcasted_iota(jnp.int32, (BLK, lanes), 0)
    qcol = lax.broadcasted_iota(jnp.int32, (BLK, lanes), 1) & (BLK - 1)

    def scores(n, dist_blocks):
        st = pl.multiple_of(n * BLK, BLK)
        parts = []
        for hd in range(n_heads):
            sl = slice(hd * HEAD_DIM, (hd + 1) * HEAD_DIM)
            parts.append(_dot_nt(k_ref[0, pl.ds(st, BLK), sl], q_ref[0, :, sl])
                         + _bias_block(bias_ref, hd, dist_blocks))
        return jnp.concatenate(parts, axis=1)

    def weighted_values(n, p):
        return jnp.concatenate(
            [_dot(vt_ref[0, n, hd * VT_ROWS:(hd + 1) * VT_ROWS, :], p[:, hd * BLK:(hd + 1) * BLK])
             for hd in range(n_heads)], axis=1)

    zpad = jnp.zeros((16 - n_blocks, HEAD_DIM), F32)
    gate = jnp.concatenate(
        [_dot_nt(jnp.concatenate([km_ref[0, :, hd * HEAD_DIM:(hd + 1) * HEAD_DIM], zpad], axis=0).astype(BF16),
                 q_ref[0, :, hd * HEAD_DIM:(hd + 1) * HEAD_DIM]) for hd in range(n_heads)], axis=1)
    val = jnp.where(nidx < qi, gate, NEG_INF)
    val = jnp.where(nidx < n_blocks, val, PAD_SCORE)
    keep = (_rank_rows(val, nidx, n_blocks) < MOBA_TOPK) & (nidx < qi)
    pen_ref[...] = jnp.where(keep, 0.0, NEG_INF)

    s = jnp.where(krow <= qcol, scores(qi, 0), NEG_INF)
    m0 = jnp.max(s, axis=0, keepdims=True)
    acc_ref[...] = weighted_values(qi, jnp.exp2(s - m0).astype(BF16))

    def body(n, m):
        s = scores(n, qi - n)
        pen = pen_ref[pl.ds(n, 1), :]
        m_new = jnp.maximum(m, jnp.max(s, axis=0, keepdims=True) + pen)
        p = jnp.exp2(s - (m_new - pen)).astype(BF16)
        acc_ref[...] = jnp.exp2(m - m_new) * acc_ref[...] + weighted_values(n, p)
        return m_new

    _loop_in_pairs(qi, body, m0)
    o_t = _finish(acc_ref[...], HEAD_DIM)
    o_ref[0] = jnp.concatenate([o_t[:, hd * BLK:(hd + 1) * BLK] for hd in range(n_heads)],
                               axis=0).T.astype(BF16)


def _moba(qd, kd, vdt, kmean, bias_d):
    b, s, width = qd.shape
    n_heads = width // HEAD_DIM
    n_blocks = s // BLK
    return pl.pallas_call(
        functools.partial(_moba_body, n_heads=n_heads, n_blocks=n_blocks),
        grid=(b, n_blocks),
        in_specs=[
            pl.BlockSpec((1, BLK, width), lambda bi, qi: (bi, qi, 0)),
            pl.BlockSpec((1, s, width), lambda bi, qi: (bi, 0, 0)),
            pl.BlockSpec((1, n_blocks, 2 * width, BLK), lambda bi, qi: (bi, 0, 0, 0)),
            pl.BlockSpec((1, n_blocks, width), lambda bi, qi: (bi, 0, 0)),
            _const_spec(bias_d.shape),
        ],
        out_specs=pl.BlockSpec((1, BLK, width), lambda bi, qi: (bi, qi, 0)),
        out_shape=jax.ShapeDtypeStruct((b, s, width), BF16),
        scratch_shapes=[pltpu.VMEM((16, n_heads * BLK), F32), pltpu.VMEM((VT_ROWS, n_heads * BLK), F32)],
        compiler_params=_cparams(2),
        name="moba",
    )(qd, kd, vdt, kmean.reshape(b, n_blocks, width), bias_d)


def kernel(x, rel_bias_table, l0_ffn1_pre, l0_ffn1_post, l0_ffn1_wg, l0_ffn1_wu, l0_ffn1_wd, l0_mix_pre, l0_mix_post, l0_w_in, l0_sinks, l0_mla_q_norm, l0_mla_w_uq, l0_mla_kv_norm, l0_mla_w_ukv, l0_w_out, l0_ffn2_pre, l0_ffn2_post, l0_ffn2_wg, l0_ffn2_wu, l0_ffn2_wd, l1_ffn1_pre, l1_ffn1_post, l1_ffn1_wg, l1_ffn1_wu, l1_ffn1_wd, l1_mix_pre, l1_mix_post, l1_w_in, l1_nsa_pe_k, l1_nsa_pe_v, l1_nsa_wk1, l1_nsa_wk2, l1_nsa_wv1, l1_nsa_wv2, l1_w_out, l1_ffn2_pre, l1_ffn2_post, l1_ffn2_wg, l1_ffn2_wu, l1_ffn2_wd):
    b, s, d = x.shape
    m = b * s
    x2 = x.reshape(m, d)
    tiles = _bias_tiles(rel_bias_table)
    bias_a = tiles[A_SLOT:A_SLOT + 8, 0:2]
    bias_c = tiles[C_SLOT:C_SLOT + 8]
    bias_d = tiles[D_SLOT:D_SLOT + 8]

    x2 = _ffn(x2, l0_ffn1_pre, l0_ffn1_post, l0_ffn1_wg, l0_ffn1_wu, l0_ffn1_wd, 0.5)
    qa, ka, vat, qb, kb, vbt = _proj0(x2, s, l0_mix_pre, l0_w_in, l0_mla_q_norm, l0_mla_w_uq,
                                      l0_mla_kv_norm, l0_mla_w_ukv)
    shp = lambda t: t.reshape(b, s, t.shape[-1])
    o_a = _swa(shp(qa), shp(ka), vat, bias_a, l0_sinks)
    o_b = _mla(shp(qb), shp(kb), vbt)
    x2 = _outproj_ffn(x2, o_a.reshape(m, -1), o_b.reshape(m, -1), l0_w_out, l0_mix_post,
                      l0_ffn2_pre, l0_ffn2_post, l0_ffn2_wg, l0_ffn2_wu, l0_ffn2_wd, 0.5)

    x2 = _ffn(x2, l1_ffn1_pre, l1_ffn1_post, l1_ffn1_wg, l1_ffn1_wu, l1_ffn1_wd, 0.5)
    qc, kc_hm, vc_hm, ks, vst, kw, vwt, gates, qd, kd, vdt, kmean = _proj1(x2, b, s, l1_mix_pre, l1_w_in)
    kcmp, vcmpt = _compress(kc_hm, vc_hm, l1_nsa_pe_k, l1_nsa_pe_v, l1_nsa_wk1, l1_nsa_wk2, l1_nsa_wv1, l1_nsa_wv2)
    o_c = _nsa(shp(qc), kcmp, vcmpt, shp(ks), vst, shp(kw), vwt, shp(gates), bias_c)
    o_d = _moba(shp(qd), shp(kd), vdt, kmean, bias_d)
    x2 = _outproj_ffn(x2, o_c.reshape(m, -1), o_d.reshape(m, -1), l1_w_out, l1_mix_post,
                      l1_ffn2_pre, l1_ffn2_post, l1_ffn2_wg, l1_ffn2_wu, l1_ffn2_wd, 0.5)
    return x2.reshape(b, s, d)
```

```python
import functools
import math

import numpy as np
import jax
import jax.numpy as jnp
from jax import lax
from jax.experimental import pallas as pl
from jax.experimental.pallas import tpu as pltpu

F32 = jnp.float32
BF16 = jnp.bfloat16

D_MODEL = 1024
HEAD_DIM = 64
D_FF = 2816
NORM_EPS = 1e-6
LOG2E = math.log2(math.e)
NEG_INF = -1e30
PAD_SCORE = -3e38
FORCE_SCORE = 1e9
REL_BUCKETS = 32
REL_MAX_DIST = 1024
REL_SLOTS = 24
A_SLOT, C_SLOT, D_SLOT = 0, 8, 16
MLA_NOPE, MLA_ROPE, MLA_V = 64, 32, 64
MLA_HEADS = 8
ROPE_THETA = 10000.0
CMP_BLOCK, CMP_STRIDE = 32, 16
SEL_BLOCK, SEL_TOPK = 64, 8
MOBA_BLOCK, MOBA_TOPK = 256, 3

LANES = 128
TILE = 128
BLK = 256
ONES_ROWS = 16
VT_ROWS = HEAD_DIM + ONES_ROWS
BIAS_TILES = 9
TOKEN_TILE = 512
FF_CHUNK = 256
VMEM_LIMIT = 56 * 1024 * 1024


def _cparams(n_axes):
    return pltpu.CompilerParams(dimension_semantics=("arbitrary",) * n_axes, vmem_limit_bytes=VMEM_LIMIT)


def _dot(a, b):
    return jnp.dot(a, b, preferred_element_type=F32)


def _dot_nt(a, b):
    return lax.dot_general(a, b, (((1,), (1,)), ((), ())), preferred_element_type=F32)


def _rms(x, g):
    return x * lax.rsqrt(jnp.mean(x * x, axis=-1, keepdims=True) + NORM_EPS) * g


def _silu(x):
    return x / (1.0 + jnp.exp(-x))


def _const_spec(shape):
    nd = len(shape)
    return pl.BlockSpec(shape, lambda *_: (0,) * nd)


def _pipelined_blocks(count, produce, consume, state):
    last = jnp.maximum(count - 1, 0)

    def trip(i, carry):
        state, stat_a = carry
        a = 2 * i
        stat_b = produce(a + 1, 1)
        state = consume(a, 0, stat_a, state)
        stat_next = produce(jnp.minimum(a + 2, last), 0)
        state = consume(a + 1, 1, stat_b, state)
        return state, stat_next

    carry = lax.fori_loop(0, count // 2, trip, (state, produce(0, 0)))
    return lax.cond(count % 2 == 1, lambda c: consume(count - 1, 0, c[1], c[0]), lambda c: c[0], carry)


def _resident_spec(shape):
    nd = len(shape)
    return pl.BlockSpec(shape, lambda *_: (0,) * nd, pipeline_mode=pl.Buffered(1))


def _ffn_tile(x, pre_ref, post_ref, wg_ref, wu_ref, wd_ref, acc_ref, weight):
    h = _rms(x, pre_ref[...]).astype(BF16)
    for c in range(wg_ref.shape[1] // FF_CHUNK):
        cols = slice(c * FF_CHUNK, (c + 1) * FF_CHUNK)
        g = _dot(h, wg_ref[:, cols].astype(BF16))
        u = _dot(h, wu_ref[:, cols].astype(BF16))
        a = (_silu(g) * u).astype(BF16)
        y = _dot(a, wd_ref[cols, :].astype(BF16))
        if c == 0:
            acc_ref[...] = y
        else:
            acc_ref[...] += y
    return x + weight * _rms(acc_ref[...], post_ref[...])


def _ffn_body(x_ref, pre_ref, post_ref, wg_ref, wu_ref, wd_ref, o_ref, acc_ref, *, weight):
    o_ref[...] = _ffn_tile(x_ref[...], pre_ref, post_ref, wg_ref, wu_ref, wd_ref, acc_ref, weight)


def _ffn(x2, pre, post, wg, wu, wd, weight):
    m, d = x2.shape
    d_ff = wg.shape[1]
    tm = TOKEN_TILE
    return pl.pallas_call(
        functools.partial(_ffn_body, weight=weight),
        grid=(m // tm,),
        in_specs=[
            pl.BlockSpec((tm, d), lambda i: (i, 0)),
            _const_spec((1, d)), _const_spec((1, d)),
            _resident_spec((d, d_ff)), _resident_spec((d, d_ff)), _resident_spec((d_ff, d)),
        ],
        out_specs=pl.BlockSpec((tm, d), lambda i: (i, 0)),
        out_shape=jax.ShapeDtypeStruct((m, d), F32),
        scratch_shapes=[pltpu.VMEM((tm, d), F32)],
        compiler_params=_cparams(1),
        name="ffn",
    )(x2, pre.reshape(1, d), post.reshape(1, d), wg, wu, wd)


def _outproj_ffn_body(x_ref, o1_ref, o2_ref, wo_ref, mpost_ref, pre_ref, post_ref, wg_ref, wu_ref, wd_ref,
                      o_ref, acc_ref, *, weight):
    n1 = o1_ref.shape[1]
    y = (_dot(o1_ref[...], wo_ref[0:n1, :].astype(BF16))
         + _dot(o2_ref[...], wo_ref[n1:, :].astype(BF16)))
    x = x_ref[...] + _rms(y, mpost_ref[...])
    o_ref[...] = _ffn_tile(x, pre_ref, post_ref, wg_ref, wu_ref, wd_ref, acc_ref, weight)


def _outproj_ffn(x2, o1, o2, w_out, mix_post, pre, post, wg, wu, wd, weight):
    m, d = x2.shape
    d_ff = wg.shape[1]
    tm = TOKEN_TILE
    tok = lambda n: pl.BlockSpec((tm, n), lambda i: (i, 0))
    return pl.pallas_call(
        functools.partial(_outproj_ffn_body, weight=weight),
        grid=(m // tm,),
        in_specs=[
            tok(d), tok(o1.shape[1]), tok(o2.shape[1]), _resident_spec(w_out.shape), _const_spec((1, d)),
            _const_spec((1, d)), _const_spec((1, d)),
            _resident_spec((d, d_ff)), _resident_spec((d, d_ff)), _resident_spec((d_ff, d)),
        ],
        out_specs=tok(d),
        out_shape=jax.ShapeDtypeStruct((m, d), F32),
        scratch_shapes=[pltpu.VMEM((tm, d), F32)],
        compiler_params=_cparams(1),
        name="outproj_ffn",
    )(x2, o1, o2, w_out, mix_post.reshape(1, d), pre.reshape(1, d), post.reshape(1, d), wg, wu, wd)


def _t5_bucket(dist):
    n = jnp.maximum(dist, 0)
    exact = REL_BUCKETS // 2
    nf = jnp.maximum(n, 1).astype(jnp.float32)
    large = exact + (jnp.log(nf / exact) / math.log(REL_MAX_DIST / exact) * (REL_BUCKETS - exact)).astype(jnp.int32)
    return jnp.where(n < exact, n, jnp.minimum(large, REL_BUCKETS - 1))


def _tile_bucket_range(d):
    dist = np.arange(max(d * TILE - (TILE - 1), 0), d * TILE + TILE)
    exact = REL_BUCKETS // 2
    large = exact + np.log(np.maximum(dist, 1) / exact) / math.log(REL_MAX_DIST / exact) * (REL_BUCKETS - exact)
    bucket = np.where(dist < exact, dist, np.minimum(large.astype(np.int64), REL_BUCKETS - 1))
    return max(int(bucket.min()) - 1, 0), min(int(bucket.max()) + 1, REL_BUCKETS - 1)


def _bias_body(tab_ref, idx_ref, o_ref):
    slot = pl.program_id(0)
    for d in range(BIAS_TILES):
        idx = idx_ref[d]
        lo, hi = _tile_bucket_range(d)
        v = jnp.full((TILE, TILE), LOG2E * tab_ref[lo * REL_SLOTS + slot], F32)
        for b in range(lo + 1, hi + 1):
            v = jnp.where(idx == b, LOG2E * tab_ref[b * REL_SLOTS + slot], v)
        o_ref[0, d] = v


def _bias_tiles(rel_table):
    n_slots = rel_table.shape[1]
    d = jnp.arange(BIAS_TILES)[:, None, None]
    key = jnp.arange(TILE)[None, :, None]
    query = jnp.arange(TILE)[None, None, :]
    idx = _t5_bucket(d * TILE + query - key).astype(jnp.int32)
    return pl.pallas_call(
        _bias_body,
        grid=(n_slots,),
        in_specs=[
            pl.BlockSpec(memory_space=pltpu.SMEM),
            _const_spec((BIAS_TILES, TILE, TILE)),
        ],
        out_specs=pl.BlockSpec((1, BIAS_TILES, TILE, TILE), lambda s: (s, 0, 0, 0)),
        out_shape=jax.ShapeDtypeStruct((n_slots, BIAS_TILES, TILE, TILE), F32),
        compiler_params=_cparams(1),
        name="bias_tiles",
    )(rel_table.astype(F32).reshape(-1), idx)


def _proj0_body(x_ref, pre_ref, w_ref, qn_ref, wq1_ref, wq2_ref, kvn_ref, wk_ref, wv_ref, cos_ref, sin_ref,
                qa_ref, ka_ref, vat_ref, qb_ref, kb_ref, vbt_ref):
    h = _rms(x_ref[...], pre_ref[...]).astype(BF16)
    y = _dot(h, w_ref[...])
    qa_ref[...] = (y[:, 0:512] * (LOG2E * HEAD_DIM ** -0.5)).astype(BF16)
    ka_ref[...] = y[:, 512:640].astype(BF16)
    for j in range(y.shape[0] // TILE):
        _store_vt_ones(vat_ref, j, y[j * TILE:(j + 1) * TILE, 640:768], 2)
    cq = _rms(y[:, 768:1024], qn_ref[...]).astype(BF16)
    ckv = _rms(y[:, 1024:1152], kvn_ref[...]).astype(BF16)
    cos_t = cos_ref[...]
    sin_t = sin_ref[...]
    lane = lax.broadcasted_iota(jnp.int32, cos_t.shape, 1)
    scale = LOG2E * (MLA_NOPE + MLA_ROPE) ** -0.5
    qmul = scale * (cos_t + jnp.where(lane < MLA_NOPE, 1.0, 0.0))
    qsin = scale * sin_t
    kr = y[:, 1152:1280] * cos_t + y[:, 1280:1408] * sin_t
    q1 = _dot(cq, wq1_ref[...])
    q2 = _dot(cq, wq2_ref[...])
    k1 = _dot(ckv, wk_ref[...])
    for hd in range(MLA_HEADS):
        sl = slice(hd * LANES, (hd + 1) * LANES)
        qb_ref[:, sl] = (q1[:, sl] * qmul + q2[:, sl] * qsin).astype(BF16)
        kb_ref[:, sl] = (k1[:, sl] + kr).astype(BF16)
    vb = _dot(ckv, wv_ref[...])
    for j in range(vb.shape[0] // BLK):
        _store_vt_ones(vbt_ref, j, vb[j * BLK:(j + 1) * BLK], MLA_HEADS)


def _rope_lane_tables(s):
    inv = ROPE_THETA ** (-jnp.arange(0, MLA_ROPE, 2, dtype=jnp.float32) / MLA_ROPE)
    ang = jnp.arange(s, dtype=jnp.float32)[:, None] * inv[None, :]
    cos, sin = jnp.cos(ang), jnp.sin(ang)
    z_lo = jnp.zeros((s, MLA_NOPE), F32)
    z_hi = jnp.zeros((s, LANES - MLA_NOPE - MLA_ROPE), F32)
    return (jnp.concatenate([z_lo, cos, cos, z_hi], axis=1), jnp.concatenate([z_lo, sin, sin, z_hi], axis=1))


def _rot_cols(w):
    half = w.shape[-1] // 2
    return jnp.concatenate([-w[..., half:], w[..., :half]], axis=-1)


def _proj0(x2, seq, pre, w_in, q_norm, w_uq, kv_norm, w_ukv):
    m, d = x2.shape
    tm = TOKEN_TILE
    kr_w = w_in[:, 1152:1184]
    z64 = jnp.zeros((d, 64), F32)
    z32 = jnp.zeros((d, 32), F32)
    w0 = jnp.concatenate([w_in[:, :1152], z64, kr_w, z32, z64, _rot_cols(kr_w), z32], axis=1).astype(BF16)
    wq = w_uq.reshape(-1, MLA_HEADS, MLA_NOPE + MLA_ROPE)
    rq = wq.shape[0]
    zq = jnp.zeros((rq, MLA_HEADS, 32), F32)
    wq1 = jnp.concatenate([wq, zq], axis=2).reshape(rq, -1).astype(BF16)
    wq2 = jnp.concatenate([jnp.zeros((rq, MLA_HEADS, 64), F32), _rot_cols(wq[:, :, MLA_NOPE:]), zq],
                          axis=2).reshape(rq, -1).astype(BF16)
    wkv = w_ukv.reshape(-1, MLA_HEADS, MLA_NOPE + MLA_V)
    rk = wkv.shape[0]
    wk = jnp.concatenate([wkv[:, :, :MLA_NOPE], jnp.zeros((rk, MLA_HEADS, 64), F32)], axis=2).reshape(rk, -1).astype(BF16)
    wv = wkv[:, :, MLA_NOPE:].reshape(rk, -1).astype(BF16)
    cos_t, sin_t = _rope_lane_tables(seq)
    n_st = seq // tm
    tok = lambda n: pl.BlockSpec((tm, n), lambda i: (i, 0))
    outs = pl.pallas_call(
        _proj0_body,
        grid=(m // tm,),
        in_specs=[
            tok(d), _const_spec((1, d)), _const_spec(w0.shape),
            _const_spec((1, rq)), _const_spec(wq1.shape), _const_spec(wq2.shape),
            _const_spec((1, rk)), _const_spec(wk.shape), _const_spec(wv.shape),
            pl.BlockSpec((tm, LANES), lambda i: (i % n_st, 0)),
            pl.BlockSpec((tm, LANES), lambda i: (i % n_st, 0)),
        ],
        out_specs=[tok(512), tok(128),
                   pl.BlockSpec((1, tm // TILE, 2 * VT_ROWS, TILE), lambda i: (i // n_st, i % n_st, 0, 0)),
                   tok(1024), tok(1024),
                   pl.BlockSpec((1, tm // BLK, MLA_HEADS * VT_ROWS, BLK), lambda i: (i // n_st, i % n_st, 0, 0))],
        out_shape=[jax.ShapeDtypeStruct((m, 512), BF16), jax.ShapeDtypeStruct((m, 128), BF16),
                   jax.ShapeDtypeStruct((m // seq, seq // TILE, 2 * VT_ROWS, TILE), BF16),
                   jax.ShapeDtypeStruct((m, 1024), BF16), jax.ShapeDtypeStruct((m, 1024), BF16),
                   jax.ShapeDtypeStruct((m // seq, seq // BLK, MLA_HEADS * VT_ROWS, BLK), BF16)],
        compiler_params=_cparams(1),
        name="proj0",
    )(x2, pre.reshape(1, d), w0, q_norm.reshape(1, rq), wq1, wq2, kv_norm.reshape(1, rk), wk, wv, cos_t, sin_t)
    return outs


def _swa_body(sink_ref, q_ref, k_ref, vt_ref, bias_ref, o_ref, *, n_kv, grp):
    qi = pl.program_id(1)
    halves = BLK // TILE
    items = [(hkv, half) for hkv in range(n_kv) for half in range(halves)]
    width = grp * TILE
    krow = lax.broadcasted_iota(jnp.int32, (TILE, len(items) * width), 0)
    qcol = lax.broadcasted_iota(jnp.int32, (TILE, len(items) * width), 1) & (TILE - 1)

    cur_s, prev_s, sink_parts, has_prev = [], [], [], []
    for hkv, half in items:
        sl = slice(hkv * HEAD_DIM, (hkv + 1) * HEAD_DIM)
        t = qi * halves + half
        tp = jnp.maximum(t - 1, 0)
        qs = jnp.concatenate([q_ref[0, half * TILE:(half + 1) * TILE, (hkv * grp + g) * HEAD_DIM:
                                    (hkv * grp + g + 1) * HEAD_DIM] for g in range(grp)], axis=0)
        bias_c = jnp.concatenate([bias_ref[hkv * grp + g, 0] for g in range(grp)], axis=1)
        bias_p = jnp.concatenate([bias_ref[hkv * grp + g, 1] for g in range(grp)], axis=1)
        cur_s.append(_dot_nt(k_ref[0, pl.ds(pl.multiple_of(t * TILE, TILE), TILE), sl], qs) + bias_c)
        prev_s.append(_dot_nt(k_ref[0, pl.ds(pl.multiple_of(tp * TILE, TILE), TILE), sl], qs) + bias_p)
        sink_parts.extend(jnp.full((1, TILE), LOG2E * sink_ref[hkv * grp + g], F32) for g in range(grp))
        has_prev.append(jnp.full((1, width), jnp.where(t >= 1, 0, TILE), jnp.int32))
    s_c = jnp.where(krow <= qcol, jnp.concatenate(cur_s, axis=1), NEG_INF)
    s_p = jnp.where(krow > qcol + jnp.concatenate(has_prev, axis=1),
                    jnp.concatenate(prev_s, axis=1), NEG_INF)
    sink = jnp.concatenate(sink_parts, axis=1)
    mx = jnp.maximum(jnp.maximum(jnp.max(s_c, axis=0, keepdims=True), jnp.max(s_p, axis=0, keepdims=True)), sink)
    p_c = jnp.exp2(s_c - mx).astype(BF16)
    p_p = jnp.exp2(s_p - mx).astype(BF16)
    accs = []
    for i, (hkv, half) in enumerate(items):
        vsl = slice(hkv * VT_ROWS, (hkv + 1) * VT_ROWS)
        ls = slice(i * width, (i + 1) * width)
        t = qi * halves + half
        accs.append(_dot(vt_ref[0, t, vsl, :], p_c[:, ls]) + _dot(vt_ref[0, jnp.maximum(t - 1, 0), vsl, :], p_p[:, ls]))
    acc = jnp.concatenate(accs, axis=1)
    o_t = acc[0:HEAD_DIM] / (acc[HEAD_DIM:HEAD_DIM + 1] + jnp.exp2(sink - mx))
    heads = []
    for hkv in range(n_kv):
        for g in range(grp):
            heads.append(jnp.concatenate(
                [o_t[:, (hkv * halves + half) * width + g * TILE:(hkv * halves + half) * width + (g + 1) * TILE]
                 for half in range(halves)], axis=1))
    o_ref[0] = jnp.concatenate(heads, axis=0).T.astype(BF16)


def _swa(qa, ka, vat, bias_a, sinks):
    b, s, width = qa.shape
    n_kv = ka.shape[-1] // HEAD_DIM
    grp = width // HEAD_DIM // n_kv
    return pl.pallas_call(
        functools.partial(_swa_body, n_kv=n_kv, grp=grp),
        grid=(b, s // BLK),
        in_specs=[
            pl.BlockSpec(memory_space=pltpu.SMEM),
            pl.BlockSpec((1, BLK, width), lambda bi, qi: (bi, qi, 0)),
            pl.BlockSpec((1, s, n_kv * HEAD_DIM), lambda bi, qi: (bi, 0, 0)),
            pl.BlockSpec((1, s // TILE, n_kv * VT_ROWS, TILE), lambda bi, qi: (bi, 0, 0, 0)),
            _const_spec(bias_a.shape),
        ],
        out_specs=pl.BlockSpec((1, BLK, width), lambda bi, qi: (bi, qi, 0)),
        out_shape=jax.ShapeDtypeStruct((b, s, width), BF16),
        compiler_params=_cparams(2),
        name="swa",
    )(sinks.astype(F32), qa, ka, vat, bias_a)


def _finish(acc, dv):
    return acc[0:dv] / acc[dv:dv + 1]


def _mla_body(q_ref, k_ref, vt_ref, o_ref, acc_ref, s_ref, *, n_heads):
    qi = pl.program_id(1)
    lanes = n_heads * BLK
    krow = lax.broadcasted_iota(jnp.int32, (BLK, lanes), 0)
    qcol = lax.broadcasted_iota(jnp.int32, (BLK, lanes), 1) & (BLK - 1)

    def scores(n):
        st = pl.multiple_of(n * BLK, BLK)
        return jnp.concatenate(
            [_dot_nt(k_ref[0, pl.ds(st, BLK), hd * LANES:(hd + 1) * LANES], q_ref[0, :, hd * LANES:(hd + 1) * LANES])
             for hd in range(n_heads)], axis=1)

    def weighted_values(n, p):
        return jnp.concatenate(
            [_dot(vt_ref[0, n, hd * VT_ROWS:(hd + 1) * VT_ROWS, :], p[:, hd * BLK:(hd + 1) * BLK])
             for hd in range(n_heads)], axis=1)

    s = jnp.where(krow <= qcol, scores(qi), NEG_INF)
    m0 = jnp.max(s, axis=0, keepdims=True)
    acc_ref[...] = weighted_values(qi, jnp.exp2(s - m0).astype(BF16))

    def produce(n, slot):
        s = scores(n)
        s_ref[slot] = s
        return jnp.max(s, axis=0, keepdims=True)

    def consume(n, slot, block_max, m):
        m_new = jnp.maximum(m, block_max)
        p = jnp.exp2(s_ref[slot] - m_new).astype(BF16)
        acc_ref[...] = jnp.exp2(m - m_new) * acc_ref[...] + weighted_values(n, p)
        return m_new

    _pipelined_blocks(qi, produce, consume, m0)
    o_t = _finish(acc_ref[...], MLA_V)
    o_ref[0] = jnp.concatenate([o_t[:, hd * BLK:(hd + 1) * BLK] for hd in range(n_heads)],
                               axis=0).T.astype(BF16)


def _mla(qb, kb, vbt):
    b, s, width = qb.shape
    n_heads = width // LANES
    n_blocks = s // BLK
    return pl.pallas_call(
        functools.partial(_mla_body, n_heads=n_heads),
        grid=(b, n_blocks),
        in_specs=[
            pl.BlockSpec((1, BLK, width), lambda bi, qi: (bi, qi, 0)),
            pl.BlockSpec((1, s, width), lambda bi, qi: (bi, 0, 0)),
            pl.BlockSpec((1, n_blocks, n_heads * VT_ROWS, BLK), lambda bi, qi: (bi, 0, 0, 0)),
        ],
        out_specs=pl.BlockSpec((1, BLK, n_heads * MLA_V), lambda bi, qi: (bi, qi, 0)),
        out_shape=jax.ShapeDtypeStruct((b, s, n_heads * MLA_V), BF16),
        scratch_shapes=[pltpu.VMEM((VT_ROWS, n_heads * BLK), F32), pltpu.VMEM((2, BLK, n_heads * BLK), F32)],
        compiler_params=_cparams(2),
        name="mla",
    )(qb, kb, vbt)


def _store_vt_ones(vt_ref, j, v, n_heads):
    vt = v.T.astype(BF16)
    ones = jnp.ones((ONES_ROWS, v.shape[0]), BF16)
    for hd in range(n_heads):
        vt_ref[0, j, hd * VT_ROWS:hd * VT_ROWS + HEAD_DIM, :] = vt[hd * HEAD_DIM:(hd + 1) * HEAD_DIM]
        vt_ref[0, j, hd * VT_ROWS + HEAD_DIM:(hd + 1) * VT_ROWS, :] = ones


def _proj1_body(x_ref, pre_ref, w_ref, qc_ref, kc_ref, vc_ref, ks_ref, vst_ref, kw_ref, vwt_ref, g_ref,
                qd_ref, kd_ref, vdt_ref, km_ref, *, tm):
    h = _rms(x_ref[...], pre_ref[...]).astype(BF16)
    y = _dot(h, w_ref[...])
    scale = LOG2E * HEAD_DIM ** -0.5
    qc_ref[...] = (y[:, 0:512] * scale).astype(BF16)
    for hkv in range(2):
        kc_ref[0, hkv] = y[:, 512 + hkv * 64:576 + hkv * 64].astype(BF16)
        vc_ref[0, hkv] = y[:, 640 + hkv * 64:704 + hkv * 64].astype(BF16)
    ks_ref[...] = y[:, 768:896].astype(BF16)
    vs = y[:, 896:1024]
    kw_ref[...] = y[:, 1024:1152].astype(BF16)
    vw = y[:, 1152:1280]
    gl = y[:, 1280:1408]
    g_ref[...] = 1.0 / (1.0 + jnp.exp(-gl))
    qd_ref[...] = (y[:, 1408:1920] * scale).astype(BF16)
    kd = y[:, 1920:2432]
    kd_ref[...] = kd.astype(BF16)
    vd = y[:, 2432:2944]
    for j in range(tm // BLK):
        rows = slice(j * BLK, (j + 1) * BLK)
        km_ref[j] = jnp.mean(kd[rows], axis=0, keepdims=True)
        _store_vt_ones(vdt_ref, j, vd[rows], 8)
        _store_vt_ones(vst_ref, j, vs[rows], 2)
        _store_vt_ones(vwt_ref, j, vw[rows], 2)


def _proj1(x2, batch, seq, pre, w_in):
    m, d = x2.shape
    tm = TOKEN_TILE
    n_st = seq // tm
    pad = jnp.zeros((d, LANES - 24), F32)
    w1 = jnp.concatenate([w_in[:, :1304], pad, w_in[:, 1304:]], axis=1).astype(BF16)
    tok = lambda n: pl.BlockSpec((tm, n), lambda i: (i, 0))
    hm = pl.BlockSpec((1, 2, tm, HEAD_DIM), lambda i: (i // n_st, 0, i % n_st, 0))
    nblk = tm // BLK
    vt_spec = lambda rows: pl.BlockSpec((1, nblk, rows, BLK), lambda i: (i // n_st, i % n_st, 0, 0))
    vt_shape = lambda rows: jax.ShapeDtypeStruct((batch, seq // BLK, rows, BLK), BF16)
    return pl.pallas_call(
        functools.partial(_proj1_body, tm=tm),
        grid=(m // tm,),
        in_specs=[tok(d), _const_spec((1, d)), _const_spec(w1.shape)],
        out_specs=[tok(512), hm, hm, tok(128), vt_spec(2 * VT_ROWS), tok(128), vt_spec(2 * VT_ROWS), tok(128), tok(512), tok(512),
                   vt_spec(8 * VT_ROWS),
                   pl.BlockSpec((nblk, 1, 512), lambda i: (i, 0, 0))],
        out_shape=[
            jax.ShapeDtypeStruct((m, 512), BF16),
            jax.ShapeDtypeStruct((batch, 2, seq, HEAD_DIM), BF16),
            jax.ShapeDtypeStruct((batch, 2, seq, HEAD_DIM), BF16),
            jax.ShapeDtypeStruct((m, 128), BF16), vt_shape(2 * VT_ROWS),
            jax.ShapeDtypeStruct((m, 128), BF16), vt_shape(2 * VT_ROWS),
            jax.ShapeDtypeStruct((m, 128), F32),
            jax.ShapeDtypeStruct((m, 512), BF16), jax.ShapeDtypeStruct((m, 512), BF16),
            vt_shape(8 * VT_ROWS),
            jax.ShapeDtypeStruct((m // MOBA_BLOCK, 1, 512), F32),
        ],
        compiler_params=_cparams(1),
        name="proj1",
    )(x2, pre.reshape(1, d), w1)


def _compress_body(kx_ref, vx_ref, pek_ref, pev_ref, wk1_ref, wk2_ref, wv1_ref, wv2_ref, ko_ref, vo_ref):
    half = CMP_STRIDE * HEAD_DIM
    for x_ref, pe_ref, w1_ref, w2_ref, o_ref in ((kx_ref, pek_ref, wk1_ref, wk2_ref, ko_ref),
                                                 (vx_ref, pev_ref, wv1_ref, wv2_ref, vo_ref)):
        pe8 = jnp.broadcast_to(pe_ref[...], (8, 2 * half)).astype(BF16)
        pe_term = _dot(pe8, w1_ref[...])[0:1]
        outs = []
        for hkv in range(2):
            x = x_ref[0, hkv]
            lo = _dot(x, w1_ref[0:half])
            hi = _dot(x, w1_ref[half:2 * half])
            hid = lo + pltpu.roll(hi, x.shape[0] - 1, 0) + pe_term
            outs.append(_dot(_silu(hid).astype(BF16), w2_ref[...]))
        out = jnp.concatenate(outs, axis=1)
        o_ref[0] = (out.T if o_ref is vo_ref else out).astype(BF16)


def _compress(kc_hm, vc_hm, pe_k, pe_v, wk1, wk2, wv1, wv2):
    b, _, s, dh = kc_hm.shape
    rows = s // CMP_STRIDE
    feat = CMP_STRIDE * dh
    kx = kc_hm.reshape(b, 2, rows, feat)
    vx = vc_hm.reshape(b, 2, rows, feat)
    xspec = pl.BlockSpec((1, 2, rows, feat), lambda bi: (bi, 0, 0, 0))
    ospec = pl.BlockSpec((1, rows, 2 * dh), lambda bi: (bi, 0, 0))
    w1s, w2s = wk1.shape, wk2.shape
    return pl.pallas_call(
        _compress_body,
        grid=(b,),
        in_specs=[xspec, xspec, _const_spec((1, 2 * feat)), _const_spec((1, 2 * feat)),
                  _const_spec(w1s), _const_spec(w2s), _const_spec(w1s), _const_spec(w2s)],
        out_specs=[ospec, ospec],
        out_shape=[jax.ShapeDtypeStruct((b, rows, 2 * dh), BF16)] * 2,
        compiler_params=_cparams(1),
        name="nsa_compress",
    )(kx, vx, pe_k.astype(F32).reshape(1, -1), pe_v.astype(F32).reshape(1, -1),
      wk1.astype(BF16), wk2.astype(BF16), wv1.astype(BF16), wv2.astype(BF16))


def _bias_block(bias_ref, hd, dist_blocks):
    sub = BLK // TILE
    rows = []
    for a in range(sub):
        cols = []
        for bq in range(sub):
            e = sub * dist_blocks + bq - a
            e = max(e, 0) if isinstance(e, int) else jnp.clip(e, 0, BIAS_TILES - 1)
            cols.append(bias_ref[hd, e])
        rows.append(jnp.concatenate(cols, axis=1))
    return jnp.concatenate(rows, axis=0)


def _rank_rows(val, ridx, n_real):
    rank = jnp.zeros(val.shape, F32)
    for j in range(n_real):
        vj = val[j:j + 1, :]
        rank = rank + jnp.where((vj > val) | ((vj == val) & (ridx > j)), 1.0, 0.0)
    return rank


def _nsa_body(q_ref, kc_ref, vct_ref, ks_ref, vst_ref, kw_ref, vwt_ref, g_ref, bias_ref, ovt_ref, o_ref,
              pen_ref, acc_ref, s_ref, *, n_sel):
    qi = pl.program_id(1)
    grp = 4
    lanes = grp * BLK
    sub = BLK // SEL_BLOCK
    own = pl.multiple_of(qi * BLK, BLK)
    prev = pl.multiple_of(jnp.maximum(qi - 1, 0) * BLK, BLK)
    prev_blk = jnp.maximum(qi - 1, 0)
    krow = lax.broadcasted_iota(jnp.int32, (BLK, lanes), 0)
    qcol = lax.broadcasted_iota(jnp.int32, (BLK, lanes), 1) & (BLK - 1)
    causal = krow <= qcol
    crow = lax.broadcasted_iota(jnp.int32, (TILE, lanes), 0)
    cq = lax.broadcasted_iota(jnp.int32, (TILE, lanes), 1) & (BLK - 1)
    visible = CMP_STRIDE * crow + (CMP_BLOCK - 1) <= qi * BLK + cq
    jrow = lax.broadcasted_iota(jnp.int32, (n_sel, BLK), 0)
    cur = qi * sub + (lax.broadcasted_iota(jnp.int32, (n_sel, BLK), 1) >> 6)
    forced = (jrow == 0) | (jrow == cur) | (jrow == cur - 1)
    gates_t = g_ref[0].T
    has_prev = jnp.where(qi >= 1, 0, BLK)

    def stack_q(hkv):
        return jnp.concatenate(
            [q_ref[0, :, (hkv * grp + g) * HEAD_DIM:(hkv * grp + g + 1) * HEAD_DIM] for g in range(grp)], axis=0)

    def group_bias(hkv, dist_blocks):
        return jnp.concatenate([_bias_block(bias_ref, hkv * grp + g, dist_blocks) for g in range(grp)], axis=1)

    def sel_scores(hkv, n, dist_blocks):
        st = pl.multiple_of(n * BLK, BLK)
        sl = slice(hkv * HEAD_DIM, (hkv + 1) * HEAD_DIM)
        return _dot_nt(ks_ref[0, pl.ds(st, BLK), sl], stack_q(hkv)) + group_bias(hkv, dist_blocks)

    def sel_pens(hkv, n):
        pens = pen_ref[hkv, n, 0:sub, :]
        return [jnp.concatenate([pens[j:j + 1]] * grp, axis=1) for j in range(sub)]

    def sel_max(s, pens):
        mj = [jnp.max(s[j * SEL_BLOCK:(j + 1) * SEL_BLOCK], axis=0, keepdims=True) + pens[j] for j in range(sub)]
        return functools.reduce(jnp.maximum, mj)

    def sel_probs(s, pens, m_new):
        return jnp.concatenate([jnp.exp2(s[j * SEL_BLOCK:(j + 1) * SEL_BLOCK] - (m_new - pens[j]))
                                for j in range(sub)], axis=0).astype(BF16)

    o_cmp, o_win, ms = [], [], []
    for hkv in range(2):
        sl = slice(hkv * HEAD_DIM, (hkv + 1) * HEAD_DIM)
        vsl = slice(hkv * VT_ROWS, (hkv + 1) * VT_ROWS)
        qs = stack_q(hkv)

        sc = jnp.where(visible, _dot_nt(kc_ref[0, :, sl], qs), NEG_INF)
        mc = jnp.max(sc, axis=0, keepdims=True)
        ec = jnp.where(visible, jnp.exp2(sc - mc), 0.0)
        lc = jnp.sum(ec, axis=0, keepdims=True)
        p = ec / jnp.where(lc > 0.0, lc, 1.0)
        o_cmp.append(_dot(vct_ref[0, sl, :], p.astype(BF16)))

        ps = p[:, 0:BLK] + p[:, BLK:2 * BLK] + p[:, 2 * BLK:3 * BLK] + p[:, 3 * BLK:4 * BLK]
        ps_hi = ps.astype(BF16)
        ps_lo = (ps - ps_hi.astype(F32)).astype(BF16)
        imp = _dot(ovt_ref[...], ps_hi) + _dot(ovt_ref[...], ps_lo)
        val = jnp.where(forced, FORCE_SCORE, jnp.where(jrow <= cur, imp, NEG_INF))
        pen = jnp.where(_rank_rows(val, jrow, n_sel) < SEL_TOPK, 0.0, NEG_INF)
        for t in range(n_sel // sub):
            pen_ref[hkv, t, 0:sub, :] = pen[t * sub:(t + 1) * sub]

        s = jnp.where(causal, sel_scores(hkv, qi, 0), NEG_INF)
        pens = sel_pens(hkv, qi)
        m = sel_max(s, pens)
        acc_ref[hkv] = _dot(vst_ref[0, qi, vsl, :], sel_probs(s, pens, m))
        ms.append(m)

        s0 = _dot_nt(kw_ref[0, pl.ds(own, BLK), sl], qs) + group_bias(hkv, 0)
        s1 = _dot_nt(kw_ref[0, pl.ds(prev, BLK), sl], qs) + group_bias(hkv, 1)
        s0 = jnp.where(causal, s0, NEG_INF)
        s1 = jnp.where(krow > qcol + has_prev, s1, NEG_INF)
        mw = jnp.maximum(jnp.max(s0, axis=0, keepdims=True), jnp.max(s1, axis=0, keepdims=True))
        ow = (_dot(vwt_ref[0, qi, vsl, :], jnp.exp2(s0 - mw).astype(BF16))
              + _dot(vwt_ref[0, prev_blk, vsl, :], jnp.exp2(s1 - mw).astype(BF16)))
        o_win.append(_finish(ow, HEAD_DIM))

    def produce(n, slot):
        stats = []
        for hkv in range(2):
            s = sel_scores(hkv, n, qi - n)
            s_ref[slot, hkv] = s
            stats.append(sel_max(s, sel_pens(hkv, n)))
        return tuple(stats)

    def consume(n, slot, block_max, ms):
        new = []
        for hkv in range(2):
            vsl = slice(hkv * VT_ROWS, (hkv + 1) * VT_ROWS)
            m_new = jnp.maximum(ms[hkv], block_max[hkv])
            pb = sel_probs(s_ref[slot, hkv], sel_pens(hkv, n), m_new)
            acc_ref[hkv] = jnp.exp2(ms[hkv] - m_new) * acc_ref[hkv] + _dot(vst_ref[0, n, vsl, :], pb)
            new.append(m_new)
        return tuple(new)

    _pipelined_blocks(qi, produce, consume, tuple(ms))
    outs = []
    for hkv in range(2):
        o_slc = _finish(acc_ref[hkv], HEAD_DIM)
        for g in range(grp):
            hd = hkv * grp + g
            ls = slice(g * BLK, (g + 1) * BLK)
            outs.append(gates_t[hd:hd + 1] * o_cmp[hkv][:, ls] + gates_t[8 + hd:9 + hd] * o_slc[:, ls]
                        + gates_t[16 + hd:17 + hd] * o_win[hkv][:, ls])
    o_ref[0] = jnp.concatenate(outs, axis=0).T.astype(BF16)


def _nsa(qc, kcmp, vcmpt, ks, vst, kw, vwt, gates, bias_c):
    b, s, width = qc.shape
    n_blocks = s // BLK
    n_sel = s // SEL_BLOCK
    n_c = (s - CMP_BLOCK) // CMP_STRIDE + 1
    j_start = np.arange(n_sel)[:, None] * SEL_BLOCK
    c_start = np.arange(TILE)[None, :] * CMP_STRIDE
    overlap_t = ((c_start < j_start + SEL_BLOCK) & (c_start + CMP_BLOCK > j_start)
                 & (np.arange(TILE)[None, :] < n_c))
    ovt = jnp.asarray(overlap_t, BF16)
    qtile = lambda n: pl.BlockSpec((1, BLK, n), lambda bi, qi: (bi, qi, 0))
    full = pl.BlockSpec((1, s, 128), lambda bi, qi: (bi, 0, 0))
    full_t = pl.BlockSpec((1, n_blocks, 2 * VT_ROWS, BLK), lambda bi, qi: (bi, 0, 0, 0))
    cmp_spec = pl.BlockSpec((1, TILE, 128), lambda bi, qi: (bi, 0, 0))
    return pl.pallas_call(
        functools.partial(_nsa_body, n_sel=n_sel),
        grid=(b, n_blocks),
        in_specs=[qtile(width), cmp_spec, cmp_spec, full, full_t, full, full_t, qtile(128),
                  _const_spec(bias_c.shape), _const_spec(ovt.shape)],
        out_specs=qtile(width),
        out_shape=jax.ShapeDtypeStruct((b, s, width), BF16),
        scratch_shapes=[pltpu.VMEM((2, n_blocks, 8, BLK), F32), pltpu.VMEM((2, VT_ROWS, 4 * BLK), F32),
                        pltpu.VMEM((2, 2, BLK, 4 * BLK), F32)],
        compiler_params=_cparams(2),
        name="nsa",
    )(qc, kcmp, vcmpt, ks, vst, kw, vwt, gates, bias_c, ovt)


def _moba_body(q_ref, k_ref, vt_ref, km_ref, bias_ref, o_ref, pen_ref, acc_ref, s_ref, *, n_heads, n_blocks):
    qi = pl.program_id(1)
    lanes = n_heads * BLK
    nidx = lax.broadcasted_iota(jnp.int32, (16, lanes), 0)
    krow = lax.broadcasted_iota(jnp.int32, (BLK, lanes), 0)
    qcol = lax.broadcasted_iota(jnp.int32, (BLK, lanes), 1) & (BLK - 1)

    def scores(n, dist_blocks):
        st = pl.multiple_of(n * BLK, BLK)
        parts = []
        for hd in range(n_heads):
            sl = slice(hd * HEAD_DIM, (hd + 1) * HEAD_DIM)
            parts.append(_dot_nt(k_ref[0, pl.ds(st, BLK), sl], q_ref[0, :, sl])
                         + _bias_block(bias_ref, hd, dist_blocks))
        return jnp.concatenate(parts, axis=1)

    def weighted_values(n, p):
        return jnp.concatenate(
            [_dot(vt_ref[0, n, hd * VT_ROWS:(hd + 1) * VT_ROWS, :], p[:, hd * BLK:(hd + 1) * BLK])
             for hd in range(n_heads)], axis=1)

    zpad = jnp.zeros((16 - n_blocks, HEAD_DIM), F32)
    gate = jnp.concatenate(
        [_dot_nt(jnp.concatenate([km_ref[0, :, hd * HEAD_DIM:(hd + 1) * HEAD_DIM], zpad], axis=0).astype(BF16),
                 q_ref[0, :, hd * HEAD_DIM:(hd + 1) * HEAD_DIM]) for hd in range(n_heads)], axis=1)
    val = jnp.where(nidx < qi, gate, NEG_INF)
    val = jnp.where(nidx < n_blocks, val, PAD_SCORE)
    keep = (_rank_rows(val, nidx, n_blocks) < MOBA_TOPK) & (nidx < qi)
    pen_ref[...] = jnp.where(keep, 0.0, NEG_INF)

    s = jnp.where(krow <= qcol, scores(qi, 0), NEG_INF)
    m0 = jnp.max(s, axis=0, keepdims=True)
    acc_ref[...] = weighted_values(qi, jnp.exp2(s - m0).astype(BF16))

    def produce(n, slot):
        s = scores(n, qi - n)
        s_ref[slot] = s
        return jnp.max(s, axis=0, keepdims=True) + pen_ref[pl.ds(n, 1), :]

    def consume(n, slot, block_max, m):
        pen = pen_ref[pl.ds(n, 1), :]
        m_new = jnp.maximum(m, block_max)
        p = jnp.exp2(s_ref[slot] - (m_new - pen)).astype(BF16)
        acc_ref[...] = jnp.exp2(m - m_new) * acc_ref[...] + weighted_values(n, p)
        return m_new

    _pipelined_blocks(qi, produce, consume, m0)
    o_t = _finish(acc_ref[...], HEAD_DIM)
    o_ref[0] = jnp.concatenate([o_t[:, hd * BLK:(hd + 1) * BLK] for hd in range(n_heads)],
                               axis=0).T.astype(BF16)


def _moba(qd, kd, vdt, kmean, bias_d):
    b, s, width = qd.shape
    n_heads = width // HEAD_DIM
    n_blocks = s // BLK
    return pl.pallas_call(
        functools.partial(_moba_body, n_heads=n_heads, n_blocks=n_blocks),
        grid=(b, n_blocks),
        in_specs=[
            pl.BlockSpec((1, BLK, width), lambda bi, qi: (bi, qi, 0)),
            pl.BlockSpec((1, s, width), lambda bi, qi: (bi, 0, 0)),
            pl.BlockSpec((1, n_blocks, 2 * width, BLK), lambda bi, qi: (bi, 0, 0, 0)),
            pl.BlockSpec((1, n_blocks, width), lambda bi, qi: (bi, 0, 0)),
            _const_spec(bias_d.shape),
        ],
        out_specs=pl.BlockSpec((1, BLK, width), lambda bi, qi: (bi, qi, 0)),
        out_shape=jax.ShapeDtypeStruct((b, s, width), BF16),
        scratch_shapes=[pltpu.VMEM((16, n_heads * BLK), F32), pltpu.VMEM((VT_ROWS, n_heads * BLK), F32),
                        pltpu.VMEM((2, BLK, n_heads * BLK), F32)],
        compiler_params=_cparams(2),
        name="moba",
    )(qd, kd, vdt, kmean.reshape(b, n_blocks, width), bias_d)


def kernel(x, rel_bias_table, l0_ffn1_pre, l0_ffn1_post, l0_ffn1_wg, l0_ffn1_wu, l0_ffn1_wd, l0_mix_pre, l0_mix_post, l0_w_in, l0_sinks, l0_mla_q_norm, l0_mla_w_uq, l0_mla_kv_norm, l0_mla_w_ukv, l0_w_out, l0_ffn2_pre, l0_ffn2_post, l0_ffn2_wg, l0_ffn2_wu, l0_ffn2_wd, l1_ffn1_pre, l1_ffn1_post, l1_ffn1_wg, l1_ffn1_wu, l1_ffn1_wd, l1_mix_pre, l1_mix_post, l1_w_in, l1_nsa_pe_k, l1_nsa_pe_v, l1_nsa_wk1, l1_nsa_wk2, l1_nsa_wv1, l1_nsa_wv2, l1_w_out, l1_ffn2_pre, l1_ffn2_post, l1_ffn2_wg, l1_ffn2_wu, l1_ffn2_wd):
    b, s, d = x.shape
    m = b * s
    x2 = x.reshape(m, d)
    tiles = _bias_tiles(rel_bias_table)
    bias_a = tiles[A_SLOT:A_SLOT + 8, 0:2]
    bias_c = tiles[C_SLOT:C_SLOT + 8]
    bias_d = tiles[D_SLOT:D_SLOT + 8]

    x2 = _ffn(x2, l0_ffn1_pre, l0_ffn1_post, l0_ffn1_wg, l0_ffn1_wu, l0_ffn1_wd, 0.5)
    qa, ka, vat, qb, kb, vbt = _proj0(x2, s, l0_mix_pre, l0_w_in, l0_mla_q_norm, l0_mla_w_uq,
                                      l0_mla_kv_norm, l0_mla_w_ukv)
    shp = lambda t: t.reshape(b, s, t.shape[-1])
    o_a = _swa(shp(qa), shp(ka), vat, bias_a, l0_sinks)
    o_b = _mla(shp(qb), shp(kb), vbt)
    x2 = _outproj_ffn(x2, o_a.reshape(m, -1), o_b.reshape(m, -1), l0_w_out, l0_mix_post,
                      l0_ffn2_pre, l0_ffn2_post, l0_ffn2_wg, l0_ffn2_wu, l0_ffn2_wd, 0.5)

    x2 = _ffn(x2, l1_ffn1_pre, l1_ffn1_post, l1_ffn1_wg, l1_ffn1_wu, l1_ffn1_wd, 0.5)
    qc, kc_hm, vc_hm, ks, vst, kw, vwt, gates, qd, kd, vdt, kmean = _proj1(x2, b, s, l1_mix_pre, l1_w_in)
    kcmp, vcmpt = _compress(kc_hm, vc_hm, l1_nsa_pe_k, l1_nsa_pe_v, l1_nsa_wk1, l1_nsa_wk2, l1_nsa_wv1, l1_nsa_wv2)
    o_c = _nsa(shp(qc), kcmp, vcmpt, shp(ks), vst, shp(kw), vwt, shp(gates), bias_c)
    o_d = _moba(shp(qd), shp(kd), vdt, kmean, bias_d)
    x2 = _outproj_ffn(x2, o_c.reshape(m, -1), o_d.reshape(m, -1), l1_w_out, l1_mix_post,
                      l1_ffn2_pre, l1_ffn2_post, l1_ffn2_wg, l1_ffn2_wu, l1_ffn2_wd, 0.5)
    return x2.reshape(b, s, d)
```

```python
import functools
import math

import numpy as np
import jax
import jax.numpy as jnp
from jax import lax
from jax.experimental import pallas as pl
from jax.experimental.pallas import tpu as pltpu

F32 = jnp.float32
BF16 = jnp.bfloat16

D_MODEL = 1024
HEAD_DIM = 64
D_FF = 2816
NORM_EPS = 1e-6
LOG2E = math.log2(math.e)
NEG_INF = -1e30
PAD_SCORE = -3e38
FORCE_SCORE = 1e9
REL_BUCKETS = 32
REL_MAX_DIST = 1024
REL_SLOTS = 24
A_SLOT, C_SLOT, D_SLOT = 0, 8, 16
MLA_NOPE, MLA_ROPE, MLA_V = 64, 32, 64
MLA_HEADS = 8
ROPE_THETA = 10000.0
CMP_BLOCK, CMP_STRIDE = 32, 16
SEL_BLOCK, SEL_TOPK = 64, 8
MOBA_BLOCK, MOBA_TOPK = 256, 3

LANES = 128
TILE = 128
BLK = 256
ONES_ROWS = 16
VT_ROWS = HEAD_DIM + ONES_ROWS
BIAS_TILES = 9
TOKEN_TILE = 512
FF_CHUNK = 256
VMEM_LIMIT = 56 * 1024 * 1024


def _cparams(n_axes):
    return pltpu.CompilerParams(dimension_semantics=("arbitrary",) * n_axes, vmem_limit_bytes=VMEM_LIMIT)


def _dot(a, b):
    return jnp.dot(a, b, preferred_element_type=F32)


def _dot_nt(a, b):
    return lax.dot_general(a, b, (((1,), (1,)), ((), ())), preferred_element_type=F32)


def _rms(x, g):
    return x * lax.rsqrt(jnp.mean(x * x, axis=-1, keepdims=True) + NORM_EPS) * g


def _silu(x):
    return x / (1.0 + jnp.exp(-x))


def _const_spec(shape):
    nd = len(shape)
    return pl.BlockSpec(shape, lambda *_: (0,) * nd)


def _pipelined_blocks(length, produce_first, produce, consume, state):
    trips = (length - 1) // 2

    def trip(t, carry):
        state, stat_a = carry
        a = 2 * t
        stat_b = produce(a + 1, 1)
        state = consume(a, 0, stat_a, state)
        stat_next = produce(a + 2, 0)
        state = consume(a + 1, 1, stat_b, state)
        return state, stat_next

    carry = lax.fori_loop(0, trips, trip, (state, produce_first(0)))
    a = 2 * trips

    def tail_one(c):
        return consume(a, 0, c[1], c[0])

    def tail_two(c):
        stat_b = produce(a + 1, 1)
        return consume(a + 1, 1, stat_b, consume(a, 0, c[1], c[0]))

    return lax.cond(length - a == 1, tail_one, tail_two, carry)


def _resident_spec(shape):
    nd = len(shape)
    return pl.BlockSpec(shape, lambda *_: (0,) * nd, pipeline_mode=pl.Buffered(1))


def _ffn_tile(x, pre_ref, post_ref, wg_ref, wu_ref, wd_ref, acc_ref, weight):
    h = _rms(x, pre_ref[...]).astype(BF16)
    for c in range(wg_ref.shape[1] // FF_CHUNK):
        cols = slice(c * FF_CHUNK, (c + 1) * FF_CHUNK)
        g = _dot(h, wg_ref[:, cols].astype(BF16))
        u = _dot(h, wu_ref[:, cols].astype(BF16))
        a = (_silu(g) * u).astype(BF16)
        y = _dot(a, wd_ref[cols, :].astype(BF16))
        if c == 0:
            acc_ref[...] = y
        else:
            acc_ref[...] += y
    return x + weight * _rms(acc_ref[...], post_ref[...])


def _ffn_body(x_ref, pre_ref, post_ref, wg_ref, wu_ref, wd_ref, o_ref, acc_ref, *, weight):
    o_ref[...] = _ffn_tile(x_ref[...], pre_ref, post_ref, wg_ref, wu_ref, wd_ref, acc_ref, weight)


def _ffn(x2, pre, post, wg, wu, wd, weight):
    m, d = x2.shape
    d_ff = wg.shape[1]
    tm = TOKEN_TILE
    return pl.pallas_call(
        functools.partial(_ffn_body, weight=weight),
        grid=(m // tm,),
        in_specs=[
            pl.BlockSpec((tm, d), lambda i: (i, 0)),
            _const_spec((1, d)), _const_spec((1, d)),
            _resident_spec((d, d_ff)), _resident_spec((d, d_ff)), _resident_spec((d_ff, d)),
        ],
        out_specs=pl.BlockSpec((tm, d), lambda i: (i, 0)),
        out_shape=jax.ShapeDtypeStruct((m, d), F32),
        scratch_shapes=[pltpu.VMEM((tm, d), F32)],
        compiler_params=_cparams(1),
        name="ffn",
    )(x2, pre.reshape(1, d), post.reshape(1, d), wg, wu, wd)


def _outproj_ffn_body(x_ref, o1_ref, o2_ref, wo_ref, mpost_ref, pre_ref, post_ref, wg_ref, wu_ref, wd_ref,
                      o_ref, acc_ref, *, weight):
    n1 = o1_ref.shape[1]
    y = (_dot(o1_ref[...], wo_ref[0:n1, :].astype(BF16))
         + _dot(o2_ref[...], wo_ref[n1:, :].astype(BF16)))
    x = x_ref[...] + _rms(y, mpost_ref[...])
    o_ref[...] = _ffn_tile(x, pre_ref, post_ref, wg_ref, wu_ref, wd_ref, acc_ref, weight)


def _outproj_ffn(x2, o1, o2, w_out, mix_post, pre, post, wg, wu, wd, weight):
    m, d = x2.shape
    d_ff = wg.shape[1]
    tm = TOKEN_TILE
    tok = lambda n: pl.BlockSpec((tm, n), lambda i: (i, 0))
    return pl.pallas_call(
        functools.partial(_outproj_ffn_body, weight=weight),
        grid=(m // tm,),
        in_specs=[
            tok(d), tok(o1.shape[1]), tok(o2.shape[1]), _resident_spec(w_out.shape), _const_spec((1, d)),
            _const_spec((1, d)), _const_spec((1, d)),
            _resident_spec((d, d_ff)), _resident_spec((d, d_ff)), _resident_spec((d_ff, d)),
        ],
        out_specs=tok(d),
        out_shape=jax.ShapeDtypeStruct((m, d), F32),
        scratch_shapes=[pltpu.VMEM((tm, d), F32)],
        compiler_params=_cparams(1),
        name="outproj_ffn",
    )(x2, o1, o2, w_out, mix_post.reshape(1, d), pre.reshape(1, d), post.reshape(1, d), wg, wu, wd)


def _t5_bucket(dist):
    n = jnp.maximum(dist, 0)
    exact = REL_BUCKETS // 2
    nf = jnp.maximum(n, 1).astype(jnp.float32)
    large = exact + (jnp.log(nf / exact) / math.log(REL_MAX_DIST / exact) * (REL_BUCKETS - exact)).astype(jnp.int32)
    return jnp.where(n < exact, n, jnp.minimum(large, REL_BUCKETS - 1))


def _tile_bucket_range(d):
    dist = np.arange(max(d * TILE - (TILE - 1), 0), d * TILE + TILE)
    exact = REL_BUCKETS // 2
    large = exact + np.log(np.maximum(dist, 1) / exact) / math.log(REL_MAX_DIST / exact) * (REL_BUCKETS - exact)
    bucket = np.where(dist < exact, dist, np.minimum(large.astype(np.int64), REL_BUCKETS - 1))
    return max(int(bucket.min()) - 1, 0), min(int(bucket.max()) + 1, REL_BUCKETS - 1)


def _bias_body(tab_ref, idx_ref, o_ref):
    slot = pl.program_id(0)
    for d in range(BIAS_TILES):
        idx = idx_ref[d]
        lo, hi = _tile_bucket_range(d)
        v = jnp.full((TILE, TILE), LOG2E * tab_ref[lo * REL_SLOTS + slot], F32)
        for b in range(lo + 1, hi + 1):
            v = jnp.where(idx == b, LOG2E * tab_ref[b * REL_SLOTS + slot], v)
        o_ref[0, d] = v


def _bias_tiles(rel_table):
    n_slots = rel_table.shape[1]
    d = jnp.arange(BIAS_TILES)[:, None, None]
    key = jnp.arange(TILE)[None, :, None]
    query = jnp.arange(TILE)[None, None, :]
    idx = _t5_bucket(d * TILE + query - key).astype(jnp.int32)
    return pl.pallas_call(
        _bias_body,
        grid=(n_slots,),
        in_specs=[
            pl.BlockSpec(memory_space=pltpu.SMEM),
            _const_spec((BIAS_TILES, TILE, TILE)),
        ],
        out_specs=pl.BlockSpec((1, BIAS_TILES, TILE, TILE), lambda s: (s, 0, 0, 0)),
        out_shape=jax.ShapeDtypeStruct((n_slots, BIAS_TILES, TILE, TILE), F32),
        compiler_params=_cparams(1),
        name="bias_tiles",
    )(rel_table.astype(F32).reshape(-1), idx)


def _proj0_body(x_ref, pre_ref, w_ref, qn_ref, wq1_ref, wq2_ref, kvn_ref, wk_ref, wv_ref, cos_ref, sin_ref,
                qa_ref, ka_ref, vat_ref, qb_ref, kb_ref, vbt_ref):
    h = _rms(x_ref[...], pre_ref[...]).astype(BF16)
    y = _dot(h, w_ref[...])
    qa_ref[...] = (y[:, 0:512] * (LOG2E * HEAD_DIM ** -0.5)).astype(BF16)
    ka_ref[...] = y[:, 512:640].astype(BF16)
    for j in range(y.shape[0] // TILE):
        _store_vt_ones(vat_ref, j, y[j * TILE:(j + 1) * TILE, 640:768], 2)
    cq = _rms(y[:, 768:1024], qn_ref[...]).astype(BF16)
    ckv = _rms(y[:, 1024:1152], kvn_ref[...]).astype(BF16)
    cos_t = cos_ref[...]
    sin_t = sin_ref[...]
    lane = lax.broadcasted_iota(jnp.int32, cos_t.shape, 1)
    scale = LOG2E * (MLA_NOPE + MLA_ROPE) ** -0.5
    qmul = scale * (cos_t + jnp.where(lane < MLA_NOPE, 1.0, 0.0))
    qsin = scale * sin_t
    kr = y[:, 1152:1280] * cos_t + y[:, 1280:1408] * sin_t
    q1 = _dot(cq, wq1_ref[...])
    q2 = _dot(cq, wq2_ref[...])
    k1 = _dot(ckv, wk_ref[...])
    for hd in range(MLA_HEADS):
        sl = slice(hd * LANES, (hd + 1) * LANES)
        qb_ref[:, sl] = (q1[:, sl] * qmul + q2[:, sl] * qsin).astype(BF16)
        kb_ref[:, sl] = (k1[:, sl] + kr).astype(BF16)
    vb = _dot(ckv, wv_ref[...])
    for j in range(vb.shape[0] // BLK):
        _store_vt_ones(vbt_ref, j, vb[j * BLK:(j + 1) * BLK], MLA_HEADS)


def _rope_lane_tables(s):
    inv = ROPE_THETA ** (-jnp.arange(0, MLA_ROPE, 2, dtype=jnp.float32) / MLA_ROPE)
    ang = jnp.arange(s, dtype=jnp.float32)[:, None] * inv[None, :]
    cos, sin = jnp.cos(ang), jnp.sin(ang)
    z_lo = jnp.zeros((s, MLA_NOPE), F32)
    z_hi = jnp.zeros((s, LANES - MLA_NOPE - MLA_ROPE), F32)
    return (jnp.concatenate([z_lo, cos, cos, z_hi], axis=1), jnp.concatenate([z_lo, sin, sin, z_hi], axis=1))


def _rot_cols(w):
    half = w.shape[-1] // 2
    return jnp.concatenate([-w[..., half:], w[..., :half]], axis=-1)


def _proj0(x2, seq, pre, w_in, q_norm, w_uq, kv_norm, w_ukv):
    m, d = x2.shape
    tm = TOKEN_TILE
    kr_w = w_in[:, 1152:1184]
    z64 = jnp.zeros((d, 64), F32)
    z32 = jnp.zeros((d, 32), F32)
    w0 = jnp.concatenate([w_in[:, :1152], z64, kr_w, z32, z64, _rot_cols(kr_w), z32], axis=1).astype(BF16)
    wq = w_uq.reshape(-1, MLA_HEADS, MLA_NOPE + MLA_ROPE)
    rq = wq.shape[0]
    zq = jnp.zeros((rq, MLA_HEADS, 32), F32)
    wq1 = jnp.concatenate([wq, zq], axis=2).reshape(rq, -1).astype(BF16)
    wq2 = jnp.concatenate([jnp.zeros((rq, MLA_HEADS, 64), F32), _rot_cols(wq[:, :, MLA_NOPE:]), zq],
                          axis=2).reshape(rq, -1).astype(BF16)
    wkv = w_ukv.reshape(-1, MLA_HEADS, MLA_NOPE + MLA_V)
    rk = wkv.shape[0]
    wk = jnp.concatenate([wkv[:, :, :MLA_NOPE], jnp.zeros((rk, MLA_HEADS, 64), F32)], axis=2).reshape(rk, -1).astype(BF16)
    wv = wkv[:, :, MLA_NOPE:].reshape(rk, -1).astype(BF16)
    cos_t, sin_t = _rope_lane_tables(seq)
    n_st = seq // tm
    tok = lambda n: pl.BlockSpec((tm, n), lambda i: (i, 0))
    outs = pl.pallas_call(
        _proj0_body,
        grid=(m // tm,),
        in_specs=[
            tok(d), _const_spec((1, d)), _const_spec(w0.shape),
            _const_spec((1, rq)), _const_spec(wq1.shape), _const_spec(wq2.shape),
            _const_spec((1, rk)), _const_spec(wk.shape), _const_spec(wv.shape),
            pl.BlockSpec((tm, LANES), lambda i: (i % n_st, 0)),
            pl.BlockSpec((tm, LANES), lambda i: (i % n_st, 0)),
        ],
        out_specs=[tok(512), tok(128),
                   pl.BlockSpec((1, tm // TILE, 2 * VT_ROWS, TILE), lambda i: (i // n_st, i % n_st, 0, 0)),
                   tok(1024), tok(1024),
                   pl.BlockSpec((1, tm // BLK, MLA_HEADS * VT_ROWS, BLK), lambda i: (i // n_st, i % n_st, 0, 0))],
        out_shape=[jax.ShapeDtypeStruct((m, 512), BF16), jax.ShapeDtypeStruct((m, 128), BF16),
                   jax.ShapeDtypeStruct((m // seq, seq // TILE, 2 * VT_ROWS, TILE), BF16),
                   jax.ShapeDtypeStruct((m, 1024), BF16), jax.ShapeDtypeStruct((m, 1024), BF16),
                   jax.ShapeDtypeStruct((m // seq, seq // BLK, MLA_HEADS * VT_ROWS, BLK), BF16)],
        compiler_params=_cparams(1),
        name="proj0",
    )(x2, pre.reshape(1, d), w0, q_norm.reshape(1, rq), wq1, wq2, kv_norm.reshape(1, rk), wk, wv, cos_t, sin_t)
    return outs


def _swa_body(sink_ref, q_ref, k_ref, vt_ref, bias_ref, o_ref, *, n_kv, grp):
    qi = pl.program_id(1)
    halves = BLK // TILE
    items = [(hkv, half) for hkv in range(n_kv) for half in range(halves)]
    width = grp * TILE
    krow = lax.broadcasted_iota(jnp.int32, (TILE, len(items) * width), 0)
    qcol = lax.broadcasted_iota(jnp.int32, (TILE, len(items) * width), 1) & (TILE - 1)

    cur_s, prev_s, sink_parts, has_prev = [], [], [], []
    for hkv, half in items:
        sl = slice(hkv * HEAD_DIM, (hkv + 1) * HEAD_DIM)
        t = qi * halves + half
        tp = jnp.maximum(t - 1, 0)
        qs = jnp.concatenate([q_ref[0, half * TILE:(half + 1) * TILE, (hkv * grp + g) * HEAD_DIM:
                                    (hkv * grp + g + 1) * HEAD_DIM] for g in range(grp)], axis=0)
        bias_c = jnp.concatenate([bias_ref[hkv * grp + g, 0] for g in range(grp)], axis=1)
        bias_p = jnp.concatenate([bias_ref[hkv * grp + g, 1] for g in range(grp)], axis=1)
        cur_s.append(_dot_nt(k_ref[0, pl.ds(pl.multiple_of(t * TILE, TILE), TILE), sl], qs) + bias_c)
        prev_s.append(_dot_nt(k_ref[0, pl.ds(pl.multiple_of(tp * TILE, TILE), TILE), sl], qs) + bias_p)
        sink_parts.extend(jnp.full((1, TILE), LOG2E * sink_ref[hkv * grp + g], F32) for g in range(grp))
        has_prev.append(jnp.full((1, width), jnp.where(t >= 1, 0, TILE), jnp.int32))
    s_c = jnp.where(krow <= qcol, jnp.concatenate(cur_s, axis=1), NEG_INF)
    s_p = jnp.where(krow > qcol + jnp.concatenate(has_prev, axis=1),
                    jnp.concatenate(prev_s, axis=1), NEG_INF)
    sink = jnp.concatenate(sink_parts, axis=1)
    mx = jnp.maximum(jnp.maximum(jnp.max(s_c, axis=0, keepdims=True), jnp.max(s_p, axis=0, keepdims=True)), sink)
    p_c = jnp.exp2(s_c - mx).astype(BF16)
    p_p = jnp.exp2(s_p - mx).astype(BF16)
    accs = []
    for i, (hkv, half) in enumerate(items):
        vsl = slice(hkv * VT_ROWS, (hkv + 1) * VT_ROWS)
        ls = slice(i * width, (i + 1) * width)
        t = qi * halves + half
        accs.append(_dot(vt_ref[0, t, vsl, :], p_c[:, ls]) + _dot(vt_ref[0, jnp.maximum(t - 1, 0), vsl, :], p_p[:, ls]))
    acc = jnp.concatenate(accs, axis=1)
    o_t = acc[0:HEAD_DIM] / (acc[HEAD_DIM:HEAD_DIM + 1] + jnp.exp2(sink - mx))
    heads = []
    for hkv in range(n_kv):
        for g in range(grp):
            heads.append(jnp.concatenate(
                [o_t[:, (hkv * halves + half) * width + g * TILE:(hkv * halves + half) * width + (g + 1) * TILE]
                 for half in range(halves)], axis=1))
    o_ref[0] = jnp.concatenate(heads, axis=0).T.astype(BF16)


def _swa(qa, ka, vat, bias_a, sinks):
    b, s, width = qa.shape
    n_kv = ka.shape[-1] // HEAD_DIM
    grp = width // HEAD_DIM // n_kv
    return pl.pallas_call(
        functools.partial(_swa_body, n_kv=n_kv, grp=grp),
        grid=(b, s // BLK),
        in_specs=[
            pl.BlockSpec(memory_space=pltpu.SMEM),
            pl.BlockSpec((1, BLK, width), lambda bi, qi: (bi, qi, 0)),
            pl.BlockSpec((1, s, n_kv * HEAD_DIM), lambda bi, qi: (bi, 0, 0)),
            pl.BlockSpec((1, s // TILE, n_kv * VT_ROWS, TILE), lambda bi, qi: (bi, 0, 0, 0)),
            _const_spec(bias_a.shape),
        ],
        out_specs=pl.BlockSpec((1, BLK, width), lambda bi, qi: (bi, qi, 0)),
        out_shape=jax.ShapeDtypeStruct((b, s, width), BF16),
        compiler_params=_cparams(2),
        name="swa",
    )(sinks.astype(F32), qa, ka, vat, bias_a)


def _finish(acc, dv):
    return acc[0:dv] / acc[dv:dv + 1]


def _mla_body(q_ref, k_ref, vt_ref, o_ref, acc_ref, s_ref, *, n_heads):
    qi = pl.program_id(1)
    lanes = n_heads * BLK
    krow = lax.broadcasted_iota(jnp.int32, (BLK, lanes), 0)
    qcol = lax.broadcasted_iota(jnp.int32, (BLK, lanes), 1) & (BLK - 1)

    def scores(n):
        st = pl.multiple_of(n * BLK, BLK)
        return jnp.concatenate(
            [_dot_nt(k_ref[0, pl.ds(st, BLK), hd * LANES:(hd + 1) * LANES], q_ref[0, :, hd * LANES:(hd + 1) * LANES])
             for hd in range(n_heads)], axis=1)

    def weighted_values(n, p):
        return jnp.concatenate(
            [_dot(vt_ref[0, n, hd * VT_ROWS:(hd + 1) * VT_ROWS, :], p[:, hd * BLK:(hd + 1) * BLK])
             for hd in range(n_heads)], axis=1)

    def produce_first(slot):
        s = jnp.where(krow <= qcol, scores(qi), NEG_INF)
        s_ref[slot] = s
        return jnp.max(s, axis=0, keepdims=True)

    def produce(i, slot):
        s = scores(i - 1)
        s_ref[slot] = s
        return jnp.max(s, axis=0, keepdims=True)

    def consume(i, slot, block_max, m):
        n = jnp.where(i == 0, qi, i - 1)
        m_new = jnp.maximum(m, block_max)
        p = jnp.exp2(s_ref[slot] - m_new).astype(BF16)
        acc_ref[...] = jnp.exp2(m - m_new) * acc_ref[...] + weighted_values(n, p)
        return m_new

    acc_ref[...] = jnp.zeros_like(acc_ref)
    _pipelined_blocks(qi + 1, produce_first, produce, consume, jnp.full((1, lanes), NEG_INF, F32))
    o_t = _finish(acc_ref[...], MLA_V)
    o_ref[0] = jnp.concatenate([o_t[:, hd * BLK:(hd + 1) * BLK] for hd in range(n_heads)],
                               axis=0).T.astype(BF16)


def _mla(qb, kb, vbt):
    b, s, width = qb.shape
    n_heads = width // LANES
    n_blocks = s // BLK
    return pl.pallas_call(
        functools.partial(_mla_body, n_heads=n_heads),
        grid=(b, n_blocks),
        in_specs=[
            pl.BlockSpec((1, BLK, width), lambda bi, qi: (bi, qi, 0)),
            pl.BlockSpec((1, s, width), lambda bi, qi: (bi, 0, 0)),
            pl.BlockSpec((1, n_blocks, n_heads * VT_ROWS, BLK), lambda bi, qi: (bi, 0, 0, 0)),
        ],
        out_specs=pl.BlockSpec((1, BLK, n_heads * MLA_V), lambda bi, qi: (bi, qi, 0)),
        out_shape=jax.ShapeDtypeStruct((b, s, n_heads * MLA_V), BF16),
        scratch_shapes=[pltpu.VMEM((VT_ROWS, n_heads * BLK), F32), pltpu.VMEM((2, BLK, n_heads * BLK), F32)],
        compiler_params=_cparams(2),
        name="mla",
    )(qb, kb, vbt)


def _store_vt_ones(vt_ref, j, v, n_heads):
    vt = v.T.astype(BF16)
    ones = jnp.ones((ONES_ROWS, v.shape[0]), BF16)
    for hd in range(n_heads):
        vt_ref[0, j, hd * VT_ROWS:hd * VT_ROWS + HEAD_DIM, :] = vt[hd * HEAD_DIM:(hd + 1) * HEAD_DIM]
        vt_ref[0, j, hd * VT_ROWS + HEAD_DIM:(hd + 1) * VT_ROWS, :] = ones


def _proj1_body(x_ref, pre_ref, w_ref, qc_ref, kc_ref, vc_ref, ks_ref, vst_ref, kw_ref, vwt_ref, g_ref,
                qd_ref, kd_ref, vdt_ref, km_ref, *, tm):
    h = _rms(x_ref[...], pre_ref[...]).astype(BF16)
    y = _dot(h, w_ref[...])
    scale = LOG2E * HEAD_DIM ** -0.5
    qc_ref[...] = (y[:, 0:512] * scale).astype(BF16)
    for hkv in range(2):
        kc_ref[0, hkv] = y[:, 512 + hkv * 64:576 + hkv * 64].astype(BF16)
        vc_ref[0, hkv] = y[:, 640 + hkv * 64:704 + hkv * 64].astype(BF16)
    ks_ref[...] = y[:, 768:896].astype(BF16)
    vs = y[:, 896:1024]
    kw_ref[...] = y[:, 1024:1152].astype(BF16)
    vw = y[:, 1152:1280]
    gl = y[:, 1280:1408]
    g_ref[...] = 1.0 / (1.0 + jnp.exp(-gl))
    qd_ref[...] = (y[:, 1408:1920] * scale).astype(BF16)
    kd = y[:, 1920:2432]
    kd_ref[...] = kd.astype(BF16)
    vd = y[:, 2432:2944]
    for j in range(tm // BLK):
        rows = slice(j * BLK, (j + 1) * BLK)
        km_ref[j] = jnp.mean(kd[rows], axis=0, keepdims=True)
        _store_vt_ones(vdt_ref, j, vd[rows], 8)
        _store_vt_ones(vst_ref, j, vs[rows], 2)
        _store_vt_ones(vwt_ref, j, vw[rows], 2)


def _proj1(x2, batch, seq, pre, w_in):
    m, d = x2.shape
    tm = TOKEN_TILE
    n_st = seq // tm
    pad = jnp.zeros((d, LANES - 24), F32)
    w1 = jnp.concatenate([w_in[:, :1304], pad, w_in[:, 1304:]], axis=1).astype(BF16)
    tok = lambda n: pl.BlockSpec((tm, n), lambda i: (i, 0))
    hm = pl.BlockSpec((1, 2, tm, HEAD_DIM), lambda i: (i // n_st, 0, i % n_st, 0))
    nblk = tm // BLK
    vt_spec = lambda rows: pl.BlockSpec((1, nblk, rows, BLK), lambda i: (i // n_st, i % n_st, 0, 0))
    vt_shape = lambda rows: jax.ShapeDtypeStruct((batch, seq // BLK, rows, BLK), BF16)
    return pl.pallas_call(
        functools.partial(_proj1_body, tm=tm),
        grid=(m // tm,),
        in_specs=[tok(d), _const_spec((1, d)), _const_spec(w1.shape)],
        out_specs=[tok(512), hm, hm, tok(128), vt_spec(2 * VT_ROWS), tok(128), vt_spec(2 * VT_ROWS), tok(128), tok(512), tok(512),
                   vt_spec(8 * VT_ROWS),
                   pl.BlockSpec((nblk, 1, 512), lambda i: (i, 0, 0))],
        out_shape=[
            jax.ShapeDtypeStruct((m, 512), BF16),
            jax.ShapeDtypeStruct((batch, 2, seq, HEAD_DIM), BF16),
            jax.ShapeDtypeStruct((batch, 2, seq, HEAD_DIM), BF16),
            jax.ShapeDtypeStruct((m, 128), BF16), vt_shape(2 * VT_ROWS),
            jax.ShapeDtypeStruct((m, 128), BF16), vt_shape(2 * VT_ROWS),
            jax.ShapeDtypeStruct((m, 128), F32),
            jax.ShapeDtypeStruct((m, 512), BF16), jax.ShapeDtypeStruct((m, 512), BF16),
            vt_shape(8 * VT_ROWS),
            jax.ShapeDtypeStruct((m // MOBA_BLOCK, 1, 512), F32),
        ],
        compiler_params=_cparams(1),
        name="proj1",
    )(x2, pre.reshape(1, d), w1)


def _compress_body(kx_ref, vx_ref, pek_ref, pev_ref, wk1_ref, wk2_ref, wv1_ref, wv2_ref, ko_ref, vo_ref):
    half = CMP_STRIDE * HEAD_DIM
    for x_ref, pe_ref, w1_ref, w2_ref, o_ref in ((kx_ref, pek_ref, wk1_ref, wk2_ref, ko_ref),
                                                 (vx_ref, pev_ref, wv1_ref, wv2_ref, vo_ref)):
        pe8 = jnp.broadcast_to(pe_ref[...], (8, 2 * half)).astype(BF16)
        pe_term = _dot(pe8, w1_ref[...])[0:1]
        outs = []
        for hkv in range(2):
            x = x_ref[0, hkv]
            lo = _dot(x, w1_ref[0:half])
            hi = _dot(x, w1_ref[half:2 * half])
            hid = lo + pltpu.roll(hi, x.shape[0] - 1, 0) + pe_term
            outs.append(_dot(_silu(hid).astype(BF16), w2_ref[...]))
        out = jnp.concatenate(outs, axis=1)
        o_ref[0] = (out.T if o_ref is vo_ref else out).astype(BF16)


def _compress(kc_hm, vc_hm, pe_k, pe_v, wk1, wk2, wv1, wv2):
    b, _, s, dh = kc_hm.shape
    rows = s // CMP_STRIDE
    feat = CMP_STRIDE * dh
    kx = kc_hm.reshape(b, 2, rows, feat)
    vx = vc_hm.reshape(b, 2, rows, feat)
    xspec = pl.BlockSpec((1, 2, rows, feat), lambda bi: (bi, 0, 0, 0))
    ospec = pl.BlockSpec((1, rows, 2 * dh), lambda bi: (bi, 0, 0))
    w1s, w2s = wk1.shape, wk2.shape
    return pl.pallas_call(
        _compress_body,
        grid=(b,),
        in_specs=[xspec, xspec, _const_spec((1, 2 * feat)), _const_spec((1, 2 * feat)),
                  _const_spec(w1s), _const_spec(w2s), _const_spec(w1s), _const_spec(w2s)],
        out_specs=[ospec, ospec],
        out_shape=[jax.ShapeDtypeStruct((b, rows, 2 * dh), BF16)] * 2,
        compiler_params=_cparams(1),
        name="nsa_compress",
    )(kx, vx, pe_k.astype(F32).reshape(1, -1), pe_v.astype(F32).reshape(1, -1),
      wk1.astype(BF16), wk2.astype(BF16), wv1.astype(BF16), wv2.astype(BF16))


def _bias_block(bias_ref, hd, dist_blocks):
    sub = BLK // TILE
    rows = []
    for a in range(sub):
        cols = []
        for bq in range(sub):
            e = sub * dist_blocks + bq - a
            e = max(e, 0) if isinstance(e, int) else jnp.clip(e, 0, BIAS_TILES - 1)
            cols.append(bias_ref[hd, e])
        rows.append(jnp.concatenate(cols, axis=1))
    return jnp.concatenate(rows, axis=0)


def _rank_rows(val, ridx, n_real):
    rank = jnp.zeros(val.shape, F32)
    for j in range(n_real):
        vj = val[j:j + 1, :]
        rank = rank + jnp.where(vj > val, 1.0, jnp.where(vj == val, jnp.where(ridx > j, 1.0, 0.0), 0.0))
    return rank


def _nsa_body(q_ref, kc_ref, vct_ref, ks_ref, vst_ref, kw_ref, vwt_ref, g_ref, bias_ref, ovt_ref, o_ref,
              pen_ref, acc_ref, s_ref, *, n_sel):
    qi = pl.program_id(1)
    grp = 4
    lanes = grp * BLK
    sub = BLK // SEL_BLOCK
    own = pl.multiple_of(qi * BLK, BLK)
    prev = pl.multiple_of(jnp.maximum(qi - 1, 0) * BLK, BLK)
    prev_blk = jnp.maximum(qi - 1, 0)
    krow = lax.broadcasted_iota(jnp.int32, (BLK, lanes), 0)
    qcol = lax.broadcasted_iota(jnp.int32, (BLK, lanes), 1) & (BLK - 1)
    causal = krow <= qcol
    crow = lax.broadcasted_iota(jnp.int32, (TILE, lanes), 0)
    cq = lax.broadcasted_iota(jnp.int32, (TILE, lanes), 1) & (BLK - 1)
    visible = CMP_STRIDE * crow + (CMP_BLOCK - 1) <= qi * BLK + cq
    jrow = lax.broadcasted_iota(jnp.int32, (n_sel, BLK), 0)
    cur = qi * sub + (lax.broadcasted_iota(jnp.int32, (n_sel, BLK), 1) >> 6)
    forced = (jrow == 0) | (jrow == cur) | (jrow == cur - 1)
    gates_t = g_ref[0].T
    has_prev = jnp.where(qi >= 1, 0, BLK)

    def stack_q(hkv):
        return jnp.concatenate(
            [q_ref[0, :, (hkv * grp + g) * HEAD_DIM:(hkv * grp + g + 1) * HEAD_DIM] for g in range(grp)], axis=0)

    def group_bias(hkv, dist_blocks):
        return jnp.concatenate([_bias_block(bias_ref, hkv * grp + g, dist_blocks) for g in range(grp)], axis=1)

    def sel_scores(hkv, n, dist_blocks):
        st = pl.multiple_of(n * BLK, BLK)
        sl = slice(hkv * HEAD_DIM, (hkv + 1) * HEAD_DIM)
        return _dot_nt(ks_ref[0, pl.ds(st, BLK), sl], stack_q(hkv)) + group_bias(hkv, dist_blocks)

    def sel_pens(hkv, n):
        pens = pen_ref[hkv, n, 0:sub, :]
        return [jnp.concatenate([pens[j:j + 1]] * grp, axis=1) for j in range(sub)]

    def sel_max(s, pens):
        mj = [jnp.max(s[j * SEL_BLOCK:(j + 1) * SEL_BLOCK], axis=0, keepdims=True) + pens[j] for j in range(sub)]
        return functools.reduce(jnp.maximum, mj)

    def sel_probs(s, pens, m_new):
        return jnp.concatenate([jnp.exp2(s[j * SEL_BLOCK:(j + 1) * SEL_BLOCK] - (m_new - pens[j]))
                                for j in range(sub)], axis=0).astype(BF16)

    o_cmp, o_win = [], []
    for hkv in range(2):
        sl = slice(hkv * HEAD_DIM, (hkv + 1) * HEAD_DIM)
        vsl = slice(hkv * VT_ROWS, (hkv + 1) * VT_ROWS)
        qs = stack_q(hkv)

        sc = jnp.where(visible, _dot_nt(kc_ref[0, :, sl], qs), NEG_INF)
        mc = jnp.max(sc, axis=0, keepdims=True)
        ec = jnp.where(visible, jnp.exp2(sc - mc), 0.0)
        lc = jnp.sum(ec, axis=0, keepdims=True)
        p = ec / jnp.where(lc > 0.0, lc, 1.0)
        o_cmp.append(_dot(vct_ref[0, sl, :], p.astype(BF16)))

        ps = p[:, 0:BLK] + p[:, BLK:2 * BLK] + p[:, 2 * BLK:3 * BLK] + p[:, 3 * BLK:4 * BLK]
        ps_hi = ps.astype(BF16)
        ps_lo = (ps - ps_hi.astype(F32)).astype(BF16)
        imp = _dot(ovt_ref[...], ps_hi) + _dot(ovt_ref[...], ps_lo)
        val = jnp.where(forced, FORCE_SCORE, jnp.where(jrow <= cur, imp, NEG_INF))
        pen = jnp.where(_rank_rows(val, jrow, n_sel) < SEL_TOPK, 0.0, NEG_INF)
        for t in range(n_sel // sub):
            pen_ref[hkv, t, 0:sub, :] = pen[t * sub:(t + 1) * sub]

        s0 = _dot_nt(kw_ref[0, pl.ds(own, BLK), sl], qs) + group_bias(hkv, 0)
        s1 = _dot_nt(kw_ref[0, pl.ds(prev, BLK), sl], qs) + group_bias(hkv, 1)
        s0 = jnp.where(causal, s0, NEG_INF)
        s1 = jnp.where(krow > qcol + has_prev, s1, NEG_INF)
        mw = jnp.maximum(jnp.max(s0, axis=0, keepdims=True), jnp.max(s1, axis=0, keepdims=True))
        ow = (_dot(vwt_ref[0, qi, vsl, :], jnp.exp2(s0 - mw).astype(BF16))
              + _dot(vwt_ref[0, prev_blk, vsl, :], jnp.exp2(s1 - mw).astype(BF16)))
        o_win.append(_finish(ow, HEAD_DIM))

    def produce_first(slot):
        stats = []
        for hkv in range(2):
            s = jnp.where(causal, sel_scores(hkv, qi, 0), NEG_INF)
            s_ref[slot, hkv] = s
            stats.append(sel_max(s, sel_pens(hkv, qi)))
        return tuple(stats)

    def produce(i, slot):
        stats = []
        for hkv in range(2):
            s = sel_scores(hkv, i - 1, qi - (i - 1))
            s_ref[slot, hkv] = s
            stats.append(sel_max(s, sel_pens(hkv, i - 1)))
        return tuple(stats)

    def consume(i, slot, block_max, ms):
        n = jnp.where(i == 0, qi, i - 1)
        new = []
        for hkv in range(2):
            vsl = slice(hkv * VT_ROWS, (hkv + 1) * VT_ROWS)
            m_new = jnp.maximum(ms[hkv], block_max[hkv])
            pb = sel_probs(s_ref[slot, hkv], sel_pens(hkv, n), m_new)
            acc_ref[hkv] = jnp.exp2(ms[hkv] - m_new) * acc_ref[hkv] + _dot(vst_ref[0, n, vsl, :], pb)
            new.append(m_new)
        return tuple(new)

    acc_ref[...] = jnp.zeros_like(acc_ref)
    m_init = jnp.full((1, lanes), NEG_INF, F32)
    _pipelined_blocks(qi + 1, produce_first, produce, consume, (m_init, m_init))
    outs = []
    for hkv in range(2):
        o_slc = _finish(acc_ref[hkv], HEAD_DIM)
        for g in range(grp):
            hd = hkv * grp + g
            ls = slice(g * BLK, (g + 1) * BLK)
            outs.append(gates_t[hd:hd + 1] * o_cmp[hkv][:, ls] + gates_t[8 + hd:9 + hd] * o_slc[:, ls]
                        + gates_t[16 + hd:17 + hd] * o_win[hkv][:, ls])
    o_ref[0] = jnp.concatenate(outs, axis=0).T.astype(BF16)


def _nsa(qc, kcmp, vcmpt, ks, vst, kw, vwt, gates, bias_c):
    b, s, width = qc.shape
    n_blocks = s // BLK
    n_sel = s // SEL_BLOCK
    n_c = (s - CMP_BLOCK) // CMP_STRIDE + 1
    j_start = np.arange(n_sel)[:, None] * SEL_BLOCK
    c_start = np.arange(TILE)[None, :] * CMP_STRIDE
    overlap_t = ((c_start < j_start + SEL_BLOCK) & (c_start + CMP_BLOCK > j_start)
                 & (np.arange(TILE)[None, :] < n_c))
    ovt = jnp.asarray(overlap_t, BF16)
    qtile = lambda n: pl.BlockSpec((1, BLK, n), lambda bi, qi: (bi, qi, 0))
    full = pl.BlockSpec((1, s, 128), lambda bi, qi: (bi, 0, 0))
    full_t = pl.BlockSpec((1, n_blocks, 2 * VT_ROWS, BLK), lambda bi, qi: (bi, 0, 0, 0))
    cmp_spec = pl.BlockSpec((1, TILE, 128), lambda bi, qi: (bi, 0, 0))
    return pl.pallas_call(
        functools.partial(_nsa_body, n_sel=n_sel),
        grid=(b, n_blocks),
        in_specs=[qtile(width), cmp_spec, cmp_spec, full, full_t, full, full_t, qtile(128),
                  _const_spec(bias_c.shape), _const_spec(ovt.shape)],
        out_specs=qtile(width),
        out_shape=jax.ShapeDtypeStruct((b, s, width), BF16),
        scratch_shapes=[pltpu.VMEM((2, n_blocks, 8, BLK), F32), pltpu.VMEM((2, VT_ROWS, 4 * BLK), F32),
                        pltpu.VMEM((2, 2, BLK, 4 * BLK), F32)],
        compiler_params=_cparams(2),
        name="nsa",
    )(qc, kcmp, vcmpt, ks, vst, kw, vwt, gates, bias_c, ovt)


def _moba_body(q_ref, k_ref, vt_ref, km_ref, bias_ref, o_ref, pen_ref, acc_ref, s_ref, *, n_heads, n_blocks):
    qi = pl.program_id(1)
    lanes = n_heads * BLK
    nidx = lax.broadcasted_iota(jnp.int32, (16, lanes), 0)
    krow = lax.broadcasted_iota(jnp.int32, (BLK, lanes), 0)
    qcol = lax.broadcasted_iota(jnp.int32, (BLK, lanes), 1) & (BLK - 1)

    def scores(n, dist_blocks):
        st = pl.multiple_of(n * BLK, BLK)
        parts = []
        for hd in range(n_heads):
            sl = slice(hd * HEAD_DIM, (hd + 1) * HEAD_DIM)
            parts.append(_dot_nt(k_ref[0, pl.ds(st, BLK), sl], q_ref[0, :, sl])
                         + _bias_block(bias_ref, hd, dist_blocks))
        return jnp.concatenate(parts, axis=1)

    def weighted_values(n, p):
        return jnp.concatenate(
            [_dot(vt_ref[0, n, hd * VT_ROWS:(hd + 1) * VT_ROWS, :], p[:, hd * BLK:(hd + 1) * BLK])
             for hd in range(n_heads)], axis=1)

    zpad = jnp.zeros((16 - n_blocks, HEAD_DIM), F32)
    gate = jnp.concatenate(
        [_dot_nt(jnp.concatenate([km_ref[0, :, hd * HEAD_DIM:(hd + 1) * HEAD_DIM], zpad], axis=0).astype(BF16),
                 q_ref[0, :, hd * HEAD_DIM:(hd + 1) * HEAD_DIM]) for hd in range(n_heads)], axis=1)
    val = jnp.where(nidx < qi, gate, NEG_INF)
    val = jnp.where(nidx < n_blocks, val, PAD_SCORE)
    keep = ((_rank_rows(val, nidx, n_blocks) < MOBA_TOPK) & (nidx < qi)) | (nidx == qi)
    pen_ref[...] = jnp.where(keep, 0.0, NEG_INF)

    def produce_first(slot):
        s = jnp.where(krow <= qcol, scores(qi, 0), NEG_INF)
        s_ref[slot] = s
        return jnp.max(s, axis=0, keepdims=True)

    def produce(i, slot):
        s = scores(i - 1, qi - (i - 1))
        s_ref[slot] = s
        return jnp.max(s, axis=0, keepdims=True) + pen_ref[pl.ds(i - 1, 1), :]

    def consume(i, slot, block_max, m):
        n = jnp.where(i == 0, qi, i - 1)
        pen = pen_ref[pl.ds(n, 1), :]
        m_new = jnp.maximum(m, block_max)
        p = jnp.exp2(s_ref[slot] - (m_new - pen)).astype(BF16)
        acc_ref[...] = jnp.exp2(m - m_new) * acc_ref[...] + weighted_values(n, p)
        return m_new

    acc_ref[...] = jnp.zeros_like(acc_ref)
    _pipelined_blocks(qi + 1, produce_first, produce, consume, jnp.full((1, lanes), NEG_INF, F32))
    o_t = _finish(acc_ref[...], HEAD_DIM)
    o_ref[0] = jnp.concatenate([o_t[:, hd * BLK:(hd + 1) * BLK] for hd in range(n_heads)],
                               axis=0).T.astype(BF16)


def _moba(qd, kd, vdt, kmean, bias_d):
    b, s, width = qd.shape
    n_heads = width // HEAD_DIM
    n_blocks = s // BLK
    return pl.pallas_call(
        functools.partial(_moba_body, n_heads=n_heads, n_blocks=n_blocks),
        grid=(b, n_blocks),
        in_specs=[
            pl.BlockSpec((1, BLK, width), lambda bi, qi: (bi, qi, 0)),
            pl.BlockSpec((1, s, width), lambda bi, qi: (bi, 0, 0)),
            pl.BlockSpec((1, n_blocks, 2 * width, BLK), lambda bi, qi: (bi, 0, 0, 0)),
            pl.BlockSpec((1, n_blocks, width), lambda bi, qi: (bi, 0, 0)),
            _const_spec(bias_d.shape),
        ],
        out_specs=pl.BlockSpec((1, BLK, width), lambda bi, qi: (bi, qi, 0)),
        out_shape=jax.ShapeDtypeStruct((b, s, width), BF16),
        scratch_shapes=[pltpu.VMEM((16, n_heads * BLK), F32), pltpu.VMEM((VT_ROWS, n_heads * BLK), F32),
                        pltpu.VMEM((2, BLK, n_heads * BLK), F32)],
        compiler_params=_cparams(2),
        name="moba",
    )(qd, kd, vdt, kmean.reshape(b, n_blocks, width), bias_d)


def kernel(x, rel_bias_table, l0_ffn1_pre, l0_ffn1_post, l0_ffn1_wg, l0_ffn1_wu, l0_ffn1_wd, l0_mix_pre, l0_mix_post, l0_w_in, l0_sinks, l0_mla_q_norm, l0_mla_w_uq, l0_mla_kv_norm, l0_mla_w_ukv, l0_w_out, l0_ffn2_pre, l0_ffn2_post, l0_ffn2_wg, l0_ffn2_wu, l0_ffn2_wd, l1_ffn1_pre, l1_ffn1_post, l1_ffn1_wg, l1_ffn1_wu, l1_ffn1_wd, l1_mix_pre, l1_mix_post, l1_w_in, l1_nsa_pe_k, l1_nsa_pe_v, l1_nsa_wk1, l1_nsa_wk2, l1_nsa_wv1, l1_nsa_wv2, l1_w_out, l1_ffn2_pre, l1_ffn2_post, l1_ffn2_wg, l1_ffn2_wu, l1_ffn2_wd):
    b, s, d = x.shape
    m = b * s
    x2 = x.reshape(m, d)
    tiles = _bias_tiles(rel_bias_table)
    bias_a = tiles[A_SLOT:A_SLOT + 8, 0:2]
    bias_c = tiles[C_SLOT:C_SLOT + 8]
    bias_d = tiles[D_SLOT:D_SLOT + 8]

    x2 = _ffn(x2, l0_ffn1_pre, l0_ffn1_post, l0_ffn1_wg, l0_ffn1_wu, l0_ffn1_wd, 0.5)
    qa, ka, vat, qb, kb, vbt = _proj0(x2, s, l0_mix_pre, l0_w_in, l0_mla_q_norm, l0_mla_w_uq,
                                      l0_mla_kv_norm, l0_mla_w_ukv)
    shp = lambda t: t.reshape(b, s, t.shape[-1])
    o_a = _swa(shp(qa), shp(ka), vat, bias_a, l0_sinks)
    o_b = _mla(shp(qb), shp(kb), vbt)
    x2 = _outproj_ffn(x2, o_a.reshape(m, -1), o_b.reshape(m, -1), l0_w_out, l0_mix_post,
                      l0_ffn2_pre, l0_ffn2_post, l0_ffn2_wg, l0_ffn2_wu, l0_ffn2_wd, 0.5)

    x2 = _ffn(x2, l1_ffn1_pre, l1_ffn1_post, l1_ffn1_wg, l1_ffn1_wu, l1_ffn1_wd, 0.5)
    qc, kc_hm, vc_hm, ks, vst, kw, vwt, gates, qd, kd, vdt, kmean = _proj1(x2, b, s, l1_mix_pre, l1_w_in)
    kcmp, vcmpt = _compress(kc_hm, vc_hm, l1_nsa_pe_k, l1_nsa_pe_v, l1_nsa_wk1, l1_nsa_wk2, l1_nsa_wv1, l1_nsa_wv2)
    o_c = _nsa(shp(qc), kcmp, vcmpt, shp(ks), vst, shp(kw), vwt, shp(gates), bias_c)
    o_d = _moba(shp(qd), shp(kd), vdt, kmean, bias_d)
    x2 = _outproj_ffn(x2, o_c.reshape(m, -1), o_d.reshape(m, -1), l1_w_out, l1_mix_post,
                      l1_ffn2_pre, l1_ffn2_post, l1_ffn2_wg, l1_ffn2_wu, l1_ffn2_wd, 0.5)
    return x2.reshape(b, s, d)
```

```python
import functools
import math

import numpy as np
import jax
import jax.numpy as jnp
from jax import lax
from jax.experimental import pallas as pl
from jax.experimental.pallas import tpu as pltpu

F32 = jnp.float32
BF16 = jnp.bfloat16

HEAD_DIM = 64
NORM_EPS = 1e-6
LOG2E = math.log2(math.e)
NEG_INF = -1e30
PAD_SCORE = -3e38
FORCE_SCORE = 1e9
REL_BUCKETS = 32
REL_MAX_DIST = 1024
REL_SLOTS = 24
A_SLOT, C_SLOT, D_SLOT = 0, 8, 16
A_HEADS, A_KV_HEADS = 8, 2
MLA_NOPE, MLA_ROPE, MLA_V = 64, 32, 64
MLA_HEADS, MLA_Q_RANK, MLA_KV_RANK = 8, 256, 128
ROPE_THETA = 10000.0
C_HEADS, C_KV_HEADS = 8, 2
CMP_BLOCK, CMP_STRIDE = 32, 16
SEL_BLOCK, SEL_TOPK = 64, 8
D_HEADS = 8
MOBA_TOPK = 3

LANES = 128
SUBLANES = 8
TILE = 128
P0_COLS = (A_HEADS * HEAD_DIM, A_KV_HEADS * HEAD_DIM, A_KV_HEADS * HEAD_DIM, MLA_Q_RANK, MLA_KV_RANK, LANES)
P1_COLS = ((C_HEADS * HEAD_DIM,) + (C_KV_HEADS * HEAD_DIM,) * 6 + (LANES,) + (D_HEADS * HEAD_DIM,) * 3)
N_GATES = 3 * C_HEADS
BLK = 256
ONES_ROWS = 16
GATE_ROWS = 16
VT_ROWS = HEAD_DIM + ONES_ROWS
BIAS_TILES = 9
TOKEN_TILE = 512
FF_CHUNK = 256
VMEM_LIMIT = 56 * 1024 * 1024


def _cparams(n_axes):
    return pltpu.CompilerParams(dimension_semantics=("arbitrary",) * n_axes, vmem_limit_bytes=VMEM_LIMIT)


def _dot(a, b):
    return jnp.dot(a, b, preferred_element_type=F32)


def _dot_nt(a, b):
    return lax.dot_general(a, b, (((1,), (1,)), ((), ())), preferred_element_type=F32)


def _rms(x, g):
    return x * lax.rsqrt(jnp.mean(x * x, axis=-1, keepdims=True) + NORM_EPS) * g


def _silu(x):
    return x / (1.0 + jnp.exp(-x))


def _col_slices(sizes):
    edges = np.cumsum((0,) + tuple(sizes)).tolist()
    return [slice(a, b) for a, b in zip(edges[:-1], edges[1:])]


def _const_spec(shape):
    nd = len(shape)
    return pl.BlockSpec(shape, lambda *_: (0,) * nd)


def _pipelined_blocks(length, produce_first, produce, consume, state):
    trips = (length - 1) // 2

    def trip(t, carry):
        state, stat_a = carry
        a = 2 * t
        stat_b = produce(a + 1, 1)
        state = consume(a, 0, stat_a, state)
        stat_next = produce(a + 2, 0)
        state = consume(a + 1, 1, stat_b, state)
        return state, stat_next

    carry = lax.fori_loop(0, trips, trip, (state, produce_first(0)))
    a = 2 * trips

    def tail_one(c):
        return consume(a, 0, c[1], c[0])

    def tail_two(c):
        stat_b = produce(a + 1, 1)
        return consume(a + 1, 1, stat_b, consume(a, 0, c[1], c[0]))

    return lax.cond(length - a == 1, tail_one, tail_two, carry)


def _resident_spec(shape):
    nd = len(shape)
    return pl.BlockSpec(shape, lambda *_: (0,) * nd, pipeline_mode=pl.Buffered(1))


def _ffn_tile(x, pre_ref, post_ref, wg_ref, wu_ref, wd_ref, acc_ref, weight):
    h = _rms(x, pre_ref[...]).astype(BF16)
    for c in range(wg_ref.shape[1] // FF_CHUNK):
        cols = slice(c * FF_CHUNK, (c + 1) * FF_CHUNK)
        g = _dot(h, wg_ref[:, cols].astype(BF16))
        u = _dot(h, wu_ref[:, cols].astype(BF16))
        a = (_silu(g) * u).astype(BF16)
        y = _dot(a, wd_ref[cols, :].astype(BF16))
        if c == 0:
            acc_ref[...] = y
        else:
            acc_ref[...] += y
    return x + weight * _rms(acc_ref[...], post_ref[...])


def _ffn_body(x_ref, pre_ref, post_ref, wg_ref, wu_ref, wd_ref, o_ref, acc_ref, *, weight):
    o_ref[...] = _ffn_tile(x_ref[...], pre_ref, post_ref, wg_ref, wu_ref, wd_ref, acc_ref, weight)


def _ffn(x2, pre, post, wg, wu, wd, weight):
    m, d = x2.shape
    d_ff = wg.shape[1]
    tm = TOKEN_TILE
    return pl.pallas_call(
        functools.partial(_ffn_body, weight=weight),
        grid=(m // tm,),
        in_specs=[
            pl.BlockSpec((tm, d), lambda i: (i, 0)),
            _const_spec((1, d)), _const_spec((1, d)),
            _resident_spec((d, d_ff)), _resident_spec((d, d_ff)), _resident_spec((d_ff, d)),
        ],
        out_specs=pl.BlockSpec((tm, d), lambda i: (i, 0)),
        out_shape=jax.ShapeDtypeStruct((m, d), F32),
        scratch_shapes=[pltpu.VMEM((tm, d), F32)],
        compiler_params=_cparams(1),
        name="ffn",
    )(x2, pre.reshape(1, d), post.reshape(1, d), wg, wu, wd)


def _outproj_ffn_body(x_ref, o1_ref, o2_ref, wo_ref, mpost_ref, pre_ref, post_ref, wg_ref, wu_ref, wd_ref,
                      o_ref, acc_ref, *, weight):
    n1 = o1_ref.shape[1]
    y = (_dot(o1_ref[...], wo_ref[0:n1, :].astype(BF16))
         + _dot(o2_ref[...], wo_ref[n1:, :].astype(BF16)))
    x = x_ref[...] + _rms(y, mpost_ref[...])
    o_ref[...] = _ffn_tile(x, pre_ref, post_ref, wg_ref, wu_ref, wd_ref, acc_ref, weight)


def _outproj_ffn(x2, o1, o2, w_out, mix_post, pre, post, wg, wu, wd, weight):
    m, d = x2.shape
    d_ff = wg.shape[1]
    tm = TOKEN_TILE
    tok = lambda n: pl.BlockSpec((tm, n), lambda i: (i, 0))
    return pl.pallas_call(
        functools.partial(_outproj_ffn_body, weight=weight),
        grid=(m // tm,),
        in_specs=[
            tok(d), tok(o1.shape[1]), tok(o2.shape[1]), _resident_spec(w_out.shape), _const_spec((1, d)),
            _const_spec((1, d)), _const_spec((1, d)),
            _resident_spec((d, d_ff)), _resident_spec((d, d_ff)), _resident_spec((d_ff, d)),
        ],
        out_specs=tok(d),
        out_shape=jax.ShapeDtypeStruct((m, d), F32),
        scratch_shapes=[pltpu.VMEM((tm, d), F32)],
        compiler_params=_cparams(1),
        name="outproj_ffn",
    )(x2, o1, o2, w_out, mix_post.reshape(1, d), pre.reshape(1, d), post.reshape(1, d), wg, wu, wd)


def _t5_bucket(dist):
    n = jnp.maximum(dist, 0)
    exact = REL_BUCKETS // 2
    nf = jnp.maximum(n, 1).astype(jnp.float32)
    large = exact + (jnp.log(nf / exact) / math.log(REL_MAX_DIST / exact) * (REL_BUCKETS - exact)).astype(jnp.int32)
    return jnp.where(n < exact, n, jnp.minimum(large, REL_BUCKETS - 1))


def _tile_bucket_range(d):
    dist = np.arange(max(d * TILE - (TILE - 1), 0), d * TILE + TILE)
    exact = REL_BUCKETS // 2
    large = exact + np.log(np.maximum(dist, 1) / exact) / math.log(REL_MAX_DIST / exact) * (REL_BUCKETS - exact)
    bucket = np.where(dist < exact, dist, np.minimum(large.astype(np.int64), REL_BUCKETS - 1))
    return max(int(bucket.min()) - 1, 0), min(int(bucket.max()) + 1, REL_BUCKETS - 1)


def _bias_body(tab_ref, idx_ref, o_ref):
    slot = pl.program_id(0)
    for d in range(BIAS_TILES):
        idx = idx_ref[d]
        lo, hi = _tile_bucket_range(d)
        v = jnp.full((TILE, TILE), LOG2E * tab_ref[lo * REL_SLOTS + slot], F32)
        for b in range(lo + 1, hi + 1):
            v = jnp.where(idx == b, LOG2E * tab_ref[b * REL_SLOTS + slot], v)
        o_ref[0, d] = v


def _bias_tiles(rel_table):
    n_slots = rel_table.shape[1]
    d = jnp.arange(BIAS_TILES)[:, None, None]
    key = jnp.arange(TILE)[None, :, None]
    query = jnp.arange(TILE)[None, None, :]
    idx = _t5_bucket(d * TILE + query - key).astype(jnp.int32)
    return pl.pallas_call(
        _bias_body,
        grid=(n_slots,),
        in_specs=[
            pl.BlockSpec(memory_space=pltpu.SMEM),
            _const_spec((BIAS_TILES, TILE, TILE)),
        ],
        out_specs=pl.BlockSpec((1, BIAS_TILES, TILE, TILE), lambda s: (s, 0, 0, 0)),
        out_shape=jax.ShapeDtypeStruct((n_slots, BIAS_TILES, TILE, TILE), F32),
        compiler_params=_cparams(1),
        name="bias_tiles",
    )(rel_table.astype(F32).reshape(-1), idx)


def _proj0_body(x_ref, pre_ref, w_ref, qn_ref, wq1_ref, wq2_ref, kvn_ref, wk_ref, wv_ref, cos_ref, sin_ref,
                qa_ref, ka_ref, vat_ref, qb_ref, kb_ref, vbt_ref):
    h = _rms(x_ref[...], pre_ref[...]).astype(BF16)
    y = _dot(h, w_ref[...])
    qa, ka, va, cq, ckv, rot = (y[:, sl] for sl in _col_slices(P0_COLS))
    qa_ref[...] = (qa * (LOG2E * HEAD_DIM ** -0.5)).astype(BF16)
    ka_ref[...] = ka.astype(BF16)
    for j in range(y.shape[0] // TILE):
        _store_vt_ones(vat_ref, j, va[j * TILE:(j + 1) * TILE], A_KV_HEADS)
    cq = _rms(cq, qn_ref[...]).astype(BF16)
    ckv = _rms(ckv, kvn_ref[...]).astype(BF16)
    cos_t = cos_ref[...]
    sin_t = sin_ref[...]
    lane = lax.broadcasted_iota(jnp.int32, cos_t.shape, 1)
    scale = LOG2E * (MLA_NOPE + MLA_ROPE) ** -0.5
    qmul = scale * (cos_t + jnp.where(lane < MLA_NOPE, 1.0, 0.0))
    qsin = scale * sin_t
    kr = rot * cos_t + pltpu.roll(rot, LANES // 2, 1) * sin_t
    q1 = _dot(cq, wq1_ref[...])
    q2 = _dot(cq, wq2_ref[...])
    k1 = _dot(ckv, wk_ref[...])
    for hd in range(MLA_HEADS):
        sl = slice(hd * LANES, (hd + 1) * LANES)
        qb_ref[:, sl] = (q1[:, sl] * qmul + q2[:, sl] * qsin).astype(BF16)
        kb_ref[:, sl] = (k1[:, sl] + kr).astype(BF16)
    vb = _dot(ckv, wv_ref[...])
    for j in range(vb.shape[0] // BLK):
        _store_vt_ones(vbt_ref, j, vb[j * BLK:(j + 1) * BLK], MLA_HEADS)


def _rope_lane_tables(s):
    inv = ROPE_THETA ** (-jnp.arange(0, MLA_ROPE, 2, dtype=jnp.float32) / MLA_ROPE)
    ang = jnp.arange(s, dtype=jnp.float32)[:, None] * inv[None, :]
    cos, sin = jnp.cos(ang), jnp.sin(ang)
    z_lo = jnp.zeros((s, MLA_NOPE), F32)
    z_hi = jnp.zeros((s, LANES - MLA_NOPE - MLA_ROPE), F32)
    return (jnp.concatenate([z_lo, cos, cos, z_hi], axis=1), jnp.concatenate([z_lo, sin, sin, z_hi], axis=1))


def _rot_cols(w):
    half = w.shape[-1] // 2
    return jnp.concatenate([-w[..., half:], w[..., :half]], axis=-1)


def _proj0(x2, seq, pre, w_in, q_norm, w_uq, kv_norm, w_ukv):
    m, d = x2.shape
    tm = TOKEN_TILE
    n_lin = sum(P0_COLS[:-1])
    kr_w = w_in[:, n_lin:n_lin + MLA_ROPE]
    z32 = jnp.zeros((d, MLA_ROPE), F32)
    w0 = jnp.concatenate([w_in[:, :n_lin], _rot_cols(kr_w), z32, kr_w, z32], axis=1).astype(BF16)
    wq = w_uq.reshape(-1, MLA_HEADS, MLA_NOPE + MLA_ROPE)
    rq = wq.shape[0]
    pad = LANES - MLA_NOPE - MLA_ROPE
    zq = jnp.zeros((rq, MLA_HEADS, pad), F32)
    wq1 = jnp.concatenate([wq, zq], axis=2).reshape(rq, -1).astype(BF16)
    wq2 = jnp.concatenate([jnp.zeros((rq, MLA_HEADS, MLA_NOPE), F32), _rot_cols(wq[:, :, MLA_NOPE:]), zq],
                          axis=2).reshape(rq, -1).astype(BF16)
    wkv = w_ukv.reshape(-1, MLA_HEADS, MLA_NOPE + MLA_V)
    rk = wkv.shape[0]
    wk = jnp.concatenate([wkv[:, :, :MLA_NOPE], jnp.zeros((rk, MLA_HEADS, LANES - MLA_NOPE), F32)],
                         axis=2).reshape(rk, -1).astype(BF16)
    wv = wkv[:, :, MLA_NOPE:].reshape(rk, -1).astype(BF16)
    cos_t, sin_t = _rope_lane_tables(seq)
    n_st = seq // tm
    tok = lambda n: pl.BlockSpec((tm, n), lambda i: (i, 0))
    outs = pl.pallas_call(
        _proj0_body,
        grid=(m // tm,),
        in_specs=[
            tok(d), _const_spec((1, d)), _const_spec(w0.shape),
            _const_spec((1, rq)), _const_spec(wq1.shape), _const_spec(wq2.shape),
            _const_spec((1, rk)), _const_spec(wk.shape), _const_spec(wv.shape),
            pl.BlockSpec((tm, LANES), lambda i: (i % n_st, 0)),
            pl.BlockSpec((tm, LANES), lambda i: (i % n_st, 0)),
        ],
        out_specs=[tok(P0_COLS[0]), tok(P0_COLS[1]),
                   pl.BlockSpec((1, tm // TILE, A_KV_HEADS * VT_ROWS, TILE), lambda i: (i // n_st, i % n_st, 0, 0)),
                   tok(MLA_HEADS * LANES), tok(MLA_HEADS * LANES),
                   pl.BlockSpec((1, tm // BLK, MLA_HEADS * VT_ROWS, BLK), lambda i: (i // n_st, i % n_st, 0, 0))],
        out_shape=[jax.ShapeDtypeStruct((m, P0_COLS[0]), BF16), jax.ShapeDtypeStruct((m, P0_COLS[1]), BF16),
                   jax.ShapeDtypeStruct((m // seq, seq // TILE, A_KV_HEADS * VT_ROWS, TILE), BF16),
                   jax.ShapeDtypeStruct((m, MLA_HEADS * LANES), BF16),
                   jax.ShapeDtypeStruct((m, MLA_HEADS * LANES), BF16),
                   jax.ShapeDtypeStruct((m // seq, seq // BLK, MLA_HEADS * VT_ROWS, BLK), BF16)],
        compiler_params=_cparams(1),
        name="proj0",
    )(x2, pre.reshape(1, d), w0, q_norm.reshape(1, rq), wq1, wq2, kv_norm.reshape(1, rk), wk, wv, cos_t, sin_t)
    return outs


def _swa_body(sink_ref, q_ref, k_ref, vt_ref, bias_ref, o_ref, *, n_kv, grp):
    qi = pl.program_id(1)
    halves = BLK // TILE
    items = [(hkv, half) for hkv in range(n_kv) for half in range(halves)]
    width = grp * TILE
    krow = lax.broadcasted_iota(jnp.int32, (TILE, len(items) * width), 0)
    qcol = lax.broadcasted_iota(jnp.int32, (TILE, len(items) * width), 1) & (TILE - 1)

    cur_s, prev_s, sink_parts, has_prev = [], [], [], []
    for hkv, half in items:
        sl = slice(hkv * HEAD_DIM, (hkv + 1) * HEAD_DIM)
        t = qi * halves + half
        tp = jnp.maximum(t - 1, 0)
        qs = jnp.concatenate([q_ref[0, half * TILE:(half + 1) * TILE, (hkv * grp + g) * HEAD_DIM:
                                    (hkv * grp + g + 1) * HEAD_DIM] for g in range(grp)], axis=0)
        bias_c = jnp.concatenate([bias_ref[hkv * grp + g, 0] for g in range(grp)], axis=1)
        bias_p = jnp.concatenate([bias_ref[hkv * grp + g, 1] for g in range(grp)], axis=1)
        cur_s.append(_dot_nt(k_ref[0, pl.ds(pl.multiple_of(t * TILE, TILE), TILE), sl], qs) + bias_c)
        prev_s.append(_dot_nt(k_ref[0, pl.ds(pl.multiple_of(tp * TILE, TILE), TILE), sl], qs) + bias_p)
        sink_parts.extend(jnp.full((1, TILE), LOG2E * sink_ref[hkv * grp + g], F32) for g in range(grp))
        has_prev.append(jnp.full((1, width), jnp.where(t >= 1, 0, TILE), jnp.int32))
    s_c = jnp.where(krow <= qcol, jnp.concatenate(cur_s, axis=1), NEG_INF)
    s_p = jnp.where(krow > qcol + jnp.concatenate(has_prev, axis=1),
                    jnp.concatenate(prev_s, axis=1), NEG_INF)
    sink = jnp.concatenate(sink_parts, axis=1)
    mx = jnp.maximum(jnp.maximum(jnp.max(s_c, axis=0, keepdims=True), jnp.max(s_p, axis=0, keepdims=True)), sink)
    p_c = jnp.exp2(s_c - mx).astype(BF16)
    p_p = jnp.exp2(s_p - mx).astype(BF16)
    accs = []
    for i, (hkv, half) in enumerate(items):
        vsl = slice(hkv * VT_ROWS, (hkv + 1) * VT_ROWS)
        ls = slice(i * width, (i + 1) * width)
        t = qi * halves + half
        accs.append(_dot(vt_ref[0, t, vsl, :], p_c[:, ls]) + _dot(vt_ref[0, jnp.maximum(t - 1, 0), vsl, :], p_p[:, ls]))
    acc = jnp.concatenate(accs, axis=1)
    o_t = acc[0:HEAD_DIM] / (acc[HEAD_DIM:HEAD_DIM + 1] + jnp.exp2(sink - mx))
    heads = []
    for hkv in range(n_kv):
        for g in range(grp):
            heads.append(jnp.concatenate(
                [o_t[:, (hkv * halves + half) * width + g * TILE:(hkv * halves + half) * width + (g + 1) * TILE]
                 for half in range(halves)], axis=1))
    o_ref[0] = jnp.concatenate(heads, axis=0).T.astype(BF16)


def _swa(qa, ka, vat, bias_a, sinks):
    b, s, width = qa.shape
    n_kv = ka.shape[-1] // HEAD_DIM
    grp = width // HEAD_DIM // n_kv
    return pl.pallas_call(
        functools.partial(_swa_body, n_kv=n_kv, grp=grp),
        grid=(b, s // BLK),
        in_specs=[
            pl.BlockSpec(memory_space=pltpu.SMEM),
            pl.BlockSpec((1, BLK, width), lambda bi, qi: (bi, qi, 0)),
            pl.BlockSpec((1, s, n_kv * HEAD_DIM), lambda bi, qi: (bi, 0, 0)),
            pl.BlockSpec((1, s // TILE, n_kv * VT_ROWS, TILE), lambda bi, qi: (bi, 0, 0, 0)),
            _const_spec(bias_a.shape),
        ],
        out_specs=pl.BlockSpec((1, BLK, width), lambda bi, qi: (bi, qi, 0)),
        out_shape=jax.ShapeDtypeStruct((b, s, width), BF16),
        compiler_params=_cparams(2),
        name="swa",
    )(sinks.astype(F32), qa, ka, vat, bias_a)


def _finish(acc, dv):
    return acc[0:dv] / acc[dv:dv + 1]


def _mla_body(q_ref, k_ref, vt_ref, o_ref, acc_ref, s_ref, *, n_heads):
    qi = pl.program_id(1)
    lanes = n_heads * BLK
    krow = lax.broadcasted_iota(jnp.int32, (BLK, lanes), 0)
    qcol = lax.broadcasted_iota(jnp.int32, (BLK, lanes), 1) & (BLK - 1)

    def scores(n):
        st = pl.multiple_of(n * BLK, BLK)
        return jnp.concatenate(
            [_dot_nt(k_ref[0, pl.ds(st, BLK), hd * LANES:(hd + 1) * LANES], q_ref[0, :, hd * LANES:(hd + 1) * LANES])
             for hd in range(n_heads)], axis=1)

    def weighted_values(n, p):
        return jnp.concatenate(
            [_dot(vt_ref[0, n, hd * VT_ROWS:(hd + 1) * VT_ROWS, :], p[:, hd * BLK:(hd + 1) * BLK])
             for hd in range(n_heads)], axis=1)

    def produce_first(slot):
        s = jnp.where(krow <= qcol, scores(qi), NEG_INF)
        s_ref[slot] = s
        return jnp.max(s, axis=0, keepdims=True)

    def produce(i, slot):
        s = scores(i - 1)
        s_ref[slot] = s
        return jnp.max(s, axis=0, keepdims=True)

    def consume(i, slot, block_max, m):
        n = jnp.where(i == 0, qi, i - 1)
        m_new = jnp.maximum(m, block_max)
        p = jnp.exp2(s_ref[slot] - m_new).astype(BF16)
        acc_ref[...] = jnp.exp2(m - m_new) * acc_ref[...] + weighted_values(n, p)
        return m_new

    acc_ref[...] = jnp.zeros_like(acc_ref)
    _pipelined_blocks(qi + 1, produce_first, produce, consume, jnp.full((1, lanes), NEG_INF, F32))
    o_t = _finish(acc_ref[...], MLA_V)
    o_ref[0] = jnp.concatenate([o_t[:, hd * BLK:(hd + 1) * BLK] for hd in range(n_heads)],
                               axis=0).T.astype(BF16)


def _mla(qb, kb, vbt):
    b, s, width = qb.shape
    n_heads = width // LANES
    n_blocks = s // BLK
    return pl.pallas_call(
        functools.partial(_mla_body, n_heads=n_heads),
        grid=(b, n_blocks),
        in_specs=[
            pl.BlockSpec((1, BLK, width), lambda bi, qi: (bi, qi, 0)),
            pl.BlockSpec((1, s, width), lambda bi, qi: (bi, 0, 0)),
            pl.BlockSpec((1, n_blocks, n_heads * VT_ROWS, BLK), lambda bi, qi: (bi, 0, 0, 0)),
        ],
        out_specs=pl.BlockSpec((1, BLK, n_heads * MLA_V), lambda bi, qi: (bi, qi, 0)),
        out_shape=jax.ShapeDtypeStruct((b, s, n_heads * MLA_V), BF16),
        scratch_shapes=[pltpu.VMEM((VT_ROWS, n_heads * BLK), F32), pltpu.VMEM((2, BLK, n_heads * BLK), F32)],
        compiler_params=_cparams(2),
        name="mla",
    )(qb, kb, vbt)


def _store_vt_ones(vt_ref, j, v, n_heads):
    vt = v.T.astype(BF16)
    ones = jnp.ones((ONES_ROWS, v.shape[0]), BF16)
    for hd in range(n_heads):
        vt_ref[0, j, hd * VT_ROWS:hd * VT_ROWS + HEAD_DIM, :] = vt[hd * HEAD_DIM:(hd + 1) * HEAD_DIM]
        vt_ref[0, j, hd * VT_ROWS + HEAD_DIM:(hd + 1) * VT_ROWS, :] = ones


def _proj1_body(x_ref, pre_ref, w_ref, qc_ref, kc_ref, vc_ref, ks_ref, vst_ref, kw_ref, vwt_ref, g_ref,
                qd_ref, kd_ref, vdt_ref, km_ref, *, tm):
    h = _rms(x_ref[...], pre_ref[...]).astype(BF16)
    y = _dot(h, w_ref[...])
    qc, kc, vc, ks, vs, kw, vw, gl, qd, kd, vd = (y[:, sl] for sl in _col_slices(P1_COLS))
    scale = LOG2E * HEAD_DIM ** -0.5
    qc_ref[...] = (qc * scale).astype(BF16)
    for hkv in range(C_KV_HEADS):
        kc_ref[0, hkv] = kc[:, hkv * HEAD_DIM:(hkv + 1) * HEAD_DIM].astype(BF16)
        vc_ref[0, hkv] = vc[:, hkv * HEAD_DIM:(hkv + 1) * HEAD_DIM].astype(BF16)
    ks_ref[...] = ks.astype(BF16)
    kw_ref[...] = kw.astype(BF16)
    g_ref[...] = 1.0 / (1.0 + jnp.exp(-gl))
    qd_ref[...] = (qd * scale).astype(BF16)
    kd_ref[...] = kd.astype(BF16)
    for j in range(tm // BLK):
        rows = slice(j * BLK, (j + 1) * BLK)
        km_ref[j] = jnp.mean(kd[rows], axis=0, keepdims=True)
        _store_vt_ones(vdt_ref, j, vd[rows], D_HEADS)
        _store_vt_ones(vst_ref, j, vs[rows], C_KV_HEADS)
        _store_vt_ones(vwt_ref, j, vw[rows], C_KV_HEADS)


def _proj1(x2, batch, seq, pre, w_in):
    m, d = x2.shape
    tm = TOKEN_TILE
    n_st = seq // tm
    n_pre = sum(P1_COLS[:7]) + N_GATES
    pad = jnp.zeros((d, LANES - N_GATES), F32)
    w1 = jnp.concatenate([w_in[:, :n_pre], pad, w_in[:, n_pre:]], axis=1).astype(BF16)
    q_w, kv_w, d_w = C_HEADS * HEAD_DIM, C_KV_HEADS * HEAD_DIM, D_HEADS * HEAD_DIM
    tok = lambda n: pl.BlockSpec((tm, n), lambda i: (i, 0))
    tok_shape = lambda n, dt=BF16: jax.ShapeDtypeStruct((m, n), dt)
    hm = pl.BlockSpec((1, C_KV_HEADS, tm, HEAD_DIM), lambda i: (i // n_st, 0, i % n_st, 0))
    hm_shape = jax.ShapeDtypeStruct((batch, C_KV_HEADS, seq, HEAD_DIM), BF16)
    nblk = tm // BLK
    vt_spec = lambda heads: pl.BlockSpec((1, nblk, heads * VT_ROWS, BLK), lambda i: (i // n_st, i % n_st, 0, 0))
    vt_shape = lambda heads: jax.ShapeDtypeStruct((batch, seq // BLK, heads * VT_ROWS, BLK), BF16)
    return pl.pallas_call(
        functools.partial(_proj1_body, tm=tm),
        grid=(m // tm,),
        in_specs=[tok(d), _const_spec((1, d)), _const_spec(w1.shape)],
        out_specs=[tok(q_w), hm, hm, tok(kv_w), vt_spec(C_KV_HEADS), tok(kv_w), vt_spec(C_KV_HEADS), tok(LANES),
                   tok(d_w), tok(d_w), vt_spec(D_HEADS),
                   pl.BlockSpec((nblk, 1, d_w), lambda i: (i, 0, 0))],
        out_shape=[
            tok_shape(q_w), hm_shape, hm_shape,
            tok_shape(kv_w), vt_shape(C_KV_HEADS), tok_shape(kv_w), vt_shape(C_KV_HEADS), tok_shape(LANES, F32),
            tok_shape(d_w), tok_shape(d_w), vt_shape(D_HEADS),
            jax.ShapeDtypeStruct((m // BLK, 1, d_w), F32),
        ],
        compiler_params=_cparams(1),
        name="proj1",
    )(x2, pre.reshape(1, d), w1)


def _compress_body(kx_ref, vx_ref, pek_ref, pev_ref, wk1_ref, wk2_ref, wv1_ref, wv2_ref, ko_ref, vo_ref):
    half = CMP_STRIDE * HEAD_DIM
    for x_ref, pe_ref, w1_ref, w2_ref, o_ref in ((kx_ref, pek_ref, wk1_ref, wk2_ref, ko_ref),
                                                 (vx_ref, pev_ref, wv1_ref, wv2_ref, vo_ref)):
        pe8 = jnp.broadcast_to(pe_ref[...], (SUBLANES, 2 * half)).astype(BF16)
        pe_term = _dot(pe8, w1_ref[...])[0:1]
        outs = []
        for hkv in range(C_KV_HEADS):
            x = x_ref[0, hkv]
            lo = _dot(x, w1_ref[0:half])
            hi = _dot(x, w1_ref[half:2 * half])
            hid = lo + pltpu.roll(hi, x.shape[0] - 1, 0) + pe_term
            outs.append(_dot(_silu(hid).astype(BF16), w2_ref[...]))
        out = jnp.concatenate(outs, axis=1)
        o_ref[0] = (out.T if o_ref is vo_ref else out).astype(BF16)


def _compress(kc_hm, vc_hm, pe_k, pe_v, wk1, wk2, wv1, wv2):
    b, _, s, dh = kc_hm.shape
    rows = s // CMP_STRIDE
    feat = CMP_STRIDE * dh
    kx = kc_hm.reshape(b, 2, rows, feat)
    vx = vc_hm.reshape(b, 2, rows, feat)
    xspec = pl.BlockSpec((1, 2, rows, feat), lambda bi: (bi, 0, 0, 0))
    ospec = pl.BlockSpec((1, rows, 2 * dh), lambda bi: (bi, 0, 0))
    w1s, w2s = wk1.shape, wk2.shape
    return pl.pallas_call(
        _compress_body,
        grid=(b,),
        in_specs=[xspec, xspec, _const_spec((1, 2 * feat)), _const_spec((1, 2 * feat)),
                  _const_spec(w1s), _const_spec(w2s), _const_spec(w1s), _const_spec(w2s)],
        out_specs=[ospec, ospec],
        out_shape=[jax.ShapeDtypeStruct((b, rows, 2 * dh), BF16)] * 2,
        compiler_params=_cparams(1),
        name="nsa_compress",
    )(kx, vx, pe_k.astype(F32).reshape(1, -1), pe_v.astype(F32).reshape(1, -1),
      wk1.astype(BF16), wk2.astype(BF16), wv1.astype(BF16), wv2.astype(BF16))


def _bias_block(bias_ref, hd, dist_blocks):
    sub = BLK // TILE
    rows = []
    for a in range(sub):
        cols = []
        for bq in range(sub):
            e = sub * dist_blocks + bq - a
            e = max(e, 0) if isinstance(e, int) else jnp.clip(e, 0, BIAS_TILES - 1)
            cols.append(bias_ref[hd, e])
        rows.append(jnp.concatenate(cols, axis=1))
    return jnp.concatenate(rows, axis=0)


def _rank_rows(val, ridx, n_real):
    rank = jnp.zeros(val.shape, F32)
    for j in range(n_real):
        vj = val[j:j + 1, :]
        rank = rank + jnp.where(vj > val, 1.0, jnp.where(vj == val, jnp.where(ridx > j, 1.0, 0.0), 0.0))
    return rank


def _nsa_body(q_ref, kc_ref, vct_ref, ks_ref, vst_ref, kw_ref, vwt_ref, g_ref, bias_ref, ovt_ref, o_ref,
              pen_ref, acc_ref, s_ref, *, n_sel):
    qi = pl.program_id(1)
    grp = C_HEADS // C_KV_HEADS
    lanes = grp * BLK
    sub = BLK // SEL_BLOCK
    own = pl.multiple_of(qi * BLK, BLK)
    prev = pl.multiple_of(jnp.maximum(qi - 1, 0) * BLK, BLK)
    prev_blk = jnp.maximum(qi - 1, 0)
    krow = lax.broadcasted_iota(jnp.int32, (BLK, lanes), 0)
    qcol = lax.broadcasted_iota(jnp.int32, (BLK, lanes), 1) & (BLK - 1)
    causal = krow <= qcol
    crow = lax.broadcasted_iota(jnp.int32, (TILE, lanes), 0)
    cq = lax.broadcasted_iota(jnp.int32, (TILE, lanes), 1) & (BLK - 1)
    visible = CMP_STRIDE * crow + (CMP_BLOCK - 1) <= qi * BLK + cq
    jrow = lax.broadcasted_iota(jnp.int32, (n_sel, BLK), 0)
    cur = qi * sub + lax.broadcasted_iota(jnp.int32, (n_sel, BLK), 1) // SEL_BLOCK
    forced = (jrow == 0) | (jrow == cur) | (jrow == cur - 1)
    gates_t = g_ref[0].T
    has_prev = jnp.where(qi >= 1, 0, BLK)

    def stack_q(hkv):
        return jnp.concatenate(
            [q_ref[0, :, (hkv * grp + g) * HEAD_DIM:(hkv * grp + g + 1) * HEAD_DIM] for g in range(grp)], axis=0)

    def group_bias(hkv, dist_blocks):
        return jnp.concatenate([_bias_block(bias_ref, hkv * grp + g, dist_blocks) for g in range(grp)], axis=1)

    def sel_scores(hkv, n, dist_blocks):
        st = pl.multiple_of(n * BLK, BLK)
        sl = slice(hkv * HEAD_DIM, (hkv + 1) * HEAD_DIM)
        return _dot_nt(ks_ref[0, pl.ds(st, BLK), sl], stack_q(hkv)) + group_bias(hkv, dist_blocks)

    def sel_pens(hkv, n):
        pens = pen_ref[hkv, n, 0:sub, :]
        return [jnp.concatenate([pens[j:j + 1]] * grp, axis=1) for j in range(sub)]

    def sel_max(s, pens):
        mj = [jnp.max(s[j * SEL_BLOCK:(j + 1) * SEL_BLOCK], axis=0, keepdims=True) + pens[j] for j in range(sub)]
        return functools.reduce(jnp.maximum, mj)

    def sel_probs(s, pens, m_new):
        return jnp.concatenate([jnp.exp2(s[j * SEL_BLOCK:(j + 1) * SEL_BLOCK] - (m_new - pens[j]))
                                for j in range(sub)], axis=0).astype(BF16)

    o_cmp, o_win = [], []
    for hkv in range(C_KV_HEADS):
        sl = slice(hkv * HEAD_DIM, (hkv + 1) * HEAD_DIM)
        vsl = slice(hkv * VT_ROWS, (hkv + 1) * VT_ROWS)
        qs = stack_q(hkv)

        sc = jnp.where(visible, _dot_nt(kc_ref[0, :, sl], qs), NEG_INF)
        mc = jnp.max(sc, axis=0, keepdims=True)
        ec = jnp.where(visible, jnp.exp2(sc - mc), 0.0)
        lc = jnp.sum(ec, axis=0, keepdims=True)
        p = ec / jnp.where(lc > 0.0, lc, 1.0)
        o_cmp.append(_dot(vct_ref[0, sl, :], p.astype(BF16)))

        ps = p[:, 0:BLK] + p[:, BLK:2 * BLK] + p[:, 2 * BLK:3 * BLK] + p[:, 3 * BLK:4 * BLK]
        ps_hi = ps.astype(BF16)
        ps_lo = (ps - ps_hi.astype(F32)).astype(BF16)
        imp = _dot(ovt_ref[...], ps_hi) + _dot(ovt_ref[...], ps_lo)
        val = jnp.where(forced, FORCE_SCORE, jnp.where(jrow <= cur, imp, NEG_INF))
        pen = jnp.where(_rank_rows(val, jrow, n_sel) < SEL_TOPK, 0.0, NEG_INF)
        for t in range(n_sel // sub):
            pen_ref[hkv, t, 0:sub, :] = pen[t * sub:(t + 1) * sub]

        s0 = _dot_nt(kw_ref[0, pl.ds(own, BLK), sl], qs) + group_bias(hkv, 0)
        s1 = _dot_nt(kw_ref[0, pl.ds(prev, BLK), sl], qs) + group_bias(hkv, 1)
        s0 = jnp.where(causal, s0, NEG_INF)
        s1 = jnp.where(krow > qcol + has_prev, s1, NEG_INF)
        mw = jnp.maximum(jnp.max(s0, axis=0, keepdims=True), jnp.max(s1, axis=0, keepdims=True))
        ow = (_dot(vwt_ref[0, qi, vsl, :], jnp.exp2(s0 - mw).astype(BF16))
              + _dot(vwt_ref[0, prev_blk, vsl, :], jnp.exp2(s1 - mw).astype(BF16)))
        o_win.append(_finish(ow, HEAD_DIM))

    def produce_first(slot):
        stats = []
        for hkv in range(C_KV_HEADS):
            s = jnp.where(causal, sel_scores(hkv, qi, 0), NEG_INF)
            s_ref[slot, hkv] = s
            stats.append(sel_max(s, sel_pens(hkv, qi)))
        return tuple(stats)

    def produce(i, slot):
        stats = []
        for hkv in range(C_KV_HEADS):
            s = sel_scores(hkv, i - 1, qi - (i - 1))
            s_ref[slot, hkv] = s
            stats.append(sel_max(s, sel_pens(hkv, i - 1)))
        return tuple(stats)

    def consume(i, slot, block_max, ms):
        n = jnp.where(i == 0, qi, i - 1)
        new = []
        for hkv in range(C_KV_HEADS):
            vsl = slice(hkv * VT_ROWS, (hkv + 1) * VT_ROWS)
            m_new = jnp.maximum(ms[hkv], block_max[hkv])
            pb = sel_probs(s_ref[slot, hkv], sel_pens(hkv, n), m_new)
            acc_ref[hkv] = jnp.exp2(ms[hkv] - m_new) * acc_ref[hkv] + _dot(vst_ref[0, n, vsl, :], pb)
            new.append(m_new)
        return tuple(new)

    acc_ref[...] = jnp.zeros_like(acc_ref)
    m_init = jnp.full((1, lanes), NEG_INF, F32)
    _pipelined_blocks(qi + 1, produce_first, produce, consume, (m_init, m_init))
    outs = []
    for hkv in range(C_KV_HEADS):
        o_slc = _finish(acc_ref[hkv], HEAD_DIM)
        for g in range(grp):
            hd = hkv * grp + g
            ls = slice(g * BLK, (g + 1) * BLK)
            g_cmp, g_slc, g_win = (gates_t[t * C_HEADS + hd:t * C_HEADS + hd + 1] for t in range(3))
            outs.append(g_cmp * o_cmp[hkv][:, ls] + g_slc * o_slc[:, ls] + g_win * o_win[hkv][:, ls])
    o_ref[0] = jnp.concatenate(outs, axis=0).T.astype(BF16)


def _nsa(qc, kcmp, vcmpt, ks, vst, kw, vwt, gates, bias_c):
    b, s, width = qc.shape
    n_blocks = s // BLK
    n_sel = s // SEL_BLOCK
    n_c = (s - CMP_BLOCK) // CMP_STRIDE + 1
    j_start = np.arange(n_sel)[:, None] * SEL_BLOCK
    c_start = np.arange(TILE)[None, :] * CMP_STRIDE
    overlap_t = ((c_start < j_start + SEL_BLOCK) & (c_start + CMP_BLOCK > j_start)
                 & (np.arange(TILE)[None, :] < n_c))
    ovt = jnp.asarray(overlap_t, BF16)
    qtile = lambda n: pl.BlockSpec((1, BLK, n), lambda bi, qi: (bi, qi, 0))
    kv_w = C_KV_HEADS * HEAD_DIM
    group_lanes = C_HEADS // C_KV_HEADS * BLK
    full = pl.BlockSpec((1, s, kv_w), lambda bi, qi: (bi, 0, 0))
    full_t = pl.BlockSpec((1, n_blocks, C_KV_HEADS * VT_ROWS, BLK), lambda bi, qi: (bi, 0, 0, 0))
    cmp_spec = pl.BlockSpec((1, TILE, kv_w), lambda bi, qi: (bi, 0, 0))
    return pl.pallas_call(
        functools.partial(_nsa_body, n_sel=n_sel),
        grid=(b, n_blocks),
        in_specs=[qtile(width), cmp_spec, cmp_spec, full, full_t, full, full_t, qtile(LANES),
                  _const_spec(bias_c.shape), _const_spec(ovt.shape)],
        out_specs=qtile(width),
        out_shape=jax.ShapeDtypeStruct((b, s, width), BF16),
        scratch_shapes=[pltpu.VMEM((C_KV_HEADS, n_blocks, SUBLANES, BLK), F32),
                        pltpu.VMEM((C_KV_HEADS, VT_ROWS, group_lanes), F32),
                        pltpu.VMEM((2, C_KV_HEADS, BLK, group_lanes), F32)],
        compiler_params=_cparams(2),
        name="nsa",
    )(qc, kcmp, vcmpt, ks, vst, kw, vwt, gates, bias_c, ovt)


def _moba_body(q_ref, k_ref, vt_ref, km_ref, bias_ref, o_ref, pen_ref, acc_ref, s_ref, *, n_heads, n_blocks):
    qi = pl.program_id(1)
    lanes = n_heads * BLK
    nidx = lax.broadcasted_iota(jnp.int32, (GATE_ROWS, lanes), 0)
    krow = lax.broadcasted_iota(jnp.int32, (BLK, lanes), 0)
    qcol = lax.broadcasted_iota(jnp.int32, (BLK, lanes), 1) & (BLK - 1)

    def scores(n, dist_blocks):
        st = pl.multiple_of(n * BLK, BLK)
        parts = []
        for hd in range(n_heads):
            sl = slice(hd * HEAD_DIM, (hd + 1) * HEAD_DIM)
            parts.append(_dot_nt(k_ref[0, pl.ds(st, BLK), sl], q_ref[0, :, sl])
                         + _bias_block(bias_ref, hd, dist_blocks))
        return jnp.concatenate(parts, axis=1)

    def weighted_values(n, p):
        return jnp.concatenate(
            [_dot(vt_ref[0, n, hd * VT_ROWS:(hd + 1) * VT_ROWS, :], p[:, hd * BLK:(hd + 1) * BLK])
             for hd in range(n_heads)], axis=1)

    zpad = jnp.zeros((GATE_ROWS - n_blocks, HEAD_DIM), F32)
    gate = jnp.concatenate(
        [_dot_nt(jnp.concatenate([km_ref[0, :, hd * HEAD_DIM:(hd + 1) * HEAD_DIM], zpad], axis=0).astype(BF16),
                 q_ref[0, :, hd * HEAD_DIM:(hd + 1) * HEAD_DIM]) for hd in range(n_heads)], axis=1)
    val = jnp.where(nidx < qi, gate, NEG_INF)
    val = jnp.where(nidx < n_blocks, val, PAD_SCORE)
    keep = ((_rank_rows(val, nidx, n_blocks) < MOBA_TOPK) & (nidx < qi)) | (nidx == qi)
    pen_ref[...] = jnp.where(keep, 0.0, NEG_INF)

    def produce_first(slot):
        s = jnp.where(krow <= qcol, scores(qi, 0), NEG_INF)
        s_ref[slot] = s
        return jnp.max(s, axis=0, keepdims=True)

    def produce(i, slot):
        s = scores(i - 1, qi - (i - 1))
        s_ref[slot] = s
        return jnp.max(s, axis=0, keepdims=True) + pen_ref[pl.ds(i - 1, 1), :]

    def consume(i, slot, block_max, m):
        n = jnp.where(i == 0, qi, i - 1)
        pen = pen_ref[pl.ds(n, 1), :]
        m_new = jnp.maximum(m, block_max)
        p = jnp.exp2(s_ref[slot] - (m_new - pen)).astype(BF16)
        acc_ref[...] = jnp.exp2(m - m_new) * acc_ref[...] + weighted_values(n, p)
        return m_new

    acc_ref[...] = jnp.zeros_like(acc_ref)
    _pipelined_blocks(qi + 1, produce_first, produce, consume, jnp.full((1, lanes), NEG_INF, F32))
    o_t = _finish(acc_ref[...], HEAD_DIM)
    o_ref[0] = jnp.concatenate([o_t[:, hd * BLK:(hd + 1) * BLK] for hd in range(n_heads)],
                               axis=0).T.astype(BF16)


def _moba(qd, kd, vdt, kmean, bias_d):
    b, s, width = qd.shape
    n_heads = width // HEAD_DIM
    n_blocks = s // BLK
    return pl.pallas_call(
        functools.partial(_moba_body, n_heads=n_heads, n_blocks=n_blocks),
        grid=(b, n_blocks),
        in_specs=[
            pl.BlockSpec((1, BLK, width), lambda bi, qi: (bi, qi, 0)),
            pl.BlockSpec((1, s, width), lambda bi, qi: (bi, 0, 0)),
            pl.BlockSpec((1, n_blocks, 2 * width, BLK), lambda bi, qi: (bi, 0, 0, 0)),
            pl.BlockSpec((1, n_blocks, width), lambda bi, qi: (bi, 0, 0)),
            _const_spec(bias_d.shape),
        ],
        out_specs=pl.BlockSpec((1, BLK, width), lambda bi, qi: (bi, qi, 0)),
        out_shape=jax.ShapeDtypeStruct((b, s, width), BF16),
        scratch_shapes=[pltpu.VMEM((GATE_ROWS, n_heads * BLK), F32), pltpu.VMEM((VT_ROWS, n_heads * BLK), F32),
                        pltpu.VMEM((2, BLK, n_heads * BLK), F32)],
        compiler_params=_cparams(2),
        name="moba",
    )(qd, kd, vdt, kmean.reshape(b, n_blocks, width), bias_d)


def kernel(x, rel_bias_table, l0_ffn1_pre, l0_ffn1_post, l0_ffn1_wg, l0_ffn1_wu, l0_ffn1_wd, l0_mix_pre, l0_mix_post, l0_w_in, l0_sinks, l0_mla_q_norm, l0_mla_w_uq, l0_mla_kv_norm, l0_mla_w_ukv, l0_w_out, l0_ffn2_pre, l0_ffn2_post, l0_ffn2_wg, l0_ffn2_wu, l0_ffn2_wd, l1_ffn1_pre, l1_ffn1_post, l1_ffn1_wg, l1_ffn1_wu, l1_ffn1_wd, l1_mix_pre, l1_mix_post, l1_w_in, l1_nsa_pe_k, l1_nsa_pe_v, l1_nsa_wk1, l1_nsa_wk2, l1_nsa_wv1, l1_nsa_wv2, l1_w_out, l1_ffn2_pre, l1_ffn2_post, l1_ffn2_wg, l1_ffn2_wu, l1_ffn2_wd):
    b, s, d = x.shape
    m = b * s
    x2 = x.reshape(m, d)
    tiles = _bias_tiles(rel_bias_table)
    bias_a = tiles[A_SLOT:A_SLOT + 8, 0:2]
    bias_c = tiles[C_SLOT:C_SLOT + 8]
    bias_d = tiles[D_SLOT:D_SLOT + 8]

    x2 = _ffn(x2, l0_ffn1_pre, l0_ffn1_post, l0_ffn1_wg, l0_ffn1_wu, l0_ffn1_wd, 0.5)
    qa, ka, vat, qb, kb, vbt = _proj0(x2, s, l0_mix_pre, l0_w_in, l0_mla_q_norm, l0_mla_w_uq,
                                      l0_mla_kv_norm, l0_mla_w_ukv)
    shp = lambda t: t.reshape(b, s, t.shape[-1])
    o_a = _swa(shp(qa), shp(ka), vat, bias_a, l0_sinks)
    o_b = _mla(shp(qb), shp(kb), vbt)
    x2 = _outproj_ffn(x2, o_a.reshape(m, -1), o_b.reshape(m, -1), l0_w_out, l0_mix_post,
                      l0_ffn2_pre, l0_ffn2_post, l0_ffn2_wg, l0_ffn2_wu, l0_ffn2_wd, 0.5)

    x2 = _ffn(x2, l1_ffn1_pre, l1_ffn1_post, l1_ffn1_wg, l1_ffn1_wu, l1_ffn1_wd, 0.5)
    qc, kc_hm, vc_hm, ks, vst, kw, vwt, gates, qd, kd, vdt, kmean = _proj1(x2, b, s, l1_mix_pre, l1_w_in)
    kcmp, vcmpt = _compress(kc_hm, vc_hm, l1_nsa_pe_k, l1_nsa_pe_v, l1_nsa_wk1, l1_nsa_wk2, l1_nsa_wv1, l1_nsa_wv2)
    o_c = _nsa(shp(qc), kcmp, vcmpt, shp(ks), vst, shp(kw), vwt, shp(gates), bias_c)
    o_d = _moba(shp(qd), shp(kd), vdt, kmean, bias_d)
    x2 = _outproj_ffn(x2, o_c.reshape(m, -1), o_d.reshape(m, -1), l1_w_out, l1_mix_post,
                      l1_ffn2_pre, l1_ffn2_post, l1_ffn2_wg, l1_ffn2_wu, l1_ffn2_wd, 0.5)
    return x2.reshape(b, s, d)
```

```python
import functools
import math

import numpy as np
import jax
import jax.numpy as jnp
from jax import lax
from jax.experimental import pallas as pl
from jax.experimental.pallas import tpu as pltpu

F32 = jnp.float32
BF16 = jnp.bfloat16

HEAD_DIM = 64
NORM_EPS = 1e-6
LOG2E = math.log2(math.e)
NEG_INF = -1e30
PAD_SCORE = -3e38
FORCE_SCORE = 1e9
REL_BUCKETS = 32
REL_MAX_DIST = 1024
REL_SLOTS = 24
A_SLOT, C_SLOT, D_SLOT = 0, 8, 16
A_HEADS, A_KV_HEADS = 8, 2
MLA_NOPE, MLA_ROPE, MLA_V = 64, 32, 64
MLA_HEADS, MLA_Q_RANK, MLA_KV_RANK = 8, 256, 128
ROPE_THETA = 10000.0
C_HEADS, C_KV_HEADS = 8, 2
CMP_BLOCK, CMP_STRIDE = 32, 16
SEL_BLOCK, SEL_TOPK = 64, 8
D_HEADS = 8
MOBA_TOPK = 3

LANES = 128
SUBLANES = 8
TILE = 128
P0_COLS = (A_HEADS * HEAD_DIM, A_KV_HEADS * HEAD_DIM, A_KV_HEADS * HEAD_DIM, MLA_Q_RANK, MLA_KV_RANK, LANES)
P1_COLS = ((C_HEADS * HEAD_DIM,) + (C_KV_HEADS * HEAD_DIM,) * 6 + (LANES,) + (D_HEADS * HEAD_DIM,) * 3)
N_GATES = 3 * C_HEADS
BLK = 256
ONES_ROWS = 16
GATE_ROWS = 16
VT_ROWS = HEAD_DIM + ONES_ROWS
BIAS_TILES = 9
TOKEN_TILE = 512
FF_CHUNK = 256
VMEM_LIMIT = 56 * 1024 * 1024


def _cparams(n_axes):
    return pltpu.CompilerParams(dimension_semantics=("arbitrary",) * n_axes, vmem_limit_bytes=VMEM_LIMIT)


def _dot(a, b):
    return jnp.dot(a, b, preferred_element_type=F32)


def _dot_nt(a, b):
    return lax.dot_general(a, b, (((1,), (1,)), ((), ())), preferred_element_type=F32)


def _rms(x, g):
    return x * lax.rsqrt(jnp.mean(x * x, axis=-1, keepdims=True) + NORM_EPS) * g


def _silu(x):
    return x / (1.0 + jnp.exp(-x))


def _col_slices(sizes):
    edges = np.cumsum((0,) + tuple(sizes)).tolist()
    return [slice(a, b) for a, b in zip(edges[:-1], edges[1:])]


def _const_spec(shape):
    nd = len(shape)
    return pl.BlockSpec(shape, lambda *_: (0,) * nd)


def _pipelined_blocks(length, produce_first, produce, consume, state):
    trips = (length - 1) // 2

    def trip(t, carry):
        state, stat_a = carry
        a = 2 * t
        stat_b = produce(a + 1, 1)
        state = consume(a, 0, stat_a, state)
        stat_next = produce(a + 2, 0)
        state = consume(a + 1, 1, stat_b, state)
        return state, stat_next

    carry = lax.fori_loop(0, trips, trip, (state, produce_first(0)))
    a = 2 * trips

    def tail_one(c):
        return consume(a, 0, c[1], c[0])

    def tail_two(c):
        stat_b = produce(a + 1, 1)
        return consume(a + 1, 1, stat_b, consume(a, 0, c[1], c[0]))

    return lax.cond(length - a == 1, tail_one, tail_two, carry)


def _per_query_tile(tile_fn, n_tiles):
    def body(*refs):
        def step(qi, carry):
            tile_fn(qi, *refs)
            return carry

        lax.fori_loop(0, n_tiles, step, 0)

    return body


def _tile_rows(qi):
    return pl.ds(pl.multiple_of(qi * BLK, BLK), BLK)


def _resident_spec(shape):
    nd = len(shape)
    return pl.BlockSpec(shape, lambda *_: (0,) * nd, pipeline_mode=pl.Buffered(1))


def _ffn_tile(x, pre_ref, post_ref, wg_ref, wu_ref, wd_ref, acc_ref, weight):
    h = _rms(x, pre_ref[...]).astype(BF16)
    for c in range(wg_ref.shape[1] // FF_CHUNK):
        cols = slice(c * FF_CHUNK, (c + 1) * FF_CHUNK)
        g = _dot(h, wg_ref[:, cols].astype(BF16))
        u = _dot(h, wu_ref[:, cols].astype(BF16))
        a = (_silu(g) * u).astype(BF16)
        y = _dot(a, wd_ref[cols, :].astype(BF16))
        if c == 0:
            acc_ref[...] = y
        else:
            acc_ref[...] += y
    return x + weight * _rms(acc_ref[...], post_ref[...])


def _ffn_body(x_ref, pre_ref, post_ref, wg_ref, wu_ref, wd_ref, o_ref, acc_ref, *, weight):
    o_ref[...] = _ffn_tile(x_ref[...], pre_ref, post_ref, wg_ref, wu_ref, wd_ref, acc_ref, weight)


def _ffn(x2, pre, post, wg, wu, wd, weight):
    m, d = x2.shape
    d_ff = wg.shape[1]
    tm = TOKEN_TILE
    return pl.pallas_call(
        functools.partial(_ffn_body, weight=weight),
        grid=(m // tm,),
        in_specs=[
            pl.BlockSpec((tm, d), lambda i: (i, 0)),
            _const_spec((1, d)), _const_spec((1, d)),
            _resident_spec((d, d_ff)), _resident_spec((d, d_ff)), _resident_spec((d_ff, d)),
        ],
        out_specs=pl.BlockSpec((tm, d), lambda i: (i, 0)),
        out_shape=jax.ShapeDtypeStruct((m, d), F32),
        scratch_shapes=[pltpu.VMEM((tm, d), F32)],
        compiler_params=_cparams(1),
        name="ffn",
    )(x2, pre.reshape(1, d), post.reshape(1, d), wg, wu, wd)


def _outproj_ffn_body(x_ref, o1_ref, o2_ref, wo_ref, mpost_ref, pre_ref, post_ref, wg_ref, wu_ref, wd_ref,
                      o_ref, acc_ref, *, weight):
    n1 = o1_ref.shape[1]
    y = (_dot(o1_ref[...], wo_ref[0:n1, :].astype(BF16))
         + _dot(o2_ref[...], wo_ref[n1:, :].astype(BF16)))
    x = x_ref[...] + _rms(y, mpost_ref[...])
    o_ref[...] = _ffn_tile(x, pre_ref, post_ref, wg_ref, wu_ref, wd_ref, acc_ref, weight)


def _outproj_ffn(x2, o1, o2, w_out, mix_post, pre, post, wg, wu, wd, weight):
    m, d = x2.shape
    d_ff = wg.shape[1]
    tm = TOKEN_TILE
    tok = lambda n: pl.BlockSpec((tm, n), lambda i: (i, 0))
    return pl.pallas_call(
        functools.partial(_outproj_ffn_body, weight=weight),
        grid=(m // tm,),
        in_specs=[
            tok(d), tok(o1.shape[1]), tok(o2.shape[1]), _resident_spec(w_out.shape), _const_spec((1, d)),
            _const_spec((1, d)), _const_spec((1, d)),
            _resident_spec((d, d_ff)), _resident_spec((d, d_ff)), _resident_spec((d_ff, d)),
        ],
        out_specs=tok(d),
        out_shape=jax.ShapeDtypeStruct((m, d), F32),
        scratch_shapes=[pltpu.VMEM((tm, d), F32)],
        compiler_params=_cparams(1),
        name="outproj_ffn",
    )(x2, o1, o2, w_out, mix_post.reshape(1, d), pre.reshape(1, d), post.reshape(1, d), wg, wu, wd)


def _t5_bucket(dist):
    n = jnp.maximum(dist, 0)
    exact = REL_BUCKETS // 2
    nf = jnp.maximum(n, 1).astype(jnp.float32)
    large = exact + (jnp.log(nf / exact) / math.log(REL_MAX_DIST / exact) * (REL_BUCKETS - exact)).astype(jnp.int32)
    return jnp.where(n < exact, n, jnp.minimum(large, REL_BUCKETS - 1))


def _tile_bucket_range(d):
    dist = np.arange(max(d * TILE - (TILE - 1), 0), d * TILE + TILE)
    exact = REL_BUCKETS // 2
    large = exact + np.log(np.maximum(dist, 1) / exact) / math.log(REL_MAX_DIST / exact) * (REL_BUCKETS - exact)
    bucket = np.where(dist < exact, dist, np.minimum(large.astype(np.int64), REL_BUCKETS - 1))
    return max(int(bucket.min()) - 1, 0), min(int(bucket.max()) + 1, REL_BUCKETS - 1)


def _bias_body(tab_ref, idx_ref, o_ref):
    slot = pl.program_id(0)
    for d in range(BIAS_TILES):
        idx = idx_ref[d]
        lo, hi = _tile_bucket_range(d)
        v = jnp.full((TILE, TILE), LOG2E * tab_ref[lo * REL_SLOTS + slot], F32)
        for b in range(lo + 1, hi + 1):
            v = jnp.where(idx == b, LOG2E * tab_ref[b * REL_SLOTS + slot], v)
        o_ref[0, d] = v


def _bias_tiles(rel_table):
    n_slots = rel_table.shape[1]
    d = jnp.arange(BIAS_TILES)[:, None, None]
    key = jnp.arange(TILE)[None, :, None]
    query = jnp.arange(TILE)[None, None, :]
    idx = _t5_bucket(d * TILE + query - key).astype(jnp.int32)
    return pl.pallas_call(
        _bias_body,
        grid=(n_slots,),
        in_specs=[
            pl.BlockSpec(memory_space=pltpu.SMEM),
            _const_spec((BIAS_TILES, TILE, TILE)),
        ],
        out_specs=pl.BlockSpec((1, BIAS_TILES, TILE, TILE), lambda s: (s, 0, 0, 0)),
        out_shape=jax.ShapeDtypeStruct((n_slots, BIAS_TILES, TILE, TILE), F32),
        compiler_params=_cparams(1),
        name="bias_tiles",
    )(rel_table.astype(F32).reshape(-1), idx)


def _proj0_body(x_ref, pre_ref, w_ref, qn_ref, wq1_ref, wq2_ref, kvn_ref, wk_ref, wv_ref, cos_ref, sin_ref,
                qa_ref, ka_ref, vat_ref, qb_ref, kb_ref, vbt_ref):
    h = _rms(x_ref[...], pre_ref[...]).astype(BF16)
    y = _dot(h, w_ref[...])
    qa, ka, va, cq, ckv, rot = (y[:, sl] for sl in _col_slices(P0_COLS))
    qa_ref[...] = (qa * (LOG2E * HEAD_DIM ** -0.5)).astype(BF16)
    ka_ref[...] = ka.astype(BF16)
    for j in range(y.shape[0] // TILE):
        _store_vt_ones(vat_ref, j, va[j * TILE:(j + 1) * TILE], A_KV_HEADS)
    cq = _rms(cq, qn_ref[...]).astype(BF16)
    ckv = _rms(ckv, kvn_ref[...]).astype(BF16)
    cos_t = cos_ref[...]
    sin_t = sin_ref[...]
    lane = lax.broadcasted_iota(jnp.int32, cos_t.shape, 1)
    scale = LOG2E * (MLA_NOPE + MLA_ROPE) ** -0.5
    qmul = scale * (cos_t + jnp.where(lane < MLA_NOPE, 1.0, 0.0))
    qsin = scale * sin_t
    kr = rot * cos_t + pltpu.roll(rot, LANES // 2, 1) * sin_t
    q1 = _dot(cq, wq1_ref[...])
    q2 = _dot(cq, wq2_ref[...])
    k1 = _dot(ckv, wk_ref[...])
    for hd in range(MLA_HEADS):
        sl = slice(hd * LANES, (hd + 1) * LANES)
        qb_ref[:, sl] = (q1[:, sl] * qmul + q2[:, sl] * qsin).astype(BF16)
        kb_ref[:, sl] = (k1[:, sl] + kr).astype(BF16)
    vb = _dot(ckv, wv_ref[...])
    for j in range(vb.shape[0] // BLK):
        _store_vt_ones(vbt_ref, j, vb[j * BLK:(j + 1) * BLK], MLA_HEADS)


def _rope_lane_tables(s):
    inv = ROPE_THETA ** (-jnp.arange(0, MLA_ROPE, 2, dtype=jnp.float32) / MLA_ROPE)
    ang = jnp.arange(s, dtype=jnp.float32)[:, None] * inv[None, :]
    cos, sin = jnp.cos(ang), jnp.sin(ang)
    z_lo = jnp.zeros((s, MLA_NOPE), F32)
    z_hi = jnp.zeros((s, LANES - MLA_NOPE - MLA_ROPE), F32)
    return (jnp.concatenate([z_lo, cos, cos, z_hi], axis=1), jnp.concatenate([z_lo, sin, sin, z_hi], axis=1))


def _rot_cols(w):
    half = w.shape[-1] // 2
    return jnp.concatenate([-w[..., half:], w[..., :half]], axis=-1)


def _proj0(x2, seq, pre, w_in, q_norm, w_uq, kv_norm, w_ukv):
    m, d = x2.shape
    tm = TOKEN_TILE
    n_lin = sum(P0_COLS[:-1])
    kr_w = w_in[:, n_lin:n_lin + MLA_ROPE]
    z32 = jnp.zeros((d, MLA_ROPE), F32)
    w0 = jnp.concatenate([w_in[:, :n_lin], _rot_cols(kr_w), z32, kr_w, z32], axis=1).astype(BF16)
    wq = w_uq.reshape(-1, MLA_HEADS, MLA_NOPE + MLA_ROPE)
    rq = wq.shape[0]
    pad = LANES - MLA_NOPE - MLA_ROPE
    zq = jnp.zeros((rq, MLA_HEADS, pad), F32)
    wq1 = jnp.concatenate([wq, zq], axis=2).reshape(rq, -1).astype(BF16)
    wq2 = jnp.concatenate([jnp.zeros((rq, MLA_HEADS, MLA_NOPE), F32), _rot_cols(wq[:, :, MLA_NOPE:]), zq],
                          axis=2).reshape(rq, -1).astype(BF16)
    wkv = w_ukv.reshape(-1, MLA_HEADS, MLA_NOPE + MLA_V)
    rk = wkv.shape[0]
    wk = jnp.concatenate([wkv[:, :, :MLA_NOPE], jnp.zeros((rk, MLA_HEADS, LANES - MLA_NOPE), F32)],
                         axis=2).reshape(rk, -1).astype(BF16)
    wv = wkv[:, :, MLA_NOPE:].reshape(rk, -1).astype(BF16)
    cos_t, sin_t = _rope_lane_tables(seq)
    n_st = seq // tm
    tok = lambda n: pl.BlockSpec((tm, n), lambda i: (i, 0))
    outs = pl.pallas_call(
        _proj0_body,
        grid=(m // tm,),
        in_specs=[
            tok(d), _const_spec((1, d)), _const_spec(w0.shape),
            _const_spec((1, rq)), _const_spec(wq1.shape), _const_spec(wq2.shape),
            _const_spec((1, rk)), _const_spec(wk.shape), _const_spec(wv.shape),
            pl.BlockSpec((tm, LANES), lambda i: (i % n_st, 0)),
            pl.BlockSpec((tm, LANES), lambda i: (i % n_st, 0)),
        ],
        out_specs=[tok(P0_COLS[0]), tok(P0_COLS[1]),
                   pl.BlockSpec((1, tm // TILE, A_KV_HEADS * VT_ROWS, TILE), lambda i: (i // n_st, i % n_st, 0, 0)),
                   tok(MLA_HEADS * LANES), tok(MLA_HEADS * LANES),
                   pl.BlockSpec((1, tm // BLK, MLA_HEADS * VT_ROWS, BLK), lambda i: (i // n_st, i % n_st, 0, 0))],
        out_shape=[jax.ShapeDtypeStruct((m, P0_COLS[0]), BF16), jax.ShapeDtypeStruct((m, P0_COLS[1]), BF16),
                   jax.ShapeDtypeStruct((m // seq, seq // TILE, A_KV_HEADS * VT_ROWS, TILE), BF16),
                   jax.ShapeDtypeStruct((m, MLA_HEADS * LANES), BF16),
                   jax.ShapeDtypeStruct((m, MLA_HEADS * LANES), BF16),
                   jax.ShapeDtypeStruct((m // seq, seq // BLK, MLA_HEADS * VT_ROWS, BLK), BF16)],
        compiler_params=_cparams(1),
        name="proj0",
    )(x2, pre.reshape(1, d), w0, q_norm.reshape(1, rq), wq1, wq2, kv_norm.reshape(1, rk), wk, wv, cos_t, sin_t)
    return outs


def _swa_tile(qi, sink_ref, q_ref, k_ref, vt_ref, bias_ref, o_ref, *, n_kv, grp):
    halves = BLK // TILE
    items = [(hkv, half) for hkv in range(n_kv) for half in range(halves)]
    width = grp * TILE
    krow = lax.broadcasted_iota(jnp.int32, (TILE, len(items) * width), 0)
    qcol = lax.broadcasted_iota(jnp.int32, (TILE, len(items) * width), 1) & (TILE - 1)

    cur_s, prev_s, sink_parts, has_prev = [], [], [], []
    for hkv, half in items:
        sl = slice(hkv * HEAD_DIM, (hkv + 1) * HEAD_DIM)
        t = qi * halves + half
        tp = jnp.maximum(t - 1, 0)
        q_rows = pl.ds(pl.multiple_of(t * TILE, TILE), TILE)
        qs = jnp.concatenate([q_ref[0, q_rows, (hkv * grp + g) * HEAD_DIM:(hkv * grp + g + 1) * HEAD_DIM]
                              for g in range(grp)], axis=0)
        bias_c = jnp.concatenate([bias_ref[hkv * grp + g, 0] for g in range(grp)], axis=1)
        bias_p = jnp.concatenate([bias_ref[hkv * grp + g, 1] for g in range(grp)], axis=1)
        cur_s.append(_dot_nt(k_ref[0, pl.ds(pl.multiple_of(t * TILE, TILE), TILE), sl], qs) + bias_c)
        prev_s.append(_dot_nt(k_ref[0, pl.ds(pl.multiple_of(tp * TILE, TILE), TILE), sl], qs) + bias_p)
        sink_parts.extend(jnp.full((1, TILE), LOG2E * sink_ref[hkv * grp + g], F32) for g in range(grp))
        has_prev.append(jnp.full((1, width), jnp.where(t >= 1, 0, TILE), jnp.int32))
    s_c = jnp.where(krow <= qcol, jnp.concatenate(cur_s, axis=1), NEG_INF)
    s_p = jnp.where(krow > qcol + jnp.concatenate(has_prev, axis=1),
                    jnp.concatenate(prev_s, axis=1), NEG_INF)
    sink = jnp.concatenate(sink_parts, axis=1)
    mx = jnp.maximum(jnp.maximum(jnp.max(s_c, axis=0, keepdims=True), jnp.max(s_p, axis=0, keepdims=True)), sink)
    p_c = jnp.exp2(s_c - mx).astype(BF16)
    p_p = jnp.exp2(s_p - mx).astype(BF16)
    accs = []
    for i, (hkv, half) in enumerate(items):
        vsl = slice(hkv * VT_ROWS, (hkv + 1) * VT_ROWS)
        ls = slice(i * width, (i + 1) * width)
        t = qi * halves + half
        accs.append(_dot(vt_ref[0, t, vsl, :], p_c[:, ls]) + _dot(vt_ref[0, jnp.maximum(t - 1, 0), vsl, :], p_p[:, ls]))
    acc = jnp.concatenate(accs, axis=1)
    o_t = acc[0:HEAD_DIM] / (acc[HEAD_DIM:HEAD_DIM + 1] + jnp.exp2(sink - mx))
    heads = []
    for hkv in range(n_kv):
        for g in range(grp):
            heads.append(jnp.concatenate(
                [o_t[:, (hkv * halves + half) * width + g * TILE:(hkv * halves + half) * width + (g + 1) * TILE]
                 for half in range(halves)], axis=1))
    o_ref[0, _tile_rows(qi), :] = jnp.concatenate(heads, axis=0).T.astype(BF16)


def _seq_spec(shape):
    nd = len(shape)
    return pl.BlockSpec((1,) + tuple(shape[1:]), lambda bi: (bi,) + (0,) * (nd - 1))


def _swa(qa, ka, vat, bias_a, sinks):
    b, s, width = qa.shape
    n_kv = ka.shape[-1] // HEAD_DIM
    grp = width // HEAD_DIM // n_kv
    return pl.pallas_call(
        _per_query_tile(functools.partial(_swa_tile, n_kv=n_kv, grp=grp), s // BLK),
        grid=(b,),
        in_specs=[pl.BlockSpec(memory_space=pltpu.SMEM), _seq_spec(qa.shape), _seq_spec(ka.shape),
                  _seq_spec(vat.shape), _const_spec(bias_a.shape)],
        out_specs=_seq_spec(qa.shape),
        out_shape=jax.ShapeDtypeStruct((b, s, width), BF16),
        compiler_params=_cparams(1),
        name="swa",
    )(sinks.astype(F32), qa, ka, vat, bias_a)


def _finish(acc, dv):
    return acc[0:dv] / acc[dv:dv + 1]


def _mla_tile(qi, q_ref, k_ref, vt_ref, o_ref, acc_ref, s_ref, *, n_heads):
    q_rows = _tile_rows(qi)
    lanes = n_heads * BLK
    krow = lax.broadcasted_iota(jnp.int32, (BLK, lanes), 0)
    qcol = lax.broadcasted_iota(jnp.int32, (BLK, lanes), 1) & (BLK - 1)

    def scores(n):
        st = pl.multiple_of(n * BLK, BLK)
        return jnp.concatenate(
            [_dot_nt(k_ref[0, pl.ds(st, BLK), hd * LANES:(hd + 1) * LANES], q_ref[0, q_rows, hd * LANES:(hd + 1) * LANES])
             for hd in range(n_heads)], axis=1)

    def weighted_values(n, p):
        return jnp.concatenate(
            [_dot(vt_ref[0, n, hd * VT_ROWS:(hd + 1) * VT_ROWS, :], p[:, hd * BLK:(hd + 1) * BLK])
             for hd in range(n_heads)], axis=1)

    def produce_first(slot):
        s = jnp.where(krow <= qcol, scores(qi), NEG_INF)
        s_ref[slot] = s
        return jnp.max(s, axis=0, keepdims=True)

    def produce(i, slot):
        s = scores(i - 1)
        s_ref[slot] = s
        return jnp.max(s, axis=0, keepdims=True)

    def consume(i, slot, block_max, m):
        n = jnp.where(i == 0, qi, i - 1)
        m_new = jnp.maximum(m, block_max)
        p = jnp.exp2(s_ref[slot] - m_new).astype(BF16)
        acc_ref[...] = jnp.exp2(m - m_new) * acc_ref[...] + weighted_values(n, p)
        return m_new

    acc_ref[...] = jnp.zeros_like(acc_ref)
    _pipelined_blocks(qi + 1, produce_first, produce, consume, jnp.full((1, lanes), NEG_INF, F32))
    o_t = _finish(acc_ref[...], MLA_V)
    o_ref[0, q_rows, :] = jnp.concatenate([o_t[:, hd * BLK:(hd + 1) * BLK] for hd in range(n_heads)],
                                          axis=0).T.astype(BF16)


def _mla(qb, kb, vbt):
    b, s, width = qb.shape
    n_heads = width // LANES
    out_shape = jax.ShapeDtypeStruct((b, s, n_heads * MLA_V), BF16)
    return pl.pallas_call(
        _per_query_tile(functools.partial(_mla_tile, n_heads=n_heads), s // BLK),
        grid=(b,),
        in_specs=[_seq_spec(qb.shape), _seq_spec(kb.shape), _seq_spec(vbt.shape)],
        out_specs=_seq_spec(out_shape.shape),
        out_shape=out_shape,
        scratch_shapes=[pltpu.VMEM((VT_ROWS, n_heads * BLK), F32), pltpu.VMEM((2, BLK, n_heads * BLK), F32)],
        compiler_params=_cparams(1),
        name="mla",
    )(qb, kb, vbt)


def _store_vt_ones(vt_ref, j, v, n_heads):
    vt = v.T.astype(BF16)
    ones = jnp.ones((ONES_ROWS, v.shape[0]), BF16)
    for hd in range(n_heads):
        vt_ref[0, j, hd * VT_ROWS:hd * VT_ROWS + HEAD_DIM, :] = vt[hd * HEAD_DIM:(hd + 1) * HEAD_DIM]
        vt_ref[0, j, hd * VT_ROWS + HEAD_DIM:(hd + 1) * VT_ROWS, :] = ones


def _proj1_body(x_ref, pre_ref, w_ref, qc_ref, kc_ref, vc_ref, ks_ref, vst_ref, kw_ref, vwt_ref, g_ref,
                qd_ref, kd_ref, vdt_ref, km_ref, *, tm):
    h = _rms(x_ref[...], pre_ref[...]).astype(BF16)
    y = _dot(h, w_ref[...])
    qc, kc, vc, ks, vs, kw, vw, gl, qd, kd, vd = (y[:, sl] for sl in _col_slices(P1_COLS))
    scale = LOG2E * HEAD_DIM ** -0.5
    qc_ref[...] = (qc * scale).astype(BF16)
    for hkv in range(C_KV_HEADS):
        kc_ref[0, hkv] = kc[:, hkv * HEAD_DIM:(hkv + 1) * HEAD_DIM].astype(BF16)
        vc_ref[0, hkv] = vc[:, hkv * HEAD_DIM:(hkv + 1) * HEAD_DIM].astype(BF16)
    ks_ref[...] = ks.astype(BF16)
    kw_ref[...] = kw.astype(BF16)
    g_ref[...] = 1.0 / (1.0 + jnp.exp(-gl))
    qd_ref[...] = (qd * scale).astype(BF16)
    kd_ref[...] = kd.astype(BF16)
    for j in range(tm // BLK):
        rows = slice(j * BLK, (j + 1) * BLK)
        km_ref[j] = jnp.mean(kd[rows], axis=0, keepdims=True)
        _store_vt_ones(vdt_ref, j, vd[rows], D_HEADS)
        _store_vt_ones(vst_ref, j, vs[rows], C_KV_HEADS)
        _store_vt_ones(vwt_ref, j, vw[rows], C_KV_HEADS)


def _proj1(x2, batch, seq, pre, w_in):
    m, d = x2.shape
    tm = TOKEN_TILE
    n_st = seq // tm
    n_pre = sum(P1_COLS[:7]) + N_GATES
    pad = jnp.zeros((d, LANES - N_GATES), F32)
    w1 = jnp.concatenate([w_in[:, :n_pre], pad, w_in[:, n_pre:]], axis=1).astype(BF16)
    q_w, kv_w, d_w = C_HEADS * HEAD_DIM, C_KV_HEADS * HEAD_DIM, D_HEADS * HEAD_DIM
    tok = lambda n: pl.BlockSpec((tm, n), lambda i: (i, 0))
    tok_shape = lambda n, dt=BF16: jax.ShapeDtypeStruct((m, n), dt)
    hm = pl.BlockSpec((1, C_KV_HEADS, tm, HEAD_DIM), lambda i: (i // n_st, 0, i % n_st, 0))
    hm_shape = jax.ShapeDtypeStruct((batch, C_KV_HEADS, seq, HEAD_DIM), BF16)
    nblk = tm // BLK
    vt_spec = lambda heads: pl.BlockSpec((1, nblk, heads * VT_ROWS, BLK), lambda i: (i // n_st, i % n_st, 0, 0))
    vt_shape = lambda heads: jax.ShapeDtypeStruct((batch, seq // BLK, heads * VT_ROWS, BLK), BF16)
    return pl.pallas_call(
        functools.partial(_proj1_body, tm=tm),
        grid=(m // tm,),
        in_specs=[tok(d), _const_spec((1, d)), _const_spec(w1.shape)],
        out_specs=[tok(q_w), hm, hm, tok(kv_w), vt_spec(C_KV_HEADS), tok(kv_w), vt_spec(C_KV_HEADS), tok(LANES),
                   tok(d_w), tok(d_w), vt_spec(D_HEADS),
                   pl.BlockSpec((nblk, 1, d_w), lambda i: (i, 0, 0))],
        out_shape=[
            tok_shape(q_w), hm_shape, hm_shape,
            tok_shape(kv_w), vt_shape(C_KV_HEADS), tok_shape(kv_w), vt_shape(C_KV_HEADS), tok_shape(LANES, F32),
            tok_shape(d_w), tok_shape(d_w), vt_shape(D_HEADS),
            jax.ShapeDtypeStruct((m // BLK, 1, d_w), F32),
        ],
        compiler_params=_cparams(1),
        name="proj1",
    )(x2, pre.reshape(1, d), w1)


def _compress_body(kx_ref, vx_ref, pek_ref, pev_ref, wk1_ref, wk2_ref, wv1_ref, wv2_ref, ko_ref, vo_ref):
    half = CMP_STRIDE * HEAD_DIM
    for x_ref, pe_ref, w1_ref, w2_ref, o_ref in ((kx_ref, pek_ref, wk1_ref, wk2_ref, ko_ref),
                                                 (vx_ref, pev_ref, wv1_ref, wv2_ref, vo_ref)):
        pe8 = jnp.broadcast_to(pe_ref[...], (SUBLANES, 2 * half)).astype(BF16)
        pe_term = _dot(pe8, w1_ref[...])[0:1]
        outs = []
        for hkv in range(C_KV_HEADS):
            x = x_ref[0, hkv]
            lo = _dot(x, w1_ref[0:half])
            hi = _dot(x, w1_ref[half:2 * half])
            hid = lo + pltpu.roll(hi, x.shape[0] - 1, 0) + pe_term
            outs.append(_dot(_silu(hid).astype(BF16), w2_ref[...]))
        out = jnp.concatenate(outs, axis=1)
        o_ref[0] = (out.T if o_ref is vo_ref else out).astype(BF16)


def _compress(kc_hm, vc_hm, pe_k, pe_v, wk1, wk2, wv1, wv2):
    b, _, s, dh = kc_hm.shape
    rows = s // CMP_STRIDE
    feat = CMP_STRIDE * dh
    kx = kc_hm.reshape(b, 2, rows, feat)
    vx = vc_hm.reshape(b, 2, rows, feat)
    xspec = pl.BlockSpec((1, 2, rows, feat), lambda bi: (bi, 0, 0, 0))
    ospec = pl.BlockSpec((1, rows, 2 * dh), lambda bi: (bi, 0, 0))
    w1s, w2s = wk1.shape, wk2.shape
    return pl.pallas_call(
        _compress_body,
        grid=(b,),
        in_specs=[xspec, xspec, _const_spec((1, 2 * feat)), _const_spec((1, 2 * feat)),
                  _const_spec(w1s), _const_spec(w2s), _const_spec(w1s), _const_spec(w2s)],
        out_specs=[ospec, ospec],
        out_shape=[jax.ShapeDtypeStruct((b, rows, 2 * dh), BF16)] * 2,
        compiler_params=_cparams(1),
        name="nsa_compress",
    )(kx, vx, pe_k.astype(F32).reshape(1, -1), pe_v.astype(F32).reshape(1, -1),
      wk1.astype(BF16), wk2.astype(BF16), wv1.astype(BF16), wv2.astype(BF16))


def _bias_block(bias_ref, hd, dist_blocks):
    sub = BLK // TILE
    rows = []
    for a in range(sub):
        cols = []
        for bq in range(sub):
            e = sub * dist_blocks + bq - a
            e = max(e, 0) if isinstance(e, int) else jnp.clip(e, 0, BIAS_TILES - 1)
            cols.append(bias_ref[hd, e])
        rows.append(jnp.concatenate(cols, axis=1))
    return jnp.concatenate(rows, axis=0)


def _rank_rows(val, ridx, n_real):
    rank = jnp.zeros(val.shape, F32)
    for j in range(n_real):
        vj = val[j:j + 1, :]
        rank = rank + jnp.where(vj > val, 1.0, jnp.where(vj == val, jnp.where(ridx > j, 1.0, 0.0), 0.0))
    return rank


def _nsa_tile(qi, q_ref, kc_ref, vct_ref, ks_ref, vst_ref, kw_ref, vwt_ref, g_ref, bias_ref, ovt_ref, o_ref,
              pen_ref, acc_ref, s_ref, *, n_sel):
    q_rows = _tile_rows(qi)
    grp = C_HEADS // C_KV_HEADS
    lanes = grp * BLK
    sub = BLK // SEL_BLOCK
    own = pl.multiple_of(qi * BLK, BLK)
    prev = pl.multiple_of(jnp.maximum(qi - 1, 0) * BLK, BLK)
    prev_blk = jnp.maximum(qi - 1, 0)
    krow = lax.broadcasted_iota(jnp.int32, (BLK, lanes), 0)
    qcol = lax.broadcasted_iota(jnp.int32, (BLK, lanes), 1) & (BLK - 1)
    causal = krow <= qcol
    crow = lax.broadcasted_iota(jnp.int32, (TILE, lanes), 0)
    cq = lax.broadcasted_iota(jnp.int32, (TILE, lanes), 1) & (BLK - 1)
    visible = CMP_STRIDE * crow + (CMP_BLOCK - 1) <= qi * BLK + cq
    jrow = lax.broadcasted_iota(jnp.int32, (n_sel, BLK), 0)
    cur = qi * sub + lax.broadcasted_iota(jnp.int32, (n_sel, BLK), 1) // SEL_BLOCK
    forced = (jrow == 0) | (jrow == cur) | (jrow == cur - 1)
    gates_t = g_ref[0, q_rows, :].T
    has_prev = jnp.where(qi >= 1, 0, BLK)

    def stack_q(hkv):
        return jnp.concatenate(
            [q_ref[0, q_rows, (hkv * grp + g) * HEAD_DIM:(hkv * grp + g + 1) * HEAD_DIM] for g in range(grp)], axis=0)

    def group_bias(hkv, dist_blocks):
        return jnp.concatenate([_bias_block(bias_ref, hkv * grp + g, dist_blocks) for g in range(grp)], axis=1)

    def sel_scores(hkv, n, dist_blocks):
        st = pl.multiple_of(n * BLK, BLK)
        sl = slice(hkv * HEAD_DIM, (hkv + 1) * HEAD_DIM)
        return _dot_nt(ks_ref[0, pl.ds(st, BLK), sl], stack_q(hkv)) + group_bias(hkv, dist_blocks)

    def sel_pens(hkv, n):
        pens = pen_ref[hkv, n, 0:sub, :]
        return [jnp.concatenate([pens[j:j + 1]] * grp, axis=1) for j in range(sub)]

    def sel_max(s, pens):
        mj = [jnp.max(s[j * SEL_BLOCK:(j + 1) * SEL_BLOCK], axis=0, keepdims=True) + pens[j] for j in range(sub)]
        return functools.reduce(jnp.maximum, mj)

    def sel_probs(s, pens, m_new):
        return jnp.concatenate([jnp.exp2(s[j * SEL_BLOCK:(j + 1) * SEL_BLOCK] - (m_new - pens[j]))
                                for j in range(sub)], axis=0).astype(BF16)

    o_cmp, o_win = [], []
    for hkv in range(C_KV_HEADS):
        sl = slice(hkv * HEAD_DIM, (hkv + 1) * HEAD_DIM)
        vsl = slice(hkv * VT_ROWS, (hkv + 1) * VT_ROWS)
        qs = stack_q(hkv)

        sc = jnp.where(visible, _dot_nt(kc_ref[0, :, sl], qs), NEG_INF)
        mc = jnp.max(sc, axis=0, keepdims=True)
        ec = jnp.where(visible, jnp.exp2(sc - mc), 0.0)
        lc = jnp.sum(ec, axis=0, keepdims=True)
        p = ec / jnp.where(lc > 0.0, lc, 1.0)
        o_cmp.append(_dot(vct_ref[0, sl, :], p.astype(BF16)))

        ps = p[:, 0:BLK] + p[:, BLK:2 * BLK] + p[:, 2 * BLK:3 * BLK] + p[:, 3 * BLK:4 * BLK]
        ps_hi = ps.astype(BF16)
        ps_lo = (ps - ps_hi.astype(F32)).astype(BF16)
        imp = _dot(ovt_ref[...], ps_hi) + _dot(ovt_ref[...], ps_lo)
        val = jnp.where(forced, FORCE_SCORE, jnp.where(jrow <= cur, imp, NEG_INF))
        pen = jnp.where(_rank_rows(val, jrow, n_sel) < SEL_TOPK, 0.0, NEG_INF)
        for t in range(n_sel // sub):
            pen_ref[hkv, t, 0:sub, :] = pen[t * sub:(t + 1) * sub]

        s0 = _dot_nt(kw_ref[0, pl.ds(own, BLK), sl], qs) + group_bias(hkv, 0)
        s1 = _dot_nt(kw_ref[0, pl.ds(prev, BLK), sl], qs) + group_bias(hkv, 1)
        s0 = jnp.where(causal, s0, NEG_INF)
        s1 = jnp.where(krow > qcol + has_prev, s1, NEG_INF)
        mw = jnp.maximum(jnp.max(s0, axis=0, keepdims=True), jnp.max(s1, axis=0, keepdims=True))
        ow = (_dot(vwt_ref[0, qi, vsl, :], jnp.exp2(s0 - mw).astype(BF16))
              + _dot(vwt_ref[0, prev_blk, vsl, :], jnp.exp2(s1 - mw).astype(BF16)))
        o_win.append(_finish(ow, HEAD_DIM))

    def produce_first(slot):
        stats = []
        for hkv in range(C_KV_HEADS):
            s = jnp.where(causal, sel_scores(hkv, qi, 0), NEG_INF)
            s_ref[slot, hkv] = s
            stats.append(sel_max(s, sel_pens(hkv, qi)))
        return tuple(stats)

    def produce(i, slot):
        stats = []
        for hkv in range(C_KV_HEADS):
            s = sel_scores(hkv, i - 1, qi - (i - 1))
            s_ref[slot, hkv] = s
            stats.append(sel_max(s, sel_pens(hkv, i - 1)))
        return tuple(stats)

    def consume(i, slot, block_max, ms):
        n = jnp.where(i == 0, qi, i - 1)
        new = []
        for hkv in range(C_KV_HEADS):
            vsl = slice(hkv * VT_ROWS, (hkv + 1) * VT_ROWS)
            m_new = jnp.maximum(ms[hkv], block_max[hkv])
            pb = sel_probs(s_ref[slot, hkv], sel_pens(hkv, n), m_new)
            acc_ref[hkv] = jnp.exp2(ms[hkv] - m_new) * acc_ref[hkv] + _dot(vst_ref[0, n, vsl, :], pb)
            new.append(m_new)
        return tuple(new)

    acc_ref[...] = jnp.zeros_like(acc_ref)
    m_init = jnp.full((1, lanes), NEG_INF, F32)
    _pipelined_blocks(qi + 1, produce_first, produce, consume, (m_init, m_init))
    outs = []
    for hkv in range(C_KV_HEADS):
        o_slc = _finish(acc_ref[hkv], HEAD_DIM)
        for g in range(grp):
            hd = hkv * grp + g
            ls = slice(g * BLK, (g + 1) * BLK)
            g_cmp, g_slc, g_win = (gates_t[t * C_HEADS + hd:t * C_HEADS + hd + 1] for t in range(3))
            outs.append(g_cmp * o_cmp[hkv][:, ls] + g_slc * o_slc[:, ls] + g_win * o_win[hkv][:, ls])
    o_ref[0, q_rows, :] = jnp.concatenate(outs, axis=0).T.astype(BF16)


def _nsa(qc, kcmp, vcmpt, ks, vst, kw, vwt, gates, bias_c):
    b, s, width = qc.shape
    n_blocks = s // BLK
    n_sel = s // SEL_BLOCK
    n_c = (s - CMP_BLOCK) // CMP_STRIDE + 1
    j_start = np.arange(n_sel)[:, None] * SEL_BLOCK
    c_start = np.arange(TILE)[None, :] * CMP_STRIDE
    overlap_t = ((c_start < j_start + SEL_BLOCK) & (c_start + CMP_BLOCK > j_start)
                 & (np.arange(TILE)[None, :] < n_c))
    ovt = jnp.asarray(overlap_t, BF16)
    group_lanes = C_HEADS // C_KV_HEADS * BLK
    operands = (qc, kcmp, vcmpt, ks, vst, kw, vwt, gates)
    return pl.pallas_call(
        _per_query_tile(functools.partial(_nsa_tile, n_sel=n_sel), n_blocks),
        grid=(b,),
        in_specs=[_seq_spec(t.shape) for t in operands] + [_const_spec(bias_c.shape), _const_spec(ovt.shape)],
        out_specs=_seq_spec(qc.shape),
        out_shape=jax.ShapeDtypeStruct((b, s, width), BF16),
        scratch_shapes=[pltpu.VMEM((C_KV_HEADS, n_blocks, SUBLANES, BLK), F32),
                        pltpu.VMEM((C_KV_HEADS, VT_ROWS, group_lanes), F32),
                        pltpu.VMEM((2, C_KV_HEADS, BLK, group_lanes), F32)],
        compiler_params=_cparams(1),
        name="nsa",
    )(*operands, bias_c, ovt)


def _moba_tile(qi, q_ref, k_ref, vt_ref, km_ref, bias_ref, o_ref, pen_ref, acc_ref, s_ref, *, n_heads, n_blocks):
    q_rows = _tile_rows(qi)
    lanes = n_heads * BLK
    nidx = lax.broadcasted_iota(jnp.int32, (GATE_ROWS, lanes), 0)
    krow = lax.broadcasted_iota(jnp.int32, (BLK, lanes), 0)
    qcol = lax.broadcasted_iota(jnp.int32, (BLK, lanes), 1) & (BLK - 1)

    def scores(n, dist_blocks):
        st = pl.multiple_of(n * BLK, BLK)
        parts = []
        for hd in range(n_heads):
            sl = slice(hd * HEAD_DIM, (hd + 1) * HEAD_DIM)
            parts.append(_dot_nt(k_ref[0, pl.ds(st, BLK), sl], q_ref[0, q_rows, sl])
                         + _bias_block(bias_ref, hd, dist_blocks))
        return jnp.concatenate(parts, axis=1)

    def weighted_values(n, p):
        return jnp.concatenate(
            [_dot(vt_ref[0, n, hd * VT_ROWS:(hd + 1) * VT_ROWS, :], p[:, hd * BLK:(hd + 1) * BLK])
             for hd in range(n_heads)], axis=1)

    zpad = jnp.zeros((GATE_ROWS - n_blocks, HEAD_DIM), F32)
    gate = jnp.concatenate(
        [_dot_nt(jnp.concatenate([km_ref[0, :, hd * HEAD_DIM:(hd + 1) * HEAD_DIM], zpad], axis=0).astype(BF16),
                 q_ref[0, q_rows, hd * HEAD_DIM:(hd + 1) * HEAD_DIM]) for hd in range(n_heads)], axis=1)
    val = jnp.where(nidx < qi, gate, NEG_INF)
    val = jnp.where(nidx < n_blocks, val, PAD_SCORE)
    keep = ((_rank_rows(val, nidx, n_blocks) < MOBA_TOPK) & (nidx < qi)) | (nidx == qi)
    pen_ref[...] = jnp.where(keep, 0.0, NEG_INF)

    def produce_first(slot):
        s = jnp.where(krow <= qcol, scores(qi, 0), NEG_INF)
        s_ref[slot] = s
        return jnp.max(s, axis=0, keepdims=True)

    def produce(i, slot):
        s = scores(i - 1, qi - (i - 1))
        s_ref[slot] = s
        return jnp.max(s, axis=0, keepdims=True) + pen_ref[pl.ds(i - 1, 1), :]

    def consume(i, slot, block_max, m):
        n = jnp.where(i == 0, qi, i - 1)
        pen = pen_ref[pl.ds(n, 1), :]
        m_new = jnp.maximum(m, block_max)
        p = jnp.exp2(s_ref[slot] - (m_new - pen)).astype(BF16)
        acc_ref[...] = jnp.exp2(m - m_new) * acc_ref[...] + weighted_values(n, p)
        return m_new

    acc_ref[...] = jnp.zeros_like(acc_ref)
    _pipelined_blocks(qi + 1, produce_first, produce, consume, jnp.full((1, lanes), NEG_INF, F32))
    o_t = _finish(acc_ref[...], HEAD_DIM)
    o_ref[0, q_rows, :] = jnp.concatenate([o_t[:, hd * BLK:(hd + 1) * BLK] for hd in range(n_heads)],
                                          axis=0).T.astype(BF16)


def _moba(qd, kd, vdt, kmean, bias_d):
    b, s, width = qd.shape
    n_heads = width // HEAD_DIM
    n_blocks = s // BLK
    operands = (qd, kd, vdt, kmean.reshape(b, n_blocks, width))
    return pl.pallas_call(
        _per_query_tile(functools.partial(_moba_tile, n_heads=n_heads, n_blocks=n_blocks), n_blocks),
        grid=(b,),
        in_specs=[_seq_spec(t.shape) for t in operands] + [_const_spec(bias_d.shape)],
        out_specs=_seq_spec(qd.shape),
        out_shape=jax.ShapeDtypeStruct((b, s, width), BF16),
        scratch_shapes=[pltpu.VMEM((GATE_ROWS, n_heads * BLK), F32), pltpu.VMEM((VT_ROWS, n_heads * BLK), F32),
                        pltpu.VMEM((2, BLK, n_heads * BLK), F32)],
        compiler_params=_cparams(1),
        name="moba",
    )(*operands, bias_d)


def kernel(x, rel_bias_table, l0_ffn1_pre, l0_ffn1_post, l0_ffn1_wg, l0_ffn1_wu, l0_ffn1_wd, l0_mix_pre, l0_mix_post, l0_w_in, l0_sinks, l0_mla_q_norm, l0_mla_w_uq, l0_mla_kv_norm, l0_mla_w_ukv, l0_w_out, l0_ffn2_pre, l0_ffn2_post, l0_ffn2_wg, l0_ffn2_wu, l0_ffn2_wd, l1_ffn1_pre, l1_ffn1_post, l1_ffn1_wg, l1_ffn1_wu, l1_ffn1_wd, l1_mix_pre, l1_mix_post, l1_w_in, l1_nsa_pe_k, l1_nsa_pe_v, l1_nsa_wk1, l1_nsa_wk2, l1_nsa_wv1, l1_nsa_wv2, l1_w_out, l1_ffn2_pre, l1_ffn2_post, l1_ffn2_wg, l1_ffn2_wu, l1_ffn2_wd):
    b, s, d = x.shape
    m = b * s
    x2 = x.reshape(m, d)
    tiles = _bias_tiles(rel_bias_table)
    bias_a = tiles[A_SLOT:A_SLOT + 8, 0:2]
    bias_c = tiles[C_SLOT:C_SLOT + 8]
    bias_d = tiles[D_SLOT:D_SLOT + 8]

    x2 = _ffn(x2, l0_ffn1_pre, l0_ffn1_post, l0_ffn1_wg, l0_ffn1_wu, l0_ffn1_wd, 0.5)
    qa, ka, vat, qb, kb, vbt = _proj0(x2, s, l0_mix_pre, l0_w_in, l0_mla_q_norm, l0_mla_w_uq,
                                      l0_mla_kv_norm, l0_mla_w_ukv)
    shp = lambda t: t.reshape(b, s, t.shape[-1])
    o_a = _swa(shp(qa), shp(ka), vat, bias_a, l0_sinks)
    o_b = _mla(shp(qb), shp(kb), vbt)
    x2 = _outproj_ffn(x2, o_a.reshape(m, -1), o_b.reshape(m, -1), l0_w_out, l0_mix_post,
                      l0_ffn2_pre, l0_ffn2_post, l0_ffn2_wg, l0_ffn2_wu, l0_ffn2_wd, 0.5)

    x2 = _ffn(x2, l1_ffn1_pre, l1_ffn1_post, l1_ffn1_wg, l1_ffn1_wu, l1_ffn1_wd, 0.5)
    qc, kc_hm, vc_hm, ks, vst, kw, vwt, gates, qd, kd, vdt, kmean = _proj1(x2, b, s, l1_mix_pre, l1_w_in)
    kcmp, vcmpt = _compress(kc_hm, vc_hm, l1_nsa_pe_k, l1_nsa_pe_v, l1_nsa_wk1, l1_nsa_wk2, l1_nsa_wv1, l1_nsa_wv2)
    o_c = _nsa(shp(qc), kcmp, vcmpt, shp(ks), vst, shp(kw), vwt, shp(gates), bias_c)
    o_d = _moba(shp(qd), shp(kd), vdt, kmean, bias_d)
    x2 = _outproj_ffn(x2, o_c.reshape(m, -1), o_d.reshape(m, -1), l1_w_out, l1_mix_post,
                      l1_ffn2_pre, l1_ffn2_post, l1_ffn2_wg, l1_ffn2_wu, l1_ffn2_wd, 0.5)
    return x2.reshape(b, s, d)
```

```python
import functools
import math

import numpy as np
import jax
import jax.numpy as jnp
from jax import lax
from jax.experimental import pallas as pl
from jax.experimental.pallas import tpu as pltpu

F32 = jnp.float32
BF16 = jnp.bfloat16

HEAD_DIM = 64
NORM_EPS = 1e-6
LOG2E = math.log2(math.e)
NEG_INF = -1e30
PAD_SCORE = -3e38
FORCE_SCORE = 1e9
REL_BUCKETS = 32
REL_MAX_DIST = 1024
REL_SLOTS = 24
A_SLOT, C_SLOT, D_SLOT = 0, 8, 16
A_HEADS, A_KV_HEADS = 8, 2
MLA_NOPE, MLA_ROPE, MLA_V = 64, 32, 64
MLA_HEADS, MLA_Q_RANK, MLA_KV_RANK = 8, 256, 128
ROPE_THETA = 10000.0
C_HEADS, C_KV_HEADS = 8, 2
CMP_BLOCK, CMP_STRIDE = 32, 16
SEL_BLOCK, SEL_TOPK = 64, 8
D_HEADS = 8
MOBA_TOPK = 3

LANES = 128
SUBLANES = 8
TILE = 128
P0_COLS = (A_HEADS * HEAD_DIM, A_KV_HEADS * HEAD_DIM, A_KV_HEADS * HEAD_DIM, MLA_Q_RANK, MLA_KV_RANK, LANES)
P1_COLS = ((C_HEADS * HEAD_DIM,) + (C_KV_HEADS * HEAD_DIM,) * 6 + (LANES,) + (D_HEADS * HEAD_DIM,) * 3)
N_GATES = 3 * C_HEADS
BLK = 256
ONES_ROWS = 16
GATE_ROWS = 16
VT_ROWS = HEAD_DIM + ONES_ROWS
BIAS_TILES = 9
TOKEN_TILE = 512
FF_CHUNK = 256
VMEM_LIMIT = 56 * 1024 * 1024


def _cparams(n_axes):
    return pltpu.CompilerParams(dimension_semantics=("arbitrary",) * n_axes, vmem_limit_bytes=VMEM_LIMIT)


def _dot(a, b):
    return jnp.dot(a, b, preferred_element_type=F32)


def _dot_nt(a, b):
    return lax.dot_general(a, b, (((1,), (1,)), ((), ())), preferred_element_type=F32)


def _rms(x, g):
    return x * lax.rsqrt(jnp.mean(x * x, axis=-1, keepdims=True) + NORM_EPS) * g


def _silu(x):
    return x / (1.0 + jnp.exp(-x))


def _col_slices(sizes):
    edges = np.cumsum((0,) + tuple(sizes)).tolist()
    return [slice(a, b) for a, b in zip(edges[:-1], edges[1:])]


def _const_spec(shape):
    nd = len(shape)
    return pl.BlockSpec(shape, lambda *_: (0,) * nd)


def _pipelined_blocks(length, produce_first, produce, consume, state):
    trips = (length - 1) // 2

    def trip(t, carry):
        state, stat_a = carry
        a = 2 * t
        stat_b = produce(a + 1, 1)
        state = consume(a, 0, stat_a, state)
        stat_next = produce(a + 2, 0)
        state = consume(a + 1, 1, stat_b, state)
        return state, stat_next

    carry = lax.fori_loop(0, trips, trip, (state, produce_first(0)))
    a = 2 * trips

    def tail_one(c):
        return consume(a, 0, c[1], c[0])

    def tail_two(c):
        stat_b = produce(a + 1, 1)
        return consume(a + 1, 1, stat_b, consume(a, 0, c[1], c[0]))

    return lax.cond(length - a == 1, tail_one, tail_two, carry)


def _per_query_tile(tile_fn, n_tiles):
    def body(*refs):
        def step(qi, carry):
            tile_fn(qi, *refs)
            return carry

        lax.fori_loop(0, n_tiles, step, 0)

    return body


def _tile_rows(qi):
    return pl.ds(pl.multiple_of(qi * BLK, BLK), BLK)


def _resident_spec(shape):
    nd = len(shape)
    return pl.BlockSpec(shape, lambda *_: (0,) * nd, pipeline_mode=pl.Buffered(1))


def _ffn_tile(x, pre_ref, post_ref, wg_ref, wu_ref, wd_ref, acc_ref, weight):
    h = _rms(x, pre_ref[...]).astype(BF16)
    for c in range(wg_ref.shape[1] // FF_CHUNK):
        cols = slice(c * FF_CHUNK, (c + 1) * FF_CHUNK)
        g = _dot(h, wg_ref[:, cols].astype(BF16))
        u = _dot(h, wu_ref[:, cols].astype(BF16))
        a = (_silu(g) * u).astype(BF16)
        y = _dot(a, wd_ref[cols, :].astype(BF16))
        if c == 0:
            acc_ref[...] = y
        else:
            acc_ref[...] += y
    return x + weight * _rms(acc_ref[...], post_ref[...])


def _ffn_body(x_ref, pre_ref, post_ref, wg_ref, wu_ref, wd_ref, o_ref, acc_ref, *, weight):
    o_ref[...] = _ffn_tile(x_ref[...], pre_ref, post_ref, wg_ref, wu_ref, wd_ref, acc_ref, weight)


def _ffn(x2, pre, post, wg, wu, wd, weight):
    m, d = x2.shape
    d_ff = wg.shape[1]
    tm = TOKEN_TILE
    return pl.pallas_call(
        functools.partial(_ffn_body, weight=weight),
        grid=(m // tm,),
        in_specs=[
            pl.BlockSpec((tm, d), lambda i: (i, 0)),
            _const_spec((1, d)), _const_spec((1, d)),
            _resident_spec((d, d_ff)), _resident_spec((d, d_ff)), _resident_spec((d_ff, d)),
        ],
        out_specs=pl.BlockSpec((tm, d), lambda i: (i, 0)),
        out_shape=jax.ShapeDtypeStruct((m, d), F32),
        scratch_shapes=[pltpu.VMEM((tm, d), F32)],
        compiler_params=_cparams(1),
        name="ffn",
    )(x2, pre.reshape(1, d), post.reshape(1, d), wg, wu, wd)


def _outproj_ffn_body(x_ref, o1_ref, o2_ref, wo_ref, mpost_ref, pre_ref, post_ref, wg_ref, wu_ref, wd_ref,
                      o_ref, acc_ref, *, weight):
    n1 = o1_ref.shape[1]
    y = (_dot(o1_ref[...], wo_ref[0:n1, :].astype(BF16))
         + _dot(o2_ref[...], wo_ref[n1:, :].astype(BF16)))
    x = x_ref[...] + _rms(y, mpost_ref[...])
    o_ref[...] = _ffn_tile(x, pre_ref, post_ref, wg_ref, wu_ref, wd_ref, acc_ref, weight)


def _outproj_ffn(x2, o1, o2, w_out, mix_post, pre, post, wg, wu, wd, weight):
    m, d = x2.shape
    d_ff = wg.shape[1]
    tm = TOKEN_TILE
    tok = lambda n: pl.BlockSpec((tm, n), lambda i: (i, 0))
    return pl.pallas_call(
        functools.partial(_outproj_ffn_body, weight=weight),
        grid=(m // tm,),
        in_specs=[
            tok(d), tok(o1.shape[1]), tok(o2.shape[1]), _resident_spec(w_out.shape), _const_spec((1, d)),
            _const_spec((1, d)), _const_spec((1, d)),
            _resident_spec((d, d_ff)), _resident_spec((d, d_ff)), _resident_spec((d_ff, d)),
        ],
        out_specs=tok(d),
        out_shape=jax.ShapeDtypeStruct((m, d), F32),
        scratch_shapes=[pltpu.VMEM((tm, d), F32)],
        compiler_params=_cparams(1),
        name="outproj_ffn",
    )(x2, o1, o2, w_out, mix_post.reshape(1, d), pre.reshape(1, d), post.reshape(1, d), wg, wu, wd)


def _t5_bucket(dist):
    n = jnp.maximum(dist, 0)
    exact = REL_BUCKETS // 2
    nf = jnp.maximum(n, 1).astype(jnp.float32)
    large = exact + (jnp.log(nf / exact) / math.log(REL_MAX_DIST / exact) * (REL_BUCKETS - exact)).astype(jnp.int32)
    return jnp.where(n < exact, n, jnp.minimum(large, REL_BUCKETS - 1))


def _tile_bucket_range(d):
    dist = np.arange(max(d * TILE - (TILE - 1), 0), d * TILE + TILE)
    exact = REL_BUCKETS // 2
    large = exact + np.log(np.maximum(dist, 1) / exact) / math.log(REL_MAX_DIST / exact) * (REL_BUCKETS - exact)
    bucket = np.where(dist < exact, dist, np.minimum(large.astype(np.int64), REL_BUCKETS - 1))
    return max(int(bucket.min()) - 1, 0), min(int(bucket.max()) + 1, REL_BUCKETS - 1)


def _bias_body(tab_ref, idx_ref, o_ref):
    slot = pl.program_id(0)
    for d in range(BIAS_TILES):
        idx = idx_ref[d]
        lo, hi = _tile_bucket_range(d)
        v = jnp.full((TILE, TILE), LOG2E * tab_ref[lo * REL_SLOTS + slot], F32)
        for b in range(lo + 1, hi + 1):
            v = jnp.where(idx == b, LOG2E * tab_ref[b * REL_SLOTS + slot], v)
        o_ref[0, d] = v


def _bias_tiles(rel_table):
    n_slots = rel_table.shape[1]
    d = jnp.arange(BIAS_TILES)[:, None, None]
    key = jnp.arange(TILE)[None, :, None]
    query = jnp.arange(TILE)[None, None, :]
    idx = _t5_bucket(d * TILE + query - key).astype(jnp.int32)
    return pl.pallas_call(
        _bias_body,
        grid=(n_slots,),
        in_specs=[
            pl.BlockSpec(memory_space=pltpu.SMEM),
            _const_spec((BIAS_TILES, TILE, TILE)),
        ],
        out_specs=pl.BlockSpec((1, BIAS_TILES, TILE, TILE), lambda s: (s, 0, 0, 0)),
        out_shape=jax.ShapeDtypeStruct((n_slots, BIAS_TILES, TILE, TILE), F32),
        compiler_params=_cparams(1),
        name="bias_tiles",
    )(rel_table.astype(F32).reshape(-1), idx)


def _proj0_body(x_ref, pre_ref, w_ref, qn_ref, wq1_ref, wq2_ref, kvn_ref, wk_ref, wv_ref, cos_ref, sin_ref,
                qa_ref, ka_ref, vat_ref, qb_ref, kb_ref, vbt_ref):
    h = _rms(x_ref[...], pre_ref[...]).astype(BF16)
    y = _dot(h, w_ref[...])
    qa, ka, va, cq, ckv, rot = (y[:, sl] for sl in _col_slices(P0_COLS))
    qa_ref[...] = (qa * (LOG2E * HEAD_DIM ** -0.5)).astype(BF16)
    ka_ref[...] = ka.astype(BF16)
    for j in range(y.shape[0] // TILE):
        _store_vt_ones(vat_ref, j, va[j * TILE:(j + 1) * TILE], A_KV_HEADS)
    cq = _rms(cq, qn_ref[...]).astype(BF16)
    ckv = _rms(ckv, kvn_ref[...]).astype(BF16)
    cos_t = cos_ref[...]
    sin_t = sin_ref[...]
    lane = lax.broadcasted_iota(jnp.int32, cos_t.shape, 1)
    scale = LOG2E * (MLA_NOPE + MLA_ROPE) ** -0.5
    qmul = scale * (cos_t + jnp.where(lane < MLA_NOPE, 1.0, 0.0))
    qsin = scale * sin_t
    kr = rot * cos_t + pltpu.roll(rot, LANES // 2, 1) * sin_t
    q1 = _dot(cq, wq1_ref[...])
    q2 = _dot(cq, wq2_ref[...])
    k1 = _dot(ckv, wk_ref[...])
    for hd in range(MLA_HEADS):
        sl = slice(hd * LANES, (hd + 1) * LANES)
        qb_ref[:, sl] = (q1[:, sl] * qmul + q2[:, sl] * qsin).astype(BF16)
        kb_ref[:, sl] = (k1[:, sl] + kr).astype(BF16)
    vb = _dot(ckv, wv_ref[...])
    for j in range(vb.shape[0] // BLK):
        _store_vt_ones(vbt_ref, j, vb[j * BLK:(j + 1) * BLK], MLA_HEADS)


def _rope_lane_tables(s):
    inv = ROPE_THETA ** (-jnp.arange(0, MLA_ROPE, 2, dtype=jnp.float32) / MLA_ROPE)
    ang = jnp.arange(s, dtype=jnp.float32)[:, None] * inv[None, :]
    cos, sin = jnp.cos(ang), jnp.sin(ang)
    z_lo = jnp.zeros((s, MLA_NOPE), F32)
    z_hi = jnp.zeros((s, LANES - MLA_NOPE - MLA_ROPE), F32)
    return (jnp.concatenate([z_lo, cos, cos, z_hi], axis=1), jnp.concatenate([z_lo, sin, sin, z_hi], axis=1))


def _rot_cols(w):
    half = w.shape[-1] // 2
    return jnp.concatenate([-w[..., half:], w[..., :half]], axis=-1)


def _proj0(x2, seq, pre, w_in, q_norm, w_uq, kv_norm, w_ukv):
    m, d = x2.shape
    tm = TOKEN_TILE
    n_lin = sum(P0_COLS[:-1])
    kr_w = w_in[:, n_lin:n_lin + MLA_ROPE]
    z32 = jnp.zeros((d, MLA_ROPE), F32)
    w0 = jnp.concatenate([w_in[:, :n_lin], _rot_cols(kr_w), z32, kr_w, z32], axis=1).astype(BF16)
    wq = w_uq.reshape(-1, MLA_HEADS, MLA_NOPE + MLA_ROPE)
    rq = wq.shape[0]
    pad = LANES - MLA_NOPE - MLA_ROPE
    zq = jnp.zeros((rq, MLA_HEADS, pad), F32)
    wq1 = jnp.concatenate([wq, zq], axis=2).reshape(rq, -1).astype(BF16)
    wq2 = jnp.concatenate([jnp.zeros((rq, MLA_HEADS, MLA_NOPE), F32), _rot_cols(wq[:, :, MLA_NOPE:]), zq],
                          axis=2).reshape(rq, -1).astype(BF16)
    wkv = w_ukv.reshape(-1, MLA_HEADS, MLA_NOPE + MLA_V)
    rk = wkv.shape[0]
    wk = jnp.concatenate([wkv[:, :, :MLA_NOPE], jnp.zeros((rk, MLA_HEADS, LANES - MLA_NOPE), F32)],
                         axis=2).reshape(rk, -1).astype(BF16)
    wv = wkv[:, :, MLA_NOPE:].reshape(rk, -1).astype(BF16)
    cos_t, sin_t = _rope_lane_tables(seq)
    n_st = seq // tm
    tok = lambda n: pl.BlockSpec((tm, n), lambda i: (i, 0))
    outs = pl.pallas_call(
        _proj0_body,
        grid=(m // tm,),
        in_specs=[
            tok(d), _const_spec((1, d)), _const_spec(w0.shape),
            _const_spec((1, rq)), _const_spec(wq1.shape), _const_spec(wq2.shape),
            _const_spec((1, rk)), _const_spec(wk.shape), _const_spec(wv.shape),
            pl.BlockSpec((tm, LANES), lambda i: (i % n_st, 0)),
            pl.BlockSpec((tm, LANES), lambda i: (i % n_st, 0)),
        ],
        out_specs=[tok(P0_COLS[0]), tok(P0_COLS[1]),
                   pl.BlockSpec((1, tm // TILE, A_KV_HEADS * VT_ROWS, TILE), lambda i: (i // n_st, i % n_st, 0, 0)),
                   tok(MLA_HEADS * LANES), tok(MLA_HEADS * LANES),
                   pl.BlockSpec((1, tm // BLK, MLA_HEADS * VT_ROWS, BLK), lambda i: (i // n_st, i % n_st, 0, 0))],
        out_shape=[jax.ShapeDtypeStruct((m, P0_COLS[0]), BF16), jax.ShapeDtypeStruct((m, P0_COLS[1]), BF16),
                   jax.ShapeDtypeStruct((m // seq, seq // TILE, A_KV_HEADS * VT_ROWS, TILE), BF16),
                   jax.ShapeDtypeStruct((m, MLA_HEADS * LANES), BF16),
                   jax.ShapeDtypeStruct((m, MLA_HEADS * LANES), BF16),
                   jax.ShapeDtypeStruct((m // seq, seq // BLK, MLA_HEADS * VT_ROWS, BLK), BF16)],
        compiler_params=_cparams(1),
        name="proj0",
    )(x2, pre.reshape(1, d), w0, q_norm.reshape(1, rq), wq1, wq2, kv_norm.reshape(1, rk), wk, wv, cos_t, sin_t)
    return outs


def _swa_tile(qi, sink_ref, q_ref, k_ref, vt_ref, bias_ref, o_ref, *, n_kv, grp):
    halves = BLK // TILE
    items = [(hkv, half) for hkv in range(n_kv) for half in range(halves)]
    width = grp * TILE
    krow = lax.broadcasted_iota(jnp.int32, (TILE, len(items) * width), 0)
    qcol = lax.broadcasted_iota(jnp.int32, (TILE, len(items) * width), 1) & (TILE - 1)

    cur_s, prev_s, sink_parts, has_prev = [], [], [], []
    for hkv, half in items:
        sl = slice(hkv * HEAD_DIM, (hkv + 1) * HEAD_DIM)
        t = qi * halves + half
        tp = jnp.maximum(t - 1, 0)
        q_rows = pl.ds(pl.multiple_of(t * TILE, TILE), TILE)
        qs = jnp.concatenate([q_ref[0, q_rows, (hkv * grp + g) * HEAD_DIM:(hkv * grp + g + 1) * HEAD_DIM]
                              for g in range(grp)], axis=0)
        bias_c = jnp.concatenate([bias_ref[hkv * grp + g, 0] for g in range(grp)], axis=1)
        bias_p = jnp.concatenate([bias_ref[hkv * grp + g, 1] for g in range(grp)], axis=1)
        cur_s.append(_dot_nt(k_ref[0, pl.ds(pl.multiple_of(t * TILE, TILE), TILE), sl], qs) + bias_c)
        prev_s.append(_dot_nt(k_ref[0, pl.ds(pl.multiple_of(tp * TILE, TILE), TILE), sl], qs) + bias_p)
        sink_parts.extend(jnp.full((1, TILE), LOG2E * sink_ref[hkv * grp + g], F32) for g in range(grp))
        has_prev.append(jnp.full((1, width), jnp.where(t >= 1, 0, TILE), jnp.int32))
    s_c = jnp.where(krow <= qcol, jnp.concatenate(cur_s, axis=1), NEG_INF)
    s_p = jnp.where(krow > qcol + jnp.concatenate(has_prev, axis=1),
                    jnp.concatenate(prev_s, axis=1), NEG_INF)
    sink = jnp.concatenate(sink_parts, axis=1)
    mx = jnp.maximum(jnp.maximum(jnp.max(s_c, axis=0, keepdims=True), jnp.max(s_p, axis=0, keepdims=True)), sink)
    p_c = jnp.exp2(s_c - mx).astype(BF16)
    p_p = jnp.exp2(s_p - mx).astype(BF16)
    accs = []
    for i, (hkv, half) in enumerate(items):
        vsl = slice(hkv * VT_ROWS, (hkv + 1) * VT_ROWS)
        ls = slice(i * width, (i + 1) * width)
        t = qi * halves + half
        accs.append(_dot(vt_ref[0, t, vsl, :], p_c[:, ls]) + _dot(vt_ref[0, jnp.maximum(t - 1, 0), vsl, :], p_p[:, ls]))
    acc = jnp.concatenate(accs, axis=1)
    o_t = acc[0:HEAD_DIM] / (acc[HEAD_DIM:HEAD_DIM + 1] + jnp.exp2(sink - mx))
    heads = []
    for hkv in range(n_kv):
        for g in range(grp):
            heads.append(jnp.concatenate(
                [o_t[:, (hkv * halves + half) * width + g * TILE:(hkv * halves + half) * width + (g + 1) * TILE]
                 for half in range(halves)], axis=1))
    o_ref[0, _tile_rows(qi), :] = jnp.concatenate(heads, axis=0).T.astype(BF16)


def _seq_spec(shape):
    nd = len(shape)
    return pl.BlockSpec((1,) + tuple(shape[1:]), lambda bi: (bi,) + (0,) * (nd - 1))


def _swa(qa, ka, vat, bias_a, sinks):
    b, s, width = qa.shape
    n_kv = ka.shape[-1] // HEAD_DIM
    grp = width // HEAD_DIM // n_kv
    return pl.pallas_call(
        _per_query_tile(functools.partial(_swa_tile, n_kv=n_kv, grp=grp), s // BLK),
        grid=(b,),
        in_specs=[pl.BlockSpec(memory_space=pltpu.SMEM), _seq_spec(qa.shape), _seq_spec(ka.shape),
                  _seq_spec(vat.shape), _const_spec(bias_a.shape)],
        out_specs=_seq_spec(qa.shape),
        out_shape=jax.ShapeDtypeStruct((b, s, width), BF16),
        compiler_params=_cparams(1),
        name="swa",
    )(sinks.astype(F32), qa, ka, vat, bias_a)


def _finish(acc, dv):
    return acc[0:dv] / acc[dv:dv + 1]


def _mla_tile(qi, q_ref, k_ref, vt_ref, o_ref, acc_ref, s_ref, *, n_heads):
    q_rows = _tile_rows(qi)
    lanes = n_heads * BLK
    krow = lax.broadcasted_iota(jnp.int32, (BLK, lanes), 0)
    qcol = lax.broadcasted_iota(jnp.int32, (BLK, lanes), 1) & (BLK - 1)

    def scores(n):
        st = pl.multiple_of(n * BLK, BLK)
        return jnp.concatenate(
            [_dot_nt(k_ref[0, pl.ds(st, BLK), hd * LANES:(hd + 1) * LANES], q_ref[0, q_rows, hd * LANES:(hd + 1) * LANES])
             for hd in range(n_heads)], axis=1)

    def weighted_values(n, p):
        return jnp.concatenate(
            [_dot(vt_ref[0, n, hd * VT_ROWS:(hd + 1) * VT_ROWS, :], p[:, hd * BLK:(hd + 1) * BLK])
             for hd in range(n_heads)], axis=1)

    def produce_first(slot):
        s = jnp.where(krow <= qcol, scores(qi), NEG_INF)
        s_ref[slot] = s
        return jnp.max(s, axis=0, keepdims=True)

    def produce(i, slot):
        s = scores(i - 1)
        s_ref[slot] = s
        return jnp.max(s, axis=0, keepdims=True)

    def consume(i, slot, block_max, m):
        n = jnp.where(i == 0, qi, i - 1)
        m_new = jnp.maximum(m, block_max)
        p = jnp.exp2(s_ref[slot] - m_new).astype(BF16)
        acc_ref[...] = jnp.exp2(m - m_new) * acc_ref[...] + weighted_values(n, p)
        return m_new

    acc_ref[...] = jnp.zeros_like(acc_ref)
    _pipelined_blocks(qi + 1, produce_first, produce, consume, jnp.full((1, lanes), NEG_INF, F32))
    o_t = _finish(acc_ref[...], MLA_V)
    o_ref[0, q_rows, :] = jnp.concatenate([o_t[:, hd * BLK:(hd + 1) * BLK] for hd in range(n_heads)],
                                          axis=0).T.astype(BF16)


def _mla(qb, kb, vbt):
    b, s, width = qb.shape
    n_heads = width // LANES
    out_shape = jax.ShapeDtypeStruct((b, s, n_heads * MLA_V), BF16)
    return pl.pallas_call(
        _per_query_tile(functools.partial(_mla_tile, n_heads=n_heads), s // BLK),
        grid=(b,),
        in_specs=[_seq_spec(qb.shape), _seq_spec(kb.shape), _seq_spec(vbt.shape)],
        out_specs=_seq_spec(out_shape.shape),
        out_shape=out_shape,
        scratch_shapes=[pltpu.VMEM((VT_ROWS, n_heads * BLK), F32), pltpu.VMEM((2, BLK, n_heads * BLK), F32)],
        compiler_params=_cparams(1),
        name="mla",
    )(qb, kb, vbt)


def _store_vt_ones(vt_ref, j, v, n_heads):
    vt = v.T.astype(BF16)
    ones = jnp.ones((ONES_ROWS, v.shape[0]), BF16)
    for hd in range(n_heads):
        vt_ref[0, j, hd * VT_ROWS:hd * VT_ROWS + HEAD_DIM, :] = vt[hd * HEAD_DIM:(hd + 1) * HEAD_DIM]
        vt_ref[0, j, hd * VT_ROWS + HEAD_DIM:(hd + 1) * VT_ROWS, :] = ones


def _proj1_body(x_ref, pre_ref, w_ref, qc_ref, kc_ref, vc_ref, ks_ref, vst_ref, kw_ref, vwt_ref, g_ref,
                qd_ref, kd_ref, vdt_ref, km_ref, *, tm):
    h = _rms(x_ref[...], pre_ref[...]).astype(BF16)
    y = _dot(h, w_ref[...])
    qc, kc, vc, ks, vs, kw, vw, gl, qd, kd, vd = (y[:, sl] for sl in _col_slices(P1_COLS))
    scale = LOG2E * HEAD_DIM ** -0.5
    qc_ref[...] = (qc * scale).astype(BF16)
    for hkv in range(C_KV_HEADS):
        kc_ref[0, hkv] = kc[:, hkv * HEAD_DIM:(hkv + 1) * HEAD_DIM].astype(BF16)
        vc_ref[0, hkv] = vc[:, hkv * HEAD_DIM:(hkv + 1) * HEAD_DIM].astype(BF16)
    ks_ref[...] = ks.astype(BF16)
    kw_ref[...] = kw.astype(BF16)
    g_ref[...] = 1.0 / (1.0 + jnp.exp(-gl))
    qd_ref[...] = (qd * scale).astype(BF16)
    kd_ref[...] = kd.astype(BF16)
    for j in range(tm // BLK):
        rows = slice(j * BLK, (j + 1) * BLK)
        km_ref[j] = jnp.mean(kd[rows], axis=0, keepdims=True)
        _store_vt_ones(vdt_ref, j, vd[rows], D_HEADS)
        _store_vt_ones(vst_ref, j, vs[rows], C_KV_HEADS)
        _store_vt_ones(vwt_ref, j, vw[rows], C_KV_HEADS)


def _proj1(x2, batch, seq, pre, w_in):
    m, d = x2.shape
    tm = TOKEN_TILE
    n_st = seq // tm
    n_pre = sum(P1_COLS[:7]) + N_GATES
    pad = jnp.zeros((d, LANES - N_GATES), F32)
    w1 = jnp.concatenate([w_in[:, :n_pre], pad, w_in[:, n_pre:]], axis=1).astype(BF16)
    q_w, kv_w, d_w = C_HEADS * HEAD_DIM, C_KV_HEADS * HEAD_DIM, D_HEADS * HEAD_DIM
    tok = lambda n: pl.BlockSpec((tm, n), lambda i: (i, 0))
    tok_shape = lambda n, dt=BF16: jax.ShapeDtypeStruct((m, n), dt)
    hm = pl.BlockSpec((1, C_KV_HEADS, tm, HEAD_DIM), lambda i: (i // n_st, 0, i % n_st, 0))
    hm_shape = jax.ShapeDtypeStruct((batch, C_KV_HEADS, seq, HEAD_DIM), BF16)
    nblk = tm // BLK
    vt_spec = lambda heads: pl.BlockSpec((1, nblk, heads * VT_ROWS, BLK), lambda i: (i // n_st, i % n_st, 0, 0))
    vt_shape = lambda heads: jax.ShapeDtypeStruct((batch, seq // BLK, heads * VT_ROWS, BLK), BF16)
    return pl.pallas_call(
        functools.partial(_proj1_body, tm=tm),
        grid=(m // tm,),
        in_specs=[tok(d), _const_spec((1, d)), _const_spec(w1.shape)],
        out_specs=[tok(q_w), hm, hm, tok(kv_w), vt_spec(C_KV_HEADS), tok(kv_w), vt_spec(C_KV_HEADS), tok(LANES),
                   tok(d_w), tok(d_w), vt_spec(D_HEADS),
                   pl.BlockSpec((nblk, 1, d_w), lambda i: (i, 0, 0))],
        out_shape=[
            tok_shape(q_w), hm_shape, hm_shape,
            tok_shape(kv_w), vt_shape(C_KV_HEADS), tok_shape(kv_w), vt_shape(C_KV_HEADS), tok_shape(LANES, F32),
            tok_shape(d_w), tok_shape(d_w), vt_shape(D_HEADS),
            jax.ShapeDtypeStruct((m // BLK, 1, d_w), F32),
        ],
        compiler_params=_cparams(1),
        name="proj1",
    )(x2, pre.reshape(1, d), w1)


def _compress_body(kx_ref, vx_ref, pek_ref, pev_ref, wk1_ref, wk2_ref, wv1_ref, wv2_ref, ko_ref, vo_ref):
    half = CMP_STRIDE * HEAD_DIM
    for x_ref, pe_ref, w1_ref, w2_ref, o_ref in ((kx_ref, pek_ref, wk1_ref, wk2_ref, ko_ref),
                                                 (vx_ref, pev_ref, wv1_ref, wv2_ref, vo_ref)):
        pe8 = jnp.broadcast_to(pe_ref[...], (SUBLANES, 2 * half)).astype(BF16)
        pe_term = _dot(pe8, w1_ref[...])[0:1]
        outs = []
        for hkv in range(C_KV_HEADS):
            x = x_ref[0, hkv]
            lo = _dot(x, w1_ref[0:half])
            hi = _dot(x, w1_ref[half:2 * half])
            hid = lo + pltpu.roll(hi, x.shape[0] - 1, 0) + pe_term
            outs.append(_dot(_silu(hid).astype(BF16), w2_ref[...]))
        out = jnp.concatenate(outs, axis=1)
        o_ref[0] = (out.T if o_ref is vo_ref else out).astype(BF16)


def _compress(kc_hm, vc_hm, pe_k, pe_v, wk1, wk2, wv1, wv2):
    b, _, s, dh = kc_hm.shape
    rows = s // CMP_STRIDE
    feat = CMP_STRIDE * dh
    kx = kc_hm.reshape(b, 2, rows, feat)
    vx = vc_hm.reshape(b, 2, rows, feat)
    xspec = pl.BlockSpec((1, 2, rows, feat), lambda bi: (bi, 0, 0, 0))
    ospec = pl.BlockSpec((1, rows, 2 * dh), lambda bi: (bi, 0, 0))
    w1s, w2s = wk1.shape, wk2.shape
    return pl.pallas_call(
        _compress_body,
        grid=(b,),
        in_specs=[xspec, xspec, _const_spec((1, 2 * feat)), _const_spec((1, 2 * feat)),
                  _const_spec(w1s), _const_spec(w2s), _const_spec(w1s), _const_spec(w2s)],
        out_specs=[ospec, ospec],
        out_shape=[jax.ShapeDtypeStruct((b, rows, 2 * dh), BF16)] * 2,
        compiler_params=_cparams(1),
        name="nsa_compress",
    )(kx, vx, pe_k.astype(F32).reshape(1, -1), pe_v.astype(F32).reshape(1, -1),
      wk1.astype(BF16), wk2.astype(BF16), wv1.astype(BF16), wv2.astype(BF16))


def _bias_block(bias_ref, hd, dist_blocks):
    sub = BLK // TILE
    rows = []
    for a in range(sub):
        cols = []
        for bq in range(sub):
            e = sub * dist_blocks + bq - a
            e = (min(max(e, 0), BIAS_TILES - 1) if isinstance(e, int)
                 else jnp.minimum(jnp.maximum(e, 0), BIAS_TILES - 1))
            cols.append(bias_ref[hd, e])
        rows.append(jnp.concatenate(cols, axis=1))
    return jnp.concatenate(rows, axis=0)


def _rank_rows(val, ridx, n_real):
    rank = jnp.zeros(val.shape, F32)
    for j in range(n_real):
        vj = val[j:j + 1, :]
        rank = rank + jnp.where(vj > val, 1.0, jnp.where(vj == val, jnp.where(ridx > j, 1.0, 0.0), 0.0))
    return rank


def _nsa_tile(qi, q_ref, kc_ref, vct_ref, ks_ref, vst_ref, kw_ref, vwt_ref, g_ref, bias_ref, ovt_ref, o_ref,
              pen_ref, acc_ref, s_ref, *, n_sel):
    q_rows = _tile_rows(qi)
    grp = C_HEADS // C_KV_HEADS
    lanes = grp * BLK
    sub = BLK // SEL_BLOCK
    own = pl.multiple_of(qi * BLK, BLK)
    prev = pl.multiple_of(jnp.maximum(qi - 1, 0) * BLK, BLK)
    prev_blk = jnp.maximum(qi - 1, 0)
    krow = lax.broadcasted_iota(jnp.int32, (BLK, lanes), 0)
    qcol = lax.broadcasted_iota(jnp.int32, (BLK, lanes), 1) & (BLK - 1)
    causal = krow <= qcol
    crow = lax.broadcasted_iota(jnp.int32, (TILE, lanes), 0)
    cq = lax.broadcasted_iota(jnp.int32, (TILE, lanes), 1) & (BLK - 1)
    visible = CMP_STRIDE * crow + (CMP_BLOCK - 1) <= qi * BLK + cq
    jrow = lax.broadcasted_iota(jnp.int32, (n_sel, BLK), 0)
    cur = qi * sub + lax.broadcasted_iota(jnp.int32, (n_sel, BLK), 1) // SEL_BLOCK
    forced = (jrow == 0) | (jrow == cur) | (jrow == cur - 1)
    gates_t = g_ref[0, q_rows, :].T
    has_prev = jnp.where(qi >= 1, 0, BLK)

    def stack_q(hkv):
        return jnp.concatenate(
            [q_ref[0, q_rows, (hkv * grp + g) * HEAD_DIM:(hkv * grp + g + 1) * HEAD_DIM] for g in range(grp)], axis=0)

    def group_bias(hkv, dist_blocks):
        return jnp.concatenate([_bias_block(bias_ref, hkv * grp + g, dist_blocks) for g in range(grp)], axis=1)

    def sel_scores(hkv, n, dist_blocks):
        st = pl.multiple_of(n * BLK, BLK)
        sl = slice(hkv * HEAD_DIM, (hkv + 1) * HEAD_DIM)
        return _dot_nt(ks_ref[0, pl.ds(st, BLK), sl], stack_q(hkv)) + group_bias(hkv, dist_blocks)

    def sel_pens(hkv, n):
        pens = pen_ref[hkv, n, 0:sub, :]
        return [jnp.concatenate([pens[j:j + 1]] * grp, axis=1) for j in range(sub)]

    def sel_max(s, pens):
        mj = [jnp.max(s[j * SEL_BLOCK:(j + 1) * SEL_BLOCK], axis=0, keepdims=True) + pens[j] for j in range(sub)]
        return functools.reduce(jnp.maximum, mj)

    def sel_probs(s, pens, m_new):
        return jnp.concatenate([jnp.exp2(s[j * SEL_BLOCK:(j + 1) * SEL_BLOCK] - (m_new - pens[j]))
                                for j in range(sub)], axis=0).astype(BF16)

    o_cmp, o_win = [], []
    for hkv in range(C_KV_HEADS):
        sl = slice(hkv * HEAD_DIM, (hkv + 1) * HEAD_DIM)
        vsl = slice(hkv * VT_ROWS, (hkv + 1) * VT_ROWS)
        qs = stack_q(hkv)

        sc = jnp.where(visible, _dot_nt(kc_ref[0, :, sl], qs), NEG_INF)
        mc = jnp.max(sc, axis=0, keepdims=True)
        ec = jnp.where(visible, jnp.exp2(sc - mc), 0.0)
        lc = jnp.sum(ec, axis=0, keepdims=True)
        p = ec / jnp.where(lc > 0.0, lc, 1.0)
        o_cmp.append(_dot(vct_ref[0, sl, :], p.astype(BF16)))

        ps = p[:, 0:BLK] + p[:, BLK:2 * BLK] + p[:, 2 * BLK:3 * BLK] + p[:, 3 * BLK:4 * BLK]
        ps_hi = ps.astype(BF16)
        ps_lo = (ps - ps_hi.astype(F32)).astype(BF16)
        imp = _dot(ovt_ref[...], ps_hi) + _dot(ovt_ref[...], ps_lo)
        val = jnp.where(forced, FORCE_SCORE, jnp.where(jrow <= cur, imp, NEG_INF))
        pen = jnp.where(_rank_rows(val, jrow, n_sel) < SEL_TOPK, 0.0, NEG_INF)
        for t in range(n_sel // sub):
            pen_ref[hkv, t, 0:sub, :] = pen[t * sub:(t + 1) * sub]

        s0 = _dot_nt(kw_ref[0, pl.ds(own, BLK), sl], qs) + group_bias(hkv, 0)
        s1 = _dot_nt(kw_ref[0, pl.ds(prev, BLK), sl], qs) + group_bias(hkv, 1)
        s0 = jnp.where(causal, s0, NEG_INF)
        s1 = jnp.where(krow > qcol + has_prev, s1, NEG_INF)
        mw = jnp.maximum(jnp.max(s0, axis=0, keepdims=True), jnp.max(s1, axis=0, keepdims=True))
        ow = (_dot(vwt_ref[0, qi, vsl, :], jnp.exp2(s0 - mw).astype(BF16))
              + _dot(vwt_ref[0, prev_blk, vsl, :], jnp.exp2(s1 - mw).astype(BF16)))
        o_win.append(_finish(ow, HEAD_DIM))

    def produce_first(slot):
        stats = []
        for hkv in range(C_KV_HEADS):
            s = jnp.where(causal, sel_scores(hkv, qi, 0), NEG_INF)
            s_ref[slot, hkv] = s
            stats.append(sel_max(s, sel_pens(hkv, qi)))
        return tuple(stats)

    def produce(i, slot):
        stats = []
        for hkv in range(C_KV_HEADS):
            s = sel_scores(hkv, i - 1, qi - (i - 1))
            s_ref[slot, hkv] = s
            stats.append(sel_max(s, sel_pens(hkv, i - 1)))
        return tuple(stats)

    def consume(i, slot, block_max, ms):
        n = jnp.where(i == 0, qi, i - 1)
        new = []
        for hkv in range(C_KV_HEADS):
            vsl = slice(hkv * VT_ROWS, (hkv + 1) * VT_ROWS)
            m_new = jnp.maximum(ms[hkv], block_max[hkv])
            pb = sel_probs(s_ref[slot, hkv], sel_pens(hkv, n), m_new)
            acc_ref[hkv] = jnp.exp2(ms[hkv] - m_new) * acc_ref[hkv] + _dot(vst_ref[0, n, vsl, :], pb)
            new.append(m_new)
        return tuple(new)

    acc_ref[...] = jnp.zeros_like(acc_ref)
    m_init = jnp.full((1, lanes), NEG_INF, F32)
    _pipelined_blocks(qi + 1, produce_first, produce, consume, (m_init, m_init))
    outs = []
    for hkv in range(C_KV_HEADS):
        o_slc = _finish(acc_ref[hkv], HEAD_DIM)
        for g in range(grp):
            hd = hkv * grp + g
            ls = slice(g * BLK, (g + 1) * BLK)
            g_cmp, g_slc, g_win = (gates_t[t * C_HEADS + hd:t * C_HEADS + hd + 1] for t in range(3))
            outs.append(g_cmp * o_cmp[hkv][:, ls] + g_slc * o_slc[:, ls] + g_win * o_win[hkv][:, ls])
    o_ref[0, q_rows, :] = jnp.concatenate(outs, axis=0).T.astype(BF16)


def _nsa(qc, kcmp, vcmpt, ks, vst, kw, vwt, gates, bias_c):
    b, s, width = qc.shape
    n_blocks = s // BLK
    n_sel = s // SEL_BLOCK
    n_c = (s - CMP_BLOCK) // CMP_STRIDE + 1
    j_start = np.arange(n_sel)[:, None] * SEL_BLOCK
    c_start = np.arange(TILE)[None, :] * CMP_STRIDE
    overlap_t = ((c_start < j_start + SEL_BLOCK) & (c_start + CMP_BLOCK > j_start)
                 & (np.arange(TILE)[None, :] < n_c))
    ovt = jnp.asarray(overlap_t, BF16)
    group_lanes = C_HEADS // C_KV_HEADS * BLK
    operands = (qc, kcmp, vcmpt, ks, vst, kw, vwt, gates)
    return pl.pallas_call(
        _per_query_tile(functools.partial(_nsa_tile, n_sel=n_sel), n_blocks),
        grid=(b,),
        in_specs=[_seq_spec(t.shape) for t in operands] + [_const_spec(bias_c.shape), _const_spec(ovt.shape)],
        out_specs=_seq_spec(qc.shape),
        out_shape=jax.ShapeDtypeStruct((b, s, width), BF16),
        scratch_shapes=[pltpu.VMEM((C_KV_HEADS, n_blocks, SUBLANES, BLK), F32),
                        pltpu.VMEM((C_KV_HEADS, VT_ROWS, group_lanes), F32),
                        pltpu.VMEM((2, C_KV_HEADS, BLK, group_lanes), F32)],
        compiler_params=_cparams(1),
        name="nsa",
    )(*operands, bias_c, ovt)


def _moba_tile(qi, q_ref, k_ref, vt_ref, km_ref, bias_ref, o_ref, pen_ref, acc_ref, s_ref, *, n_heads, n_blocks):
    q_rows = _tile_rows(qi)
    lanes = n_heads * BLK
    nidx = lax.broadcasted_iota(jnp.int32, (GATE_ROWS, lanes), 0)
    krow = lax.broadcasted_iota(jnp.int32, (BLK, lanes), 0)
    qcol = lax.broadcasted_iota(jnp.int32, (BLK, lanes), 1) & (BLK - 1)

    def scores(n, dist_blocks):
        st = pl.multiple_of(n * BLK, BLK)
        parts = []
        for hd in range(n_heads):
            sl = slice(hd * HEAD_DIM, (hd + 1) * HEAD_DIM)
            parts.append(_dot_nt(k_ref[0, pl.ds(st, BLK), sl], q_ref[0, q_rows, sl])
                         + _bias_block(bias_ref, hd, dist_blocks))
        return jnp.concatenate(parts, axis=1)

    def weighted_values(n, p):
        return jnp.concatenate(
            [_dot(vt_ref[0, n, hd * VT_ROWS:(hd + 1) * VT_ROWS, :], p[:, hd * BLK:(hd + 1) * BLK])
             for hd in range(n_heads)], axis=1)

    zpad = jnp.zeros((GATE_ROWS - n_blocks, HEAD_DIM), F32)
    gate = jnp.concatenate(
        [_dot_nt(jnp.concatenate([km_ref[0, :, hd * HEAD_DIM:(hd + 1) * HEAD_DIM], zpad], axis=0).astype(BF16),
                 q_ref[0, q_rows, hd * HEAD_DIM:(hd + 1) * HEAD_DIM]) for hd in range(n_heads)], axis=1)
    val = jnp.where(nidx < qi, gate, NEG_INF)
    val = jnp.where(nidx < n_blocks, val, PAD_SCORE)
    keep = ((_rank_rows(val, nidx, n_blocks) < MOBA_TOPK) & (nidx < qi)) | (nidx == qi)
    pen_ref[...] = jnp.where(keep, 0.0, NEG_INF)

    def produce_first(slot):
        s = jnp.where(krow <= qcol, scores(qi, 0), NEG_INF)
        s_ref[slot] = s
        return jnp.max(s, axis=0, keepdims=True)

    def produce(i, slot):
        s = scores(i - 1, qi - (i - 1))
        s_ref[slot] = s
        return jnp.max(s, axis=0, keepdims=True) + pen_ref[pl.ds(i - 1, 1), :]

    def consume(i, slot, block_max, m):
        n = jnp.where(i == 0, qi, i - 1)
        pen = pen_ref[pl.ds(n, 1), :]
        m_new = jnp.maximum(m, block_max)
        p = jnp.exp2(s_ref[slot] - (m_new - pen)).astype(BF16)
        acc_ref[...] = jnp.exp2(m - m_new) * acc_ref[...] + weighted_values(n, p)
        return m_new

    acc_ref[...] = jnp.zeros_like(acc_ref)
    _pipelined_blocks(qi + 1, produce_first, produce, consume, jnp.full((1, lanes), NEG_INF, F32))
    o_t = _finish(acc_ref[...], HEAD_DIM)
    o_ref[0, q_rows, :] = jnp.concatenate([o_t[:, hd * BLK:(hd + 1) * BLK] for hd in range(n_heads)],
                                          axis=0).T.astype(BF16)


def _moba(qd, kd, vdt, kmean, bias_d):
    b, s, width = qd.shape
    n_heads = width // HEAD_DIM
    n_blocks = s // BLK
    operands = (qd, kd, vdt, kmean.reshape(b, n_blocks, width))
    return pl.pallas_call(
        _per_query_tile(functools.partial(_moba_tile, n_heads=n_heads, n_blocks=n_blocks), n_blocks),
        grid=(b,),
        in_specs=[_seq_spec(t.shape) for t in operands] + [_const_spec(bias_d.shape)],
        out_specs=_seq_spec(qd.shape),
        out_shape=jax.ShapeDtypeStruct((b, s, width), BF16),
        scratch_shapes=[pltpu.VMEM((GATE_ROWS, n_heads * BLK), F32), pltpu.VMEM((VT_ROWS, n_heads * BLK), F32),
                        pltpu.VMEM((2, BLK, n_heads * BLK), F32)],
        compiler_params=_cparams(1),
        name="moba",
    )(*operands, bias_d)


def kernel(x, rel_bias_table, l0_ffn1_pre, l0_ffn1_post, l0_ffn1_wg, l0_ffn1_wu, l0_ffn1_wd, l0_mix_pre, l0_mix_post, l0_w_in, l0_sinks, l0_mla_q_norm, l0_mla_w_uq, l0_mla_kv_norm, l0_mla_w_ukv, l0_w_out, l0_ffn2_pre, l0_ffn2_post, l0_ffn2_wg, l0_ffn2_wu, l0_ffn2_wd, l1_ffn1_pre, l1_ffn1_post, l1_ffn1_wg, l1_ffn1_wu, l1_ffn1_wd, l1_mix_pre, l1_mix_post, l1_w_in, l1_nsa_pe_k, l1_nsa_pe_v, l1_nsa_wk1, l1_nsa_wk2, l1_nsa_wv1, l1_nsa_wv2, l1_w_out, l1_ffn2_pre, l1_ffn2_post, l1_ffn2_wg, l1_ffn2_wu, l1_ffn2_wd):
    b, s, d = x.shape
    m = b * s
    x2 = x.reshape(m, d)
    tiles = _bias_tiles(rel_bias_table)
    bias_a = tiles[A_SLOT:A_SLOT + 8, 0:2]
    bias_c = tiles[C_SLOT:C_SLOT + 8]
    bias_d = tiles[D_SLOT:D_SLOT + 8]

    x2 = _ffn(x2, l0_ffn1_pre, l0_ffn1_post, l0_ffn1_wg, l0_ffn1_wu, l0_ffn1_wd, 0.5)
    qa, ka, vat, qb, kb, vbt = _proj0(x2, s, l0_mix_pre, l0_w_in, l0_mla_q_norm, l0_mla_w_uq,
                                      l0_mla_kv_norm, l0_mla_w_ukv)
    shp = lambda t: t.reshape(b, s, t.shape[-1])
    o_a = _swa(shp(qa), shp(ka), vat, bias_a, l0_sinks)
    o_b = _mla(shp(qb), shp(kb), vbt)
    x2 = _outproj_ffn(x2, o_a.reshape(m, -1), o_b.reshape(m, -1), l0_w_out, l0_mix_post,
                      l0_ffn2_pre, l0_ffn2_post, l0_ffn2_wg, l0_ffn2_wu, l0_ffn2_wd, 0.5)

    x2 = _ffn(x2, l1_ffn1_pre, l1_ffn1_post, l1_ffn1_wg, l1_ffn1_wu, l1_ffn1_wd, 0.5)
    qc, kc_hm, vc_hm, ks, vst, kw, vwt, gates, qd, kd, vdt, kmean = _proj1(x2, b, s, l1_mix_pre, l1_w_in)
    kcmp, vcmpt = _compress(kc_hm, vc_hm, l1_nsa_pe_k, l1_nsa_pe_v, l1_nsa_wk1, l1_nsa_wk2, l1_nsa_wv1, l1_nsa_wv2)
    o_c = _nsa(shp(qc), kcmp, vcmpt, shp(ks), vst, shp(kw), vwt, shp(gates), bias_c)
    o_d = _moba(shp(qd), shp(kd), vdt, kmean, bias_d)
    x2 = _outproj_ffn(x2, o_c.reshape(m, -1), o_d.reshape(m, -1), l1_w_out, l1_mix_post,
                      l1_ffn2_pre, l1_ffn2_post, l1_ffn2_wg, l1_ffn2_wu, l1_ffn2_wd, 0.5)
    return x2.reshape(b, s, d)
```

```python
import functools
import math

import numpy as np
import jax
import jax.numpy as jnp
from jax import lax
from jax.experimental import pallas as pl
from jax.experimental.pallas import tpu as pltpu

F32 = jnp.float32
BF16 = jnp.bfloat16

HEAD_DIM = 64
MACARON_WEIGHT = 0.5
NORM_EPS = 1e-6
LOG2E = math.log2(math.e)
NEG_INF = -1e30
PAD_SCORE = -3e38
FORCE_SCORE = 1e9
REL_BUCKETS = 32
REL_MAX_DIST = 1024
REL_SLOTS = 24
A_SLOT, C_SLOT, D_SLOT = 0, 8, 16
A_HEADS, A_KV_HEADS = 8, 2
MLA_NOPE, MLA_ROPE, MLA_V = 64, 32, 64
MLA_HEADS, MLA_Q_RANK, MLA_KV_RANK = 8, 256, 128
ROPE_THETA = 10000.0
C_HEADS, C_KV_HEADS = 8, 2
CMP_BLOCK, CMP_STRIDE = 32, 16
SEL_BLOCK, SEL_TOPK = 64, 8
D_HEADS = 8
MOBA_TOPK = 3

LANES = 128
SUBLANES = 8
TILE = 128
P0_COLS = (A_HEADS * HEAD_DIM, A_KV_HEADS * HEAD_DIM, A_KV_HEADS * HEAD_DIM, MLA_Q_RANK, MLA_KV_RANK, LANES)
P1_COLS = ((C_HEADS * HEAD_DIM,) + (C_KV_HEADS * HEAD_DIM,) * 6 + (LANES,) + (D_HEADS * HEAD_DIM,) * 3)
N_GATES = 3 * C_HEADS
BLK = 256
ONES_ROWS = 16
GATE_ROWS = 16
VT_ROWS = HEAD_DIM + ONES_ROWS
BIAS_TILES = 9
TOKEN_TILE = 512
FF_CHUNK = 256
VMEM_LIMIT = 56 * 1024 * 1024


def _cparams(n_axes):
    return pltpu.CompilerParams(dimension_semantics=("arbitrary",) * n_axes, vmem_limit_bytes=VMEM_LIMIT)


def _dot(a, b):
    return jnp.dot(a, b, preferred_element_type=F32)


def _dot_nt(a, b):
    return lax.dot_general(a, b, (((1,), (1,)), ((), ())), preferred_element_type=F32)


def _rms(x, g):
    return x * lax.rsqrt(jnp.mean(x * x, axis=-1, keepdims=True) + NORM_EPS) * g


def _silu(x):
    return x / (1.0 + jnp.exp(-x))


def _col_slices(sizes):
    edges = np.cumsum((0,) + tuple(sizes)).tolist()
    return [slice(a, b) for a, b in zip(edges[:-1], edges[1:])]


def _const_spec(shape):
    nd = len(shape)
    return pl.BlockSpec(shape, lambda *_: (0,) * nd)


def _pipelined_blocks(length, produce_first, produce, consume, state):
    trips = (length - 1) // 2

    def trip(t, carry):
        state, stat_a = carry
        a = 2 * t
        stat_b = produce(a + 1, 1)
        state = consume(a, 0, stat_a, state)
        stat_next = produce(a + 2, 0)
        state = consume(a + 1, 1, stat_b, state)
        return state, stat_next

    carry = lax.fori_loop(0, trips, trip, (state, produce_first(0)))
    a = 2 * trips

    def tail_one(c):
        return consume(a, 0, c[1], c[0])

    def tail_two(c):
        stat_b = produce(a + 1, 1)
        return consume(a + 1, 1, stat_b, consume(a, 0, c[1], c[0]))

    return lax.cond(length - a == 1, tail_one, tail_two, carry)


def _per_query_tile(tile_fn, n_tiles):
    def body(*refs):
        def step(qi, carry):
            tile_fn(qi, *refs)
            return carry

        lax.fori_loop(0, n_tiles, step, 0)

    return body


def _tile_rows(qi):
    return pl.ds(pl.multiple_of(qi * BLK, BLK), BLK)


def _resident_spec(shape):
    nd = len(shape)
    return pl.BlockSpec(shape, lambda *_: (0,) * nd, pipeline_mode=pl.Buffered(1))


def _ffn_tile(x, pre_ref, post_ref, wg_ref, wu_ref, wd_ref, acc_ref, weight):
    h = _rms(x, pre_ref[...]).astype(BF16)
    for c in range(wg_ref.shape[1] // FF_CHUNK):
        cols = slice(c * FF_CHUNK, (c + 1) * FF_CHUNK)
        g = _dot(h, wg_ref[:, cols].astype(BF16))
        u = _dot(h, wu_ref[:, cols].astype(BF16))
        a = (_silu(g) * u).astype(BF16)
        y = _dot(a, wd_ref[cols, :].astype(BF16))
        if c == 0:
            acc_ref[...] = y
        else:
            acc_ref[...] += y
    return x + weight * _rms(acc_ref[...], post_ref[...])


def _ffn_body(x_ref, pre_ref, post_ref, wg_ref, wu_ref, wd_ref, o_ref, acc_ref, *, weight):
    o_ref[...] = _ffn_tile(x_ref[...], pre_ref, post_ref, wg_ref, wu_ref, wd_ref, acc_ref, weight)


def _ffn(x2, pre, post, wg, wu, wd, weight):
    m, d = x2.shape
    d_ff = wg.shape[1]
    tm = TOKEN_TILE
    return pl.pallas_call(
        functools.partial(_ffn_body, weight=weight),
        grid=(m // tm,),
        in_specs=[
            pl.BlockSpec((tm, d), lambda i: (i, 0)),
            _const_spec((1, d)), _const_spec((1, d)),
            _resident_spec((d, d_ff)), _resident_spec((d, d_ff)), _resident_spec((d_ff, d)),
        ],
        out_specs=pl.BlockSpec((tm, d), lambda i: (i, 0)),
        out_shape=jax.ShapeDtypeStruct((m, d), F32),
        scratch_shapes=[pltpu.VMEM((tm, d), F32)],
        compiler_params=_cparams(1),
        name="ffn",
    )(x2, pre.reshape(1, d), post.reshape(1, d), wg, wu, wd)


def _outproj_ffn_body(x_ref, o1_ref, o2_ref, wo_ref, mpost_ref, pre_ref, post_ref, wg_ref, wu_ref, wd_ref,
                      o_ref, acc_ref, *, weight):
    n1 = o1_ref.shape[1]
    y = (_dot(o1_ref[...], wo_ref[0:n1, :].astype(BF16))
         + _dot(o2_ref[...], wo_ref[n1:, :].astype(BF16)))
    x = x_ref[...] + _rms(y, mpost_ref[...])
    o_ref[...] = _ffn_tile(x, pre_ref, post_ref, wg_ref, wu_ref, wd_ref, acc_ref, weight)


def _outproj_ffn(x2, o1, o2, w_out, mix_post, pre, post, wg, wu, wd, weight):
    m, d = x2.shape
    d_ff = wg.shape[1]
    tm = TOKEN_TILE
    tok = lambda n: pl.BlockSpec((tm, n), lambda i: (i, 0))
    return pl.pallas_call(
        functools.partial(_outproj_ffn_body, weight=weight),
        grid=(m // tm,),
        in_specs=[
            tok(d), tok(o1.shape[1]), tok(o2.shape[1]), _resident_spec(w_out.shape), _const_spec((1, d)),
            _const_spec((1, d)), _const_spec((1, d)),
            _resident_spec((d, d_ff)), _resident_spec((d, d_ff)), _resident_spec((d_ff, d)),
        ],
        out_specs=tok(d),
        out_shape=jax.ShapeDtypeStruct((m, d), F32),
        scratch_shapes=[pltpu.VMEM((tm, d), F32)],
        compiler_params=_cparams(1),
        name="outproj_ffn",
    )(x2, o1, o2, w_out, mix_post.reshape(1, d), pre.reshape(1, d), post.reshape(1, d), wg, wu, wd)


def _t5_bucket(dist):
    n = jnp.maximum(dist, 0)
    exact = REL_BUCKETS // 2
    nf = jnp.maximum(n, 1).astype(jnp.float32)
    large = exact + (jnp.log(nf / exact) / math.log(REL_MAX_DIST / exact) * (REL_BUCKETS - exact)).astype(jnp.int32)
    return jnp.where(n < exact, n, jnp.minimum(large, REL_BUCKETS - 1))


def _bias_body(tab_ref, idx_ref, *o_refs, first_slots):
    head = pl.program_id(0)
    for o_ref, first in zip(o_refs, first_slots):
        column = jnp.broadcast_to(LOG2E * tab_ref[pl.ds(first + head, 1), :], (TILE, LANES))
        for d in range(o_ref.shape[1]):
            o_ref[0, d] = jnp.take_along_axis(column, idx_ref[d], axis=1)


def _bias_tiles(rel_table, groups):
    heads = 8
    d = jnp.arange(BIAS_TILES)[:, None, None]
    key = jnp.arange(TILE)[None, :, None]
    query = jnp.arange(TILE)[None, None, :]
    idx = _t5_bucket(d * TILE + query - key).astype(jnp.int32)
    n_buckets, n_slots = rel_table.shape
    table_t = jnp.pad(rel_table.astype(F32).T, ((0, 0), (0, LANES - n_buckets)))
    return pl.pallas_call(
        functools.partial(_bias_body, first_slots=tuple(first for first, _ in groups)),
        grid=(heads,),
        in_specs=[_const_spec((n_slots, LANES)), _const_spec((BIAS_TILES, TILE, TILE))],
        out_specs=[pl.BlockSpec((1, n, TILE, TILE), lambda s: (s, 0, 0, 0)) for _, n in groups],
        out_shape=[jax.ShapeDtypeStruct((heads, n, TILE, TILE), F32) for _, n in groups],
        compiler_params=_cparams(1),
        name="bias_tiles",
    )(table_t, idx)


def _proj0_body(x_ref, pre_ref, w_ref, qn_ref, wq1_ref, wq2_ref, kvn_ref, wk_ref, wv_ref, cos_ref, sin_ref,
                qa_ref, ka_ref, vat_ref, qb_ref, kb_ref, vbt_ref):
    h = _rms(x_ref[...], pre_ref[...]).astype(BF16)
    y = _dot(h, w_ref[...])
    qa, ka, va, cq, ckv, rot = (y[:, sl] for sl in _col_slices(P0_COLS))
    qa_ref[...] = (qa * (LOG2E * HEAD_DIM ** -0.5)).astype(BF16)
    ka_ref[...] = ka.astype(BF16)
    for j in range(y.shape[0] // TILE):
        _store_vt_ones(vat_ref, j, va[j * TILE:(j + 1) * TILE], A_KV_HEADS)
    cq = _rms(cq, qn_ref[...]).astype(BF16)
    ckv = _rms(ckv, kvn_ref[...]).astype(BF16)
    cos_t = cos_ref[...]
    sin_t = sin_ref[...]
    lane = lax.broadcasted_iota(jnp.int32, cos_t.shape, 1)
    scale = LOG2E * (MLA_NOPE + MLA_ROPE) ** -0.5
    qmul = scale * (cos_t + jnp.where(lane < MLA_NOPE, 1.0, 0.0))
    qsin = scale * sin_t
    kr = rot * cos_t + pltpu.roll(rot, LANES // 2, 1) * sin_t
    q1 = _dot(cq, wq1_ref[...])
    q2 = _dot(cq, wq2_ref[...])
    k1 = _dot(ckv, wk_ref[...])
    for hd in range(MLA_HEADS):
        sl = slice(hd * LANES, (hd + 1) * LANES)
        qb_ref[:, sl] = (q1[:, sl] * qmul + q2[:, sl] * qsin).astype(BF16)
        kb_ref[:, sl] = (k1[:, sl] + kr).astype(BF16)
    vb = _dot(ckv, wv_ref[...])
    for j in range(vb.shape[0] // BLK):
        _store_vt_ones(vbt_ref, j, vb[j * BLK:(j + 1) * BLK], MLA_HEADS)


def _rope_lane_tables(s):
    inv = ROPE_THETA ** (-jnp.arange(0, MLA_ROPE, 2, dtype=jnp.float32) / MLA_ROPE)
    ang = jnp.arange(s, dtype=jnp.float32)[:, None] * inv[None, :]
    cos, sin = jnp.cos(ang), jnp.sin(ang)
    z_lo = jnp.zeros((s, MLA_NOPE), F32)
    z_hi = jnp.zeros((s, LANES - MLA_NOPE - MLA_ROPE), F32)
    return (jnp.concatenate([z_lo, cos, cos, z_hi], axis=1), jnp.concatenate([z_lo, sin, sin, z_hi], axis=1))


def _rot_cols(w):
    half = w.shape[-1] // 2
    return jnp.concatenate([-w[..., half:], w[..., :half]], axis=-1)


def _proj0(x2, seq, pre, w_in, q_norm, w_uq, kv_norm, w_ukv):
    m, d = x2.shape
    tm = TOKEN_TILE
    n_lin = sum(P0_COLS[:-1])
    kr_w = w_in[:, n_lin:n_lin + MLA_ROPE]
    z32 = jnp.zeros((d, MLA_ROPE), F32)
    w0 = jnp.concatenate([w_in[:, :n_lin], _rot_cols(kr_w), z32, kr_w, z32], axis=1).astype(BF16)
    wq = w_uq.reshape(-1, MLA_HEADS, MLA_NOPE + MLA_ROPE)
    rq = wq.shape[0]
    pad = LANES - MLA_NOPE - MLA_ROPE
    zq = jnp.zeros((rq, MLA_HEADS, pad), F32)
    wq1 = jnp.concatenate([wq, zq], axis=2).reshape(rq, -1).astype(BF16)
    wq2 = jnp.concatenate([jnp.zeros((rq, MLA_HEADS, MLA_NOPE), F32), _rot_cols(wq[:, :, MLA_NOPE:]), zq],
                          axis=2).reshape(rq, -1).astype(BF16)
    wkv = w_ukv.reshape(-1, MLA_HEADS, MLA_NOPE + MLA_V)
    rk = wkv.shape[0]
    wk = jnp.concatenate([wkv[:, :, :MLA_NOPE], jnp.zeros((rk, MLA_HEADS, LANES - MLA_NOPE), F32)],
                         axis=2).reshape(rk, -1).astype(BF16)
    wv = wkv[:, :, MLA_NOPE:].reshape(rk, -1).astype(BF16)
    cos_t, sin_t = _rope_lane_tables(seq)
    n_st = seq // tm
    tok = lambda n: pl.BlockSpec((tm, n), lambda i: (i, 0))
    outs = pl.pallas_call(
        _proj0_body,
        grid=(m // tm,),
        in_specs=[
            tok(d), _const_spec((1, d)), _const_spec(w0.shape),
            _const_spec((1, rq)), _const_spec(wq1.shape), _const_spec(wq2.shape),
            _const_spec((1, rk)), _const_spec(wk.shape), _const_spec(wv.shape),
            pl.BlockSpec((tm, LANES), lambda i: (i % n_st, 0)),
            pl.BlockSpec((tm, LANES), lambda i: (i % n_st, 0)),
        ],
        out_specs=[tok(P0_COLS[0]), tok(P0_COLS[1]),
                   pl.BlockSpec((1, tm // TILE, A_KV_HEADS * VT_ROWS, TILE), lambda i: (i // n_st, i % n_st, 0, 0)),
                   tok(MLA_HEADS * LANES), tok(MLA_HEADS * LANES),
                   pl.BlockSpec((1, tm // BLK, MLA_HEADS * VT_ROWS, BLK), lambda i: (i // n_st, i % n_st, 0, 0))],
        out_shape=[jax.ShapeDtypeStruct((m, P0_COLS[0]), BF16), jax.ShapeDtypeStruct((m, P0_COLS[1]), BF16),
                   jax.ShapeDtypeStruct((m // seq, seq // TILE, A_KV_HEADS * VT_ROWS, TILE), BF16),
                   jax.ShapeDtypeStruct((m, MLA_HEADS * LANES), BF16),
                   jax.ShapeDtypeStruct((m, MLA_HEADS * LANES), BF16),
                   jax.ShapeDtypeStruct((m // seq, seq // BLK, MLA_HEADS * VT_ROWS, BLK), BF16)],
        compiler_params=_cparams(1),
        name="proj0",
    )(x2, pre.reshape(1, d), w0, q_norm.reshape(1, rq), wq1, wq2, kv_norm.reshape(1, rk), wk, wv, cos_t, sin_t)
    return outs


def _swa_tile(qi, sink_ref, q_ref, k_ref, vt_ref, bias_ref, o_ref, *, n_kv, grp):
    halves = BLK // TILE
    items = [(hkv, half) for hkv in range(n_kv) for half in range(halves)]
    width = grp * TILE
    krow = lax.broadcasted_iota(jnp.int32, (TILE, len(items) * width), 0)
    qcol = lax.broadcasted_iota(jnp.int32, (TILE, len(items) * width), 1) & (TILE - 1)

    cur_s, prev_s, sink_parts, has_prev = [], [], [], []
    for hkv, half in items:
        sl = slice(hkv * HEAD_DIM, (hkv + 1) * HEAD_DIM)
        t = qi * halves + half
        tp = jnp.maximum(t - 1, 0)
        q_rows = pl.ds(pl.multiple_of(t * TILE, TILE), TILE)
        qs = jnp.concatenate([q_ref[0, q_rows, (hkv * grp + g) * HEAD_DIM:(hkv * grp + g + 1) * HEAD_DIM]
                              for g in range(grp)], axis=0)
        bias_c = jnp.concatenate([bias_ref[hkv * grp + g, 0] for g in range(grp)], axis=1)
        bias_p = jnp.concatenate([bias_ref[hkv * grp + g, 1] for g in range(grp)], axis=1)
        cur_s.append(_dot_nt(k_ref[0, pl.ds(pl.multiple_of(t * TILE, TILE), TILE), sl], qs) + bias_c)
        prev_s.append(_dot_nt(k_ref[0, pl.ds(pl.multiple_of(tp * TILE, TILE), TILE), sl], qs) + bias_p)
        sink_parts.extend(jnp.full((1, TILE), LOG2E * sink_ref[hkv * grp + g], F32) for g in range(grp))
        has_prev.append(jnp.full((1, width), jnp.where(t >= 1, 0, TILE), jnp.int32))
    s_c = jnp.where(krow <= qcol, jnp.concatenate(cur_s, axis=1), NEG_INF)
    s_p = jnp.where(krow > qcol + jnp.concatenate(has_prev, axis=1),
                    jnp.concatenate(prev_s, axis=1), NEG_INF)
    sink = jnp.concatenate(sink_parts, axis=1)
    mx = jnp.maximum(jnp.maximum(jnp.max(s_c, axis=0, keepdims=True), jnp.max(s_p, axis=0, keepdims=True)), sink)
    p_c = jnp.exp2(s_c - mx).astype(BF16)
    p_p = jnp.exp2(s_p - mx).astype(BF16)
    accs = []
    for i, (hkv, half) in enumerate(items):
        vsl = slice(hkv * VT_ROWS, (hkv + 1) * VT_ROWS)
        ls = slice(i * width, (i + 1) * width)
        t = qi * halves + half
        accs.append(_dot(vt_ref[0, t, vsl, :], p_c[:, ls]) + _dot(vt_ref[0, jnp.maximum(t - 1, 0), vsl, :], p_p[:, ls]))
    acc = jnp.concatenate(accs, axis=1)
    o_t = acc[0:HEAD_DIM] / (acc[HEAD_DIM:HEAD_DIM + 1] + jnp.exp2(sink - mx))
    heads = []
    for hkv in range(n_kv):
        for g in range(grp):
            heads.append(jnp.concatenate(
                [o_t[:, (hkv * halves + half) * width + g * TILE:(hkv * halves + half) * width + (g + 1) * TILE]
                 for half in range(halves)], axis=1))
    o_ref[0, _tile_rows(qi), :] = jnp.concatenate(heads, axis=0).T.astype(BF16)


def _seq_spec(shape):
    nd = len(shape)
    return pl.BlockSpec((1,) + tuple(shape[1:]), lambda bi: (bi,) + (0,) * (nd - 1))


def _swa(qa, ka, vat, bias_a, sinks):
    b, s, width = qa.shape
    n_kv = ka.shape[-1] // HEAD_DIM
    grp = width // HEAD_DIM // n_kv
    return pl.pallas_call(
        _per_query_tile(functools.partial(_swa_tile, n_kv=n_kv, grp=grp), s // BLK),
        grid=(b,),
        in_specs=[pl.BlockSpec(memory_space=pltpu.SMEM), _seq_spec(qa.shape), _seq_spec(ka.shape),
                  _seq_spec(vat.shape), _const_spec(bias_a.shape)],
        out_specs=_seq_spec(qa.shape),
        out_shape=jax.ShapeDtypeStruct((b, s, width), BF16),
        compiler_params=_cparams(1),
        name="swa",
    )(sinks.astype(F32), qa, ka, vat, bias_a)


def _finish(acc, dv):
    return acc[0:dv] / acc[dv:dv + 1]


def _mla_tile(qi, q_ref, k_ref, vt_ref, o_ref, acc_ref, s_ref, *, n_heads):
    q_rows = _tile_rows(qi)
    lanes = n_heads * BLK
    krow = lax.broadcasted_iota(jnp.int32, (BLK, lanes), 0)
    qcol = lax.broadcasted_iota(jnp.int32, (BLK, lanes), 1) & (BLK - 1)

    def scores(n):
        st = pl.multiple_of(n * BLK, BLK)
        return jnp.concatenate(
            [_dot_nt(k_ref[0, pl.ds(st, BLK), hd * LANES:(hd + 1) * LANES], q_ref[0, q_rows, hd * LANES:(hd + 1) * LANES])
             for hd in range(n_heads)], axis=1)

    def weighted_values(n, p):
        return jnp.concatenate(
            [_dot(vt_ref[0, n, hd * VT_ROWS:(hd + 1) * VT_ROWS, :], p[:, hd * BLK:(hd + 1) * BLK])
             for hd in range(n_heads)], axis=1)

    def produce_first(slot):
        s = jnp.where(krow <= qcol, scores(qi), NEG_INF)
        s_ref[slot] = s
        return jnp.max(s, axis=0, keepdims=True)

    def produce(i, slot):
        s = scores(i - 1)
        s_ref[slot] = s
        return jnp.max(s, axis=0, keepdims=True)

    def consume(i, slot, block_max, m):
        n = jnp.where(i == 0, qi, i - 1)
        m_new = jnp.maximum(m, block_max)
        p = jnp.exp2(s_ref[slot] - m_new).astype(BF16)
        acc_ref[...] = jnp.exp2(m - m_new) * acc_ref[...] + weighted_values(n, p)
        return m_new

    acc_ref[...] = jnp.zeros_like(acc_ref)
    _pipelined_blocks(qi + 1, produce_first, produce, consume, jnp.full((1, lanes), NEG_INF, F32))
    o_t = _finish(acc_ref[...], MLA_V)
    o_ref[0, q_rows, :] = jnp.concatenate([o_t[:, hd * BLK:(hd + 1) * BLK] for hd in range(n_heads)],
                                          axis=0).T.astype(BF16)


def _mla(qb, kb, vbt):
    b, s, width = qb.shape
    n_heads = width // LANES
    out_shape = jax.ShapeDtypeStruct((b, s, n_heads * MLA_V), BF16)
    return pl.pallas_call(
        _per_query_tile(functools.partial(_mla_tile, n_heads=n_heads), s // BLK),
        grid=(b,),
        in_specs=[_seq_spec(qb.shape), _seq_spec(kb.shape), _seq_spec(vbt.shape)],
        out_specs=_seq_spec(out_shape.shape),
        out_shape=out_shape,
        scratch_shapes=[pltpu.VMEM((VT_ROWS, n_heads * BLK), F32), pltpu.VMEM((2, BLK, n_heads * BLK), F32)],
        compiler_params=_cparams(1),
        name="mla",
    )(qb, kb, vbt)


def _store_vt_ones(vt_ref, j, v, n_heads):
    vt = v.T.astype(BF16)
    ones = jnp.ones((ONES_ROWS, v.shape[0]), BF16)
    for hd in range(n_heads):
        vt_ref[0, j, hd * VT_ROWS:hd * VT_ROWS + HEAD_DIM, :] = vt[hd * HEAD_DIM:(hd + 1) * HEAD_DIM]
        vt_ref[0, j, hd * VT_ROWS + HEAD_DIM:(hd + 1) * VT_ROWS, :] = ones


def _proj1_body(x_ref, pre_ref, w_ref, qc_ref, kx_ref, vx_ref, ks_ref, vst_ref, kw_ref, vwt_ref, g_ref,
                qd_ref, kd_ref, vdt_ref, km_ref, kstage_ref, vstage_ref, *, tm):
    h = _rms(x_ref[...], pre_ref[...]).astype(BF16)
    y = _dot(h, w_ref[...])
    qc, kc, vc, ks, vs, kw, vw, gl, qd, kd, vd = (y[:, sl] for sl in _col_slices(P1_COLS))
    scale = LOG2E * HEAD_DIM ** -0.5
    qc_ref[...] = (qc * scale).astype(BF16)
    for stage_ref, cols, x_out in ((kstage_ref, kc, kx_ref), (vstage_ref, vc, vx_ref)):
        stage_ref[...] = cols
        for i in range(CMP_STRIDE):
            piece = stage_ref[pl.ds(i, tm // CMP_STRIDE, stride=CMP_STRIDE), :].astype(BF16)
            for hkv in range(C_KV_HEADS):
                x_out[0, hkv, :, i * HEAD_DIM:(i + 1) * HEAD_DIM] = piece[:, hkv * HEAD_DIM:(hkv + 1) * HEAD_DIM]
    ks_ref[...] = ks.astype(BF16)
    kw_ref[...] = kw.astype(BF16)
    g_ref[...] = 1.0 / (1.0 + jnp.exp(-gl))
    qd_ref[...] = (qd * scale).astype(BF16)
    kd_ref[...] = kd.astype(BF16)
    for j in range(tm // BLK):
        rows = slice(j * BLK, (j + 1) * BLK)
        km_ref[j] = jnp.mean(kd[rows], axis=0, keepdims=True)
        _store_vt_ones(vdt_ref, j, vd[rows], D_HEADS)
        _store_vt_ones(vst_ref, j, vs[rows], C_KV_HEADS)
        _store_vt_ones(vwt_ref, j, vw[rows], C_KV_HEADS)


def _proj1(x2, batch, seq, pre, w_in):
    m, d = x2.shape
    tm = TOKEN_TILE
    n_st = seq // tm
    n_pre = sum(P1_COLS[:7]) + N_GATES
    pad = jnp.zeros((d, LANES - N_GATES), F32)
    w1 = jnp.concatenate([w_in[:, :n_pre], pad, w_in[:, n_pre:]], axis=1).astype(BF16)
    q_w, kv_w, d_w = C_HEADS * HEAD_DIM, C_KV_HEADS * HEAD_DIM, D_HEADS * HEAD_DIM
    tok = lambda n: pl.BlockSpec((tm, n), lambda i: (i, 0))
    tok_shape = lambda n, dt=BF16: jax.ShapeDtypeStruct((m, n), dt)
    feat = CMP_STRIDE * HEAD_DIM
    hm = pl.BlockSpec((1, C_KV_HEADS, tm // CMP_STRIDE, feat), lambda i: (i // n_st, 0, i % n_st, 0))
    hm_shape = jax.ShapeDtypeStruct((batch, C_KV_HEADS, seq // CMP_STRIDE, feat), BF16)
    nblk = tm // BLK
    vt_spec = lambda heads: pl.BlockSpec((1, nblk, heads * VT_ROWS, BLK), lambda i: (i // n_st, i % n_st, 0, 0))
    vt_shape = lambda heads: jax.ShapeDtypeStruct((batch, seq // BLK, heads * VT_ROWS, BLK), BF16)
    return pl.pallas_call(
        functools.partial(_proj1_body, tm=tm),
        grid=(m // tm,),
        in_specs=[tok(d), _const_spec((1, d)), _const_spec(w1.shape)],
        out_specs=[tok(q_w), hm, hm, tok(kv_w), vt_spec(C_KV_HEADS), tok(kv_w), vt_spec(C_KV_HEADS), tok(LANES),
                   tok(d_w), tok(d_w), vt_spec(D_HEADS),
                   pl.BlockSpec((nblk, 1, d_w), lambda i: (i, 0, 0))],
        out_shape=[
            tok_shape(q_w), hm_shape, hm_shape,
            tok_shape(kv_w), vt_shape(C_KV_HEADS), tok_shape(kv_w), vt_shape(C_KV_HEADS), tok_shape(LANES, F32),
            tok_shape(d_w), tok_shape(d_w), vt_shape(D_HEADS),
            jax.ShapeDtypeStruct((m // BLK, 1, d_w), F32),
        ],
        scratch_shapes=[pltpu.VMEM((tm, kv_w), F32), pltpu.VMEM((tm, kv_w), F32)],
        compiler_params=_cparams(1),
        name="proj1",
    )(x2, pre.reshape(1, d), w1)


def _compress_body(kx_ref, vx_ref, pek_ref, pev_ref, wk1_ref, wk2_ref, wv1_ref, wv2_ref, ko_ref, vo_ref):
    half = CMP_STRIDE * HEAD_DIM
    for x_ref, pe_ref, w1_ref, w2_ref, o_ref in ((kx_ref, pek_ref, wk1_ref, wk2_ref, ko_ref),
                                                 (vx_ref, pev_ref, wv1_ref, wv2_ref, vo_ref)):
        pe8 = jnp.broadcast_to(pe_ref[...], (SUBLANES, 2 * half)).astype(BF16)
        pe_term = _dot(pe8, w1_ref[...])[0:1]
        outs = []
        for hkv in range(C_KV_HEADS):
            x = x_ref[0, hkv]
            lo = _dot(x, w1_ref[0:half])
            hi = _dot(x, w1_ref[half:2 * half])
            hid = lo + pltpu.roll(hi, x.shape[0] - 1, 0) + pe_term
            outs.append(_dot(_silu(hid).astype(BF16), w2_ref[...]))
        out = jnp.concatenate(outs, axis=1)
        o_ref[0] = (out.T if o_ref is vo_ref else out).astype(BF16)


def _compress(kx, vx, pe_k, pe_v, wk1, wk2, wv1, wv2):
    b, _, rows, feat = kx.shape
    dh = feat // CMP_STRIDE
    xspec = pl.BlockSpec((1, 2, rows, feat), lambda bi: (bi, 0, 0, 0))
    ospec = pl.BlockSpec((1, rows, 2 * dh), lambda bi: (bi, 0, 0))
    w1s, w2s = wk1.shape, wk2.shape
    return pl.pallas_call(
        _compress_body,
        grid=(b,),
        in_specs=[xspec, xspec, _const_spec((1, 2 * feat)), _const_spec((1, 2 * feat)),
                  _const_spec(w1s), _const_spec(w2s), _const_spec(w1s), _const_spec(w2s)],
        out_specs=[ospec, ospec],
        out_shape=[jax.ShapeDtypeStruct((b, rows, 2 * dh), BF16)] * 2,
        compiler_params=_cparams(1),
        name="nsa_compress",
    )(kx, vx, pe_k.astype(F32).reshape(1, -1), pe_v.astype(F32).reshape(1, -1),
      wk1.astype(BF16), wk2.astype(BF16), wv1.astype(BF16), wv2.astype(BF16))


def _bias_block(bias_ref, hd, dist_blocks):
    sub = BLK // TILE
    rows = []
    for a in range(sub):
        cols = []
        for bq in range(sub):
            e = sub * dist_blocks + bq - a
            e = (min(max(e, 0), BIAS_TILES - 1) if isinstance(e, int)
                 else jnp.minimum(jnp.maximum(e, 0), BIAS_TILES - 1))
            cols.append(bias_ref[hd, e])
        rows.append(jnp.concatenate(cols, axis=1))
    return jnp.concatenate(rows, axis=0)


def _rank_rows(val, ridx, n_real):
    rank = jnp.zeros(val.shape, F32)
    for j in range(n_real):
        vj = val[j:j + 1, :]
        rank = rank + jnp.where(vj > val, 1.0, jnp.where(vj == val, jnp.where(ridx > j, 1.0, 0.0), 0.0))
    return rank


def _nsa_tile(qi, q_ref, kc_ref, vct_ref, ks_ref, vst_ref, kw_ref, vwt_ref, g_ref, bias_ref, ovt_ref, o_ref,
              pen_ref, acc_ref, s_ref, *, n_sel):
    q_rows = _tile_rows(qi)
    grp = C_HEADS // C_KV_HEADS
    lanes = grp * BLK
    sub = BLK // SEL_BLOCK
    own = pl.multiple_of(qi * BLK, BLK)
    prev = pl.multiple_of(jnp.maximum(qi - 1, 0) * BLK, BLK)
    prev_blk = jnp.maximum(qi - 1, 0)
    krow = lax.broadcasted_iota(jnp.int32, (BLK, lanes), 0)
    qcol = lax.broadcasted_iota(jnp.int32, (BLK, lanes), 1) & (BLK - 1)
    causal = krow <= qcol
    crow = lax.broadcasted_iota(jnp.int32, (TILE, lanes), 0)
    cq = lax.broadcasted_iota(jnp.int32, (TILE, lanes), 1) & (BLK - 1)
    visible = CMP_STRIDE * crow + (CMP_BLOCK - 1) <= qi * BLK + cq
    jrow = lax.broadcasted_iota(jnp.int32, (n_sel, BLK), 0)
    cur = qi * sub + lax.broadcasted_iota(jnp.int32, (n_sel, BLK), 1) // SEL_BLOCK
    forced = (jrow == 0) | (jrow == cur) | (jrow == cur - 1)
    gates_t = g_ref[0, q_rows, :].T
    has_prev = jnp.where(qi >= 1, 0, BLK)

    def stack_q(hkv):
        return jnp.concatenate(
            [q_ref[0, q_rows, (hkv * grp + g) * HEAD_DIM:(hkv * grp + g + 1) * HEAD_DIM] for g in range(grp)], axis=0)

    def group_bias(hkv, dist_blocks):
        return jnp.concatenate([_bias_block(bias_ref, hkv * grp + g, dist_blocks) for g in range(grp)], axis=1)

    def sel_scores(hkv, n, dist_blocks):
        st = pl.multiple_of(n * BLK, BLK)
        sl = slice(hkv * HEAD_DIM, (hkv + 1) * HEAD_DIM)
        return _dot_nt(ks_ref[0, pl.ds(st, BLK), sl], stack_q(hkv)) + group_bias(hkv, dist_blocks)

    def sel_pens(hkv, n):
        pens = pen_ref[hkv, n, 0:sub, :]
        return [jnp.concatenate([pens[j:j + 1]] * grp, axis=1) for j in range(sub)]

    def sel_max(s, pens):
        mj = [jnp.max(s[j * SEL_BLOCK:(j + 1) * SEL_BLOCK], axis=0, keepdims=True) + pens[j] for j in range(sub)]
        return functools.reduce(jnp.maximum, mj)

    def sel_probs(s, pens, m_new):
        return jnp.concatenate([jnp.exp2(s[j * SEL_BLOCK:(j + 1) * SEL_BLOCK] - (m_new - pens[j]))
                                for j in range(sub)], axis=0).astype(BF16)

    o_cmp, o_win = [], []
    for hkv in range(C_KV_HEADS):
        sl = slice(hkv * HEAD_DIM, (hkv + 1) * HEAD_DIM)
        vsl = slice(hkv * VT_ROWS, (hkv + 1) * VT_ROWS)
        qs = stack_q(hkv)

        sc = jnp.where(visible, _dot_nt(kc_ref[0, :, sl], qs), NEG_INF)
        mc = jnp.max(sc, axis=0, keepdims=True)
        ec = jnp.where(visible, jnp.exp2(sc - mc), 0.0)
        lc = jnp.sum(ec, axis=0, keepdims=True)
        p = ec / jnp.where(lc > 0.0, lc, 1.0)
        o_cmp.append(_dot(vct_ref[0, sl, :], p.astype(BF16)))

        ps = p[:, 0:BLK] + p[:, BLK:2 * BLK] + p[:, 2 * BLK:3 * BLK] + p[:, 3 * BLK:4 * BLK]
        ps_hi = ps.astype(BF16)
        ps_lo = (ps - ps_hi.astype(F32)).astype(BF16)
        imp = _dot(ovt_ref[...], ps_hi) + _dot(ovt_ref[...], ps_lo)
        val = jnp.where(forced, FORCE_SCORE, jnp.where(jrow <= cur, imp, NEG_INF))
        pen = jnp.where(_rank_rows(val, jrow, n_sel) < SEL_TOPK, 0.0, NEG_INF)
        for t in range(n_sel // sub):
            pen_ref[hkv, t, 0:sub, :] = pen[t * sub:(t + 1) * sub]

        s0 = _dot_nt(kw_ref[0, pl.ds(own, BLK), sl], qs) + group_bias(hkv, 0)
        s1 = _dot_nt(kw_ref[0, pl.ds(prev, BLK), sl], qs) + group_bias(hkv, 1)
        s0 = jnp.where(causal, s0, NEG_INF)
        s1 = jnp.where(krow > qcol + has_prev, s1, NEG_INF)
        mw = jnp.maximum(jnp.max(s0, axis=0, keepdims=True), jnp.max(s1, axis=0, keepdims=True))
        ow = (_dot(vwt_ref[0, qi, vsl, :], jnp.exp2(s0 - mw).astype(BF16))
              + _dot(vwt_ref[0, prev_blk, vsl, :], jnp.exp2(s1 - mw).astype(BF16)))
        o_win.append(_finish(ow, HEAD_DIM))

    def produce_first(slot):
        stats = []
        for hkv in range(C_KV_HEADS):
            s = jnp.where(causal, sel_scores(hkv, qi, 0), NEG_INF)
            s_ref[slot, hkv] = s
            stats.append(sel_max(s, sel_pens(hkv, qi)))
        return tuple(stats)

    def produce(i, slot):
        stats = []
        for hkv in range(C_KV_HEADS):
            s = sel_scores(hkv, i - 1, qi - (i - 1))
            s_ref[slot, hkv] = s
            stats.append(sel_max(s, sel_pens(hkv, i - 1)))
        return tuple(stats)

    def consume(i, slot, block_max, ms):
        n = jnp.where(i == 0, qi, i - 1)
        new = []
        for hkv in range(C_KV_HEADS):
            vsl = slice(hkv * VT_ROWS, (hkv + 1) * VT_ROWS)
            m_new = jnp.maximum(ms[hkv], block_max[hkv])
            pb = sel_probs(s_ref[slot, hkv], sel_pens(hkv, n), m_new)
            acc_ref[hkv] = jnp.exp2(ms[hkv] - m_new) * acc_ref[hkv] + _dot(vst_ref[0, n, vsl, :], pb)
            new.append(m_new)
        return tuple(new)

    acc_ref[...] = jnp.zeros_like(acc_ref)
    m_init = jnp.full((1, lanes), NEG_INF, F32)
    _pipelined_blocks(qi + 1, produce_first, produce, consume, (m_init, m_init))
    outs = []
    for hkv in range(C_KV_HEADS):
        o_slc = _finish(acc_ref[hkv], HEAD_DIM)
        for g in range(grp):
            hd = hkv * grp + g
            ls = slice(g * BLK, (g + 1) * BLK)
            g_cmp, g_slc, g_win = (gates_t[t * C_HEADS + hd:t * C_HEADS + hd + 1] for t in range(N_GATES // C_HEADS))
            outs.append(g_cmp * o_cmp[hkv][:, ls] + g_slc * o_slc[:, ls] + g_win * o_win[hkv][:, ls])
    o_ref[0, q_rows, :] = jnp.concatenate(outs, axis=0).T.astype(BF16)


def _nsa(qc, kcmp, vcmpt, ks, vst, kw, vwt, gates, bias_c):
    b, s, width = qc.shape
    n_blocks = s // BLK
    n_sel = s // SEL_BLOCK
    n_c = (s - CMP_BLOCK) // CMP_STRIDE + 1
    j_start = np.arange(n_sel)[:, None] * SEL_BLOCK
    c_start = np.arange(TILE)[None, :] * CMP_STRIDE
    overlap_t = ((c_start < j_start + SEL_BLOCK) & (c_start + CMP_BLOCK > j_start)
                 & (np.arange(TILE)[None, :] < n_c))
    ovt = jnp.asarray(overlap_t, BF16)
    group_lanes = C_HEADS // C_KV_HEADS * BLK
    operands = (qc, kcmp, vcmpt, ks, vst, kw, vwt, gates)
    return pl.pallas_call(
        _per_query_tile(functools.partial(_nsa_tile, n_sel=n_sel), n_blocks),
        grid=(b,),
        in_specs=[_seq_spec(t.shape) for t in operands] + [_const_spec(bias_c.shape), _const_spec(ovt.shape)],
        out_specs=_seq_spec(qc.shape),
        out_shape=jax.ShapeDtypeStruct((b, s, width), BF16),
        scratch_shapes=[pltpu.VMEM((C_KV_HEADS, n_blocks, SUBLANES, BLK), F32),
                        pltpu.VMEM((C_KV_HEADS, VT_ROWS, group_lanes), F32),
                        pltpu.VMEM((2, C_KV_HEADS, BLK, group_lanes), F32)],
        compiler_params=_cparams(1),
        name="nsa",
    )(*operands, bias_c, ovt)


def _moba_tile(qi, q_ref, k_ref, vt_ref, km_ref, bias_ref, o_ref, pen_ref, acc_ref, s_ref, *, n_heads, n_blocks):
    q_rows = _tile_rows(qi)
    lanes = n_heads * BLK
    nidx = lax.broadcasted_iota(jnp.int32, (GATE_ROWS, lanes), 0)
    krow = lax.broadcasted_iota(jnp.int32, (BLK, lanes), 0)
    qcol = lax.broadcasted_iota(jnp.int32, (BLK, lanes), 1) & (BLK - 1)

    def scores(n, dist_blocks):
        st = pl.multiple_of(n * BLK, BLK)
        parts = []
        for hd in range(n_heads):
            sl = slice(hd * HEAD_DIM, (hd + 1) * HEAD_DIM)
            parts.append(_dot_nt(k_ref[0, pl.ds(st, BLK), sl], q_ref[0, q_rows, sl])
                         + _bias_block(bias_ref, hd, dist_blocks))
        return jnp.concatenate(parts, axis=1)

    def weighted_values(n, p):
        return jnp.concatenate(
            [_dot(vt_ref[0, n, hd * VT_ROWS:(hd + 1) * VT_ROWS, :], p[:, hd * BLK:(hd + 1) * BLK])
             for hd in range(n_heads)], axis=1)

    zpad = jnp.zeros((GATE_ROWS - n_blocks, HEAD_DIM), F32)
    gate = jnp.concatenate(
        [_dot_nt(jnp.concatenate([km_ref[0, :, hd * HEAD_DIM:(hd + 1) * HEAD_DIM], zpad], axis=0).astype(BF16),
                 q_ref[0, q_rows, hd * HEAD_DIM:(hd + 1) * HEAD_DIM]) for hd in range(n_heads)], axis=1)
    val = jnp.where(nidx < qi, gate, NEG_INF)
    val = jnp.where(nidx < n_blocks, val, PAD_SCORE)
    keep = ((_rank_rows(val, nidx, n_blocks) < MOBA_TOPK) & (nidx < qi)) | (nidx == qi)
    pen_ref[...] = jnp.where(keep, 0.0, NEG_INF)

    def produce_first(slot):
        s = jnp.where(krow <= qcol, scores(qi, 0), NEG_INF)
        s_ref[slot] = s
        return jnp.max(s, axis=0, keepdims=True)

    def produce(i, slot):
        s = scores(i - 1, qi - (i - 1))
        s_ref[slot] = s
        return jnp.max(s, axis=0, keepdims=True) + pen_ref[pl.ds(i - 1, 1), :]

    def consume(i, slot, block_max, m):
        n = jnp.where(i == 0, qi, i - 1)
        pen = pen_ref[pl.ds(n, 1), :]
        m_new = jnp.maximum(m, block_max)
        p = jnp.exp2(s_ref[slot] - (m_new - pen)).astype(BF16)
        acc_ref[...] = jnp.exp2(m - m_new) * acc_ref[...] + weighted_values(n, p)
        return m_new

    acc_ref[...] = jnp.zeros_like(acc_ref)
    _pipelined_blocks(qi + 1, produce_first, produce, consume, jnp.full((1, lanes), NEG_INF, F32))
    o_t = _finish(acc_ref[...], HEAD_DIM)
    o_ref[0, q_rows, :] = jnp.concatenate([o_t[:, hd * BLK:(hd + 1) * BLK] for hd in range(n_heads)],
                                          axis=0).T.astype(BF16)


def _moba(qd, kd, vdt, kmean, bias_d):
    b, s, width = qd.shape
    n_heads = width // HEAD_DIM
    n_blocks = s // BLK
    operands = (qd, kd, vdt, kmean.reshape(b, n_blocks, width))
    return pl.pallas_call(
        _per_query_tile(functools.partial(_moba_tile, n_heads=n_heads, n_blocks=n_blocks), n_blocks),
        grid=(b,),
        in_specs=[_seq_spec(t.shape) for t in operands] + [_const_spec(bias_d.shape)],
        out_specs=_seq_spec(qd.shape),
        out_shape=jax.ShapeDtypeStruct((b, s, width), BF16),
        scratch_shapes=[pltpu.VMEM((GATE_ROWS, n_heads * BLK), F32), pltpu.VMEM((VT_ROWS, n_heads * BLK), F32),
                        pltpu.VMEM((2, BLK, n_heads * BLK), F32)],
        compiler_params=_cparams(1),
        name="moba",
    )(*operands, bias_d)


def kernel(x, rel_bias_table, l0_ffn1_pre, l0_ffn1_post, l0_ffn1_wg, l0_ffn1_wu, l0_ffn1_wd, l0_mix_pre, l0_mix_post, l0_w_in, l0_sinks, l0_mla_q_norm, l0_mla_w_uq, l0_mla_kv_norm, l0_mla_w_ukv, l0_w_out, l0_ffn2_pre, l0_ffn2_post, l0_ffn2_wg, l0_ffn2_wu, l0_ffn2_wd, l1_ffn1_pre, l1_ffn1_post, l1_ffn1_wg, l1_ffn1_wu, l1_ffn1_wd, l1_mix_pre, l1_mix_post, l1_w_in, l1_nsa_pe_k, l1_nsa_pe_v, l1_nsa_wk1, l1_nsa_wk2, l1_nsa_wv1, l1_nsa_wv2, l1_w_out, l1_ffn2_pre, l1_ffn2_post, l1_ffn2_wg, l1_ffn2_wu, l1_ffn2_wd):
    b, s, d = x.shape
    m = b * s
    x2 = x.reshape(m, d)
    bias_a, bias_c, bias_d = _bias_tiles(rel_bias_table, ((A_SLOT, 2), (C_SLOT, BIAS_TILES), (D_SLOT, BIAS_TILES)))

    x2 = _ffn(x2, l0_ffn1_pre, l0_ffn1_post, l0_ffn1_wg, l0_ffn1_wu, l0_ffn1_wd, MACARON_WEIGHT)
    qa, ka, vat, qb, kb, vbt = _proj0(x2, s, l0_mix_pre, l0_w_in, l0_mla_q_norm, l0_mla_w_uq,
                                      l0_mla_kv_norm, l0_mla_w_ukv)
    shp = lambda t: t.reshape(b, s, t.shape[-1])
    o_a = _swa(shp(qa), shp(ka), vat, bias_a, l0_sinks)
    o_b = _mla(shp(qb), shp(kb), vbt)
    x2 = _outproj_ffn(x2, o_a.reshape(m, -1), o_b.reshape(m, -1), l0_w_out, l0_mix_post,
                      l0_ffn2_pre, l0_ffn2_post, l0_ffn2_wg, l0_ffn2_wu, l0_ffn2_wd, MACARON_WEIGHT)

    x2 = _ffn(x2, l1_ffn1_pre, l1_ffn1_post, l1_ffn1_wg, l1_ffn1_wu, l1_ffn1_wd, MACARON_WEIGHT)
    qc, kc_hm, vc_hm, ks, vst, kw, vwt, gates, qd, kd, vdt, kmean = _proj1(x2, b, s, l1_mix_pre, l1_w_in)
    kcmp, vcmpt = _compress(kc_hm, vc_hm, l1_nsa_pe_k, l1_nsa_pe_v, l1_nsa_wk1, l1_nsa_wk2, l1_nsa_wv1, l1_nsa_wv2)
    o_c = _nsa(shp(qc), kcmp, vcmpt, shp(ks), vst, shp(kw), vwt, shp(gates), bias_c)
    o_d = _moba(shp(qd), shp(kd), vdt, kmean, bias_d)
    x2 = _outproj_ffn(x2, o_c.reshape(m, -1), o_d.reshape(m, -1), l1_w_out, l1_mix_post,
                      l1_ffn2_pre, l1_ffn2_post, l1_ffn2_wg, l1_ffn2_wu, l1_ffn2_wd, MACARON_WEIGHT)
    return x2.reshape(b, s, d)
```

```python
import functools
import math

import numpy as np
import jax
import jax.numpy as jnp
from jax import lax
from jax.experimental import pallas as pl
from jax.experimental.pallas import tpu as pltpu

F32 = jnp.float32
BF16 = jnp.bfloat16

HEAD_DIM = 64
MACARON_WEIGHT = 0.5
NORM_EPS = 1e-6
LOG2E = math.log2(math.e)
NEG_INF = -1e30
PAD_SCORE = -3e38
FORCE_SCORE = 1e9
REL_BUCKETS = 32
REL_MAX_DIST = 1024
REL_SLOTS = 24
A_SLOT, C_SLOT, D_SLOT = 0, 8, 16
A_HEADS, A_KV_HEADS = 8, 2
MLA_NOPE, MLA_ROPE, MLA_V = 64, 32, 64
MLA_HEADS, MLA_Q_RANK, MLA_KV_RANK = 8, 256, 128
ROPE_THETA = 10000.0
C_HEADS, C_KV_HEADS = 8, 2
CMP_BLOCK, CMP_STRIDE = 32, 16
SEL_BLOCK, SEL_TOPK = 64, 8
D_HEADS = 8
MOBA_TOPK = 3

LANES = 128
SUBLANES = 8
TILE = 128
P0_COLS = (A_HEADS * HEAD_DIM, A_KV_HEADS * HEAD_DIM, A_KV_HEADS * HEAD_DIM, MLA_Q_RANK, MLA_KV_RANK, LANES)
P1_COLS = ((C_HEADS * HEAD_DIM,) + (C_KV_HEADS * HEAD_DIM,) * 6 + (LANES,) + (D_HEADS * HEAD_DIM,) * 3)
N_GATES = 3 * C_HEADS
BLK = 256
ONES_ROWS = 16
GATE_ROWS = 16
VT_ROWS = HEAD_DIM + ONES_ROWS
BIAS_TILES = 9
TOKEN_TILE = 512
FF_CHUNK = 256
VMEM_LIMIT = 56 * 1024 * 1024


def _cparams(n_axes):
    return pltpu.CompilerParams(dimension_semantics=("arbitrary",) * n_axes, vmem_limit_bytes=VMEM_LIMIT)


def _dot(a, b):
    return jnp.dot(a, b, preferred_element_type=F32)


def _dot_nt(a, b):
    return lax.dot_general(a, b, (((1,), (1,)), ((), ())), preferred_element_type=F32)


def _rms(x, g):
    return x * lax.rsqrt(jnp.mean(x * x, axis=-1, keepdims=True) + NORM_EPS) * g


def _silu(x):
    return x / (1.0 + jnp.exp(-x))


def _col_slices(sizes):
    edges = np.cumsum((0,) + tuple(sizes)).tolist()
    return [slice(a, b) for a, b in zip(edges[:-1], edges[1:])]


def _const_spec(shape):
    nd = len(shape)
    return pl.BlockSpec(shape, lambda *_: (0,) * nd)


def _pipelined_blocks(length, produce_first, produce, consume, state):
    trips = (length - 1) // 2

    def trip(t, carry):
        state, stat_a = carry
        a = 2 * t
        stat_b = produce(a + 1, 1)
        state = consume(a, 0, stat_a, state)
        stat_next = produce(a + 2, 0)
        state = consume(a + 1, 1, stat_b, state)
        return state, stat_next

    carry = lax.fori_loop(0, trips, trip, (state, produce_first(0)))
    a = 2 * trips

    def tail_one(c):
        return consume(a, 0, c[1], c[0])

    def tail_two(c):
        stat_b = produce(a + 1, 1)
        return consume(a + 1, 1, stat_b, consume(a, 0, c[1], c[0]))

    return lax.cond(length - a == 1, tail_one, tail_two, carry)


def _per_query_tile(tile_fn, n_tiles):
    def body(*refs):
        def step(qi, carry):
            tile_fn(qi, *refs)
            return carry

        lax.fori_loop(0, n_tiles, step, 0)

    return body


def _tile_rows(qi):
    return pl.ds(pl.multiple_of(qi * BLK, BLK), BLK)


def _resident_spec(shape):
    nd = len(shape)
    return pl.BlockSpec(shape, lambda *_: (0,) * nd, pipeline_mode=pl.Buffered(1))


def _ffn_tile(x, pre_ref, post_ref, wg_ref, wu_ref, wd_ref, acc_ref, weight):
    h = _rms(x, pre_ref[...]).astype(BF16)
    for c in range(wg_ref.shape[1] // FF_CHUNK):
        cols = slice(c * FF_CHUNK, (c + 1) * FF_CHUNK)
        g = _dot(h, wg_ref[:, cols].astype(BF16))
        u = _dot(h, wu_ref[:, cols].astype(BF16))
        a = (_silu(g) * u).astype(BF16)
        y = _dot(a, wd_ref[cols, :].astype(BF16))
        if c == 0:
            acc_ref[...] = y
        else:
            acc_ref[...] += y
    return x + weight * _rms(acc_ref[...], post_ref[...])


def _ffn_body(x_ref, pre_ref, post_ref, wg_ref, wu_ref, wd_ref, o_ref, acc_ref, *, weight):
    o_ref[...] = _ffn_tile(x_ref[...], pre_ref, post_ref, wg_ref, wu_ref, wd_ref, acc_ref, weight)


def _ffn(x2, pre, post, wg, wu, wd, weight):
    m, d = x2.shape
    d_ff = wg.shape[1]
    tm = TOKEN_TILE
    return pl.pallas_call(
        functools.partial(_ffn_body, weight=weight),
        grid=(m // tm,),
        in_specs=[
            pl.BlockSpec((tm, d), lambda i: (i, 0)),
            _const_spec((1, d)), _const_spec((1, d)),
            _resident_spec((d, d_ff)), _resident_spec((d, d_ff)), _resident_spec((d_ff, d)),
        ],
        out_specs=pl.BlockSpec((tm, d), lambda i: (i, 0)),
        out_shape=jax.ShapeDtypeStruct((m, d), F32),
        scratch_shapes=[pltpu.VMEM((tm, d), F32)],
        compiler_params=_cparams(1),
        name="ffn",
    )(x2, pre.reshape(1, d), post.reshape(1, d), wg, wu, wd)


def _outproj_ffn_body(x_ref, o1_ref, o2_ref, wo_ref, mpost_ref, pre_ref, post_ref, wg_ref, wu_ref, wd_ref,
                      o_ref, acc_ref, *, weight):
    n1 = o1_ref.shape[1]
    y = (_dot(o1_ref[...], wo_ref[0:n1, :].astype(BF16))
         + _dot(o2_ref[...], wo_ref[n1:, :].astype(BF16)))
    x = x_ref[...] + _rms(y, mpost_ref[...])
    o_ref[...] = _ffn_tile(x, pre_ref, post_ref, wg_ref, wu_ref, wd_ref, acc_ref, weight)


def _outproj_ffn(x2, o1, o2, w_out, mix_post, pre, post, wg, wu, wd, weight):
    m, d = x2.shape
    d_ff = wg.shape[1]
    tm = TOKEN_TILE
    tok = lambda n: pl.BlockSpec((tm, n), lambda i: (i, 0))
    return pl.pallas_call(
        functools.partial(_outproj_ffn_body, weight=weight),
        grid=(m // tm,),
        in_specs=[
            tok(d), tok(o1.shape[1]), tok(o2.shape[1]), _resident_spec(w_out.shape), _const_spec((1, d)),
            _const_spec((1, d)), _const_spec((1, d)),
            _resident_spec((d, d_ff)), _resident_spec((d, d_ff)), _resident_spec((d_ff, d)),
        ],
        out_specs=tok(d),
        out_shape=jax.ShapeDtypeStruct((m, d), F32),
        scratch_shapes=[pltpu.VMEM((tm, d), F32)],
        compiler_params=_cparams(1),
        name="outproj_ffn",
    )(x2, o1, o2, w_out, mix_post.reshape(1, d), pre.reshape(1, d), post.reshape(1, d), wg, wu, wd)


def _t5_bucket(dist):
    n = jnp.maximum(dist, 0)
    exact = REL_BUCKETS // 2
    nf = jnp.maximum(n, 1).astype(jnp.float32)
    large = exact + (jnp.log(nf / exact) / math.log(REL_MAX_DIST / exact) * (REL_BUCKETS - exact)).astype(jnp.int32)
    return jnp.where(n < exact, n, jnp.minimum(large, REL_BUCKETS - 1))


def _bias_body(tab_ref, idx_ref, *o_refs, first_slots):
    head = pl.program_id(0)
    for o_ref, first in zip(o_refs, first_slots):
        column = jnp.broadcast_to(LOG2E * tab_ref[pl.ds(first + head, 1), :], (TILE, LANES))
        for d in range(o_ref.shape[1]):
            o_ref[0, d] = jnp.take_along_axis(column, idx_ref[d], axis=1)


def _bias_tiles(rel_table, groups):
    heads = 8
    d = jnp.arange(BIAS_TILES)[:, None, None]
    key = jnp.arange(TILE)[None, :, None]
    query = jnp.arange(TILE)[None, None, :]
    idx = _t5_bucket(d * TILE + query - key).astype(jnp.int32)
    n_buckets, n_slots = rel_table.shape
    table_t = jnp.pad(rel_table.astype(F32).T, ((0, 0), (0, LANES - n_buckets)))
    return pl.pallas_call(
        functools.partial(_bias_body, first_slots=tuple(first for first, _ in groups)),
        grid=(heads,),
        in_specs=[_const_spec((n_slots, LANES)), _const_spec((BIAS_TILES, TILE, TILE))],
        out_specs=[pl.BlockSpec((1, n, TILE, TILE), lambda s: (s, 0, 0, 0)) for _, n in groups],
        out_shape=[jax.ShapeDtypeStruct((heads, n, TILE, TILE), F32) for _, n in groups],
        compiler_params=_cparams(1),
        name="bias_tiles",
    )(table_t, idx)


def _proj0_body(x_ref, pre_ref, w_ref, qn_ref, wq1_ref, wq2_ref, kvn_ref, wk_ref, wv_ref, cos_ref, sin_ref,
                qa_ref, ka_ref, vat_ref, qb_ref, kb_ref, vbt_ref):
    h = _rms(x_ref[...], pre_ref[...]).astype(BF16)
    y = _dot(h, w_ref[...])
    qa, ka, va, cq, ckv, rot = (y[:, sl] for sl in _col_slices(P0_COLS))
    qa_ref[...] = (qa * (LOG2E * HEAD_DIM ** -0.5)).astype(BF16)
    ka_ref[...] = ka.astype(BF16)
    for j in range(y.shape[0] // TILE):
        _store_vt_ones(vat_ref, j, va[j * TILE:(j + 1) * TILE], A_KV_HEADS)
    cq = _rms(cq, qn_ref[...]).astype(BF16)
    ckv = _rms(ckv, kvn_ref[...]).astype(BF16)
    cos_t = cos_ref[...]
    sin_t = sin_ref[...]
    lane = lax.broadcasted_iota(jnp.int32, cos_t.shape, 1)
    scale = LOG2E * (MLA_NOPE + MLA_ROPE) ** -0.5
    qmul = scale * (cos_t + jnp.where(lane < MLA_NOPE, 1.0, 0.0))
    qsin = scale * sin_t
    kr = rot * cos_t + pltpu.roll(rot, LANES // 2, 1) * sin_t
    q1 = _dot(cq, wq1_ref[...])
    q2 = _dot(cq, wq2_ref[...])
    k1 = _dot(ckv, wk_ref[...])
    for hd in range(MLA_HEADS):
        sl = slice(hd * LANES, (hd + 1) * LANES)
        qb_ref[:, sl] = (q1[:, sl] * qmul + q2[:, sl] * qsin).astype(BF16)
        kb_ref[:, sl] = (k1[:, sl] + kr).astype(BF16)
    vb = _dot(ckv, wv_ref[...])
    for j in range(vb.shape[0] // BLK):
        _store_vt_ones(vbt_ref, j, vb[j * BLK:(j + 1) * BLK], MLA_HEADS)


def _rope_lane_tables(s):
    inv = ROPE_THETA ** (-jnp.arange(0, MLA_ROPE, 2, dtype=jnp.float32) / MLA_ROPE)
    ang = jnp.arange(s, dtype=jnp.float32)[:, None] * inv[None, :]
    cos, sin = jnp.cos(ang), jnp.sin(ang)
    z_lo = jnp.zeros((s, MLA_NOPE), F32)
    z_hi = jnp.zeros((s, LANES - MLA_NOPE - MLA_ROPE), F32)
    return (jnp.concatenate([z_lo, cos, cos, z_hi], axis=1), jnp.concatenate([z_lo, sin, sin, z_hi], axis=1))


def _rot_cols(w):
    half = w.shape[-1] // 2
    return jnp.concatenate([-w[..., half:], w[..., :half]], axis=-1)


def _proj0(x2, seq, pre, w_in, q_norm, w_uq, kv_norm, w_ukv):
    m, d = x2.shape
    tm = TOKEN_TILE
    n_lin = sum(P0_COLS[:-1])
    kr_w = w_in[:, n_lin:n_lin + MLA_ROPE]
    z32 = jnp.zeros((d, MLA_ROPE), F32)
    w0 = jnp.concatenate([w_in[:, :n_lin], _rot_cols(kr_w), z32, kr_w, z32], axis=1).astype(BF16)
    wq = w_uq.reshape(-1, MLA_HEADS, MLA_NOPE + MLA_ROPE)
    rq = wq.shape[0]
    pad = LANES - MLA_NOPE - MLA_ROPE
    zq = jnp.zeros((rq, MLA_HEADS, pad), F32)
    wq1 = jnp.concatenate([wq, zq], axis=2).reshape(rq, -1).astype(BF16)
    wq2 = jnp.concatenate([jnp.zeros((rq, MLA_HEADS, MLA_NOPE), F32), _rot_cols(wq[:, :, MLA_NOPE:]), zq],
                          axis=2).reshape(rq, -1).astype(BF16)
    wkv = w_ukv.reshape(-1, MLA_HEADS, MLA_NOPE + MLA_V)
    rk = wkv.shape[0]
    wk = jnp.concatenate([wkv[:, :, :MLA_NOPE], jnp.zeros((rk, MLA_HEADS, LANES - MLA_NOPE), F32)],
                         axis=2).reshape(rk, -1).astype(BF16)
    wv = wkv[:, :, MLA_NOPE:].reshape(rk, -1).astype(BF16)
    cos_t, sin_t = _rope_lane_tables(seq)
    n_st = seq // tm
    tok = lambda n: pl.BlockSpec((tm, n), lambda i: (i, 0))
    outs = pl.pallas_call(
        _proj0_body,
        grid=(m // tm,),
        in_specs=[
            tok(d), _const_spec((1, d)), _const_spec(w0.shape),
            _const_spec((1, rq)), _const_spec(wq1.shape), _const_spec(wq2.shape),
            _const_spec((1, rk)), _const_spec(wk.shape), _const_spec(wv.shape),
            pl.BlockSpec((tm, LANES), lambda i: (i % n_st, 0)),
            pl.BlockSpec((tm, LANES), lambda i: (i % n_st, 0)),
        ],
        out_specs=[tok(P0_COLS[0]), tok(P0_COLS[1]),
                   pl.BlockSpec((1, tm // TILE, A_KV_HEADS * VT_ROWS, TILE), lambda i: (i // n_st, i % n_st, 0, 0)),
                   tok(MLA_HEADS * LANES), tok(MLA_HEADS * LANES),
                   pl.BlockSpec((1, tm // BLK, MLA_HEADS * VT_ROWS, BLK), lambda i: (i // n_st, i % n_st, 0, 0))],
        out_shape=[jax.ShapeDtypeStruct((m, P0_COLS[0]), BF16), jax.ShapeDtypeStruct((m, P0_COLS[1]), BF16),
                   jax.ShapeDtypeStruct((m // seq, seq // TILE, A_KV_HEADS * VT_ROWS, TILE), BF16),
                   jax.ShapeDtypeStruct((m, MLA_HEADS * LANES), BF16),
                   jax.ShapeDtypeStruct((m, MLA_HEADS * LANES), BF16),
                   jax.ShapeDtypeStruct((m // seq, seq // BLK, MLA_HEADS * VT_ROWS, BLK), BF16)],
        compiler_params=_cparams(1),
        name="proj0",
    )(x2, pre.reshape(1, d), w0, q_norm.reshape(1, rq), wq1, wq2, kv_norm.reshape(1, rk), wk, wv, cos_t, sin_t)
    return outs


def _swa_tile(qi, sink_ref, q_ref, k_ref, vt_ref, bias_ref, o_ref, *, n_kv, grp):
    halves = BLK // TILE
    items = [(hkv, half) for hkv in range(n_kv) for half in range(halves)]
    width = grp * TILE
    krow = lax.broadcasted_iota(jnp.int32, (TILE, len(items) * width), 0)
    qcol = lax.broadcasted_iota(jnp.int32, (TILE, len(items) * width), 1) & (TILE - 1)

    cur_s, prev_s, sink_parts, has_prev = [], [], [], []
    for hkv, half in items:
        sl = slice(hkv * HEAD_DIM, (hkv + 1) * HEAD_DIM)
        t = qi * halves + half
        tp = jnp.maximum(t - 1, 0)
        q_rows = pl.ds(pl.multiple_of(t * TILE, TILE), TILE)
        qs = jnp.concatenate([q_ref[0, q_rows, (hkv * grp + g) * HEAD_DIM:(hkv * grp + g + 1) * HEAD_DIM]
                              for g in range(grp)], axis=0)
        bias_c = jnp.concatenate([bias_ref[hkv * grp + g, 0] for g in range(grp)], axis=1)
        bias_p = jnp.concatenate([bias_ref[hkv * grp + g, 1] for g in range(grp)], axis=1)
        cur_s.append(_dot_nt(k_ref[0, pl.ds(pl.multiple_of(t * TILE, TILE), TILE), sl], qs) + bias_c)
        prev_s.append(_dot_nt(k_ref[0, pl.ds(pl.multiple_of(tp * TILE, TILE), TILE), sl], qs) + bias_p)
        sink_parts.extend(jnp.full((1, TILE), LOG2E * sink_ref[hkv * grp + g], F32) for g in range(grp))
        has_prev.append(jnp.full((1, width), jnp.where(t >= 1, 0, TILE), jnp.int32))
    s_c = jnp.where(krow <= qcol, jnp.concatenate(cur_s, axis=1), NEG_INF)
    s_p = jnp.where(krow > qcol + jnp.concatenate(has_prev, axis=1),
                    jnp.concatenate(prev_s, axis=1), NEG_INF)
    sink = jnp.concatenate(sink_parts, axis=1)
    mx = jnp.maximum(jnp.maximum(jnp.max(s_c, axis=0, keepdims=True), jnp.max(s_p, axis=0, keepdims=True)), sink)
    p_c = jnp.exp2(s_c - mx).astype(BF16)
    p_p = jnp.exp2(s_p - mx).astype(BF16)
    accs = []
    for i, (hkv, half) in enumerate(items):
        vsl = slice(hkv * VT_ROWS, (hkv + 1) * VT_ROWS)
        ls = slice(i * width, (i + 1) * width)
        t = qi * halves + half
        accs.append(_dot(vt_ref[0, t, vsl, :], p_c[:, ls]) + _dot(vt_ref[0, jnp.maximum(t - 1, 0), vsl, :], p_p[:, ls]))
    acc = jnp.concatenate(accs, axis=1)
    o_t = acc[0:HEAD_DIM] / (acc[HEAD_DIM:HEAD_DIM + 1] + jnp.exp2(sink - mx))
    heads = []
    for hkv in range(n_kv):
        for g in range(grp):
            heads.append(jnp.concatenate(
                [o_t[:, (hkv * halves + half) * width + g * TILE:(hkv * halves + half) * width + (g + 1) * TILE]
                 for half in range(halves)], axis=1))
    o_ref[0, _tile_rows(qi), :] = jnp.concatenate(heads, axis=0).T.astype(BF16)


def _seq_spec(shape):
    nd = len(shape)
    return pl.BlockSpec((1,) + tuple(shape[1:]), lambda bi: (bi,) + (0,) * (nd - 1))


def _swa(qa, ka, vat, bias_a, sinks):
    b, s, width = qa.shape
    n_kv = ka.shape[-1] // HEAD_DIM
    grp = width // HEAD_DIM // n_kv
    return pl.pallas_call(
        _per_query_tile(functools.partial(_swa_tile, n_kv=n_kv, grp=grp), s // BLK),
        grid=(b,),
        in_specs=[pl.BlockSpec(memory_space=pltpu.SMEM), _seq_spec(qa.shape), _seq_spec(ka.shape),
                  _seq_spec(vat.shape), _const_spec(bias_a.shape)],
        out_specs=_seq_spec(qa.shape),
        out_shape=jax.ShapeDtypeStruct((b, s, width), BF16),
        compiler_params=_cparams(1),
        name="swa",
    )(sinks.astype(F32), qa, ka, vat, bias_a)


def _finish(acc, dv):
    return acc[0:dv] / acc[dv:dv + 1]


def _mla_tile(qi, q_ref, k_ref, vt_ref, o_ref, acc_ref, s_ref, *, n_heads):
    q_rows = _tile_rows(qi)
    lanes = n_heads * BLK
    krow = lax.broadcasted_iota(jnp.int32, (BLK, lanes), 0)
    qcol = lax.broadcasted_iota(jnp.int32, (BLK, lanes), 1) & (BLK - 1)

    def scores(n):
        st = pl.multiple_of(n * BLK, BLK)
        return jnp.concatenate(
            [_dot_nt(k_ref[0, pl.ds(st, BLK), hd * LANES:(hd + 1) * LANES], q_ref[0, q_rows, hd * LANES:(hd + 1) * LANES])
             for hd in range(n_heads)], axis=1)

    def weighted_values(n, p):
        return jnp.concatenate(
            [_dot(vt_ref[0, n, hd * VT_ROWS:(hd + 1) * VT_ROWS, :], p[:, hd * BLK:(hd + 1) * BLK])
             for hd in range(n_heads)], axis=1)

    def produce_first(slot):
        s = jnp.where(krow <= qcol, scores(qi), NEG_INF)
        s_ref[slot] = s
        return jnp.max(s, axis=0, keepdims=True)

    def produce(i, slot):
        s = scores(i - 1)
        s_ref[slot] = s
        return jnp.max(s, axis=0, keepdims=True)

    def consume(i, slot, block_max, m):
        n = jnp.where(i == 0, qi, i - 1)
        m_new = jnp.maximum(m, block_max)
        p = jnp.exp2(s_ref[slot] - m_new).astype(BF16)
        acc_ref[...] = jnp.exp2(m - m_new) * acc_ref[...] + weighted_values(n, p)
        return m_new

    acc_ref[...] = jnp.zeros_like(acc_ref)
    _pipelined_blocks(qi + 1, produce_first, produce, consume, jnp.full((1, lanes), NEG_INF, F32))
    o_t = _finish(acc_ref[...], MLA_V)
    o_ref[0, q_rows, :] = jnp.concatenate([o_t[:, hd * BLK:(hd + 1) * BLK] for hd in range(n_heads)],
                                          axis=0).T.astype(BF16)


def _mla(qb, kb, vbt):
    b, s, width = qb.shape
    n_heads = width // LANES
    out_shape = jax.ShapeDtypeStruct((b, s, n_heads * MLA_V), BF16)
    return pl.pallas_call(
        _per_query_tile(functools.partial(_mla_tile, n_heads=n_heads), s // BLK),
        grid=(b,),
        in_specs=[_seq_spec(qb.shape), _seq_spec(kb.shape), _seq_spec(vbt.shape)],
        out_specs=_seq_spec(out_shape.shape),
        out_shape=out_shape,
        scratch_shapes=[pltpu.VMEM((VT_ROWS, n_heads * BLK), F32), pltpu.VMEM((2, BLK, n_heads * BLK), F32)],
        compiler_params=_cparams(1),
        name="mla",
    )(qb, kb, vbt)


def _store_vt_ones(vt_ref, j, v, n_heads):
    vt = v.T.astype(BF16)
    ones = jnp.ones((ONES_ROWS, v.shape[0]), BF16)
    for hd in range(n_heads):
        vt_ref[0, j, hd * VT_ROWS:hd * VT_ROWS + HEAD_DIM, :] = vt[hd * HEAD_DIM:(hd + 1) * HEAD_DIM]
        vt_ref[0, j, hd * VT_ROWS + HEAD_DIM:(hd + 1) * VT_ROWS, :] = ones


def _proj1_body(x_ref, pre_ref, w_ref, qc_ref, kx_ref, vx_ref, ks_ref, vst_ref, kw_ref, vwt_ref, g_ref,
                qd_ref, kd_ref, vdt_ref, km_ref, kstage_ref, vstage_ref, *, tm):
    h = _rms(x_ref[...], pre_ref[...]).astype(BF16)
    y = _dot(h, w_ref[...])
    qc, kc, vc, ks, vs, kw, vw, gl, qd, kd, vd = (y[:, sl] for sl in _col_slices(P1_COLS))
    scale = LOG2E * HEAD_DIM ** -0.5
    qc_ref[...] = (qc * scale).astype(BF16)
    for stage_ref, cols, x_out in ((kstage_ref, kc, kx_ref), (vstage_ref, vc, vx_ref)):
        stage_ref[...] = cols
        for i in range(CMP_STRIDE):
            piece = stage_ref[pl.ds(i, tm // CMP_STRIDE, stride=CMP_STRIDE), :].astype(BF16)
            for hkv in range(C_KV_HEADS):
                x_out[0, hkv, :, i * HEAD_DIM:(i + 1) * HEAD_DIM] = piece[:, hkv * HEAD_DIM:(hkv + 1) * HEAD_DIM]
    ks_ref[...] = ks.astype(BF16)
    kw_ref[...] = kw.astype(BF16)
    g_ref[...] = 1.0 / (1.0 + jnp.exp(-gl))
    qd_ref[...] = (qd * scale).astype(BF16)
    kd_ref[...] = kd.astype(BF16)
    for j in range(tm // BLK):
        rows = slice(j * BLK, (j + 1) * BLK)
        km_ref[j] = jnp.mean(kd[rows], axis=0, keepdims=True)
        _store_vt_ones(vdt_ref, j, vd[rows], D_HEADS)
        _store_vt_ones(vst_ref, j, vs[rows], C_KV_HEADS)
        _store_vt_ones(vwt_ref, j, vw[rows], C_KV_HEADS)


def _proj1(x2, batch, seq, pre, w_in):
    m, d = x2.shape
    tm = TOKEN_TILE
    n_st = seq // tm
    n_pre = sum(P1_COLS[:7]) + N_GATES
    pad = jnp.zeros((d, LANES - N_GATES), F32)
    w1 = jnp.concatenate([w_in[:, :n_pre], pad, w_in[:, n_pre:]], axis=1).astype(BF16)
    q_w, kv_w, d_w = C_HEADS * HEAD_DIM, C_KV_HEADS * HEAD_DIM, D_HEADS * HEAD_DIM
    tok = lambda n: pl.BlockSpec((tm, n), lambda i: (i, 0))
    tok_shape = lambda n, dt=BF16: jax.ShapeDtypeStruct((m, n), dt)
    feat = CMP_STRIDE * HEAD_DIM
    hm = pl.BlockSpec((1, C_KV_HEADS, tm // CMP_STRIDE, feat), lambda i: (i // n_st, 0, i % n_st, 0))
    hm_shape = jax.ShapeDtypeStruct((batch, C_KV_HEADS, seq // CMP_STRIDE, feat), BF16)
    nblk = tm // BLK
    vt_spec = lambda heads: pl.BlockSpec((1, nblk, heads * VT_ROWS, BLK), lambda i: (i // n_st, i % n_st, 0, 0))
    vt_shape = lambda heads: jax.ShapeDtypeStruct((batch, seq // BLK, heads * VT_ROWS, BLK), BF16)
    return pl.pallas_call(
        functools.partial(_proj1_body, tm=tm),
        grid=(m // tm,),
        in_specs=[tok(d), _const_spec((1, d)), _const_spec(w1.shape)],
        out_specs=[tok(q_w), hm, hm, tok(kv_w), vt_spec(C_KV_HEADS), tok(kv_w), vt_spec(C_KV_HEADS), tok(LANES),
                   tok(d_w), tok(d_w), vt_spec(D_HEADS),
                   pl.BlockSpec((nblk, 1, d_w), lambda i: (i, 0, 0))],
        out_shape=[
            tok_shape(q_w), hm_shape, hm_shape,
            tok_shape(kv_w), vt_shape(C_KV_HEADS), tok_shape(kv_w), vt_shape(C_KV_HEADS), tok_shape(LANES, F32),
            tok_shape(d_w), tok_shape(d_w), vt_shape(D_HEADS),
            jax.ShapeDtypeStruct((m // BLK, 1, d_w), F32),
        ],
        scratch_shapes=[pltpu.VMEM((tm, kv_w), F32), pltpu.VMEM((tm, kv_w), F32)],
        compiler_params=_cparams(1),
        name="proj1",
    )(x2, pre.reshape(1, d), w1)


def _compress_body(kx_ref, vx_ref, pek_ref, pev_ref, wk1_ref, wk2_ref, wv1_ref, wv2_ref, ko_ref, vo_ref):
    half = CMP_STRIDE * HEAD_DIM
    for x_ref, pe_ref, w1_ref, w2_ref, o_ref in ((kx_ref, pek_ref, wk1_ref, wk2_ref, ko_ref),
                                                 (vx_ref, pev_ref, wv1_ref, wv2_ref, vo_ref)):
        pe8 = jnp.broadcast_to(pe_ref[...], (SUBLANES, 2 * half)).astype(BF16)
        pe_term = _dot(pe8, w1_ref[...])[0:1]
        outs = []
        for hkv in range(C_KV_HEADS):
            x = x_ref[0, hkv]
            lo = _dot(x, w1_ref[0:half])
            hi = _dot(x, w1_ref[half:2 * half])
            hid = lo + pltpu.roll(hi, x.shape[0] - 1, 0) + pe_term
            outs.append(_dot(_silu(hid).astype(BF16), w2_ref[...]))
        out = jnp.concatenate(outs, axis=1)
        o_ref[0] = (out.T if o_ref is vo_ref else out).astype(BF16)


def _compress(kx, vx, pe_k, pe_v, wk1, wk2, wv1, wv2):
    b, _, rows, feat = kx.shape
    dh = feat // CMP_STRIDE
    xspec = pl.BlockSpec((1, 2, rows, feat), lambda bi: (bi, 0, 0, 0))
    ospec = pl.BlockSpec((1, rows, 2 * dh), lambda bi: (bi, 0, 0))
    w1s, w2s = wk1.shape, wk2.shape
    return pl.pallas_call(
        _compress_body,
        grid=(b,),
        in_specs=[xspec, xspec, _const_spec((1, 2 * feat)), _const_spec((1, 2 * feat)),
                  _const_spec(w1s), _const_spec(w2s), _const_spec(w1s), _const_spec(w2s)],
        out_specs=[ospec, ospec],
        out_shape=[jax.ShapeDtypeStruct((b, rows, 2 * dh), BF16)] * 2,
        compiler_params=_cparams(1),
        name="nsa_compress",
    )(kx, vx, pe_k.astype(F32).reshape(1, -1), pe_v.astype(F32).reshape(1, -1),
      wk1.astype(BF16), wk2.astype(BF16), wv1.astype(BF16), wv2.astype(BF16))


def _bias_block(bias_ref, hd, dist_blocks):
    sub = BLK // TILE
    rows = []
    for a in range(sub):
        cols = []
        for bq in range(sub):
            e = sub * dist_blocks + bq - a
            e = (min(max(e, 0), BIAS_TILES - 1) if isinstance(e, int)
                 else jnp.minimum(jnp.maximum(e, 0), BIAS_TILES - 1))
            cols.append(bias_ref[hd, e])
        rows.append(jnp.concatenate(cols, axis=1))
    return jnp.concatenate(rows, axis=0)


def _topk_rows(val, ridx, k):
    n_rows = val.shape[0]
    picked = jnp.zeros(val.shape, F32)
    for _ in range(k):
        best = jnp.max(val, axis=0, keepdims=True)
        first = jnp.min(jnp.where(val == best, ridx, n_rows), axis=0, keepdims=True)
        hit = ridx == first
        picked = jnp.where(hit, 1.0, picked)
        val = jnp.where(hit, PAD_SCORE, val)
    return picked


def _nsa_tile(qi, q_ref, kc_ref, vct_ref, ks_ref, vst_ref, kw_ref, vwt_ref, g_ref, bias_ref, ovt_ref, o_ref,
              pen_ref, acc_ref, s_ref, *, n_sel):
    q_rows = _tile_rows(qi)
    grp = C_HEADS // C_KV_HEADS
    lanes = grp * BLK
    sub = BLK // SEL_BLOCK
    own = pl.multiple_of(qi * BLK, BLK)
    prev = pl.multiple_of(jnp.maximum(qi - 1, 0) * BLK, BLK)
    prev_blk = jnp.maximum(qi - 1, 0)
    krow = lax.broadcasted_iota(jnp.int32, (BLK, lanes), 0)
    qcol = lax.broadcasted_iota(jnp.int32, (BLK, lanes), 1) & (BLK - 1)
    causal = krow <= qcol
    crow = lax.broadcasted_iota(jnp.int32, (TILE, lanes), 0)
    cq = lax.broadcasted_iota(jnp.int32, (TILE, lanes), 1) & (BLK - 1)
    visible = CMP_STRIDE * crow + (CMP_BLOCK - 1) <= qi * BLK + cq
    jrow = lax.broadcasted_iota(jnp.int32, (n_sel, BLK), 0)
    cur = qi * sub + lax.broadcasted_iota(jnp.int32, (n_sel, BLK), 1) // SEL_BLOCK
    forced = (jrow == 0) | (jrow == cur) | (jrow == cur - 1)
    gates_t = g_ref[0, q_rows, :].T
    has_prev = jnp.where(qi >= 1, 0, BLK)

    def stack_q(hkv):
        return jnp.concatenate(
            [q_ref[0, q_rows, (hkv * grp + g) * HEAD_DIM:(hkv * grp + g + 1) * HEAD_DIM] for g in range(grp)], axis=0)

    def group_bias(hkv, dist_blocks):
        return jnp.concatenate([_bias_block(bias_ref, hkv * grp + g, dist_blocks) for g in range(grp)], axis=1)

    def sel_scores(hkv, n, dist_blocks):
        st = pl.multiple_of(n * BLK, BLK)
        sl = slice(hkv * HEAD_DIM, (hkv + 1) * HEAD_DIM)
        return _dot_nt(ks_ref[0, pl.ds(st, BLK), sl], stack_q(hkv)) + group_bias(hkv, dist_blocks)

    def sel_pens(hkv, n):
        pens = pen_ref[hkv, n, 0:sub, :]
        return [jnp.concatenate([pens[j:j + 1]] * grp, axis=1) for j in range(sub)]

    def sel_max(s, pens):
        mj = [jnp.max(s[j * SEL_BLOCK:(j + 1) * SEL_BLOCK], axis=0, keepdims=True) + pens[j] for j in range(sub)]
        return functools.reduce(jnp.maximum, mj)

    def sel_probs(s, pens, m_new):
        return jnp.concatenate([jnp.exp2(s[j * SEL_BLOCK:(j + 1) * SEL_BLOCK] - (m_new - pens[j]))
                                for j in range(sub)], axis=0).astype(BF16)

    o_cmp, o_win = [], []
    for hkv in range(C_KV_HEADS):
        sl = slice(hkv * HEAD_DIM, (hkv + 1) * HEAD_DIM)
        vsl = slice(hkv * VT_ROWS, (hkv + 1) * VT_ROWS)
        qs = stack_q(hkv)

        sc = jnp.where(visible, _dot_nt(kc_ref[0, :, sl], qs), NEG_INF)
        mc = jnp.max(sc, axis=0, keepdims=True)
        ec = jnp.where(visible, jnp.exp2(sc - mc), 0.0)
        lc = jnp.sum(ec, axis=0, keepdims=True)
        p = ec / jnp.where(lc > 0.0, lc, 1.0)
        o_cmp.append(_dot(vct_ref[0, sl, :], p.astype(BF16)))

        ps = p[:, 0:BLK] + p[:, BLK:2 * BLK] + p[:, 2 * BLK:3 * BLK] + p[:, 3 * BLK:4 * BLK]
        ps_hi = ps.astype(BF16)
        ps_lo = (ps - ps_hi.astype(F32)).astype(BF16)
        imp = _dot(ovt_ref[...], ps_hi) + _dot(ovt_ref[...], ps_lo)
        val = jnp.where(forced, FORCE_SCORE, jnp.where(jrow <= cur, imp, NEG_INF))
        pen = jnp.where(_topk_rows(val, jrow, SEL_TOPK) > 0.5, 0.0, NEG_INF)
        for t in range(n_sel // sub):
            pen_ref[hkv, t, 0:sub, :] = pen[t * sub:(t + 1) * sub]

        s0 = _dot_nt(kw_ref[0, pl.ds(own, BLK), sl], qs) + group_bias(hkv, 0)
        s1 = _dot_nt(kw_ref[0, pl.ds(prev, BLK), sl], qs) + group_bias(hkv, 1)
        s0 = jnp.where(causal, s0, NEG_INF)
        s1 = jnp.where(krow > qcol + has_prev, s1, NEG_INF)
        mw = jnp.maximum(jnp.max(s0, axis=0, keepdims=True), jnp.max(s1, axis=0, keepdims=True))
        ow = (_dot(vwt_ref[0, qi, vsl, :], jnp.exp2(s0 - mw).astype(BF16))
              + _dot(vwt_ref[0, prev_blk, vsl, :], jnp.exp2(s1 - mw).astype(BF16)))
        o_win.append(_finish(ow, HEAD_DIM))

    def produce_first(slot):
        stats = []
        for hkv in range(C_KV_HEADS):
            s = jnp.where(causal, sel_scores(hkv, qi, 0), NEG_INF)
            s_ref[slot, hkv] = s
            stats.append(sel_max(s, sel_pens(hkv, qi)))
        return tuple(stats)

    def produce(i, slot):
        stats = []
        for hkv in range(C_KV_HEADS):
            s = sel_scores(hkv, i - 1, qi - (i - 1))
            s_ref[slot, hkv] = s
            stats.append(sel_max(s, sel_pens(hkv, i - 1)))
        return tuple(stats)

    def consume(i, slot, block_max, ms):
        n = jnp.where(i == 0, qi, i - 1)
        new = []
        for hkv in range(C_KV_HEADS):
            vsl = slice(hkv * VT_ROWS, (hkv + 1) * VT_ROWS)
            m_new = jnp.maximum(ms[hkv], block_max[hkv])
            pb = sel_probs(s_ref[slot, hkv], sel_pens(hkv, n), m_new)
            acc_ref[hkv] = jnp.exp2(ms[hkv] - m_new) * acc_ref[hkv] + _dot(vst_ref[0, n, vsl, :], pb)
            new.append(m_new)
        return tuple(new)

    acc_ref[...] = jnp.zeros_like(acc_ref)
    m_init = jnp.full((1, lanes), NEG_INF, F32)
    _pipelined_blocks(qi + 1, produce_first, produce, consume, (m_init, m_init))
    outs = []
    for hkv in range(C_KV_HEADS):
        o_slc = _finish(acc_ref[hkv], HEAD_DIM)
        for g in range(grp):
            hd = hkv * grp + g
            ls = slice(g * BLK, (g + 1) * BLK)
            g_cmp, g_slc, g_win = (gates_t[t * C_HEADS + hd:t * C_HEADS + hd + 1] for t in range(N_GATES // C_HEADS))
            outs.append(g_cmp * o_cmp[hkv][:, ls] + g_slc * o_slc[:, ls] + g_win * o_win[hkv][:, ls])
    o_ref[0, q_rows, :] = jnp.concatenate(outs, axis=0).T.astype(BF16)


def _nsa(qc, kcmp, vcmpt, ks, vst, kw, vwt, gates, bias_c):
    b, s, width = qc.shape
    n_blocks = s // BLK
    n_sel = s // SEL_BLOCK
    n_c = (s - CMP_BLOCK) // CMP_STRIDE + 1
    j_start = np.arange(n_sel)[:, None] * SEL_BLOCK
    c_start = np.arange(TILE)[None, :] * CMP_STRIDE
    overlap_t = ((c_start < j_start + SEL_BLOCK) & (c_start + CMP_BLOCK > j_start)
                 & (np.arange(TILE)[None, :] < n_c))
    ovt = jnp.asarray(overlap_t, BF16)
    group_lanes = C_HEADS // C_KV_HEADS * BLK
    operands = (qc, kcmp, vcmpt, ks, vst, kw, vwt, gates)
    return pl.pallas_call(
        _per_query_tile(functools.partial(_nsa_tile, n_sel=n_sel), n_blocks),
        grid=(b,),
        in_specs=[_seq_spec(t.shape) for t in operands] + [_const_spec(bias_c.shape), _const_spec(ovt.shape)],
        out_specs=_seq_spec(qc.shape),
        out_shape=jax.ShapeDtypeStruct((b, s, width), BF16),
        scratch_shapes=[pltpu.VMEM((C_KV_HEADS, n_blocks, SUBLANES, BLK), F32),
                        pltpu.VMEM((C_KV_HEADS, VT_ROWS, group_lanes), F32),
                        pltpu.VMEM((2, C_KV_HEADS, BLK, group_lanes), F32)],
        compiler_params=_cparams(1),
        name="nsa",
    )(*operands, bias_c, ovt)


def _moba_tile(qi, q_ref, k_ref, vt_ref, km_ref, bias_ref, o_ref, pen_ref, acc_ref, s_ref, *, n_heads, n_blocks):
    q_rows = _tile_rows(qi)
    lanes = n_heads * BLK
    nidx = lax.broadcasted_iota(jnp.int32, (GATE_ROWS, lanes), 0)
    krow = lax.broadcasted_iota(jnp.int32, (BLK, lanes), 0)
    qcol = lax.broadcasted_iota(jnp.int32, (BLK, lanes), 1) & (BLK - 1)

    def scores(n, dist_blocks):
        st = pl.multiple_of(n * BLK, BLK)
        parts = []
        for hd in range(n_heads):
            sl = slice(hd * HEAD_DIM, (hd + 1) * HEAD_DIM)
            parts.append(_dot_nt(k_ref[0, pl.ds(st, BLK), sl], q_ref[0, q_rows, sl])
                         + _bias_block(bias_ref, hd, dist_blocks))
        return jnp.concatenate(parts, axis=1)

    def weighted_values(n, p):
        return jnp.concatenate(
            [_dot(vt_ref[0, n, hd * VT_ROWS:(hd + 1) * VT_ROWS, :], p[:, hd * BLK:(hd + 1) * BLK])
             for hd in range(n_heads)], axis=1)

    zpad = jnp.zeros((GATE_ROWS - n_blocks, HEAD_DIM), F32)
    gate = jnp.concatenate(
        [_dot_nt(jnp.concatenate([km_ref[0, :, hd * HEAD_DIM:(hd + 1) * HEAD_DIM], zpad], axis=0).astype(BF16),
                 q_ref[0, q_rows, hd * HEAD_DIM:(hd + 1) * HEAD_DIM]) for hd in range(n_heads)], axis=1)
    val = jnp.where(nidx < qi, gate, NEG_INF)
    val = jnp.where(nidx < n_blocks, val, PAD_SCORE)
    keep = ((_topk_rows(val, nidx, MOBA_TOPK) > 0.5) & (nidx < qi)) | (nidx == qi)
    pen_ref[...] = jnp.where(keep, 0.0, NEG_INF)

    def produce_first(slot):
        s = jnp.where(krow <= qcol, scores(qi, 0), NEG_INF)
        s_ref[slot] = s
        return jnp.max(s, axis=0, keepdims=True)

    def produce(i, slot):
        s = scores(i - 1, qi - (i - 1))
        s_ref[slot] = s
        return jnp.max(s, axis=0, keepdims=True) + pen_ref[pl.ds(i - 1, 1), :]

    def consume(i, slot, block_max, m):
        n = jnp.where(i == 0, qi, i - 1)
        pen = pen_ref[pl.ds(n, 1), :]
        m_new = jnp.maximum(m, block_max)
        p = jnp.exp2(s_ref[slot] - (m_new - pen)).astype(BF16)
        acc_ref[...] = jnp.exp2(m - m_new) * acc_ref[...] + weighted_values(n, p)
        return m_new

    acc_ref[...] = jnp.zeros_like(acc_ref)
    _pipelined_blocks(qi + 1, produce_first, produce, consume, jnp.full((1, lanes), NEG_INF, F32))
    o_t = _finish(acc_ref[...], HEAD_DIM)
    o_ref[0, q_rows, :] = jnp.concatenate([o_t[:, hd * BLK:(hd + 1) * BLK] for hd in range(n_heads)],
                                          axis=0).T.astype(BF16)


def _moba(qd, kd, vdt, kmean, bias_d):
    b, s, width = qd.shape
    n_heads = width // HEAD_DIM
    n_blocks = s // BLK
    operands = (qd, kd, vdt, kmean.reshape(b, n_blocks, width))
    return pl.pallas_call(
        _per_query_tile(functools.partial(_moba_tile, n_heads=n_heads, n_blocks=n_blocks), n_blocks),
        grid=(b,),
        in_specs=[_seq_spec(t.shape) for t in operands] + [_const_spec(bias_d.shape)],
        out_specs=_seq_spec(qd.shape),
        out_shape=jax.ShapeDtypeStruct((b, s, width), BF16),
        scratch_shapes=[pltpu.VMEM((GATE_ROWS, n_heads * BLK), F32), pltpu.VMEM((VT_ROWS, n_heads * BLK), F32),
                        pltpu.VMEM((2, BLK, n_heads * BLK), F32)],
        compiler_params=_cparams(1),
        name="moba",
    )(*operands, bias_d)


def kernel(x, rel_bias_table, l0_ffn1_pre, l0_ffn1_post, l0_ffn1_wg, l0_ffn1_wu, l0_ffn1_wd, l0_mix_pre, l0_mix_post, l0_w_in, l0_sinks, l0_mla_q_norm, l0_mla_w_uq, l0_mla_kv_norm, l0_mla_w_ukv, l0_w_out, l0_ffn2_pre, l0_ffn2_post, l0_ffn2_wg, l0_ffn2_wu, l0_ffn2_wd, l1_ffn1_pre, l1_ffn1_post, l1_ffn1_wg, l1_ffn1_wu, l1_ffn1_wd, l1_mix_pre, l1_mix_post, l1_w_in, l1_nsa_pe_k, l1_nsa_pe_v, l1_nsa_wk1, l1_nsa_wk2, l1_nsa_wv1, l1_nsa_wv2, l1_w_out, l1_ffn2_pre, l1_ffn2_post, l1_ffn2_wg, l1_ffn2_wu, l1_ffn2_wd):
    b, s, d = x.shape
    m = b * s
    x2 = x.reshape(m, d)
    bias_a, bias_c, bias_d = _bias_tiles(rel_bias_table, ((A_SLOT, 2), (C_SLOT, BIAS_TILES), (D_SLOT, BIAS_TILES)))

    x2 = _ffn(x2, l0_ffn1_pre, l0_ffn1_post, l0_ffn1_wg, l0_ffn1_wu, l0_ffn1_wd, MACARON_WEIGHT)
    qa, ka, vat, qb, kb, vbt = _proj0(x2, s, l0_mix_pre, l0_w_in, l0_mla_q_norm, l0_mla_w_uq,
                                      l0_mla_kv_norm, l0_mla_w_ukv)
    shp = lambda t: t.reshape(b, s, t.shape[-1])
    o_a = _swa(shp(qa), shp(ka), vat, bias_a, l0_sinks)
    o_b = _mla(shp(qb), shp(kb), vbt)
    x2 = _outproj_ffn(x2, o_a.reshape(m, -1), o_b.reshape(m, -1), l0_w_out, l0_mix_post,
                      l0_ffn2_pre, l0_ffn2_post, l0_ffn2_wg, l0_ffn2_wu, l0_ffn2_wd, MACARON_WEIGHT)

    x2 = _ffn(x2, l1_ffn1_pre, l1_ffn1_post, l1_ffn1_wg, l1_ffn1_wu, l1_ffn1_wd, MACARON_WEIGHT)
    qc, kc_hm, vc_hm, ks, vst, kw, vwt, gates, qd, kd, vdt, kmean = _proj1(x2, b, s, l1_mix_pre, l1_w_in)
    kcmp, vcmpt = _compress(kc_hm, vc_hm, l1_nsa_pe_k, l1_nsa_pe_v, l1_nsa_wk1, l1_nsa_wk2, l1_nsa_wv1, l1_nsa_wv2)
    o_c = _nsa(shp(qc), kcmp, vcmpt, shp(ks), vst, shp(kw), vwt, shp(gates), bias_c)
    o_d = _moba(shp(qd), shp(kd), vdt, kmean, bias_d)
    x2 = _outproj_ffn(x2, o_c.reshape(m, -1), o_d.reshape(m, -1), l1_w_out, l1_mix_post,
                      l1_ffn2_pre, l1_ffn2_post, l1_ffn2_wg, l1_ffn2_wu, l1_ffn2_wd, MACARON_WEIGHT)
    return x2.reshape(b, s, d)
```

```python
import functools
import math

import numpy as np
import jax
import jax.numpy as jnp
from jax import lax
from jax.experimental import pallas as pl
from jax.experimental.pallas import tpu as pltpu

F32 = jnp.float32
BF16 = jnp.bfloat16

HEAD_DIM = 64
MACARON_WEIGHT = 0.5
NORM_EPS = 1e-6
LOG2E = math.log2(math.e)
NEG_INF = -1e30
PAD_SCORE = -3e38
FORCE_SCORE = 1e9
REL_BUCKETS = 32
REL_MAX_DIST = 1024
REL_SLOTS = 24
A_SLOT, C_SLOT, D_SLOT = 0, 8, 16
A_HEADS, A_KV_HEADS = 8, 2
MLA_NOPE, MLA_ROPE, MLA_V = 64, 32, 64
MLA_HEADS, MLA_Q_RANK, MLA_KV_RANK = 8, 256, 128
ROPE_THETA = 10000.0
C_HEADS, C_KV_HEADS = 8, 2
CMP_BLOCK, CMP_STRIDE = 32, 16
SEL_BLOCK, SEL_TOPK = 64, 8
D_HEADS = 8
MOBA_TOPK = 3

LANES = 128
SUBLANES = 8
TILE = 128
P0_COLS = (A_HEADS * HEAD_DIM, A_KV_HEADS * HEAD_DIM, A_KV_HEADS * HEAD_DIM, MLA_Q_RANK, MLA_KV_RANK, LANES)
P1_COLS = ((C_HEADS * HEAD_DIM,) + (C_KV_HEADS * HEAD_DIM,) * 6 + (LANES,) + (D_HEADS * HEAD_DIM,) * 3)
N_GATES = 3 * C_HEADS
BLK = 256
ONES_ROWS = 16
GATE_ROWS = 16
VT_ROWS = HEAD_DIM + ONES_ROWS
BIAS_TILES = 9
TOKEN_TILE = 512
PROJ_TILE = 1024
FF_CHUNK = 256
VMEM_LIMIT = 56 * 1024 * 1024


def _cparams(n_axes):
    return pltpu.CompilerParams(dimension_semantics=("arbitrary",) * n_axes, vmem_limit_bytes=VMEM_LIMIT)


def _dot(a, b):
    return jnp.dot(a, b, preferred_element_type=F32)


def _dot_nt(a, b):
    return lax.dot_general(a, b, (((1,), (1,)), ((), ())), preferred_element_type=F32)


def _rms(x, g):
    return x * lax.rsqrt(jnp.mean(x * x, axis=-1, keepdims=True) + NORM_EPS) * g


def _silu(x):
    return x / (1.0 + jnp.exp(-x))


def _col_slices(sizes):
    edges = np.cumsum((0,) + tuple(sizes)).tolist()
    return [slice(a, b) for a, b in zip(edges[:-1], edges[1:])]


def _const_spec(shape):
    nd = len(shape)
    return pl.BlockSpec(shape, lambda *_: (0,) * nd)


def _pipelined_blocks(length, produce_first, produce, consume, state):
    trips = (length - 1) // 2

    def trip(t, carry):
        state, stat_a = carry
        a = 2 * t
        stat_b = produce(a + 1, 1)
        state = consume(a, 0, stat_a, state)
        stat_next = produce(a + 2, 0)
        state = consume(a + 1, 1, stat_b, state)
        return state, stat_next

    carry = lax.fori_loop(0, trips, trip, (state, produce_first(0)))
    a = 2 * trips

    def tail_one(c):
        return consume(a, 0, c[1], c[0])

    def tail_two(c):
        stat_b = produce(a + 1, 1)
        return consume(a + 1, 1, stat_b, consume(a, 0, c[1], c[0]))

    return lax.cond(length - a == 1, tail_one, tail_two, carry)


def _per_query_tile(tile_fn, n_tiles):
    def body(*refs):
        def step(qi, carry):
            tile_fn(qi, *refs)
            return carry

        lax.fori_loop(0, n_tiles, step, 0)

    return body


def _tile_rows(qi):
    return pl.ds(pl.multiple_of(qi * BLK, BLK), BLK)


def _resident_spec(shape):
    nd = len(shape)
    return pl.BlockSpec(shape, lambda *_: (0,) * nd, pipeline_mode=pl.Buffered(1))


def _ffn_tile(x, pre_ref, post_ref, wg_ref, wu_ref, wd_ref, acc_ref, weight):
    h = _rms(x, pre_ref[...]).astype(BF16)
    for c in range(wg_ref.shape[1] // FF_CHUNK):
        cols = slice(c * FF_CHUNK, (c + 1) * FF_CHUNK)
        g = _dot(h, wg_ref[:, cols].astype(BF16))
        u = _dot(h, wu_ref[:, cols].astype(BF16))
        a = (_silu(g) * u).astype(BF16)
        y = _dot(a, wd_ref[cols, :].astype(BF16))
        if c == 0:
            acc_ref[...] = y
        else:
            acc_ref[...] += y
    return x + weight * _rms(acc_ref[...], post_ref[...])


def _ffn_body(x_ref, pre_ref, post_ref, wg_ref, wu_ref, wd_ref, o_ref, acc_ref, *, weight):
    o_ref[...] = _ffn_tile(x_ref[...], pre_ref, post_ref, wg_ref, wu_ref, wd_ref, acc_ref, weight)


def _ffn(x2, pre, post, wg, wu, wd, weight):
    m, d = x2.shape
    d_ff = wg.shape[1]
    tm = TOKEN_TILE
    return pl.pallas_call(
        functools.partial(_ffn_body, weight=weight),
        grid=(m // tm,),
        in_specs=[
            pl.BlockSpec((tm, d), lambda i: (i, 0)),
            _const_spec((1, d)), _const_spec((1, d)),
            _resident_spec((d, d_ff)), _resident_spec((d, d_ff)), _resident_spec((d_ff, d)),
        ],
        out_specs=pl.BlockSpec((tm, d), lambda i: (i, 0)),
        out_shape=jax.ShapeDtypeStruct((m, d), F32),
        scratch_shapes=[pltpu.VMEM((tm, d), F32)],
        compiler_params=_cparams(1),
        name="ffn",
    )(x2, pre.reshape(1, d), post.reshape(1, d), wg, wu, wd)


def _outproj_ffn_body(x_ref, o1_ref, o2_ref, wo_ref, mpost_ref, pre_ref, post_ref, wg_ref, wu_ref, wd_ref,
                      o_ref, acc_ref, *, weight):
    n1 = o1_ref.shape[1]
    y = (_dot(o1_ref[...], wo_ref[0:n1, :].astype(BF16))
         + _dot(o2_ref[...], wo_ref[n1:, :].astype(BF16)))
    x = x_ref[...] + _rms(y, mpost_ref[...])
    o_ref[...] = _ffn_tile(x, pre_ref, post_ref, wg_ref, wu_ref, wd_ref, acc_ref, weight)


def _outproj_ffn(x2, o1, o2, w_out, mix_post, pre, post, wg, wu, wd, weight):
    m, d = x2.shape
    d_ff = wg.shape[1]
    tm = TOKEN_TILE
    tok = lambda n: pl.BlockSpec((tm, n), lambda i: (i, 0))
    return pl.pallas_call(
        functools.partial(_outproj_ffn_body, weight=weight),
        grid=(m // tm,),
        in_specs=[
            tok(d), tok(o1.shape[1]), tok(o2.shape[1]), _resident_spec(w_out.shape), _const_spec((1, d)),
            _const_spec((1, d)), _const_spec((1, d)),
            _resident_spec((d, d_ff)), _resident_spec((d, d_ff)), _resident_spec((d_ff, d)),
        ],
        out_specs=tok(d),
        out_shape=jax.ShapeDtypeStruct((m, d), F32),
        scratch_shapes=[pltpu.VMEM((tm, d), F32)],
        compiler_params=_cparams(1),
        name="outproj_ffn",
    )(x2, o1, o2, w_out, mix_post.reshape(1, d), pre.reshape(1, d), post.reshape(1, d), wg, wu, wd)


def _t5_bucket(dist):
    n = jnp.maximum(dist, 0)
    exact = REL_BUCKETS // 2
    nf = jnp.maximum(n, 1).astype(jnp.float32)
    large = exact + (jnp.log(nf / exact) / math.log(REL_MAX_DIST / exact) * (REL_BUCKETS - exact)).astype(jnp.int32)
    return jnp.where(n < exact, n, jnp.minimum(large, REL_BUCKETS - 1))


def _bias_body(tab_ref, idx_ref, *o_refs, first_slots):
    head = pl.program_id(0)
    for o_ref, first in zip(o_refs, first_slots):
        column = jnp.broadcast_to(LOG2E * tab_ref[pl.ds(first + head, 1), :], (TILE, LANES))
        for d in range(o_ref.shape[1]):
            o_ref[0, d] = jnp.take_along_axis(column, idx_ref[d], axis=1)


def _bias_tiles(rel_table, groups):
    heads = 8
    d = jnp.arange(BIAS_TILES)[:, None, None]
    key = jnp.arange(TILE)[None, :, None]
    query = jnp.arange(TILE)[None, None, :]
    idx = _t5_bucket(d * TILE + query - key).astype(jnp.int32)
    n_buckets, n_slots = rel_table.shape
    table_t = jnp.pad(rel_table.astype(F32).T, ((0, 0), (0, LANES - n_buckets)))
    return pl.pallas_call(
        functools.partial(_bias_body, first_slots=tuple(first for first, _ in groups)),
        grid=(heads,),
        in_specs=[_const_spec((n_slots, LANES)), _const_spec((BIAS_TILES, TILE, TILE))],
        out_specs=[pl.BlockSpec((1, n, TILE, TILE), lambda s: (s, 0, 0, 0)) for _, n in groups],
        out_shape=[jax.ShapeDtypeStruct((heads, n, TILE, TILE), F32) for _, n in groups],
        compiler_params=_cparams(1),
        name="bias_tiles",
    )(table_t, idx)


def _proj0_body(x_ref, pre_ref, w_ref, qn_ref, wq1_ref, wq2_ref, kvn_ref, wk_ref, wv_ref, cos_ref, sin_ref,
                qa_ref, ka_ref, vat_ref, qb_ref, kb_ref, vbt_ref):
    h = _rms(x_ref[...], pre_ref[...]).astype(BF16)
    y = _dot(h, w_ref[...])
    qa, ka, va, cq, ckv, rot = (y[:, sl] for sl in _col_slices(P0_COLS))
    qa_ref[...] = (qa * (LOG2E * HEAD_DIM ** -0.5)).astype(BF16)
    ka_ref[...] = ka.astype(BF16)
    for j in range(y.shape[0] // TILE):
        _store_vt_ones(vat_ref, j, va[j * TILE:(j + 1) * TILE], A_KV_HEADS)
    cq = _rms(cq, qn_ref[...]).astype(BF16)
    ckv = _rms(ckv, kvn_ref[...]).astype(BF16)
    cos_t = cos_ref[...]
    sin_t = sin_ref[...]
    lane = lax.broadcasted_iota(jnp.int32, cos_t.shape, 1)
    scale = LOG2E * (MLA_NOPE + MLA_ROPE) ** -0.5
    qmul = scale * (cos_t + jnp.where(lane < MLA_NOPE, 1.0, 0.0))
    qsin = scale * sin_t
    kr = rot * cos_t + pltpu.roll(rot, LANES // 2, 1) * sin_t
    q1 = _dot(cq, wq1_ref[...])
    q2 = _dot(cq, wq2_ref[...])
    k1 = _dot(ckv, wk_ref[...])
    for hd in range(MLA_HEADS):
        sl = slice(hd * LANES, (hd + 1) * LANES)
        qb_ref[:, sl] = (q1[:, sl] * qmul + q2[:, sl] * qsin).astype(BF16)
        kb_ref[:, sl] = (k1[:, sl] + kr).astype(BF16)
    vb = _dot(ckv, wv_ref[...])
    for j in range(vb.shape[0] // BLK):
        _store_vt_ones(vbt_ref, j, vb[j * BLK:(j + 1) * BLK], MLA_HEADS)


def _rope_lane_tables(s):
    inv = ROPE_THETA ** (-jnp.arange(0, MLA_ROPE, 2, dtype=jnp.float32) / MLA_ROPE)
    ang = jnp.arange(s, dtype=jnp.float32)[:, None] * inv[None, :]
    cos, sin = jnp.cos(ang), jnp.sin(ang)
    z_lo = jnp.zeros((s, MLA_NOPE), F32)
    z_hi = jnp.zeros((s, LANES - MLA_NOPE - MLA_ROPE), F32)
    return (jnp.concatenate([z_lo, cos, cos, z_hi], axis=1), jnp.concatenate([z_lo, sin, sin, z_hi], axis=1))


def _rot_cols(w):
    half = w.shape[-1] // 2
    return jnp.concatenate([-w[..., half:], w[..., :half]], axis=-1)


def _proj0(x2, seq, pre, w_in, q_norm, w_uq, kv_norm, w_ukv):
    m, d = x2.shape
    tm = PROJ_TILE
    n_lin = sum(P0_COLS[:-1])
    kr_w = w_in[:, n_lin:n_lin + MLA_ROPE]
    z32 = jnp.zeros((d, MLA_ROPE), F32)
    w0 = jnp.concatenate([w_in[:, :n_lin], _rot_cols(kr_w), z32, kr_w, z32], axis=1).astype(BF16)
    wq = w_uq.reshape(-1, MLA_HEADS, MLA_NOPE + MLA_ROPE)
    rq = wq.shape[0]
    pad = LANES - MLA_NOPE - MLA_ROPE
    zq = jnp.zeros((rq, MLA_HEADS, pad), F32)
    wq1 = jnp.concatenate([wq, zq], axis=2).reshape(rq, -1).astype(BF16)
    wq2 = jnp.concatenate([jnp.zeros((rq, MLA_HEADS, MLA_NOPE), F32), _rot_cols(wq[:, :, MLA_NOPE:]), zq],
                          axis=2).reshape(rq, -1).astype(BF16)
    wkv = w_ukv.reshape(-1, MLA_HEADS, MLA_NOPE + MLA_V)
    rk = wkv.shape[0]
    wk = jnp.concatenate([wkv[:, :, :MLA_NOPE], jnp.zeros((rk, MLA_HEADS, LANES - MLA_NOPE), F32)],
                         axis=2).reshape(rk, -1).astype(BF16)
    wv = wkv[:, :, MLA_NOPE:].reshape(rk, -1).astype(BF16)
    cos_t, sin_t = _rope_lane_tables(seq)
    n_st = seq // tm
    tok = lambda n: pl.BlockSpec((tm, n), lambda i: (i, 0))
    outs = pl.pallas_call(
        _proj0_body,
        grid=(m // tm,),
        in_specs=[
            tok(d), _const_spec((1, d)), _const_spec(w0.shape),
            _const_spec((1, rq)), _const_spec(wq1.shape), _const_spec(wq2.shape),
            _const_spec((1, rk)), _const_spec(wk.shape), _const_spec(wv.shape),
            pl.BlockSpec((tm, LANES), lambda i: (i % n_st, 0)),
            pl.BlockSpec((tm, LANES), lambda i: (i % n_st, 0)),
        ],
        out_specs=[tok(P0_COLS[0]), tok(P0_COLS[1]),
                   pl.BlockSpec((1, tm // TILE, A_KV_HEADS * VT_ROWS, TILE), lambda i: (i // n_st, i % n_st, 0, 0)),
                   tok(MLA_HEADS * LANES), tok(MLA_HEADS * LANES),
                   pl.BlockSpec((1, tm // BLK, MLA_HEADS * VT_ROWS, BLK), lambda i: (i // n_st, i % n_st, 0, 0))],
        out_shape=[jax.ShapeDtypeStruct((m, P0_COLS[0]), BF16), jax.ShapeDtypeStruct((m, P0_COLS[1]), BF16),
                   jax.ShapeDtypeStruct((m // seq, seq // TILE, A_KV_HEADS * VT_ROWS, TILE), BF16),
                   jax.ShapeDtypeStruct((m, MLA_HEADS * LANES), BF16),
                   jax.ShapeDtypeStruct((m, MLA_HEADS * LANES), BF16),
                   jax.ShapeDtypeStruct((m // seq, seq // BLK, MLA_HEADS * VT_ROWS, BLK), BF16)],
        compiler_params=_cparams(1),
        name="proj0",
    )(x2, pre.reshape(1, d), w0, q_norm.reshape(1, rq), wq1, wq2, kv_norm.reshape(1, rk), wk, wv, cos_t, sin_t)
    return outs


def _swa_tile(qi, sink_ref, q_ref, k_ref, vt_ref, bias_ref, o_ref, *, n_kv, grp):
    halves = BLK // TILE
    items = [(hkv, half) for hkv in range(n_kv) for half in range(halves)]
    width = grp * TILE
    krow = lax.broadcasted_iota(jnp.int32, (TILE, len(items) * width), 0)
    qcol = lax.broadcasted_iota(jnp.int32, (TILE, len(items) * width), 1) & (TILE - 1)

    cur_s, prev_s, sink_parts, has_prev = [], [], [], []
    for hkv, half in items:
        sl = slice(hkv * HEAD_DIM, (hkv + 1) * HEAD_DIM)
        t = qi * halves + half
        tp = jnp.maximum(t - 1, 0)
        q_rows = pl.ds(pl.multiple_of(t * TILE, TILE), TILE)
        qs = jnp.concatenate([q_ref[0, q_rows, (hkv * grp + g) * HEAD_DIM:(hkv * grp + g + 1) * HEAD_DIM]
                              for g in range(grp)], axis=0)
        bias_c = jnp.concatenate([bias_ref[hkv * grp + g, 0] for g in range(grp)], axis=1)
        bias_p = jnp.concatenate([bias_ref[hkv * grp + g, 1] for g in range(grp)], axis=1)
        cur_s.append(_dot_nt(k_ref[0, pl.ds(pl.multiple_of(t * TILE, TILE), TILE), sl], qs) + bias_c)
        prev_s.append(_dot_nt(k_ref[0, pl.ds(pl.multiple_of(tp * TILE, TILE), TILE), sl], qs) + bias_p)
        sink_parts.extend(jnp.full((1, TILE), LOG2E * sink_ref[hkv * grp + g], F32) for g in range(grp))
        has_prev.append(jnp.full((1, width), jnp.where(t >= 1, 0, TILE), jnp.int32))
    s_c = jnp.where(krow <= qcol, jnp.concatenate(cur_s, axis=1), NEG_INF)
    s_p = jnp.where(krow > qcol + jnp.concatenate(has_prev, axis=1),
                    jnp.concatenate(prev_s, axis=1), NEG_INF)
    sink = jnp.concatenate(sink_parts, axis=1)
    mx = jnp.maximum(jnp.maximum(jnp.max(s_c, axis=0, keepdims=True), jnp.max(s_p, axis=0, keepdims=True)), sink)
    p_c = jnp.exp2(s_c - mx).astype(BF16)
    p_p = jnp.exp2(s_p - mx).astype(BF16)
    accs = []
    for i, (hkv, half) in enumerate(items):
        vsl = slice(hkv * VT_ROWS, (hkv + 1) * VT_ROWS)
        ls = slice(i * width, (i + 1) * width)
        t = qi * halves + half
        accs.append(_dot(vt_ref[0, t, vsl, :], p_c[:, ls]) + _dot(vt_ref[0, jnp.maximum(t - 1, 0), vsl, :], p_p[:, ls]))
    acc = jnp.concatenate(accs, axis=1)
    o_t = acc[0:HEAD_DIM] / (acc[HEAD_DIM:HEAD_DIM + 1] + jnp.exp2(sink - mx))
    heads = []
    for hkv in range(n_kv):
        for g in range(grp):
            heads.append(jnp.concatenate(
                [o_t[:, (hkv * halves + half) * width + g * TILE:(hkv * halves + half) * width + (g + 1) * TILE]
                 for half in range(halves)], axis=1))
    o_ref[0, _tile_rows(qi), :] = jnp.concatenate(heads, axis=0).T.astype(BF16)


def _seq_spec(shape):
    nd = len(shape)
    return pl.BlockSpec((1,) + tuple(shape[1:]), lambda bi: (bi,) + (0,) * (nd - 1))


def _swa(qa, ka, vat, bias_a, sinks):
    b, s, width = qa.shape
    n_kv = ka.shape[-1] // HEAD_DIM
    grp = width // HEAD_DIM // n_kv
    return pl.pallas_call(
        _per_query_tile(functools.partial(_swa_tile, n_kv=n_kv, grp=grp), s // BLK),
        grid=(b,),
        in_specs=[pl.BlockSpec(memory_space=pltpu.SMEM), _seq_spec(qa.shape), _seq_spec(ka.shape),
                  _seq_spec(vat.shape), _const_spec(bias_a.shape)],
        out_specs=_seq_spec(qa.shape),
        out_shape=jax.ShapeDtypeStruct((b, s, width), BF16),
        compiler_params=_cparams(1),
        name="swa",
    )(sinks.astype(F32), qa, ka, vat, bias_a)


def _finish(acc, dv):
    return acc[0:dv] / acc[dv:dv + 1]


def _mla_tile(qi, q_ref, k_ref, vt_ref, o_ref, acc_ref, s_ref, *, n_heads):
    q_rows = _tile_rows(qi)
    lanes = n_heads * BLK
    krow = lax.broadcasted_iota(jnp.int32, (BLK, lanes), 0)
    qcol = lax.broadcasted_iota(jnp.int32, (BLK, lanes), 1) & (BLK - 1)

    def scores(n):
        st = pl.multiple_of(n * BLK, BLK)
        return jnp.concatenate(
            [_dot_nt(k_ref[0, pl.ds(st, BLK), hd * LANES:(hd + 1) * LANES], q_ref[0, q_rows, hd * LANES:(hd + 1) * LANES])
             for hd in range(n_heads)], axis=1)

    def weighted_values(n, p):
        return jnp.concatenate(
            [_dot(vt_ref[0, n, hd * VT_ROWS:(hd + 1) * VT_ROWS, :], p[:, hd * BLK:(hd + 1) * BLK])
             for hd in range(n_heads)], axis=1)

    def produce_first(slot):
        s = jnp.where(krow <= qcol, scores(qi), NEG_INF)
        s_ref[slot] = s
        return jnp.max(s, axis=0, keepdims=True)

    def produce(i, slot):
        s = scores(i - 1)
        s_ref[slot] = s
        return jnp.max(s, axis=0, keepdims=True)

    def consume(i, slot, block_max, m):
        n = jnp.where(i == 0, qi, i - 1)
        m_new = jnp.maximum(m, block_max)
        p = jnp.exp2(s_ref[slot] - m_new).astype(BF16)
        acc_ref[...] = jnp.exp2(m - m_new) * acc_ref[...] + weighted_values(n, p)
        return m_new

    acc_ref[...] = jnp.zeros_like(acc_ref)
    _pipelined_blocks(qi + 1, produce_first, produce, consume, jnp.full((1, lanes), NEG_INF, F32))
    o_t = _finish(acc_ref[...], MLA_V)
    o_ref[0, q_rows, :] = jnp.concatenate([o_t[:, hd * BLK:(hd + 1) * BLK] for hd in range(n_heads)],
                                          axis=0).T.astype(BF16)


def _mla(qb, kb, vbt):
    b, s, width = qb.shape
    n_heads = width // LANES
    out_shape = jax.ShapeDtypeStruct((b, s, n_heads * MLA_V), BF16)
    return pl.pallas_call(
        _per_query_tile(functools.partial(_mla_tile, n_heads=n_heads), s // BLK),
        grid=(b,),
        in_specs=[_seq_spec(qb.shape), _seq_spec(kb.shape), _seq_spec(vbt.shape)],
        out_specs=_seq_spec(out_shape.shape),
        out_shape=out_shape,
        scratch_shapes=[pltpu.VMEM((VT_ROWS, n_heads * BLK), F32), pltpu.VMEM((2, BLK, n_heads * BLK), F32)],
        compiler_params=_cparams(1),
        name="mla",
    )(qb, kb, vbt)


def _store_vt_ones(vt_ref, j, v, n_heads):
    vt = v.T.astype(BF16)
    ones = jnp.ones((ONES_ROWS, v.shape[0]), BF16)
    for hd in range(n_heads):
        vt_ref[0, j, hd * VT_ROWS:hd * VT_ROWS + HEAD_DIM, :] = vt[hd * HEAD_DIM:(hd + 1) * HEAD_DIM]
        vt_ref[0, j, hd * VT_ROWS + HEAD_DIM:(hd + 1) * VT_ROWS, :] = ones


def _proj1_body(x_ref, pre_ref, w_ref, qc_ref, kx_ref, vx_ref, ks_ref, vst_ref, kw_ref, vwt_ref, g_ref,
                qd_ref, kd_ref, vdt_ref, km_ref, kstage_ref, vstage_ref, *, tm):
    h = _rms(x_ref[...], pre_ref[...]).astype(BF16)
    y = _dot(h, w_ref[...])
    qc, kc, vc, ks, vs, kw, vw, gl, qd, kd, vd = (y[:, sl] for sl in _col_slices(P1_COLS))
    scale = LOG2E * HEAD_DIM ** -0.5
    qc_ref[...] = (qc * scale).astype(BF16)
    for stage_ref, cols, x_out in ((kstage_ref, kc, kx_ref), (vstage_ref, vc, vx_ref)):
        stage_ref[...] = cols
        for i in range(CMP_STRIDE):
            piece = stage_ref[pl.ds(i, tm // CMP_STRIDE, stride=CMP_STRIDE), :].astype(BF16)
            for hkv in range(C_KV_HEADS):
                x_out[0, hkv, :, i * HEAD_DIM:(i + 1) * HEAD_DIM] = piece[:, hkv * HEAD_DIM:(hkv + 1) * HEAD_DIM]
    ks_ref[...] = ks.astype(BF16)
    kw_ref[...] = kw.astype(BF16)
    g_ref[...] = 1.0 / (1.0 + jnp.exp(-gl))
    qd_ref[...] = (qd * scale).astype(BF16)
    kd_ref[...] = kd.astype(BF16)
    for j in range(tm // BLK):
        rows = slice(j * BLK, (j + 1) * BLK)
        km_ref[j] = jnp.mean(kd[rows], axis=0, keepdims=True)
        _store_vt_ones(vdt_ref, j, vd[rows], D_HEADS)
        _store_vt_ones(vst_ref, j, vs[rows], C_KV_HEADS)
        _store_vt_ones(vwt_ref, j, vw[rows], C_KV_HEADS)


def _proj1(x2, batch, seq, pre, w_in):
    m, d = x2.shape
    tm = PROJ_TILE
    n_st = seq // tm
    n_pre = sum(P1_COLS[:7]) + N_GATES
    pad = jnp.zeros((d, LANES - N_GATES), F32)
    w1 = jnp.concatenate([w_in[:, :n_pre], pad, w_in[:, n_pre:]], axis=1).astype(BF16)
    q_w, kv_w, d_w = C_HEADS * HEAD_DIM, C_KV_HEADS * HEAD_DIM, D_HEADS * HEAD_DIM
    tok = lambda n: pl.BlockSpec((tm, n), lambda i: (i, 0))
    tok_shape = lambda n, dt=BF16: jax.ShapeDtypeStruct((m, n), dt)
    feat = CMP_STRIDE * HEAD_DIM
    hm = pl.BlockSpec((1, C_KV_HEADS, tm // CMP_STRIDE, feat), lambda i: (i // n_st, 0, i % n_st, 0))
    hm_shape = jax.ShapeDtypeStruct((batch, C_KV_HEADS, seq // CMP_STRIDE, feat), BF16)
    nblk = tm // BLK
    vt_spec = lambda heads: pl.BlockSpec((1, nblk, heads * VT_ROWS, BLK), lambda i: (i // n_st, i % n_st, 0, 0))
    vt_shape = lambda heads: jax.ShapeDtypeStruct((batch, seq // BLK, heads * VT_ROWS, BLK), BF16)
    return pl.pallas_call(
        functools.partial(_proj1_body, tm=tm),
        grid=(m // tm,),
        in_specs=[tok(d), _const_spec((1, d)), _const_spec(w1.shape)],
        out_specs=[tok(q_w), hm, hm, tok(kv_w), vt_spec(C_KV_HEADS), tok(kv_w), vt_spec(C_KV_HEADS), tok(LANES),
                   tok(d_w), tok(d_w), vt_spec(D_HEADS),
                   pl.BlockSpec((nblk, 1, d_w), lambda i: (i, 0, 0))],
        out_shape=[
            tok_shape(q_w), hm_shape, hm_shape,
            tok_shape(kv_w), vt_shape(C_KV_HEADS), tok_shape(kv_w), vt_shape(C_KV_HEADS), tok_shape(LANES, F32),
            tok_shape(d_w), tok_shape(d_w), vt_shape(D_HEADS),
            jax.ShapeDtypeStruct((m // BLK, 1, d_w), F32),
        ],
        scratch_shapes=[pltpu.VMEM((tm, kv_w), F32), pltpu.VMEM((tm, kv_w), F32)],
        compiler_params=_cparams(1),
        name="proj1",
    )(x2, pre.reshape(1, d), w1)


def _compress_body(kx_ref, vx_ref, pek_ref, pev_ref, wk1_ref, wk2_ref, wv1_ref, wv2_ref, ko_ref, vo_ref):
    half = CMP_STRIDE * HEAD_DIM
    for x_ref, pe_ref, w1_ref, w2_ref, o_ref in ((kx_ref, pek_ref, wk1_ref, wk2_ref, ko_ref),
                                                 (vx_ref, pev_ref, wv1_ref, wv2_ref, vo_ref)):
        pe8 = jnp.broadcast_to(pe_ref[...], (SUBLANES, 2 * half)).astype(BF16)
        pe_term = _dot(pe8, w1_ref[...])[0:1]
        outs = []
        for hkv in range(C_KV_HEADS):
            x = x_ref[0, hkv]
            lo = _dot(x, w1_ref[0:half])
            hi = _dot(x, w1_ref[half:2 * half])
            hid = lo + pltpu.roll(hi, x.shape[0] - 1, 0) + pe_term
            outs.append(_dot(_silu(hid).astype(BF16), w2_ref[...]))
        out = jnp.concatenate(outs, axis=1)
        o_ref[0] = (out.T if o_ref is vo_ref else out).astype(BF16)


def _compress(kx, vx, pe_k, pe_v, wk1, wk2, wv1, wv2):
    b, _, rows, feat = kx.shape
    dh = feat // CMP_STRIDE
    xspec = pl.BlockSpec((1, 2, rows, feat), lambda bi: (bi, 0, 0, 0))
    ospec = pl.BlockSpec((1, rows, 2 * dh), lambda bi: (bi, 0, 0))
    w1s, w2s = wk1.shape, wk2.shape
    return pl.pallas_call(
        _compress_body,
        grid=(b,),
        in_specs=[xspec, xspec, _const_spec((1, 2 * feat)), _const_spec((1, 2 * feat)),
                  _const_spec(w1s), _const_spec(w2s), _const_spec(w1s), _const_spec(w2s)],
        out_specs=[ospec, ospec],
        out_shape=[jax.ShapeDtypeStruct((b, rows, 2 * dh), BF16)] * 2,
        compiler_params=_cparams(1),
        name="nsa_compress",
    )(kx, vx, pe_k.astype(F32).reshape(1, -1), pe_v.astype(F32).reshape(1, -1),
      wk1.astype(BF16), wk2.astype(BF16), wv1.astype(BF16), wv2.astype(BF16))


def _bias_block(bias_ref, hd, dist_blocks):
    sub = BLK // TILE
    rows = []
    for a in range(sub):
        cols = []
        for bq in range(sub):
            e = sub * dist_blocks + bq - a
            e = (min(max(e, 0), BIAS_TILES - 1) if isinstance(e, int)
                 else jnp.minimum(jnp.maximum(e, 0), BIAS_TILES - 1))
            cols.append(bias_ref[hd, e])
        rows.append(jnp.concatenate(cols, axis=1))
    return jnp.concatenate(rows, axis=0)


def _topk_rows(val, ridx, k):
    n_rows = val.shape[0]
    picked = jnp.zeros(val.shape, F32)
    for _ in range(k):
        best = jnp.max(val, axis=0, keepdims=True)
        first = jnp.min(jnp.where(val == best, ridx, n_rows), axis=0, keepdims=True)
        hit = ridx == first
        picked = jnp.where(hit, 1.0, picked)
        val = jnp.where(hit, PAD_SCORE, val)
    return picked


def _nsa_tile(qi, q_ref, kc_ref, vct_ref, ks_ref, vst_ref, kw_ref, vwt_ref, g_ref, bias_ref, ovt_ref, o_ref,
              pen_ref, acc_ref, s_ref, *, n_sel):
    q_rows = _tile_rows(qi)
    grp = C_HEADS // C_KV_HEADS
    lanes = grp * BLK
    sub = BLK // SEL_BLOCK
    own = pl.multiple_of(qi * BLK, BLK)
    prev = pl.multiple_of(jnp.maximum(qi - 1, 0) * BLK, BLK)
    prev_blk = jnp.maximum(qi - 1, 0)
    krow = lax.broadcasted_iota(jnp.int32, (BLK, lanes), 0)
    qcol = lax.broadcasted_iota(jnp.int32, (BLK, lanes), 1) & (BLK - 1)
    causal = krow <= qcol
    crow = lax.broadcasted_iota(jnp.int32, (TILE, lanes), 0)
    cq = lax.broadcasted_iota(jnp.int32, (TILE, lanes), 1) & (BLK - 1)
    visible = CMP_STRIDE * crow + (CMP_BLOCK - 1) <= qi * BLK + cq
    jrow = lax.broadcasted_iota(jnp.int32, (n_sel, BLK), 0)
    cur = qi * sub + lax.broadcasted_iota(jnp.int32, (n_sel, BLK), 1) // SEL_BLOCK
    forced = (jrow == 0) | (jrow == cur) | (jrow == cur - 1)
    gates_t = g_ref[0, q_rows, :].T
    has_prev = jnp.where(qi >= 1, 0, BLK)

    def stack_q(hkv):
        return jnp.concatenate(
            [q_ref[0, q_rows, (hkv * grp + g) * HEAD_DIM:(hkv * grp + g + 1) * HEAD_DIM] for g in range(grp)], axis=0)

    def group_bias(hkv, dist_blocks):
        return jnp.concatenate([_bias_block(bias_ref, hkv * grp + g, dist_blocks) for g in range(grp)], axis=1)

    def sel_scores(hkv, n, dist_blocks):
        st = pl.multiple_of(n * BLK, BLK)
        sl = slice(hkv * HEAD_DIM, (hkv + 1) * HEAD_DIM)
        return _dot_nt(ks_ref[0, pl.ds(st, BLK), sl], stack_q(hkv)) + group_bias(hkv, dist_blocks)

    def sel_pens(hkv, n):
        pens = pen_ref[hkv, n, 0:sub, :]
        return [jnp.concatenate([pens[j:j + 1]] * grp, axis=1) for j in range(sub)]

    def sel_max(s, pens):
        mj = [jnp.max(s[j * SEL_BLOCK:(j + 1) * SEL_BLOCK], axis=0, keepdims=True) + pens[j] for j in range(sub)]
        return functools.reduce(jnp.maximum, mj)

    def sel_probs(s, pens, m_new):
        return jnp.concatenate([jnp.exp2(s[j * SEL_BLOCK:(j + 1) * SEL_BLOCK] - (m_new - pens[j]))
                                for j in range(sub)], axis=0).astype(BF16)

    o_cmp, o_win = [], []
    for hkv in range(C_KV_HEADS):
        sl = slice(hkv * HEAD_DIM, (hkv + 1) * HEAD_DIM)
        vsl = slice(hkv * VT_ROWS, (hkv + 1) * VT_ROWS)
        qs = stack_q(hkv)

        sc = jnp.where(visible, _dot_nt(kc_ref[0, :, sl], qs), NEG_INF)
        mc = jnp.max(sc, axis=0, keepdims=True)
        ec = jnp.where(visible, jnp.exp2(sc - mc), 0.0)
        lc = jnp.sum(ec, axis=0, keepdims=True)
        p = ec / jnp.where(lc > 0.0, lc, 1.0)
        o_cmp.append(_dot(vct_ref[0, sl, :], p.astype(BF16)))

        ps = p[:, 0:BLK] + p[:, BLK:2 * BLK] + p[:, 2 * BLK:3 * BLK] + p[:, 3 * BLK:4 * BLK]
        ps_hi = ps.astype(BF16)
        ps_lo = (ps - ps_hi.astype(F32)).astype(BF16)
        imp = _dot(ovt_ref[...], ps_hi) + _dot(ovt_ref[...], ps_lo)
        val = jnp.where(forced, FORCE_SCORE, jnp.where(jrow <= cur, imp, NEG_INF))
        pen = jnp.where(_topk_rows(val, jrow, SEL_TOPK) > 0.5, 0.0, NEG_INF)
        for t in range(n_sel // sub):
            pen_ref[hkv, t, 0:sub, :] = pen[t * sub:(t + 1) * sub]

        s0 = _dot_nt(kw_ref[0, pl.ds(own, BLK), sl], qs) + group_bias(hkv, 0)
        s1 = _dot_nt(kw_ref[0, pl.ds(prev, BLK), sl], qs) + group_bias(hkv, 1)
        s0 = jnp.where(causal, s0, NEG_INF)
        s1 = jnp.where(krow > qcol + has_prev, s1, NEG_INF)
        mw = jnp.maximum(jnp.max(s0, axis=0, keepdims=True), jnp.max(s1, axis=0, keepdims=True))
        ow = (_dot(vwt_ref[0, qi, vsl, :], jnp.exp2(s0 - mw).astype(BF16))
              + _dot(vwt_ref[0, prev_blk, vsl, :], jnp.exp2(s1 - mw).astype(BF16)))
        o_win.append(_finish(ow, HEAD_DIM))

    def produce_first(slot):
        stats = []
        for hkv in range(C_KV_HEADS):
            s = jnp.where(causal, sel_scores(hkv, qi, 0), NEG_INF)
            s_ref[slot, hkv] = s
            stats.append(sel_max(s, sel_pens(hkv, qi)))
        return tuple(stats)

    def produce(i, slot):
        stats = []
        for hkv in range(C_KV_HEADS):
            s = sel_scores(hkv, i - 1, qi - (i - 1))
            s_ref[slot, hkv] = s
            stats.append(sel_max(s, sel_pens(hkv, i - 1)))
        return tuple(stats)

    def consume(i, slot, block_max, ms):
        n = jnp.where(i == 0, qi, i - 1)
        new = []
        for hkv in range(C_KV_HEADS):
            vsl = slice(hkv * VT_ROWS, (hkv + 1) * VT_ROWS)
            m_new = jnp.maximum(ms[hkv], block_max[hkv])
            pb = sel_probs(s_ref[slot, hkv], sel_pens(hkv, n), m_new)
            acc_ref[hkv] = jnp.exp2(ms[hkv] - m_new) * acc_ref[hkv] + _dot(vst_ref[0, n, vsl, :], pb)
            new.append(m_new)
        return tuple(new)

    acc_ref[...] = jnp.zeros_like(acc_ref)
    m_init = jnp.full((1, lanes), NEG_INF, F32)
    _pipelined_blocks(qi + 1, produce_first, produce, consume, (m_init, m_init))
    outs = []
    for hkv in range(C_KV_HEADS):
        o_slc = _finish(acc_ref[hkv], HEAD_DIM)
        for g in range(grp):
            hd = hkv * grp + g
            ls = slice(g * BLK, (g + 1) * BLK)
            g_cmp, g_slc, g_win = (gates_t[t * C_HEADS + hd:t * C_HEADS + hd + 1] for t in range(N_GATES // C_HEADS))
            outs.append(g_cmp * o_cmp[hkv][:, ls] + g_slc * o_slc[:, ls] + g_win * o_win[hkv][:, ls])
    o_ref[0, q_rows, :] = jnp.concatenate(outs, axis=0).T.astype(BF16)


def _nsa(qc, kcmp, vcmpt, ks, vst, kw, vwt, gates, bias_c):
    b, s, width = qc.shape
    n_blocks = s // BLK
    n_sel = s // SEL_BLOCK
    n_c = (s - CMP_BLOCK) // CMP_STRIDE + 1
    j_start = np.arange(n_sel)[:, None] * SEL_BLOCK
    c_start = np.arange(TILE)[None, :] * CMP_STRIDE
    overlap_t = ((c_start < j_start + SEL_BLOCK) & (c_start + CMP_BLOCK > j_start)
                 & (np.arange(TILE)[None, :] < n_c))
    ovt = jnp.asarray(overlap_t, BF16)
    group_lanes = C_HEADS // C_KV_HEADS * BLK
    operands = (qc, kcmp, vcmpt, ks, vst, kw, vwt, gates)
    return pl.pallas_call(
        _per_query_tile(functools.partial(_nsa_tile, n_sel=n_sel), n_blocks),
        grid=(b,),
        in_specs=[_seq_spec(t.shape) for t in operands] + [_const_spec(bias_c.shape), _const_spec(ovt.shape)],
        out_specs=_seq_spec(qc.shape),
        out_shape=jax.ShapeDtypeStruct((b, s, width), BF16),
        scratch_shapes=[pltpu.VMEM((C_KV_HEADS, n_blocks, SUBLANES, BLK), F32),
                        pltpu.VMEM((C_KV_HEADS, VT_ROWS, group_lanes), F32),
                        pltpu.VMEM((2, C_KV_HEADS, BLK, group_lanes), F32)],
        compiler_params=_cparams(1),
        name="nsa",
    )(*operands, bias_c, ovt)


def _moba_tile(qi, q_ref, k_ref, vt_ref, km_ref, bias_ref, o_ref, pen_ref, acc_ref, s_ref, *, n_heads, n_blocks):
    q_rows = _tile_rows(qi)
    lanes = n_heads * BLK
    nidx = lax.broadcasted_iota(jnp.int32, (GATE_ROWS, lanes), 0)
    krow = lax.broadcasted_iota(jnp.int32, (BLK, lanes), 0)
    qcol = lax.broadcasted_iota(jnp.int32, (BLK, lanes), 1) & (BLK - 1)

    def scores(n, dist_blocks):
        st = pl.multiple_of(n * BLK, BLK)
        parts = []
        for hd in range(n_heads):
            sl = slice(hd * HEAD_DIM, (hd + 1) * HEAD_DIM)
            parts.append(_dot_nt(k_ref[0, pl.ds(st, BLK), sl], q_ref[0, q_rows, sl])
                         + _bias_block(bias_ref, hd, dist_blocks))
        return jnp.concatenate(parts, axis=1)

    def weighted_values(n, p):
        return jnp.concatenate(
            [_dot(vt_ref[0, n, hd * VT_ROWS:(hd + 1) * VT_ROWS, :], p[:, hd * BLK:(hd + 1) * BLK])
             for hd in range(n_heads)], axis=1)

    zpad = jnp.zeros((GATE_ROWS - n_blocks, HEAD_DIM), F32)
    gate = jnp.concatenate(
        [_dot_nt(jnp.concatenate([km_ref[0, :, hd * HEAD_DIM:(hd + 1) * HEAD_DIM], zpad], axis=0).astype(BF16),
                 q_ref[0, q_rows, hd * HEAD_DIM:(hd + 1) * HEAD_DIM]) for hd in range(n_heads)], axis=1)
    val = jnp.where(nidx < qi, gate, NEG_INF)
    val = jnp.where(nidx < n_blocks, val, PAD_SCORE)
    keep = ((_topk_rows(val, nidx, MOBA_TOPK) > 0.5) & (nidx < qi)) | (nidx == qi)
    pen_ref[...] = jnp.where(keep, 0.0, NEG_INF)

    def produce_first(slot):
        s = jnp.where(krow <= qcol, scores(qi, 0), NEG_INF)
        s_ref[slot] = s
        return jnp.max(s, axis=0, keepdims=True)

    def produce(i, slot):
        s = scores(i - 1, qi - (i - 1))
        s_ref[slot] = s
        return jnp.max(s, axis=0, keepdims=True) + pen_ref[pl.ds(i - 1, 1), :]

    def consume(i, slot, block_max, m):
        n = jnp.where(i == 0, qi, i - 1)
        pen = pen_ref[pl.ds(n, 1), :]
        m_new = jnp.maximum(m, block_max)
        p = jnp.exp2(s_ref[slot] - (m_new - pen)).astype(BF16)
        acc_ref[...] = jnp.exp2(m - m_new) * acc_ref[...] + weighted_values(n, p)
        return m_new

    acc_ref[...] = jnp.zeros_like(acc_ref)
    _pipelined_blocks(qi + 1, produce_first, produce, consume, jnp.full((1, lanes), NEG_INF, F32))
    o_t = _finish(acc_ref[...], HEAD_DIM)
    o_ref[0, q_rows, :] = jnp.concatenate([o_t[:, hd * BLK:(hd + 1) * BLK] for hd in range(n_heads)],
                                          axis=0).T.astype(BF16)


def _moba(qd, kd, vdt, kmean, bias_d):
    b, s, width = qd.shape
    n_heads = width // HEAD_DIM
    n_blocks = s // BLK
    operands = (qd, kd, vdt, kmean.reshape(b, n_blocks, width))
    return pl.pallas_call(
        _per_query_tile(functools.partial(_moba_tile, n_heads=n_heads, n_blocks=n_blocks), n_blocks),
        grid=(b,),
        in_specs=[_seq_spec(t.shape) for t in operands] + [_const_spec(bias_d.shape)],
        out_specs=_seq_spec(qd.shape),
        out_shape=jax.ShapeDtypeStruct((b, s, width), BF16),
        scratch_shapes=[pltpu.VMEM((GATE_ROWS, n_heads * BLK), F32), pltpu.VMEM((VT_ROWS, n_heads * BLK), F32),
                        pltpu.VMEM((2, BLK, n_heads * BLK), F32)],
        compiler_params=_cparams(1),
        name="moba",
    )(*operands, bias_d)


def kernel(x, rel_bias_table, l0_ffn1_pre, l0_ffn1_post, l0_ffn1_wg, l0_ffn1_wu, l0_ffn1_wd, l0_mix_pre, l0_mix_post, l0_w_in, l0_sinks, l0_mla_q_norm, l0_mla_w_uq, l0_mla_kv_norm, l0_mla_w_ukv, l0_w_out, l0_ffn2_pre, l0_ffn2_post, l0_ffn2_wg, l0_ffn2_wu, l0_ffn2_wd, l1_ffn1_pre, l1_ffn1_post, l1_ffn1_wg, l1_ffn1_wu, l1_ffn1_wd, l1_mix_pre, l1_mix_post, l1_w_in, l1_nsa_pe_k, l1_nsa_pe_v, l1_nsa_wk1, l1_nsa_wk2, l1_nsa_wv1, l1_nsa_wv2, l1_w_out, l1_ffn2_pre, l1_ffn2_post, l1_ffn2_wg, l1_ffn2_wu, l1_ffn2_wd):
    b, s, d = x.shape
    m = b * s
    x2 = x.reshape(m, d)
    bias_a, bias_c, bias_d = _bias_tiles(rel_bias_table, ((A_SLOT, 2), (C_SLOT, BIAS_TILES), (D_SLOT, BIAS_TILES)))

    x2 = _ffn(x2, l0_ffn1_pre, l0_ffn1_post, l0_ffn1_wg, l0_ffn1_wu, l0_ffn1_wd, MACARON_WEIGHT)
    qa, ka, vat, qb, kb, vbt = _proj0(x2, s, l0_mix_pre, l0_w_in, l0_mla_q_norm, l0_mla_w_uq,
                                      l0_mla_kv_norm, l0_mla_w_ukv)
    shp = lambda t: t.reshape(b, s, t.shape[-1])
    o_a = _swa(shp(qa), shp(ka), vat, bias_a, l0_sinks)
    o_b = _mla(shp(qb), shp(kb), vbt)
    x2 = _outproj_ffn(x2, o_a.reshape(m, -1), o_b.reshape(m, -1), l0_w_out, l0_mix_post,
                      l0_ffn2_pre, l0_ffn2_post, l0_ffn2_wg, l0_ffn2_wu, l0_ffn2_wd, MACARON_WEIGHT)

    x2 = _ffn(x2, l1_ffn1_pre, l1_ffn1_post, l1_ffn1_wg, l1_ffn1_wu, l1_ffn1_wd, MACARON_WEIGHT)
    qc, kc_hm, vc_hm, ks, vst, kw, vwt, gates, qd, kd, vdt, kmean = _proj1(x2, b, s, l1_mix_pre, l1_w_in)
    kcmp, vcmpt = _compress(kc_hm, vc_hm, l1_nsa_pe_k, l1_nsa_pe_v, l1_nsa_wk1, l1_nsa_wk2, l1_nsa_wv1, l1_nsa_wv2)
    o_c = _nsa(shp(qc), kcmp, vcmpt, shp(ks), vst, shp(kw), vwt, shp(gates), bias_c)
    o_d = _moba(shp(qd), shp(kd), vdt, kmean, bias_d)
    x2 = _outproj_ffn(x2, o_c.reshape(m, -1), o_d.reshape(m, -1), l1_w_out, l1_mix_post,
                      l1_ffn2_pre, l1_ffn2_post, l1_ffn2_wg, l1_ffn2_wu, l1_ffn2_wd, MACARON_WEIGHT)
    return x2.reshape(b, s, d)
```

```python
import functools
import math

import numpy as np
import jax
import jax.numpy as jnp
from jax import lax
from jax.experimental import pallas as pl
from jax.experimental.pallas import tpu as pltpu

F32 = jnp.float32
BF16 = jnp.bfloat16

HEAD_DIM = 64
MACARON_WEIGHT = 0.5
NORM_EPS = 1e-6
LOG2E = math.log2(math.e)
NEG_INF = -1e30
PAD_SCORE = -3e38
FORCE_SCORE = 1e9
REL_BUCKETS = 32
REL_MAX_DIST = 1024
REL_SLOTS = 24
A_SLOT, C_SLOT, D_SLOT = 0, 8, 16
A_HEADS, A_KV_HEADS = 8, 2
MLA_NOPE, MLA_ROPE, MLA_V = 64, 32, 64
MLA_HEADS, MLA_Q_RANK, MLA_KV_RANK = 8, 256, 128
ROPE_THETA = 10000.0
C_HEADS, C_KV_HEADS = 8, 2
CMP_BLOCK, CMP_STRIDE = 32, 16
SEL_BLOCK, SEL_TOPK = 64, 8
D_HEADS = 8
MOBA_TOPK = 3

LANES = 128
SUBLANES = 8
TILE = 128
P0_COLS = (A_HEADS * HEAD_DIM, A_KV_HEADS * HEAD_DIM, A_KV_HEADS * HEAD_DIM, MLA_Q_RANK, MLA_KV_RANK, LANES)
P1_COLS = ((C_HEADS * HEAD_DIM,) + (C_KV_HEADS * HEAD_DIM,) * 6 + (LANES,) + (D_HEADS * HEAD_DIM,) * 3)
N_GATES = 3 * C_HEADS
BLK = 256
ONES_ROWS = 16
GATE_ROWS = 16
VT_ROWS = HEAD_DIM + ONES_ROWS
BIAS_TILES = 9
TOKEN_TILE = 512
PROJ_TILE = 1024
FF_CHUNK = 256
VMEM_LIMIT = 56 * 1024 * 1024


def _cparams(n_axes):
    return pltpu.CompilerParams(dimension_semantics=("arbitrary",) * n_axes, vmem_limit_bytes=VMEM_LIMIT)


def _dot(a, b):
    return jnp.dot(a, b, preferred_element_type=F32)


def _dot_nt(a, b):
    return lax.dot_general(a, b, (((1,), (1,)), ((), ())), preferred_element_type=F32)


def _rms(x, g):
    return x * lax.rsqrt(jnp.mean(x * x, axis=-1, keepdims=True) + NORM_EPS) * g


def _silu(x):
    return x / (1.0 + jnp.exp(-x))


def _col_slices(sizes):
    edges = np.cumsum((0,) + tuple(sizes)).tolist()
    return [slice(a, b) for a, b in zip(edges[:-1], edges[1:])]


def _const_spec(shape):
    nd = len(shape)
    return pl.BlockSpec(shape, lambda *_: (0,) * nd)


def _pipelined_blocks(length, produce_first, produce, consume, state):
    trips = (length - 1) // 2

    def trip(t, carry):
        state, stat_a = carry
        a = 2 * t
        stat_b = produce(a + 1, 1)
        state = consume(a, 0, stat_a, state)
        stat_next = produce(a + 2, 0)
        state = consume(a + 1, 1, stat_b, state)
        return state, stat_next

    carry = lax.fori_loop(0, trips, trip, (state, produce_first(0)))
    a = 2 * trips

    def tail_one(c):
        return consume(a, 0, c[1], c[0])

    def tail_two(c):
        stat_b = produce(a + 1, 1)
        return consume(a + 1, 1, stat_b, consume(a, 0, c[1], c[0]))

    return lax.cond(length - a == 1, tail_one, tail_two, carry)


def _per_query_tile(tile_fn, n_tiles):
    def body(*refs):
        def step(qi, carry):
            tile_fn(qi, *refs)
            return carry

        lax.fori_loop(0, n_tiles, step, 0)

    return body


def _tile_rows(qi):
    return pl.ds(pl.multiple_of(qi * BLK, BLK), BLK)


def _resident_spec(shape):
    nd = len(shape)
    return pl.BlockSpec(shape, lambda *_: (0,) * nd, pipeline_mode=pl.Buffered(1))


def _ffn_tile(x, pre_ref, post_ref, wg_ref, wu_ref, wd_ref, acc_ref, weight):
    h = _rms(x, pre_ref[...]).astype(BF16)
    for c in range(wg_ref.shape[1] // FF_CHUNK):
        cols = slice(c * FF_CHUNK, (c + 1) * FF_CHUNK)
        g = _dot(h, wg_ref[:, cols].astype(BF16))
        u = _dot(h, wu_ref[:, cols].astype(BF16))
        a = (_silu(g) * u).astype(BF16)
        y = _dot(a, wd_ref[cols, :].astype(BF16))
        if c == 0:
            acc_ref[...] = y
        else:
            acc_ref[...] += y
    return x + weight * _rms(acc_ref[...], post_ref[...])


def _ffn_body(x_ref, pre_ref, post_ref, wg_ref, wu_ref, wd_ref, o_ref, acc_ref, *, weight):
    o_ref[...] = _ffn_tile(x_ref[...], pre_ref, post_ref, wg_ref, wu_ref, wd_ref, acc_ref, weight)


def _ffn(x2, pre, post, wg, wu, wd, weight):
    m, d = x2.shape
    d_ff = wg.shape[1]
    tm = TOKEN_TILE
    return pl.pallas_call(
        functools.partial(_ffn_body, weight=weight),
        grid=(m // tm,),
        in_specs=[
            pl.BlockSpec((tm, d), lambda i: (i, 0)),
            _const_spec((1, d)), _const_spec((1, d)),
            _resident_spec((d, d_ff)), _resident_spec((d, d_ff)), _resident_spec((d_ff, d)),
        ],
        out_specs=pl.BlockSpec((tm, d), lambda i: (i, 0)),
        out_shape=jax.ShapeDtypeStruct((m, d), F32),
        scratch_shapes=[pltpu.VMEM((tm, d), F32)],
        compiler_params=_cparams(1),
        name="ffn",
    )(x2, pre.reshape(1, d), post.reshape(1, d), wg, wu, wd)


def _outproj_ffn_body(x_ref, o1_ref, o2_ref, wo_ref, mpost_ref, pre_ref, post_ref, wg_ref, wu_ref, wd_ref,
                      o_ref, acc_ref, *, weight):
    n1 = o1_ref.shape[1]
    y = (_dot(o1_ref[...], wo_ref[0:n1, :].astype(BF16))
         + _dot(o2_ref[...], wo_ref[n1:, :].astype(BF16)))
    x = x_ref[...] + _rms(y, mpost_ref[...])
    o_ref[...] = _ffn_tile(x, pre_ref, post_ref, wg_ref, wu_ref, wd_ref, acc_ref, weight)


def _outproj_ffn(x2, o1, o2, w_out, mix_post, pre, post, wg, wu, wd, weight):
    m, d = x2.shape
    d_ff = wg.shape[1]
    tm = TOKEN_TILE
    tok = lambda n: pl.BlockSpec((tm, n), lambda i: (i, 0))
    return pl.pallas_call(
        functools.partial(_outproj_ffn_body, weight=weight),
        grid=(m // tm,),
        in_specs=[
            tok(d), tok(o1.shape[1]), tok(o2.shape[1]), _resident_spec(w_out.shape), _const_spec((1, d)),
            _const_spec((1, d)), _const_spec((1, d)),
            _resident_spec((d, d_ff)), _resident_spec((d, d_ff)), _resident_spec((d_ff, d)),
        ],
        out_specs=tok(d),
        out_shape=jax.ShapeDtypeStruct((m, d), F32),
        scratch_shapes=[pltpu.VMEM((tm, d), F32)],
        compiler_params=_cparams(1),
        name="outproj_ffn",
    )(x2, o1, o2, w_out, mix_post.reshape(1, d), pre.reshape(1, d), post.reshape(1, d), wg, wu, wd)


def _t5_bucket(dist):
    n = jnp.maximum(dist, 0)
    exact = REL_BUCKETS // 2
    nf = jnp.maximum(n, 1).astype(jnp.float32)
    large = exact + (jnp.log(nf / exact) / math.log(REL_MAX_DIST / exact) * (REL_BUCKETS - exact)).astype(jnp.int32)
    return jnp.where(n < exact, n, jnp.minimum(large, REL_BUCKETS - 1))


def _bias_body(tab_ref, idx_ref, *o_refs, first_slots):
    head = pl.program_id(0)
    for o_ref, first in zip(o_refs, first_slots):
        column = jnp.broadcast_to(LOG2E * tab_ref[pl.ds(first + head, 1), :], (TILE, LANES))
        for d in range(o_ref.shape[1]):
            o_ref[0, d] = jnp.take_along_axis(column, idx_ref[d], axis=1)


def _bias_tiles(rel_table, groups):
    heads = 8
    d = jnp.arange(BIAS_TILES)[:, None, None]
    key = jnp.arange(TILE)[None, :, None]
    query = jnp.arange(TILE)[None, None, :]
    idx = _t5_bucket(d * TILE + query - key).astype(jnp.int32)
    n_buckets, n_slots = rel_table.shape
    table_t = jnp.pad(rel_table.astype(F32).T, ((0, 0), (0, LANES - n_buckets)))
    return pl.pallas_call(
        functools.partial(_bias_body, first_slots=tuple(first for first, _ in groups)),
        grid=(heads,),
        in_specs=[_const_spec((n_slots, LANES)), _const_spec((BIAS_TILES, TILE, TILE))],
        out_specs=[pl.BlockSpec((1, n, TILE, TILE), lambda s: (s, 0, 0, 0)) for _, n in groups],
        out_shape=[jax.ShapeDtypeStruct((heads, n, TILE, TILE), F32) for _, n in groups],
        compiler_params=_cparams(1),
        name="bias_tiles",
    )(table_t, idx)


def _proj0_body(x_ref, pre_ref, w_ref, qn_ref, wq1_ref, wq2_ref, kvn_ref, wk_ref, wv_ref, cos_ref, sin_ref,
                qa_ref, ka_ref, vat_ref, qb_ref, kb_ref, vbt_ref):
    h = _rms(x_ref[...], pre_ref[...]).astype(BF16)
    y = _dot(h, w_ref[...])
    qa, ka, va, cq, ckv, rot = (y[:, sl] for sl in _col_slices(P0_COLS))
    qa_ref[...] = (qa * (LOG2E * HEAD_DIM ** -0.5)).astype(BF16)
    ka_ref[...] = ka.astype(BF16)
    for j in range(y.shape[0] // TILE):
        _store_vt_ones(vat_ref, j, va[j * TILE:(j + 1) * TILE], A_KV_HEADS)
    cq = _rms(cq, qn_ref[...]).astype(BF16)
    ckv = _rms(ckv, kvn_ref[...]).astype(BF16)
    cos_t = cos_ref[...]
    sin_t = sin_ref[...]
    lane = lax.broadcasted_iota(jnp.int32, cos_t.shape, 1)
    scale = LOG2E * (MLA_NOPE + MLA_ROPE) ** -0.5
    qmul = scale * (cos_t + jnp.where(lane < MLA_NOPE, 1.0, 0.0))
    qsin = scale * sin_t
    kr = rot * cos_t + pltpu.roll(rot, LANES // 2, 1) * sin_t
    q1 = _dot(cq, wq1_ref[...])
    q2 = _dot(cq, wq2_ref[...])
    k1 = _dot(ckv, wk_ref[...])
    for hd in range(MLA_HEADS):
        sl = slice(hd * LANES, (hd + 1) * LANES)
        qb_ref[:, sl] = (q1[:, sl] * qmul + q2[:, sl] * qsin).astype(BF16)
        kb_ref[:, sl] = (k1[:, sl] + kr).astype(BF16)
    vb = _dot(ckv, wv_ref[...])
    for j in range(vb.shape[0] // BLK):
        _store_vt_ones(vbt_ref, j, vb[j * BLK:(j + 1) * BLK], MLA_HEADS)


def _rope_lane_tables(s):
    inv = ROPE_THETA ** (-jnp.arange(0, MLA_ROPE, 2, dtype=jnp.float32) / MLA_ROPE)
    ang = jnp.arange(s, dtype=jnp.float32)[:, None] * inv[None, :]
    cos, sin = jnp.cos(ang), jnp.sin(ang)
    z_lo = jnp.zeros((s, MLA_NOPE), F32)
    z_hi = jnp.zeros((s, LANES - MLA_NOPE - MLA_ROPE), F32)
    return (jnp.concatenate([z_lo, cos, cos, z_hi], axis=1), jnp.concatenate([z_lo, sin, sin, z_hi], axis=1))


def _rot_cols(w):
    half = w.shape[-1] // 2
    return jnp.concatenate([-w[..., half:], w[..., :half]], axis=-1)


def _proj0(x2, seq, pre, w_in, q_norm, w_uq, kv_norm, w_ukv):
    m, d = x2.shape
    tm = PROJ_TILE
    n_lin = sum(P0_COLS[:-1])
    kr_w = w_in[:, n_lin:n_lin + MLA_ROPE]
    z32 = jnp.zeros((d, MLA_ROPE), F32)
    w0 = jnp.concatenate([w_in[:, :n_lin], _rot_cols(kr_w), z32, kr_w, z32], axis=1).astype(BF16)
    wq = w_uq.reshape(-1, MLA_HEADS, MLA_NOPE + MLA_ROPE)
    rq = wq.shape[0]
    pad = LANES - MLA_NOPE - MLA_ROPE
    zq = jnp.zeros((rq, MLA_HEADS, pad), F32)
    wq1 = jnp.concatenate([wq, zq], axis=2).reshape(rq, -1).astype(BF16)
    wq2 = jnp.concatenate([jnp.zeros((rq, MLA_HEADS, MLA_NOPE), F32), _rot_cols(wq[:, :, MLA_NOPE:]), zq],
                          axis=2).reshape(rq, -1).astype(BF16)
    wkv = w_ukv.reshape(-1, MLA_HEADS, MLA_NOPE + MLA_V)
    rk = wkv.shape[0]
    wk = jnp.concatenate([wkv[:, :, :MLA_NOPE], jnp.zeros((rk, MLA_HEADS, LANES - MLA_NOPE), F32)],
                         axis=2).reshape(rk, -1).astype(BF16)
    wv = wkv[:, :, MLA_NOPE:].reshape(rk, -1).astype(BF16)
    cos_t, sin_t = _rope_lane_tables(seq)
    n_st = seq // tm
    tok = lambda n: pl.BlockSpec((tm, n), lambda i: (i, 0))
    outs = pl.pallas_call(
        _proj0_body,
        grid=(m // tm,),
        in_specs=[
            tok(d), _const_spec((1, d)), _const_spec(w0.shape),
            _const_spec((1, rq)), _const_spec(wq1.shape), _const_spec(wq2.shape),
            _const_spec((1, rk)), _const_spec(wk.shape), _const_spec(wv.shape),
            pl.BlockSpec((tm, LANES), lambda i: (i % n_st, 0)),
            pl.BlockSpec((tm, LANES), lambda i: (i % n_st, 0)),
        ],
        out_specs=[tok(P0_COLS[0]), tok(P0_COLS[1]),
                   pl.BlockSpec((1, tm // TILE, A_KV_HEADS * VT_ROWS, TILE), lambda i: (i // n_st, i % n_st, 0, 0)),
                   tok(MLA_HEADS * LANES), tok(MLA_HEADS * LANES),
                   pl.BlockSpec((1, tm // BLK, MLA_HEADS * VT_ROWS, BLK), lambda i: (i // n_st, i % n_st, 0, 0))],
        out_shape=[jax.ShapeDtypeStruct((m, P0_COLS[0]), BF16), jax.ShapeDtypeStruct((m, P0_COLS[1]), BF16),
                   jax.ShapeDtypeStruct((m // seq, seq // TILE, A_KV_HEADS * VT_ROWS, TILE), BF16),
                   jax.ShapeDtypeStruct((m, MLA_HEADS * LANES), BF16),
                   jax.ShapeDtypeStruct((m, MLA_HEADS * LANES), BF16),
                   jax.ShapeDtypeStruct((m // seq, seq // BLK, MLA_HEADS * VT_ROWS, BLK), BF16)],
        compiler_params=_cparams(1),
        name="proj0",
    )(x2, pre.reshape(1, d), w0, q_norm.reshape(1, rq), wq1, wq2, kv_norm.reshape(1, rk), wk, wv, cos_t, sin_t)
    return outs


def _swa_body(sink_ref, q_ref, k_ref, vt_ref, bias_ref, o_ref, s_ref, *, n_kv, grp, n_tiles):
    halves = BLK // TILE
    items = [(hkv, half) for hkv in range(n_kv) for half in range(halves)]
    width = grp * TILE
    krow = lax.broadcasted_iota(jnp.int32, (TILE, len(items) * width), 0)
    qcol = lax.broadcasted_iota(jnp.int32, (TILE, len(items) * width), 1) & (TILE - 1)
    sink = jnp.concatenate([jnp.full((1, TILE), LOG2E * sink_ref[hkv * grp + g], F32)
                            for hkv, _ in items for g in range(grp)], axis=1)

    def produce(qi, slot):
        cur_s, prev_s, has_prev = [], [], []
        for hkv, half in items:
            sl = slice(hkv * HEAD_DIM, (hkv + 1) * HEAD_DIM)
            t = qi * halves + half
            tp = jnp.maximum(t - 1, 0)
            q_rows = pl.ds(pl.multiple_of(t * TILE, TILE), TILE)
            qs = jnp.concatenate([q_ref[0, q_rows, (hkv * grp + g) * HEAD_DIM:(hkv * grp + g + 1) * HEAD_DIM]
                                  for g in range(grp)], axis=0)
            bias_c = jnp.concatenate([bias_ref[hkv * grp + g, 0] for g in range(grp)], axis=1)
            bias_p = jnp.concatenate([bias_ref[hkv * grp + g, 1] for g in range(grp)], axis=1)
            cur_s.append(_dot_nt(k_ref[0, pl.ds(pl.multiple_of(t * TILE, TILE), TILE), sl], qs) + bias_c)
            prev_s.append(_dot_nt(k_ref[0, pl.ds(pl.multiple_of(tp * TILE, TILE), TILE), sl], qs) + bias_p)
            has_prev.append(jnp.full((1, width), jnp.where(t >= 1, 0, TILE), jnp.int32))
        s_c = jnp.where(krow <= qcol, jnp.concatenate(cur_s, axis=1), NEG_INF)
        s_p = jnp.where(krow > qcol + jnp.concatenate(has_prev, axis=1),
                        jnp.concatenate(prev_s, axis=1), NEG_INF)
        s_ref[slot, 0] = s_c
        s_ref[slot, 1] = s_p
        return jnp.maximum(jnp.maximum(jnp.max(s_c, axis=0, keepdims=True), jnp.max(s_p, axis=0, keepdims=True)), sink)

    def consume(qi, slot, mx, state):
        p_c = jnp.exp2(s_ref[slot, 0] - mx).astype(BF16)
        p_p = jnp.exp2(s_ref[slot, 1] - mx).astype(BF16)
        accs = []
        for i, (hkv, half) in enumerate(items):
            vsl = slice(hkv * VT_ROWS, (hkv + 1) * VT_ROWS)
            ls = slice(i * width, (i + 1) * width)
            t = qi * halves + half
            accs.append(_dot(vt_ref[0, t, vsl, :], p_c[:, ls])
                        + _dot(vt_ref[0, jnp.maximum(t - 1, 0), vsl, :], p_p[:, ls]))
        acc = jnp.concatenate(accs, axis=1)
        o_t = acc[0:HEAD_DIM] / (acc[HEAD_DIM:HEAD_DIM + 1] + jnp.exp2(sink - mx))
        heads = []
        for hkv in range(n_kv):
            for g in range(grp):
                heads.append(jnp.concatenate(
                    [o_t[:, (hkv * halves + half) * width + g * TILE:(hkv * halves + half) * width + (g + 1) * TILE]
                     for half in range(halves)], axis=1))
        o_ref[0, _tile_rows(qi), :] = jnp.concatenate(heads, axis=0).T.astype(BF16)
        return state

    _pipelined_blocks(n_tiles, functools.partial(produce, 0), produce, consume, 0)


def _seq_spec(shape):
    nd = len(shape)
    return pl.BlockSpec((1,) + tuple(shape[1:]), lambda bi: (bi,) + (0,) * (nd - 1))


def _swa(qa, ka, vat, bias_a, sinks):
    b, s, width = qa.shape
    n_kv = ka.shape[-1] // HEAD_DIM
    grp = width // HEAD_DIM // n_kv
    return pl.pallas_call(
        functools.partial(_swa_body, n_kv=n_kv, grp=grp, n_tiles=s // BLK),
        grid=(b,),
        in_specs=[pl.BlockSpec(memory_space=pltpu.SMEM), _seq_spec(qa.shape), _seq_spec(ka.shape),
                  _seq_spec(vat.shape), _const_spec(bias_a.shape)],
        out_specs=_seq_spec(qa.shape),
        out_shape=jax.ShapeDtypeStruct((b, s, width), BF16),
        scratch_shapes=[pltpu.VMEM((2, 2, TILE, n_kv * (BLK // TILE) * grp * TILE), F32)],
        compiler_params=_cparams(1),
        name="swa",
    )(sinks.astype(F32), qa, ka, vat, bias_a)


def _finish(acc, dv):
    return acc[0:dv] / acc[dv:dv + 1]


def _mla_tile(qi, q_ref, k_ref, vt_ref, o_ref, acc_ref, s_ref, *, n_heads):
    q_rows = _tile_rows(qi)
    lanes = n_heads * BLK
    krow = lax.broadcasted_iota(jnp.int32, (BLK, lanes), 0)
    qcol = lax.broadcasted_iota(jnp.int32, (BLK, lanes), 1) & (BLK - 1)

    def scores(n):
        st = pl.multiple_of(n * BLK, BLK)
        return jnp.concatenate(
            [_dot_nt(k_ref[0, pl.ds(st, BLK), hd * LANES:(hd + 1) * LANES], q_ref[0, q_rows, hd * LANES:(hd + 1) * LANES])
             for hd in range(n_heads)], axis=1)

    def weighted_values(n, p):
        return jnp.concatenate(
            [_dot(vt_ref[0, n, hd * VT_ROWS:(hd + 1) * VT_ROWS, :], p[:, hd * BLK:(hd + 1) * BLK])
             for hd in range(n_heads)], axis=1)

    def produce_first(slot):
        s = jnp.where(krow <= qcol, scores(qi), NEG_INF)
        s_ref[slot] = s
        return jnp.max(s, axis=0, keepdims=True)

    def produce(i, slot):
        s = scores(i - 1)
        s_ref[slot] = s
        return jnp.max(s, axis=0, keepdims=True)

    def consume(i, slot, block_max, m):
        n = jnp.where(i == 0, qi, i - 1)
        m_new = jnp.maximum(m, block_max)
        p = jnp.exp2(s_ref[slot] - m_new).astype(BF16)
        acc_ref[...] = jnp.exp2(m - m_new) * acc_ref[...] + weighted_values(n, p)
        return m_new

    acc_ref[...] = jnp.zeros_like(acc_ref)
    _pipelined_blocks(qi + 1, produce_first, produce, consume, jnp.full((1, lanes), NEG_INF, F32))
    o_t = _finish(acc_ref[...], MLA_V)
    o_ref[0, q_rows, :] = jnp.concatenate([o_t[:, hd * BLK:(hd + 1) * BLK] for hd in range(n_heads)],
                                          axis=0).T.astype(BF16)


def _mla(qb, kb, vbt):
    b, s, width = qb.shape
    n_heads = width // LANES
    out_shape = jax.ShapeDtypeStruct((b, s, n_heads * MLA_V), BF16)
    return pl.pallas_call(
        _per_query_tile(functools.partial(_mla_tile, n_heads=n_heads), s // BLK),
        grid=(b,),
        in_specs=[_seq_spec(qb.shape), _seq_spec(kb.shape), _seq_spec(vbt.shape)],
        out_specs=_seq_spec(out_shape.shape),
        out_shape=out_shape,
        scratch_shapes=[pltpu.VMEM((VT_ROWS, n_heads * BLK), F32), pltpu.VMEM((2, BLK, n_heads * BLK), F32)],
        compiler_params=_cparams(1),
        name="mla",
    )(qb, kb, vbt)


def _store_vt_ones(vt_ref, j, v, n_heads):
    vt = v.T.astype(BF16)
    ones = jnp.ones((ONES_ROWS, v.shape[0]), BF16)
    for hd in range(n_heads):
        vt_ref[0, j, hd * VT_ROWS:hd * VT_ROWS + HEAD_DIM, :] = vt[hd * HEAD_DIM:(hd + 1) * HEAD_DIM]
        vt_ref[0, j, hd * VT_ROWS + HEAD_DIM:(hd + 1) * VT_ROWS, :] = ones


def _proj1_body(x_ref, pre_ref, w_ref, qc_ref, kx_ref, vx_ref, ks_ref, vst_ref, kw_ref, vwt_ref, g_ref,
                qd_ref, kd_ref, vdt_ref, km_ref, kstage_ref, vstage_ref, *, tm):
    h = _rms(x_ref[...], pre_ref[...]).astype(BF16)
    y = _dot(h, w_ref[...])
    qc, kc, vc, ks, vs, kw, vw, gl, qd, kd, vd = (y[:, sl] for sl in _col_slices(P1_COLS))
    scale = LOG2E * HEAD_DIM ** -0.5
    qc_ref[...] = (qc * scale).astype(BF16)
    for stage_ref, cols, x_out in ((kstage_ref, kc, kx_ref), (vstage_ref, vc, vx_ref)):
        stage_ref[...] = cols
        for i in range(CMP_STRIDE):
            piece = stage_ref[pl.ds(i, tm // CMP_STRIDE, stride=CMP_STRIDE), :].astype(BF16)
            for hkv in range(C_KV_HEADS):
                x_out[0, hkv, :, i * HEAD_DIM:(i + 1) * HEAD_DIM] = piece[:, hkv * HEAD_DIM:(hkv + 1) * HEAD_DIM]
    ks_ref[...] = ks.astype(BF16)
    kw_ref[...] = kw.astype(BF16)
    g_ref[...] = 1.0 / (1.0 + jnp.exp(-gl))
    qd_ref[...] = (qd * scale).astype(BF16)
    kd_ref[...] = kd.astype(BF16)
    for j in range(tm // BLK):
        rows = slice(j * BLK, (j + 1) * BLK)
        km_ref[j] = jnp.mean(kd[rows], axis=0, keepdims=True)
        _store_vt_ones(vdt_ref, j, vd[rows], D_HEADS)
        _store_vt_ones(vst_ref, j, vs[rows], C_KV_HEADS)
        _store_vt_ones(vwt_ref, j, vw[rows], C_KV_HEADS)


def _proj1(x2, batch, seq, pre, w_in):
    m, d = x2.shape
    tm = PROJ_TILE
    n_st = seq // tm
    n_pre = sum(P1_COLS[:7]) + N_GATES
    pad = jnp.zeros((d, LANES - N_GATES), F32)
    w1 = jnp.concatenate([w_in[:, :n_pre], pad, w_in[:, n_pre:]], axis=1).astype(BF16)
    q_w, kv_w, d_w = C_HEADS * HEAD_DIM, C_KV_HEADS * HEAD_DIM, D_HEADS * HEAD_DIM
    tok = lambda n: pl.BlockSpec((tm, n), lambda i: (i, 0))
    tok_shape = lambda n, dt=BF16: jax.ShapeDtypeStruct((m, n), dt)
    feat = CMP_STRIDE * HEAD_DIM
    hm = pl.BlockSpec((1, C_KV_HEADS, tm // CMP_STRIDE, feat), lambda i: (i // n_st, 0, i % n_st, 0))
    hm_shape = jax.ShapeDtypeStruct((batch, C_KV_HEADS, seq // CMP_STRIDE, feat), BF16)
    nblk = tm // BLK
    vt_spec = lambda heads: pl.BlockSpec((1, nblk, heads * VT_ROWS, BLK), lambda i: (i // n_st, i % n_st, 0, 0))
    vt_shape = lambda heads: jax.ShapeDtypeStruct((batch, seq // BLK, heads * VT_ROWS, BLK), BF16)
    return pl.pallas_call(
        functools.partial(_proj1_body, tm=tm),
        grid=(m // tm,),
        in_specs=[tok(d), _const_spec((1, d)), _const_spec(w1.shape)],
        out_specs=[tok(q_w), hm, hm, tok(kv_w), vt_spec(C_KV_HEADS), tok(kv_w), vt_spec(C_KV_HEADS), tok(LANES),
                   tok(d_w), tok(d_w), vt_spec(D_HEADS),
                   pl.BlockSpec((nblk, 1, d_w), lambda i: (i, 0, 0))],
        out_shape=[
            tok_shape(q_w), hm_shape, hm_shape,
            tok_shape(kv_w), vt_shape(C_KV_HEADS), tok_shape(kv_w), vt_shape(C_KV_HEADS), tok_shape(LANES, F32),
            tok_shape(d_w), tok_shape(d_w), vt_shape(D_HEADS),
            jax.ShapeDtypeStruct((m // BLK, 1, d_w), F32),
        ],
        scratch_shapes=[pltpu.VMEM((tm, kv_w), F32), pltpu.VMEM((tm, kv_w), F32)],
        compiler_params=_cparams(1),
        name="proj1",
    )(x2, pre.reshape(1, d), w1)


def _compress_body(kx_ref, vx_ref, pek_ref, pev_ref, wk1_ref, wk2_ref, wv1_ref, wv2_ref, ko_ref, vo_ref):
    half = CMP_STRIDE * HEAD_DIM
    for x_ref, pe_ref, w1_ref, w2_ref, o_ref in ((kx_ref, pek_ref, wk1_ref, wk2_ref, ko_ref),
                                                 (vx_ref, pev_ref, wv1_ref, wv2_ref, vo_ref)):
        pe8 = jnp.broadcast_to(pe_ref[...], (SUBLANES, 2 * half)).astype(BF16)
        pe_term = _dot(pe8, w1_ref[...])[0:1]
        outs = []
        for hkv in range(C_KV_HEADS):
            x = x_ref[0, hkv]
            lo = _dot(x, w1_ref[0:half])
            hi = _dot(x, w1_ref[half:2 * half])
            hid = lo + pltpu.roll(hi, x.shape[0] - 1, 0) + pe_term
            outs.append(_dot(_silu(hid).astype(BF16), w2_ref[...]))
        out = jnp.concatenate(outs, axis=1)
        o_ref[0] = (out.T if o_ref is vo_ref else out).astype(BF16)


def _compress(kx, vx, pe_k, pe_v, wk1, wk2, wv1, wv2):
    b, _, rows, feat = kx.shape
    dh = feat // CMP_STRIDE
    xspec = pl.BlockSpec((1, 2, rows, feat), lambda bi: (bi, 0, 0, 0))
    ospec = pl.BlockSpec((1, rows, 2 * dh), lambda bi: (bi, 0, 0))
    w1s, w2s = wk1.shape, wk2.shape
    return pl.pallas_call(
        _compress_body,
        grid=(b,),
        in_specs=[xspec, xspec, _const_spec((1, 2 * feat)), _const_spec((1, 2 * feat)),
                  _const_spec(w1s), _const_spec(w2s), _const_spec(w1s), _const_spec(w2s)],
        out_specs=[ospec, ospec],
        out_shape=[jax.ShapeDtypeStruct((b, rows, 2 * dh), BF16)] * 2,
        compiler_params=_cparams(1),
        name="nsa_compress",
    )(kx, vx, pe_k.astype(F32).reshape(1, -1), pe_v.astype(F32).reshape(1, -1),
      wk1.astype(BF16), wk2.astype(BF16), wv1.astype(BF16), wv2.astype(BF16))


def _bias_block(bias_ref, hd, dist_blocks):
    sub = BLK // TILE
    rows = []
    for a in range(sub):
        cols = []
        for bq in range(sub):
            e = sub * dist_blocks + bq - a
            e = (min(max(e, 0), BIAS_TILES - 1) if isinstance(e, int)
                 else jnp.minimum(jnp.maximum(e, 0), BIAS_TILES - 1))
            cols.append(bias_ref[hd, e])
        rows.append(jnp.concatenate(cols, axis=1))
    return jnp.concatenate(rows, axis=0)


def _topk_rows(val, ridx, k):
    n_rows = val.shape[0]
    picked = jnp.zeros(val.shape, F32)
    for _ in range(k):
        best = jnp.max(val, axis=0, keepdims=True)
        first = jnp.min(jnp.where(val == best, ridx, n_rows), axis=0, keepdims=True)
        hit = ridx == first
        picked = jnp.where(hit, 1.0, picked)
        val = jnp.where(hit, PAD_SCORE, val)
    return picked


def _nsa_tile(qi, q_ref, kc_ref, vct_ref, ks_ref, vst_ref, kw_ref, vwt_ref, g_ref, bias_ref, ovt_ref, o_ref,
              pen_ref, acc_ref, s_ref, *, n_sel):
    q_rows = _tile_rows(qi)
    grp = C_HEADS // C_KV_HEADS
    lanes = grp * BLK
    sub = BLK // SEL_BLOCK
    own = pl.multiple_of(qi * BLK, BLK)
    prev = pl.multiple_of(jnp.maximum(qi - 1, 0) * BLK, BLK)
    prev_blk = jnp.maximum(qi - 1, 0)
    krow = lax.broadcasted_iota(jnp.int32, (BLK, lanes), 0)
    qcol = lax.broadcasted_iota(jnp.int32, (BLK, lanes), 1) & (BLK - 1)
    causal = krow <= qcol
    crow = lax.broadcasted_iota(jnp.int32, (TILE, lanes), 0)
    cq = lax.broadcasted_iota(jnp.int32, (TILE, lanes), 1) & (BLK - 1)
    visible = CMP_STRIDE * crow + (CMP_BLOCK - 1) <= qi * BLK + cq
    jrow = lax.broadcasted_iota(jnp.int32, (n_sel, BLK), 0)
    cur = qi * sub + lax.broadcasted_iota(jnp.int32, (n_sel, BLK), 1) // SEL_BLOCK
    forced = (jrow == 0) | (jrow == cur) | (jrow == cur - 1)
    gates_t = g_ref[0, q_rows, :].T
    has_prev = jnp.where(qi >= 1, 0, BLK)

    def stack_q(hkv):
        return jnp.concatenate(
            [q_ref[0, q_rows, (hkv * grp + g) * HEAD_DIM:(hkv * grp + g + 1) * HEAD_DIM] for g in range(grp)], axis=0)

    def group_bias(hkv, dist_blocks):
        return jnp.concatenate([_bias_block(bias_ref, hkv * grp + g, dist_blocks) for g in range(grp)], axis=1)

    def sel_scores(hkv, n, dist_blocks):
        st = pl.multiple_of(n * BLK, BLK)
        sl = slice(hkv * HEAD_DIM, (hkv + 1) * HEAD_DIM)
        return _dot_nt(ks_ref[0, pl.ds(st, BLK), sl], stack_q(hkv)) + group_bias(hkv, dist_blocks)

    def sel_pens(hkv, n):
        pens = pen_ref[hkv, n, 0:sub, :]
        return [jnp.concatenate([pens[j:j + 1]] * grp, axis=1) for j in range(sub)]

    def sel_max(s, pens):
        mj = [jnp.max(s[j * SEL_BLOCK:(j + 1) * SEL_BLOCK], axis=0, keepdims=True) + pens[j] for j in range(sub)]
        return functools.reduce(jnp.maximum, mj)

    def sel_probs(s, pens, m_new):
        return jnp.concatenate([jnp.exp2(s[j * SEL_BLOCK:(j + 1) * SEL_BLOCK] - (m_new - pens[j]))
                                for j in range(sub)], axis=0).astype(BF16)

    o_cmp, o_win = [], []
    for hkv in range(C_KV_HEADS):
        sl = slice(hkv * HEAD_DIM, (hkv + 1) * HEAD_DIM)
        vsl = slice(hkv * VT_ROWS, (hkv + 1) * VT_ROWS)
        qs = stack_q(hkv)

        sc = jnp.where(visible, _dot_nt(kc_ref[0, :, sl], qs), NEG_INF)
        mc = jnp.max(sc, axis=0, keepdims=True)
        ec = jnp.where(visible, jnp.exp2(sc - mc), 0.0)
        lc = jnp.sum(ec, axis=0, keepdims=True)
        p = ec / jnp.where(lc > 0.0, lc, 1.0)
        o_cmp.append(_dot(vct_ref[0, sl, :], p.astype(BF16)))

        ps = p[:, 0:BLK] + p[:, BLK:2 * BLK] + p[:, 2 * BLK:3 * BLK] + p[:, 3 * BLK:4 * BLK]
        ps_hi = ps.astype(BF16)
        ps_lo = (ps - ps_hi.astype(F32)).astype(BF16)
        imp = _dot(ovt_ref[...], ps_hi) + _dot(ovt_ref[...], ps_lo)
        val = jnp.where(forced, FORCE_SCORE, jnp.where(jrow <= cur, imp, NEG_INF))
        pen = jnp.where(_topk_rows(val, jrow, SEL_TOPK) > 0.5, 0.0, NEG_INF)
        for t in range(n_sel // sub):
            pen_ref[hkv, t, 0:sub, :] = pen[t * sub:(t + 1) * sub]

        s0 = _dot_nt(kw_ref[0, pl.ds(own, BLK), sl], qs) + group_bias(hkv, 0)
        s1 = _dot_nt(kw_ref[0, pl.ds(prev, BLK), sl], qs) + group_bias(hkv, 1)
        s0 = jnp.where(causal, s0, NEG_INF)
        s1 = jnp.where(krow > qcol + has_prev, s1, NEG_INF)
        mw = jnp.maximum(jnp.max(s0, axis=0, keepdims=True), jnp.max(s1, axis=0, keepdims=True))
        ow = (_dot(vwt_ref[0, qi, vsl, :], jnp.exp2(s0 - mw).astype(BF16))
              + _dot(vwt_ref[0, prev_blk, vsl, :], jnp.exp2(s1 - mw).astype(BF16)))
        o_win.append(_finish(ow, HEAD_DIM))

    def produce_first(slot):
        stats = []
        for hkv in range(C_KV_HEADS):
            s = jnp.where(causal, sel_scores(hkv, qi, 0), NEG_INF)
            s_ref[slot, hkv] = s
            stats.append(sel_max(s, sel_pens(hkv, qi)))
        return tuple(stats)

    def produce(i, slot):
        stats = []
        for hkv in range(C_KV_HEADS):
            s = sel_scores(hkv, i - 1, qi - (i - 1))
            s_ref[slot, hkv] = s
            stats.append(sel_max(s, sel_pens(hkv, i - 1)))
        return tuple(stats)

    def consume(i, slot, block_max, ms):
        n = jnp.where(i == 0, qi, i - 1)
        new = []
        for hkv in range(C_KV_HEADS):
            vsl = slice(hkv * VT_ROWS, (hkv + 1) * VT_ROWS)
            m_new = jnp.maximum(ms[hkv], block_max[hkv])
            pb = sel_probs(s_ref[slot, hkv], sel_pens(hkv, n), m_new)
            acc_ref[hkv] = jnp.exp2(ms[hkv] - m_new) * acc_ref[hkv] + _dot(vst_ref[0, n, vsl, :], pb)
            new.append(m_new)
        return tuple(new)

    acc_ref[...] = jnp.zeros_like(acc_ref)
    m_init = jnp.full((1, lanes), NEG_INF, F32)
    _pipelined_blocks(qi + 1, produce_first, produce, consume, (m_init, m_init))
    outs = []
    for hkv in range(C_KV_HEADS):
        o_slc = _finish(acc_ref[hkv], HEAD_DIM)
        for g in range(grp):
            hd = hkv * grp + g
            ls = slice(g * BLK, (g + 1) * BLK)
            g_cmp, g_slc, g_win = (gates_t[t * C_HEADS + hd:t * C_HEADS + hd + 1] for t in range(N_GATES // C_HEADS))
            outs.append(g_cmp * o_cmp[hkv][:, ls] + g_slc * o_slc[:, ls] + g_win * o_win[hkv][:, ls])
    o_ref[0, q_rows, :] = jnp.concatenate(outs, axis=0).T.astype(BF16)


def _nsa(qc, kcmp, vcmpt, ks, vst, kw, vwt, gates, bias_c):
    b, s, width = qc.shape
    n_blocks = s // BLK
    n_sel = s // SEL_BLOCK
    n_c = (s - CMP_BLOCK) // CMP_STRIDE + 1
    j_start = np.arange(n_sel)[:, None] * SEL_BLOCK
    c_start = np.arange(TILE)[None, :] * CMP_STRIDE
    overlap_t = ((c_start < j_start + SEL_BLOCK) & (c_start + CMP_BLOCK > j_start)
                 & (np.arange(TILE)[None, :] < n_c))
    ovt = jnp.asarray(overlap_t, BF16)
    group_lanes = C_HEADS // C_KV_HEADS * BLK
    operands = (qc, kcmp, vcmpt, ks, vst, kw, vwt, gates)
    return pl.pallas_call(
        _per_query_tile(functools.partial(_nsa_tile, n_sel=n_sel), n_blocks),
        grid=(b,),
        in_specs=[_seq_spec(t.shape) for t in operands] + [_const_spec(bias_c.shape), _const_spec(ovt.shape)],
        out_specs=_seq_spec(qc.shape),
        out_shape=jax.ShapeDtypeStruct((b, s, width), BF16),
        scratch_shapes=[pltpu.VMEM((C_KV_HEADS, n_blocks, SUBLANES, BLK), F32),
                        pltpu.VMEM((C_KV_HEADS, VT_ROWS, group_lanes), F32),
                        pltpu.VMEM((2, C_KV_HEADS, BLK, group_lanes), F32)],
        compiler_params=_cparams(1),
        name="nsa",
    )(*operands, bias_c, ovt)


def _moba_tile(qi, q_ref, k_ref, vt_ref, km_ref, bias_ref, o_ref, pen_ref, acc_ref, s_ref, *, n_heads, n_blocks):
    q_rows = _tile_rows(qi)
    lanes = n_heads * BLK
    nidx = lax.broadcasted_iota(jnp.int32, (GATE_ROWS, lanes), 0)
    krow = lax.broadcasted_iota(jnp.int32, (BLK, lanes), 0)
    qcol = lax.broadcasted_iota(jnp.int32, (BLK, lanes), 1) & (BLK - 1)

    def scores(n, dist_blocks):
        st = pl.multiple_of(n * BLK, BLK)
        parts = []
        for hd in range(n_heads):
            sl = slice(hd * HEAD_DIM, (hd + 1) * HEAD_DIM)
            parts.append(_dot_nt(k_ref[0, pl.ds(st, BLK), sl], q_ref[0, q_rows, sl])
                         + _bias_block(bias_ref, hd, dist_blocks))
        return jnp.concatenate(parts, axis=1)

    def weighted_values(n, p):
        return jnp.concatenate(
            [_dot(vt_ref[0, n, hd * VT_ROWS:(hd + 1) * VT_ROWS, :], p[:, hd * BLK:(hd + 1) * BLK])
             for hd in range(n_heads)], axis=1)

    zpad = jnp.zeros((GATE_ROWS - n_blocks, HEAD_DIM), F32)
    gate = jnp.concatenate(
        [_dot_nt(jnp.concatenate([km_ref[0, :, hd * HEAD_DIM:(hd + 1) * HEAD_DIM], zpad], axis=0).astype(BF16),
                 q_ref[0, q_rows, hd * HEAD_DIM:(hd + 1) * HEAD_DIM]) for hd in range(n_heads)], axis=1)
    val = jnp.where(nidx < qi, gate, NEG_INF)
    val = jnp.where(nidx < n_blocks, val, PAD_SCORE)
    keep = ((_topk_rows(val, nidx, MOBA_TOPK) > 0.5) & (nidx < qi)) | (nidx == qi)
    pen_ref[...] = jnp.where(keep, 0.0, NEG_INF)

    def produce_first(slot):
        s = jnp.where(krow <= qcol, scores(qi, 0), NEG_INF)
        s_ref[slot] = s
        return jnp.max(s, axis=0, keepdims=True)

    def produce(i, slot):
        s = scores(i - 1, qi - (i - 1))
        s_ref[slot] = s
        return jnp.max(s, axis=0, keepdims=True) + pen_ref[pl.ds(i - 1, 1), :]

    def consume(i, slot, block_max, m):
        n = jnp.where(i == 0, qi, i - 1)
        pen = pen_ref[pl.ds(n, 1), :]
        m_new = jnp.maximum(m, block_max)
        p = jnp.exp2(s_ref[slot] - (m_new - pen)).astype(BF16)
        acc_ref[...] = jnp.exp2(m - m_new) * acc_ref[...] + weighted_values(n, p)
        return m_new

    acc_ref[...] = jnp.zeros_like(acc_ref)
    _pipelined_blocks(qi + 1, produce_first, produce, consume, jnp.full((1, lanes), NEG_INF, F32))
    o_t = _finish(acc_ref[...], HEAD_DIM)
    o_ref[0, q_rows, :] = jnp.concatenate([o_t[:, hd * BLK:(hd + 1) * BLK] for hd in range(n_heads)],
                                          axis=0).T.astype(BF16)


def _moba(qd, kd, vdt, kmean, bias_d):
    b, s, width = qd.shape
    n_heads = width // HEAD_DIM
    n_blocks = s // BLK
    operands = (qd, kd, vdt, kmean.reshape(b, n_blocks, width))
    return pl.pallas_call(
        _per_query_tile(functools.partial(_moba_tile, n_heads=n_heads, n_blocks=n_blocks), n_blocks),
        grid=(b,),
        in_specs=[_seq_spec(t.shape) for t in operands] + [_const_spec(bias_d.shape)],
        out_specs=_seq_spec(qd.shape),
        out_shape=jax.ShapeDtypeStruct((b, s, width), BF16),
        scratch_shapes=[pltpu.VMEM((GATE_ROWS, n_heads * BLK), F32), pltpu.VMEM((VT_ROWS, n_heads * BLK), F32),
                        pltpu.VMEM((2, BLK, n_heads * BLK), F32)],
        compiler_params=_cparams(1),
        name="moba",
    )(*operands, bias_d)


def kernel(x, rel_bias_table, l0_ffn1_pre, l0_ffn1_post, l0_ffn1_wg, l0_ffn1_wu, l0_ffn1_wd, l0_mix_pre, l0_mix_post, l0_w_in, l0_sinks, l0_mla_q_norm, l0_mla_w_uq, l0_mla_kv_norm, l0_mla_w_ukv, l0_w_out, l0_ffn2_pre, l0_ffn2_post, l0_ffn2_wg, l0_ffn2_wu, l0_ffn2_wd, l1_ffn1_pre, l1_ffn1_post, l1_ffn1_wg, l1_ffn1_wu, l1_ffn1_wd, l1_mix_pre, l1_mix_post, l1_w_in, l1_nsa_pe_k, l1_nsa_pe_v, l1_nsa_wk1, l1_nsa_wk2, l1_nsa_wv1, l1_nsa_wv2, l1_w_out, l1_ffn2_pre, l1_ffn2_post, l1_ffn2_wg, l1_ffn2_wu, l1_ffn2_wd):
    b, s, d = x.shape
    m = b * s
    x2 = x.reshape(m, d)
    bias_a, bias_c, bias_d = _bias_tiles(rel_bias_table, ((A_SLOT, 2), (C_SLOT, BIAS_TILES), (D_SLOT, BIAS_TILES)))

    x2 = _ffn(x2, l0_ffn1_pre, l0_ffn1_post, l0_ffn1_wg, l0_ffn1_wu, l0_ffn1_wd, MACARON_WEIGHT)
    qa, ka, vat, qb, kb, vbt = _proj0(x2, s, l0_mix_pre, l0_w_in, l0_mla_q_norm, l0_mla_w_uq,
                                      l0_mla_kv_norm, l0_mla_w_ukv)
    shp = lambda t: t.reshape(b, s, t.shape[-1])
    o_a = _swa(shp(qa), shp(ka), vat, bias_a, l0_sinks)
    o_b = _mla(shp(qb), shp(kb), vbt)
    x2 = _outproj_ffn(x2, o_a.reshape(m, -1), o_b.reshape(m, -1), l0_w_out, l0_mix_post,
                      l0_ffn2_pre, l0_ffn2_post, l0_ffn2_wg, l0_ffn2_wu, l0_ffn2_wd, MACARON_WEIGHT)

    x2 = _ffn(x2, l1_ffn1_pre, l1_ffn1_post, l1_ffn1_wg, l1_ffn1_wu, l1_ffn1_wd, MACARON_WEIGHT)
    qc, kc_hm, vc_hm, ks, vst, kw, vwt, gates, qd, kd, vdt, kmean = _proj1(x2, b, s, l1_mix_pre, l1_w_in)
    kcmp, vcmpt = _compress(kc_hm, vc_hm, l1_nsa_pe_k, l1_nsa_pe_v, l1_nsa_wk1, l1_nsa_wk2, l1_nsa_wv1, l1_nsa_wv2)
    o_c = _nsa(shp(qc), kcmp, vcmpt, shp(ks), vst, shp(kw), vwt, shp(gates), bias_c)
    o_d = _moba(shp(qd), shp(kd), vdt, kmean, bias_d)
    x2 = _outproj_ffn(x2, o_c.reshape(m, -1), o_d.reshape(m, -1), l1_w_out, l1_mix_post,
                      l1_ffn2_pre, l1_ffn2_post, l1_ffn2_wg, l1_ffn2_wu, l1_ffn2_wd, MACARON_WEIGHT)
    return x2.reshape(b, s, d)
```

```python
import functools
import math

import numpy as np
import jax
import jax.numpy as jnp
from jax import lax
from jax.experimental import pallas as pl
from jax.experimental.pallas import tpu as pltpu

F32 = jnp.float32
BF16 = jnp.bfloat16

HEAD_DIM = 64
MACARON_WEIGHT = 0.5
NORM_EPS = 1e-6
LOG2E = math.log2(math.e)
NEG_INF = -1e30
PAD_SCORE = -3e38
FORCE_SCORE = 1e9
REL_BUCKETS = 32
REL_MAX_DIST = 1024
REL_SLOTS = 24
A_SLOT, C_SLOT, D_SLOT = 0, 8, 16
A_HEADS, A_KV_HEADS = 8, 2
MLA_NOPE, MLA_ROPE, MLA_V = 64, 32, 64
MLA_HEADS, MLA_Q_RANK, MLA_KV_RANK = 8, 256, 128
ROPE_THETA = 10000.0
C_HEADS, C_KV_HEADS = 8, 2
CMP_BLOCK, CMP_STRIDE = 32, 16
SEL_BLOCK, SEL_TOPK = 64, 8
D_HEADS = 8
MOBA_TOPK = 3

LANES = 128
SUBLANES = 8
TILE = 128
P0_COLS = (A_HEADS * HEAD_DIM, A_KV_HEADS * HEAD_DIM, A_KV_HEADS * HEAD_DIM, MLA_Q_RANK, MLA_KV_RANK, LANES)
P1_COLS = ((C_HEADS * HEAD_DIM,) + (C_KV_HEADS * HEAD_DIM,) * 6 + (LANES,) + (D_HEADS * HEAD_DIM,) * 3)
N_GATES = 3 * C_HEADS
BLK = 256
ONES_ROWS = 16
GATE_ROWS = 16
VT_ROWS = HEAD_DIM + ONES_ROWS
BIAS_TILES = 9
TOKEN_TILE = 512
PROJ_TILE = 1024
FF_CHUNK = 256
VMEM_LIMIT = 56 * 1024 * 1024


def _cparams(n_axes):
    return pltpu.CompilerParams(dimension_semantics=("arbitrary",) * n_axes, vmem_limit_bytes=VMEM_LIMIT)


def _dot(a, b):
    return jnp.dot(a, b, preferred_element_type=F32)


def _dot_nt(a, b):
    return lax.dot_general(a, b, (((1,), (1,)), ((), ())), preferred_element_type=F32)


def _rms(x, g):
    return x * lax.rsqrt(jnp.mean(x * x, axis=-1, keepdims=True) + NORM_EPS) * g


def _silu(x):
    return x / (1.0 + jnp.exp(-x))


def _col_slices(sizes):
    edges = np.cumsum((0,) + tuple(sizes)).tolist()
    return [slice(a, b) for a, b in zip(edges[:-1], edges[1:])]


def _const_spec(shape):
    nd = len(shape)
    return pl.BlockSpec(shape, lambda *_: (0,) * nd)


def _pipelined_blocks(length, produce_first, produce, consume, state):
    trips = (length - 1) // 2

    def trip(t, carry):
        state, stat_a = carry
        a = 2 * t
        stat_b = produce(a + 1, 1)
        state = consume(a, 0, stat_a, state)
        stat_next = produce(a + 2, 0)
        state = consume(a + 1, 1, stat_b, state)
        return state, stat_next

    carry = lax.fori_loop(0, trips, trip, (state, produce_first(0)))
    a = 2 * trips

    def tail_one(c):
        return consume(a, 0, c[1], c[0])

    def tail_two(c):
        stat_b = produce(a + 1, 1)
        return consume(a + 1, 1, stat_b, consume(a, 0, c[1], c[0]))

    return lax.cond(length - a == 1, tail_one, tail_two, carry)


def _per_query_tile(tile_fn, n_tiles):
    def body(*refs):
        def step(qi, carry):
            tile_fn(qi, *refs)
            return carry

        lax.fori_loop(0, n_tiles, step, 0)

    return body


def _tile_rows(qi):
    return pl.ds(pl.multiple_of(qi * BLK, BLK), BLK)


def _resident_spec(shape):
    nd = len(shape)
    return pl.BlockSpec(shape, lambda *_: (0,) * nd, pipeline_mode=pl.Buffered(1))


def _ffn_tile(x, pre_ref, post_ref, wg_ref, wu_ref, wd_ref, acc_ref, weight):
    h = _rms(x, pre_ref[...]).astype(BF16)
    for c in range(wg_ref.shape[1] // FF_CHUNK):
        cols = slice(c * FF_CHUNK, (c + 1) * FF_CHUNK)
        g = _dot(h, wg_ref[:, cols].astype(BF16))
        u = _dot(h, wu_ref[:, cols].astype(BF16))
        a = (_silu(g) * u).astype(BF16)
        y = _dot(a, wd_ref[cols, :].astype(BF16))
        if c == 0:
            acc_ref[...] = y
        else:
            acc_ref[...] += y
    return x + weight * _rms(acc_ref[...], post_ref[...])


def _ffn_body(x_ref, pre_ref, post_ref, wg_ref, wu_ref, wd_ref, o_ref, acc_ref, *, weight):
    o_ref[...] = _ffn_tile(x_ref[...], pre_ref, post_ref, wg_ref, wu_ref, wd_ref, acc_ref, weight)


def _ffn(x2, pre, post, wg, wu, wd, weight):
    m, d = x2.shape
    d_ff = wg.shape[1]
    tm = TOKEN_TILE
    return pl.pallas_call(
        functools.partial(_ffn_body, weight=weight),
        grid=(m // tm,),
        in_specs=[
            pl.BlockSpec((tm, d), lambda i: (i, 0)),
            _const_spec((1, d)), _const_spec((1, d)),
            _resident_spec((d, d_ff)), _resident_spec((d, d_ff)), _resident_spec((d_ff, d)),
        ],
        out_specs=pl.BlockSpec((tm, d), lambda i: (i, 0)),
        out_shape=jax.ShapeDtypeStruct((m, d), F32),
        scratch_shapes=[pltpu.VMEM((tm, d), F32)],
        compiler_params=_cparams(1),
        name="ffn",
    )(x2, pre.reshape(1, d), post.reshape(1, d), wg, wu, wd)


def _outproj_ffn_body(x_ref, o1_ref, o2_ref, wo_ref, mpost_ref, pre_ref, post_ref, wg_ref, wu_ref, wd_ref,
                      o_ref, acc_ref, *, weight):
    n1 = o1_ref.shape[1]
    y = (_dot(o1_ref[...], wo_ref[0:n1, :].astype(BF16))
         + _dot(o2_ref[...], wo_ref[n1:, :].astype(BF16)))
    x = x_ref[...] + _rms(y, mpost_ref[...])
    o_ref[...] = _ffn_tile(x, pre_ref, post_ref, wg_ref, wu_ref, wd_ref, acc_ref, weight)


def _outproj_ffn(x2, o1, o2, w_out, mix_post, pre, post, wg, wu, wd, weight):
    m, d = x2.shape
    d_ff = wg.shape[1]
    tm = TOKEN_TILE
    tok = lambda n: pl.BlockSpec((tm, n), lambda i: (i, 0))
    return pl.pallas_call(
        functools.partial(_outproj_ffn_body, weight=weight),
        grid=(m // tm,),
        in_specs=[
            tok(d), tok(o1.shape[1]), tok(o2.shape[1]), _resident_spec(w_out.shape), _const_spec((1, d)),
            _const_spec((1, d)), _const_spec((1, d)),
            _resident_spec((d, d_ff)), _resident_spec((d, d_ff)), _resident_spec((d_ff, d)),
        ],
        out_specs=tok(d),
        out_shape=jax.ShapeDtypeStruct((m, d), F32),
        scratch_shapes=[pltpu.VMEM((tm, d), F32)],
        compiler_params=_cparams(1),
        name="outproj_ffn",
    )(x2, o1, o2, w_out, mix_post.reshape(1, d), pre.reshape(1, d), post.reshape(1, d), wg, wu, wd)


def _t5_bucket(dist):
    n = jnp.maximum(dist, 0)
    exact = REL_BUCKETS // 2
    nf = jnp.maximum(n, 1).astype(jnp.float32)
    large = exact + (jnp.log(nf / exact) / math.log(REL_MAX_DIST / exact) * (REL_BUCKETS - exact)).astype(jnp.int32)
    return jnp.where(n < exact, n, jnp.minimum(large, REL_BUCKETS - 1))


def _bias_body(tab_ref, idx_ref, *o_refs, first_slots):
    head = pl.program_id(0)
    for o_ref, first in zip(o_refs, first_slots):
        column = jnp.broadcast_to(LOG2E * tab_ref[pl.ds(first + head, 1), :], (TILE, LANES))
        for d in range(o_ref.shape[1]):
            o_ref[0, d] = jnp.take_along_axis(column, idx_ref[d], axis=1)


def _bias_tiles(rel_table, groups):
    heads = 8
    d = jnp.arange(BIAS_TILES)[:, None, None]
    key = jnp.arange(TILE)[None, :, None]
    query = jnp.arange(TILE)[None, None, :]
    idx = _t5_bucket(d * TILE + query - key).astype(jnp.int32)
    n_buckets, n_slots = rel_table.shape
    table_t = jnp.pad(rel_table.astype(F32).T, ((0, 0), (0, LANES - n_buckets)))
    return pl.pallas_call(
        functools.partial(_bias_body, first_slots=tuple(first for first, _ in groups)),
        grid=(heads,),
        in_specs=[_const_spec((n_slots, LANES)), _const_spec((BIAS_TILES, TILE, TILE))],
        out_specs=[pl.BlockSpec((1, n, TILE, TILE), lambda s: (s, 0, 0, 0)) for _, n in groups],
        out_shape=[jax.ShapeDtypeStruct((heads, n, TILE, TILE), F32) for _, n in groups],
        compiler_params=_cparams(1),
        name="bias_tiles",
    )(table_t, idx)


def _proj0_body(x_ref, pre_ref, w_ref, qn_ref, wq1_ref, wq2_ref, kvn_ref, wk_ref, wv_ref, cos_ref, sin_ref,
                qa_ref, ka_ref, vat_ref, qb_ref, kb_ref, vbt_ref):
    h = _rms(x_ref[...], pre_ref[...]).astype(BF16)
    y = _dot(h, w_ref[...])
    qa, ka, va, cq, ckv, rot = (y[:, sl] for sl in _col_slices(P0_COLS))
    qa_ref[...] = (qa * (LOG2E * HEAD_DIM ** -0.5)).astype(BF16)
    ka_ref[...] = ka.astype(BF16)
    for j in range(y.shape[0] // TILE):
        _store_vt_ones(vat_ref, j, va[j * TILE:(j + 1) * TILE], A_KV_HEADS)
    cq = _rms(cq, qn_ref[...]).astype(BF16)
    ckv = _rms(ckv, kvn_ref[...]).astype(BF16)
    cos_t = cos_ref[...]
    sin_t = sin_ref[...]
    lane = lax.broadcasted_iota(jnp.int32, cos_t.shape, 1)
    scale = LOG2E * (MLA_NOPE + MLA_ROPE) ** -0.5
    qmul = scale * (cos_t + jnp.where(lane < MLA_NOPE, 1.0, 0.0))
    qsin = scale * sin_t
    kr = rot * cos_t + pltpu.roll(rot, LANES // 2, 1) * sin_t
    q1 = _dot(cq, wq1_ref[...])
    q2 = _dot(cq, wq2_ref[...])
    k1 = _dot(ckv, wk_ref[...])
    for hd in range(MLA_HEADS):
        sl = slice(hd * LANES, (hd + 1) * LANES)
        qb_ref[:, sl] = (q1[:, sl] * qmul + q2[:, sl] * qsin).astype(BF16)
        kb_ref[:, sl] = (k1[:, sl] + kr).astype(BF16)
    vb = _dot(ckv, wv_ref[...])
    for j in range(vb.shape[0] // BLK):
        _store_vt_ones(vbt_ref, j, vb[j * BLK:(j + 1) * BLK], MLA_HEADS)


def _rope_lane_tables(s):
    inv = ROPE_THETA ** (-jnp.arange(0, MLA_ROPE, 2, dtype=jnp.float32) / MLA_ROPE)
    ang = jnp.arange(s, dtype=jnp.float32)[:, None] * inv[None, :]
    cos, sin = jnp.cos(ang), jnp.sin(ang)
    z_lo = jnp.zeros((s, MLA_NOPE), F32)
    z_hi = jnp.zeros((s, LANES - MLA_NOPE - MLA_ROPE), F32)
    return (jnp.concatenate([z_lo, cos, cos, z_hi], axis=1), jnp.concatenate([z_lo, sin, sin, z_hi], axis=1))


def _rot_cols(w):
    half = w.shape[-1] // 2
    return jnp.concatenate([-w[..., half:], w[..., :half]], axis=-1)


def _proj0(x2, seq, pre, w_in, q_norm, w_uq, kv_norm, w_ukv):
    m, d = x2.shape
    tm = PROJ_TILE
    n_lin = sum(P0_COLS[:-1])
    kr_w = w_in[:, n_lin:n_lin + MLA_ROPE]
    z32 = jnp.zeros((d, MLA_ROPE), F32)
    w0 = jnp.concatenate([w_in[:, :n_lin], _rot_cols(kr_w), z32, kr_w, z32], axis=1).astype(BF16)
    wq = w_uq.reshape(-1, MLA_HEADS, MLA_NOPE + MLA_ROPE)
    rq = wq.shape[0]
    pad = LANES - MLA_NOPE - MLA_ROPE
    zq = jnp.zeros((rq, MLA_HEADS, pad), F32)
    wq1 = jnp.concatenate([wq, zq], axis=2).reshape(rq, -1).astype(BF16)
    wq2 = jnp.concatenate([jnp.zeros((rq, MLA_HEADS, MLA_NOPE), F32), _rot_cols(wq[:, :, MLA_NOPE:]), zq],
                          axis=2).reshape(rq, -1).astype(BF16)
    wkv = w_ukv.reshape(-1, MLA_HEADS, MLA_NOPE + MLA_V)
    rk = wkv.shape[0]
    wk = jnp.concatenate([wkv[:, :, :MLA_NOPE], jnp.zeros((rk, MLA_HEADS, LANES - MLA_NOPE), F32)],
                         axis=2).reshape(rk, -1).astype(BF16)
    wv = wkv[:, :, MLA_NOPE:].reshape(rk, -1).astype(BF16)
    cos_t, sin_t = _rope_lane_tables(seq)
    n_st = seq // tm
    tok = lambda n: pl.BlockSpec((tm, n), lambda i: (i, 0))
    outs = pl.pallas_call(
        _proj0_body,
        grid=(m // tm,),
        in_specs=[
            tok(d), _const_spec((1, d)), _const_spec(w0.shape),
            _const_spec((1, rq)), _const_spec(wq1.shape), _const_spec(wq2.shape),
            _const_spec((1, rk)), _const_spec(wk.shape), _const_spec(wv.shape),
            pl.BlockSpec((tm, LANES), lambda i: (i % n_st, 0)),
            pl.BlockSpec((tm, LANES), lambda i: (i % n_st, 0)),
        ],
        out_specs=[tok(P0_COLS[0]), tok(P0_COLS[1]),
                   pl.BlockSpec((1, tm // TILE, A_KV_HEADS * VT_ROWS, TILE), lambda i: (i // n_st, i % n_st, 0, 0)),
                   tok(MLA_HEADS * LANES), tok(MLA_HEADS * LANES),
                   pl.BlockSpec((1, tm // BLK, MLA_HEADS * VT_ROWS, BLK), lambda i: (i // n_st, i % n_st, 0, 0))],
        out_shape=[jax.ShapeDtypeStruct((m, P0_COLS[0]), BF16), jax.ShapeDtypeStruct((m, P0_COLS[1]), BF16),
                   jax.ShapeDtypeStruct((m // seq, seq // TILE, A_KV_HEADS * VT_ROWS, TILE), BF16),
                   jax.ShapeDtypeStruct((m, MLA_HEADS * LANES), BF16),
                   jax.ShapeDtypeStruct((m, MLA_HEADS * LANES), BF16),
                   jax.ShapeDtypeStruct((m // seq, seq // BLK, MLA_HEADS * VT_ROWS, BLK), BF16)],
        compiler_params=_cparams(1),
        name="proj0",
    )(x2, pre.reshape(1, d), w0, q_norm.reshape(1, rq), wq1, wq2, kv_norm.reshape(1, rk), wk, wv, cos_t, sin_t)
    return outs


def _swa_body(sink_ref, q_ref, k_ref, vt_ref, bias_ref, o_ref, s_ref, *, n_kv, grp, n_tiles):
    halves = BLK // TILE
    items = [(hkv, half) for hkv in range(n_kv) for half in range(halves)]
    width = grp * TILE
    krow = lax.broadcasted_iota(jnp.int32, (TILE, len(items) * width), 0)
    qcol = lax.broadcasted_iota(jnp.int32, (TILE, len(items) * width), 1) & (TILE - 1)
    sink = jnp.concatenate([jnp.full((1, TILE), LOG2E * sink_ref[hkv * grp + g], F32)
                            for hkv, _ in items for g in range(grp)], axis=1)

    def produce(qi, slot):
        cur_s, prev_s, has_prev = [], [], []
        for hkv, half in items:
            sl = slice(hkv * HEAD_DIM, (hkv + 1) * HEAD_DIM)
            t = qi * halves + half
            tp = jnp.maximum(t - 1, 0)
            q_rows = pl.ds(pl.multiple_of(t * TILE, TILE), TILE)
            qs = jnp.concatenate([q_ref[0, q_rows, (hkv * grp + g) * HEAD_DIM:(hkv * grp + g + 1) * HEAD_DIM]
                                  for g in range(grp)], axis=0)
            bias_c = jnp.concatenate([bias_ref[hkv * grp + g, 0] for g in range(grp)], axis=1)
            bias_p = jnp.concatenate([bias_ref[hkv * grp + g, 1] for g in range(grp)], axis=1)
            cur_s.append(_dot_nt(k_ref[0, pl.ds(pl.multiple_of(t * TILE, TILE), TILE), sl], qs) + bias_c)
            prev_s.append(_dot_nt(k_ref[0, pl.ds(pl.multiple_of(tp * TILE, TILE), TILE), sl], qs) + bias_p)
            has_prev.append(jnp.full((1, width), jnp.where(t >= 1, 0, TILE), jnp.int32))
        s_c = jnp.where(krow <= qcol, jnp.concatenate(cur_s, axis=1), NEG_INF)
        s_p = jnp.where(krow > qcol + jnp.concatenate(has_prev, axis=1),
                        jnp.concatenate(prev_s, axis=1), NEG_INF)
        s_ref[slot, 0] = s_c
        s_ref[slot, 1] = s_p
        return jnp.maximum(jnp.maximum(jnp.max(s_c, axis=0, keepdims=True), jnp.max(s_p, axis=0, keepdims=True)), sink)

    def consume(qi, slot, mx, state):
        p_c = jnp.exp2(s_ref[slot, 0] - mx).astype(BF16)
        p_p = jnp.exp2(s_ref[slot, 1] - mx).astype(BF16)
        accs = []
        for i, (hkv, half) in enumerate(items):
            vsl = slice(hkv * VT_ROWS, (hkv + 1) * VT_ROWS)
            ls = slice(i * width, (i + 1) * width)
            t = qi * halves + half
            accs.append(_dot(vt_ref[0, t, vsl, :], p_c[:, ls])
                        + _dot(vt_ref[0, jnp.maximum(t - 1, 0), vsl, :], p_p[:, ls]))
        acc = jnp.concatenate(accs, axis=1)
        o_t = acc[0:HEAD_DIM] / (acc[HEAD_DIM:HEAD_DIM + 1] + jnp.exp2(sink - mx))
        heads = []
        for hkv in range(n_kv):
            for g in range(grp):
                heads.append(jnp.concatenate(
                    [o_t[:, (hkv * halves + half) * width + g * TILE:(hkv * halves + half) * width + (g + 1) * TILE]
                     for half in range(halves)], axis=1))
        o_ref[0, _tile_rows(qi), :] = jnp.concatenate(heads, axis=0).T.astype(BF16)
        return state

    _pipelined_blocks(n_tiles, functools.partial(produce, 0), produce, consume, 0)


def _seq_spec(shape):
    nd = len(shape)
    return pl.BlockSpec((1,) + tuple(shape[1:]), lambda bi: (bi,) + (0,) * (nd - 1))


def _swa(qa, ka, vat, bias_a, sinks):
    b, s, width = qa.shape
    n_kv = ka.shape[-1] // HEAD_DIM
    grp = width // HEAD_DIM // n_kv
    return pl.pallas_call(
        functools.partial(_swa_body, n_kv=n_kv, grp=grp, n_tiles=s // BLK),
        grid=(b,),
        in_specs=[pl.BlockSpec(memory_space=pltpu.SMEM), _seq_spec(qa.shape), _seq_spec(ka.shape),
                  _seq_spec(vat.shape), _const_spec(bias_a.shape)],
        out_specs=_seq_spec(qa.shape),
        out_shape=jax.ShapeDtypeStruct((b, s, width), BF16),
        scratch_shapes=[pltpu.VMEM((2, 2, TILE, n_kv * (BLK // TILE) * grp * TILE), F32)],
        compiler_params=_cparams(1),
        name="swa",
    )(sinks.astype(F32), qa, ka, vat, bias_a)


def _finish(acc, dv):
    return acc[0:dv] / acc[dv:dv + 1]


def _mla_tile(qi, q_ref, k_ref, vt_ref, o_ref, acc_ref, s_ref, *, n_heads):
    q_rows = _tile_rows(qi)
    lanes = n_heads * BLK
    krow = lax.broadcasted_iota(jnp.int32, (BLK, lanes), 0)
    qcol = lax.broadcasted_iota(jnp.int32, (BLK, lanes), 1) & (BLK - 1)

    def scores(n):
        st = pl.multiple_of(n * BLK, BLK)
        return jnp.concatenate(
            [_dot_nt(k_ref[0, pl.ds(st, BLK), hd * LANES:(hd + 1) * LANES], q_ref[0, q_rows, hd * LANES:(hd + 1) * LANES])
             for hd in range(n_heads)], axis=1)

    def weighted_values(n, p):
        return jnp.concatenate(
            [_dot(vt_ref[0, n, hd * VT_ROWS:(hd + 1) * VT_ROWS, :], p[:, hd * BLK:(hd + 1) * BLK])
             for hd in range(n_heads)], axis=1)

    def produce_first(slot):
        s = jnp.where(krow <= qcol, scores(qi), NEG_INF)
        s_ref[slot] = s
        return jnp.max(s, axis=0, keepdims=True)

    def produce(i, slot):
        s = scores(i - 1)
        s_ref[slot] = s
        return jnp.max(s, axis=0, keepdims=True)

    def consume(i, slot, block_max, m):
        n = jnp.where(i == 0, qi, i - 1)
        m_new = jnp.maximum(m, block_max)
        p = jnp.exp2(s_ref[slot] - m_new).astype(BF16)
        acc_ref[...] = jnp.exp2(m - m_new) * acc_ref[...] + weighted_values(n, p)
        return m_new

    acc_ref[...] = jnp.zeros_like(acc_ref)
    _pipelined_blocks(qi + 1, produce_first, produce, consume, jnp.full((1, lanes), NEG_INF, F32))
    o_t = _finish(acc_ref[...], MLA_V)
    o_ref[0, q_rows, :] = jnp.concatenate([o_t[:, hd * BLK:(hd + 1) * BLK] for hd in range(n_heads)],
                                          axis=0).T.astype(BF16)


def _mla(qb, kb, vbt):
    b, s, width = qb.shape
    n_heads = width // LANES
    out_shape = jax.ShapeDtypeStruct((b, s, n_heads * MLA_V), BF16)
    return pl.pallas_call(
        _per_query_tile(functools.partial(_mla_tile, n_heads=n_heads), s // BLK),
        grid=(b,),
        in_specs=[_seq_spec(qb.shape), _seq_spec(kb.shape), _seq_spec(vbt.shape)],
        out_specs=_seq_spec(out_shape.shape),
        out_shape=out_shape,
        scratch_shapes=[pltpu.VMEM((VT_ROWS, n_heads * BLK), F32), pltpu.VMEM((2, BLK, n_heads * BLK), F32)],
        compiler_params=_cparams(1),
        name="mla",
    )(qb, kb, vbt)


def _store_vt_ones(vt_ref, j, v, n_heads):
    vt = v.T.astype(BF16)
    ones = jnp.ones((ONES_ROWS, v.shape[0]), BF16)
    for hd in range(n_heads):
        vt_ref[0, j, hd * VT_ROWS:hd * VT_ROWS + HEAD_DIM, :] = vt[hd * HEAD_DIM:(hd + 1) * HEAD_DIM]
        vt_ref[0, j, hd * VT_ROWS + HEAD_DIM:(hd + 1) * VT_ROWS, :] = ones


def _proj1_body(x_ref, pre_ref, w_ref, qc_ref, kx_ref, vx_ref, ks_ref, vst_ref, kw_ref, vwt_ref, g_ref,
                qd_ref, kd_ref, vdt_ref, km_ref, kstage_ref, vstage_ref, *, tm):
    h = _rms(x_ref[...], pre_ref[...]).astype(BF16)
    y = _dot(h, w_ref[...])
    qc, kc, vc, ks, vs, kw, vw, gl, qd, kd, vd = (y[:, sl] for sl in _col_slices(P1_COLS))
    scale = LOG2E * HEAD_DIM ** -0.5
    qc_ref[...] = (qc * scale).astype(BF16)
    for stage_ref, cols, x_out in ((kstage_ref, kc, kx_ref), (vstage_ref, vc, vx_ref)):
        stage_ref[...] = cols
        for i in range(CMP_STRIDE):
            piece = stage_ref[pl.ds(i, tm // CMP_STRIDE, stride=CMP_STRIDE), :].astype(BF16)
            for hkv in range(C_KV_HEADS):
                x_out[0, hkv, :, i * HEAD_DIM:(i + 1) * HEAD_DIM] = piece[:, hkv * HEAD_DIM:(hkv + 1) * HEAD_DIM]
    ks_ref[...] = ks.astype(BF16)
    kw_ref[...] = kw.astype(BF16)
    g_ref[...] = 1.0 / (1.0 + jnp.exp(-gl))
    qd_ref[...] = (qd * scale).astype(BF16)
    kd_ref[...] = kd.astype(BF16)
    for j in range(tm // BLK):
        rows = slice(j * BLK, (j + 1) * BLK)
        km_ref[j] = jnp.mean(kd[rows], axis=0, keepdims=True)
        _store_vt_ones(vdt_ref, j, vd[rows], D_HEADS)
        _store_vt_ones(vst_ref, j, vs[rows], C_KV_HEADS)
        _store_vt_ones(vwt_ref, j, vw[rows], C_KV_HEADS)


def _proj1(x2, batch, seq, pre, w_in):
    m, d = x2.shape
    tm = PROJ_TILE
    n_st = seq // tm
    n_pre = sum(P1_COLS[:7]) + N_GATES
    pad = jnp.zeros((d, LANES - N_GATES), F32)
    w1 = jnp.concatenate([w_in[:, :n_pre], pad, w_in[:, n_pre:]], axis=1).astype(BF16)
    q_w, kv_w, d_w = C_HEADS * HEAD_DIM, C_KV_HEADS * HEAD_DIM, D_HEADS * HEAD_DIM
    tok = lambda n: pl.BlockSpec((tm, n), lambda i: (i, 0))
    tok_shape = lambda n, dt=BF16: jax.ShapeDtypeStruct((m, n), dt)
    feat = CMP_STRIDE * HEAD_DIM
    hm = pl.BlockSpec((1, C_KV_HEADS, tm // CMP_STRIDE, feat), lambda i: (i // n_st, 0, i % n_st, 0))
    hm_shape = jax.ShapeDtypeStruct((batch, C_KV_HEADS, seq // CMP_STRIDE, feat), BF16)
    nblk = tm // BLK
    vt_spec = lambda heads: pl.BlockSpec((1, nblk, heads * VT_ROWS, BLK), lambda i: (i // n_st, i % n_st, 0, 0))
    vt_shape = lambda heads: jax.ShapeDtypeStruct((batch, seq // BLK, heads * VT_ROWS, BLK), BF16)
    return pl.pallas_call(
        functools.partial(_proj1_body, tm=tm),
        grid=(m // tm,),
        in_specs=[tok(d), _const_spec((1, d)), _const_spec(w1.shape)],
        out_specs=[tok(q_w), hm, hm, tok(kv_w), vt_spec(C_KV_HEADS), tok(kv_w), vt_spec(C_KV_HEADS), tok(LANES),
                   tok(d_w), tok(d_w), vt_spec(D_HEADS),
                   pl.BlockSpec((nblk, 1, d_w), lambda i: (i, 0, 0))],
        out_shape=[
            tok_shape(q_w), hm_shape, hm_shape,
            tok_shape(kv_w), vt_shape(C_KV_HEADS), tok_shape(kv_w), vt_shape(C_KV_HEADS), tok_shape(LANES, F32),
            tok_shape(d_w), tok_shape(d_w), vt_shape(D_HEADS),
            jax.ShapeDtypeStruct((m // BLK, 1, d_w), F32),
        ],
        scratch_shapes=[pltpu.VMEM((tm, kv_w), F32), pltpu.VMEM((tm, kv_w), F32)],
        compiler_params=_cparams(1),
        name="proj1",
    )(x2, pre.reshape(1, d), w1)


def _compress_body(kx_ref, vx_ref, pek_ref, pev_ref, wk1_ref, wk2_ref, wv1_ref, wv2_ref, ko_ref, vo_ref):
    half = CMP_STRIDE * HEAD_DIM
    for x_ref, pe_ref, w1_ref, w2_ref, o_ref in ((kx_ref, pek_ref, wk1_ref, wk2_ref, ko_ref),
                                                 (vx_ref, pev_ref, wv1_ref, wv2_ref, vo_ref)):
        pe8 = jnp.broadcast_to(pe_ref[...], (SUBLANES, 2 * half)).astype(BF16)
        pe_term = _dot(pe8, w1_ref[...])[0:1]
        outs = []
        for hkv in range(C_KV_HEADS):
            x = x_ref[0, hkv]
            lo = _dot(x, w1_ref[0:half])
            hi = _dot(x, w1_ref[half:2 * half])
            hid = lo + pltpu.roll(hi, x.shape[0] - 1, 0) + pe_term
            outs.append(_dot(_silu(hid).astype(BF16), w2_ref[...]))
        out = jnp.concatenate(outs, axis=1)
        o_ref[0] = (out.T if o_ref is vo_ref else out).astype(BF16)


def _compress(kx, vx, pe_k, pe_v, wk1, wk2, wv1, wv2):
    b, _, rows, feat = kx.shape
    dh = feat // CMP_STRIDE
    xspec = pl.BlockSpec((1, 2, rows, feat), lambda bi: (bi, 0, 0, 0))
    ospec = pl.BlockSpec((1, rows, 2 * dh), lambda bi: (bi, 0, 0))
    w1s, w2s = wk1.shape, wk2.shape
    return pl.pallas_call(
        _compress_body,
        grid=(b,),
        in_specs=[xspec, xspec, _const_spec((1, 2 * feat)), _const_spec((1, 2 * feat)),
                  _const_spec(w1s), _const_spec(w2s), _const_spec(w1s), _const_spec(w2s)],
        out_specs=[ospec, ospec],
        out_shape=[jax.ShapeDtypeStruct((b, rows, 2 * dh), BF16)] * 2,
        compiler_params=_cparams(1),
        name="nsa_compress",
    )(kx, vx, pe_k.astype(F32).reshape(1, -1), pe_v.astype(F32).reshape(1, -1),
      wk1.astype(BF16), wk2.astype(BF16), wv1.astype(BF16), wv2.astype(BF16))


def _bias_block(bias_ref, hd, dist_blocks):
    sub = BLK // TILE
    rows = []
    for a in range(sub):
        cols = []
        for bq in range(sub):
            e = sub * dist_blocks + bq - a
            e = (min(max(e, 0), BIAS_TILES - 1) if isinstance(e, int)
                 else jnp.minimum(jnp.maximum(e, 0), BIAS_TILES - 1))
            cols.append(bias_ref[hd, e])
        rows.append(jnp.concatenate(cols, axis=1))
    return jnp.concatenate(rows, axis=0)


def _topk_rows(val, ridx, k):
    n_rows = val.shape[0]
    picked = jnp.zeros(val.shape, F32)
    for _ in range(k):
        best = jnp.max(val, axis=0, keepdims=True)
        first = jnp.min(jnp.where(val == best, ridx, n_rows), axis=0, keepdims=True)
        hit = ridx == first
        picked = jnp.where(hit, 1.0, picked)
        val = jnp.where(hit, PAD_SCORE, val)
    return picked


def _stack_group_q(q_ref, qi, hkv, grp):
    return jnp.concatenate([q_ref[0, _tile_rows(qi), (hkv * grp + g) * HEAD_DIM:(hkv * grp + g + 1) * HEAD_DIM]
                            for g in range(grp)], axis=0)


def _group_bias(bias_ref, hkv, grp, dist_blocks):
    return jnp.concatenate([_bias_block(bias_ref, hkv * grp + g, dist_blocks) for g in range(grp)], axis=1)


def _nsa_window_pass(q_ref, kw_ref, vwt_ref, bias_ref, sw_ref, owin_ref, n_tiles):
    grp = C_HEADS // C_KV_HEADS
    lanes = grp * BLK
    krow = lax.broadcasted_iota(jnp.int32, (BLK, lanes), 0)
    qcol = lax.broadcasted_iota(jnp.int32, (BLK, lanes), 1) & (BLK - 1)

    def produce(qi, slot):
        own = pl.multiple_of(qi * BLK, BLK)
        prev = pl.multiple_of(jnp.maximum(qi - 1, 0) * BLK, BLK)
        has_prev = jnp.where(qi >= 1, 0, BLK)
        stats = []
        for hkv in range(C_KV_HEADS):
            sl = slice(hkv * HEAD_DIM, (hkv + 1) * HEAD_DIM)
            qs = _stack_group_q(q_ref, qi, hkv, grp)
            s0 = _dot_nt(kw_ref[0, pl.ds(own, BLK), sl], qs) + _group_bias(bias_ref, hkv, grp, 0)
            s1 = _dot_nt(kw_ref[0, pl.ds(prev, BLK), sl], qs) + _group_bias(bias_ref, hkv, grp, 1)
            s0 = jnp.where(krow <= qcol, s0, NEG_INF)
            s1 = jnp.where(krow > qcol + has_prev, s1, NEG_INF)
            sw_ref[slot, hkv, 0] = s0
            sw_ref[slot, hkv, 1] = s1
            stats.append(jnp.maximum(jnp.max(s0, axis=0, keepdims=True), jnp.max(s1, axis=0, keepdims=True)))
        return tuple(stats)

    def consume(qi, slot, mws, state):
        prev_blk = jnp.maximum(qi - 1, 0)
        for hkv in range(C_KV_HEADS):
            vsl = slice(hkv * VT_ROWS, (hkv + 1) * VT_ROWS)
            ow = (_dot(vwt_ref[0, qi, vsl, :], jnp.exp2(sw_ref[slot, hkv, 0] - mws[hkv]).astype(BF16))
                  + _dot(vwt_ref[0, prev_blk, vsl, :], jnp.exp2(sw_ref[slot, hkv, 1] - mws[hkv]).astype(BF16)))
            owin_ref[qi, hkv] = _finish(ow, HEAD_DIM)
        return state

    _pipelined_blocks(n_tiles, functools.partial(produce, 0), produce, consume, 0)


def _nsa_body(q_ref, kc_ref, vct_ref, ks_ref, vst_ref, kw_ref, vwt_ref, g_ref, bias_ref, ovt_ref, o_ref,
              pen_ref, acc_ref, s_ref, sw_ref, owin_ref, sc_ref, ocmp_ref, *, n_sel, n_tiles):
    _nsa_window_pass(q_ref, kw_ref, vwt_ref, bias_ref, sw_ref, owin_ref, n_tiles)
    _nsa_compressed_pass(q_ref, kc_ref, vct_ref, ovt_ref, sc_ref, ocmp_ref, pen_ref, n_tiles, n_sel)

    def step(qi, carry):
        _nsa_tile(qi, q_ref, ks_ref, vst_ref, g_ref, bias_ref, o_ref, pen_ref, acc_ref, s_ref, owin_ref, ocmp_ref)
        return carry

    lax.fori_loop(0, n_tiles, step, 0)


def _nsa_compressed_pass(q_ref, kc_ref, vct_ref, ovt_ref, sc_ref, ocmp_ref, pen_ref, n_tiles, n_sel):
    grp = C_HEADS // C_KV_HEADS
    lanes = grp * BLK
    sub = BLK // SEL_BLOCK
    crow = lax.broadcasted_iota(jnp.int32, (TILE, lanes), 0)
    cq = lax.broadcasted_iota(jnp.int32, (TILE, lanes), 1) & (BLK - 1)
    jrow = lax.broadcasted_iota(jnp.int32, (n_sel, BLK), 0)
    jq = lax.broadcasted_iota(jnp.int32, (n_sel, BLK), 1) // SEL_BLOCK

    def visible(qi):
        return CMP_STRIDE * crow + (CMP_BLOCK - 1) <= qi * BLK + cq

    def produce(qi, slot):
        stats = []
        for hkv in range(C_KV_HEADS):
            sl = slice(hkv * HEAD_DIM, (hkv + 1) * HEAD_DIM)
            sc = jnp.where(visible(qi), _dot_nt(kc_ref[0, :, sl], _stack_group_q(q_ref, qi, hkv, grp)), NEG_INF)
            sc_ref[slot, hkv] = sc
            stats.append(jnp.max(sc, axis=0, keepdims=True))
        return tuple(stats)

    def consume(qi, slot, mcs, state):
        cur = qi * sub + jq
        forced = (jrow == 0) | (jrow == cur) | (jrow == cur - 1)
        for hkv in range(C_KV_HEADS):
            sl = slice(hkv * HEAD_DIM, (hkv + 1) * HEAD_DIM)
            ec = jnp.where(visible(qi), jnp.exp2(sc_ref[slot, hkv] - mcs[hkv]), 0.0)
            lc = jnp.sum(ec, axis=0, keepdims=True)
            p = ec / jnp.where(lc > 0.0, lc, 1.0)
            ocmp_ref[qi, hkv] = _dot(vct_ref[0, sl, :], p.astype(BF16))
            ps = p[:, 0:BLK] + p[:, BLK:2 * BLK] + p[:, 2 * BLK:3 * BLK] + p[:, 3 * BLK:4 * BLK]
            ps_hi = ps.astype(BF16)
            ps_lo = (ps - ps_hi.astype(F32)).astype(BF16)
            imp = _dot(ovt_ref[...], ps_hi) + _dot(ovt_ref[...], ps_lo)
            val = jnp.where(forced, FORCE_SCORE, jnp.where(jrow <= cur, imp, NEG_INF))
            pen = jnp.where(_topk_rows(val, jrow, SEL_TOPK) > 0.5, 0.0, NEG_INF)
            for t in range(n_sel // sub):
                pen_ref[qi, hkv, t, 0:sub, :] = pen[t * sub:(t + 1) * sub]
        return state

    _pipelined_blocks(n_tiles, functools.partial(produce, 0), produce, consume, 0)


def _nsa_tile(qi, q_ref, ks_ref, vst_ref, g_ref, bias_ref, o_ref, pen_ref, acc_ref, s_ref, owin_ref, ocmp_ref):
    q_rows = _tile_rows(qi)
    grp = C_HEADS // C_KV_HEADS
    lanes = grp * BLK
    sub = BLK // SEL_BLOCK
    krow = lax.broadcasted_iota(jnp.int32, (BLK, lanes), 0)
    qcol = lax.broadcasted_iota(jnp.int32, (BLK, lanes), 1) & (BLK - 1)
    causal = krow <= qcol
    gates_t = g_ref[0, q_rows, :].T

    def stack_q(hkv):
        return _stack_group_q(q_ref, qi, hkv, grp)

    def sel_scores(hkv, n, dist_blocks):
        st = pl.multiple_of(n * BLK, BLK)
        sl = slice(hkv * HEAD_DIM, (hkv + 1) * HEAD_DIM)
        return (_dot_nt(ks_ref[0, pl.ds(st, BLK), sl], stack_q(hkv))
                + _group_bias(bias_ref, hkv, grp, dist_blocks))

    def sel_pens(hkv, n):
        pens = pen_ref[qi, hkv, n, 0:sub, :]
        return [jnp.concatenate([pens[j:j + 1]] * grp, axis=1) for j in range(sub)]

    def sel_max(s, pens):
        mj = [jnp.max(s[j * SEL_BLOCK:(j + 1) * SEL_BLOCK], axis=0, keepdims=True) + pens[j] for j in range(sub)]
        return functools.reduce(jnp.maximum, mj)

    def sel_probs(s, pens, m_new):
        return jnp.concatenate([jnp.exp2(s[j * SEL_BLOCK:(j + 1) * SEL_BLOCK] - (m_new - pens[j]))
                                for j in range(sub)], axis=0).astype(BF16)

    def produce_first(slot):
        stats = []
        for hkv in range(C_KV_HEADS):
            s = jnp.where(causal, sel_scores(hkv, qi, 0), NEG_INF)
            s_ref[slot, hkv] = s
            stats.append(sel_max(s, sel_pens(hkv, qi)))
        return tuple(stats)

    def produce(i, slot):
        stats = []
        for hkv in range(C_KV_HEADS):
            s = sel_scores(hkv, i - 1, qi - (i - 1))
            s_ref[slot, hkv] = s
            stats.append(sel_max(s, sel_pens(hkv, i - 1)))
        return tuple(stats)

    def consume(i, slot, block_max, ms):
        n = jnp.where(i == 0, qi, i - 1)
        new = []
        for hkv in range(C_KV_HEADS):
            vsl = slice(hkv * VT_ROWS, (hkv + 1) * VT_ROWS)
            m_new = jnp.maximum(ms[hkv], block_max[hkv])
            pb = sel_probs(s_ref[slot, hkv], sel_pens(hkv, n), m_new)
            acc_ref[hkv] = jnp.exp2(ms[hkv] - m_new) * acc_ref[hkv] + _dot(vst_ref[0, n, vsl, :], pb)
            new.append(m_new)
        return tuple(new)

    acc_ref[...] = jnp.zeros_like(acc_ref)
    m_init = jnp.full((1, lanes), NEG_INF, F32)
    _pipelined_blocks(qi + 1, produce_first, produce, consume, (m_init, m_init))
    outs = []
    for hkv in range(C_KV_HEADS):
        o_slc = _finish(acc_ref[hkv], HEAD_DIM)
        for g in range(grp):
            hd = hkv * grp + g
            ls = slice(g * BLK, (g + 1) * BLK)
            g_cmp, g_slc, g_win = (gates_t[t * C_HEADS + hd:t * C_HEADS + hd + 1] for t in range(N_GATES // C_HEADS))
            outs.append(g_cmp * ocmp_ref[qi, hkv, :, ls] + g_slc * o_slc[:, ls] + g_win * owin_ref[qi, hkv, :, ls])
    o_ref[0, q_rows, :] = jnp.concatenate(outs, axis=0).T.astype(BF16)


def _nsa(qc, kcmp, vcmpt, ks, vst, kw, vwt, gates, bias_c):
    b, s, width = qc.shape
    n_blocks = s // BLK
    n_sel = s // SEL_BLOCK
    n_c = (s - CMP_BLOCK) // CMP_STRIDE + 1
    j_start = np.arange(n_sel)[:, None] * SEL_BLOCK
    c_start = np.arange(TILE)[None, :] * CMP_STRIDE
    overlap_t = ((c_start < j_start + SEL_BLOCK) & (c_start + CMP_BLOCK > j_start)
                 & (np.arange(TILE)[None, :] < n_c))
    ovt = jnp.asarray(overlap_t, BF16)
    group_lanes = C_HEADS // C_KV_HEADS * BLK
    operands = (qc, kcmp, vcmpt, ks, vst, kw, vwt, gates)
    return pl.pallas_call(
        functools.partial(_nsa_body, n_sel=n_sel, n_tiles=n_blocks),
        grid=(b,),
        in_specs=[_seq_spec(t.shape) for t in operands] + [_const_spec(bias_c.shape), _const_spec(ovt.shape)],
        out_specs=_seq_spec(qc.shape),
        out_shape=jax.ShapeDtypeStruct((b, s, width), BF16),
        scratch_shapes=[pltpu.VMEM((n_blocks, C_KV_HEADS, n_blocks, SUBLANES, BLK), F32),
                        pltpu.VMEM((C_KV_HEADS, VT_ROWS, group_lanes), F32),
                        pltpu.VMEM((2, C_KV_HEADS, BLK, group_lanes), F32),
                        pltpu.VMEM((2, C_KV_HEADS, 2, BLK, group_lanes), F32),
                        pltpu.VMEM((n_blocks, C_KV_HEADS, HEAD_DIM, group_lanes), F32),
                        pltpu.VMEM((2, C_KV_HEADS, TILE, group_lanes), F32),
                        pltpu.VMEM((n_blocks, C_KV_HEADS, HEAD_DIM, group_lanes), F32)],
        compiler_params=_cparams(1),
        name="nsa",
    )(*operands, bias_c, ovt)


def _moba_tile(qi, q_ref, k_ref, vt_ref, km_ref, bias_ref, o_ref, pen_ref, acc_ref, s_ref, *, n_heads, n_blocks):
    q_rows = _tile_rows(qi)
    lanes = n_heads * BLK
    nidx = lax.broadcasted_iota(jnp.int32, (GATE_ROWS, lanes), 0)
    krow = lax.broadcasted_iota(jnp.int32, (BLK, lanes), 0)
    qcol = lax.broadcasted_iota(jnp.int32, (BLK, lanes), 1) & (BLK - 1)

    def scores(n, dist_blocks):
        st = pl.multiple_of(n * BLK, BLK)
        parts = []
        for hd in range(n_heads):
            sl = slice(hd * HEAD_DIM, (hd + 1) * HEAD_DIM)
            parts.append(_dot_nt(k_ref[0, pl.ds(st, BLK), sl], q_ref[0, q_rows, sl])
                         + _bias_block(bias_ref, hd, dist_blocks))
        return jnp.concatenate(parts, axis=1)

    def weighted_values(n, p):
        return jnp.concatenate(
            [_dot(vt_ref[0, n, hd * VT_ROWS:(hd + 1) * VT_ROWS, :], p[:, hd * BLK:(hd + 1) * BLK])
             for hd in range(n_heads)], axis=1)

    zpad = jnp.zeros((GATE_ROWS - n_blocks, HEAD_DIM), F32)
    gate = jnp.concatenate(
        [_dot_nt(jnp.concatenate([km_ref[0, :, hd * HEAD_DIM:(hd + 1) * HEAD_DIM], zpad], axis=0).astype(BF16),
                 q_ref[0, q_rows, hd * HEAD_DIM:(hd + 1) * HEAD_DIM]) for hd in range(n_heads)], axis=1)
    val = jnp.where(nidx < qi, gate, NEG_INF)
    val = jnp.where(nidx < n_blocks, val, PAD_SCORE)
    keep = ((_topk_rows(val, nidx, MOBA_TOPK) > 0.5) & (nidx < qi)) | (nidx == qi)
    pen_ref[...] = jnp.where(keep, 0.0, NEG_INF)

    def produce_first(slot):
        s = jnp.where(krow <= qcol, scores(qi, 0), NEG_INF)
        s_ref[slot] = s
        return jnp.max(s, axis=0, keepdims=True)

    def produce(i, slot):
        s = scores(i - 1, qi - (i - 1))
        s_ref[slot] = s
        return jnp.max(s, axis=0, keepdims=True) + pen_ref[pl.ds(i - 1, 1), :]

    def consume(i, slot, block_max, m):
        n = jnp.where(i == 0, qi, i - 1)
        pen = pen_ref[pl.ds(n, 1), :]
        m_new = jnp.maximum(m, block_max)
        p = jnp.exp2(s_ref[slot] - (m_new - pen)).astype(BF16)
        acc_ref[...] = jnp.exp2(m - m_new) * acc_ref[...] + weighted_values(n, p)
        return m_new

    acc_ref[...] = jnp.zeros_like(acc_ref)
    _pipelined_blocks(qi + 1, produce_first, produce, consume, jnp.full((1, lanes), NEG_INF, F32))
    o_t = _finish(acc_ref[...], HEAD_DIM)
    o_ref[0, q_rows, :] = jnp.concatenate([o_t[:, hd * BLK:(hd + 1) * BLK] for hd in range(n_heads)],
                                          axis=0).T.astype(BF16)


def _moba(qd, kd, vdt, kmean, bias_d):
    b, s, width = qd.shape
    n_heads = width // HEAD_DIM
    n_blocks = s // BLK
    operands = (qd, kd, vdt, kmean.reshape(b, n_blocks, width))
    return pl.pallas_call(
        _per_query_tile(functools.partial(_moba_tile, n_heads=n_heads, n_blocks=n_blocks), n_blocks),
        grid=(b,),
        in_specs=[_seq_spec(t.shape) for t in operands] + [_const_spec(bias_d.shape)],
        out_specs=_seq_spec(qd.shape),
        out_shape=jax.ShapeDtypeStruct((b, s, width), BF16),
        scratch_shapes=[pltpu.VMEM((GATE_ROWS, n_heads * BLK), F32), pltpu.VMEM((VT_ROWS, n_heads * BLK), F32),
                        pltpu.VMEM((2, BLK, n_heads * BLK), F32)],
        compiler_params=_cparams(1),
        name="moba",
    )(*operands, bias_d)


def kernel(x, rel_bias_table, l0_ffn1_pre, l0_ffn1_post, l0_ffn1_wg, l0_ffn1_wu, l0_ffn1_wd, l0_mix_pre, l0_mix_post, l0_w_in, l0_sinks, l0_mla_q_norm, l0_mla_w_uq, l0_mla_kv_norm, l0_mla_w_ukv, l0_w_out, l0_ffn2_pre, l0_ffn2_post, l0_ffn2_wg, l0_ffn2_wu, l0_ffn2_wd, l1_ffn1_pre, l1_ffn1_post, l1_ffn1_wg, l1_ffn1_wu, l1_ffn1_wd, l1_mix_pre, l1_mix_post, l1_w_in, l1_nsa_pe_k, l1_nsa_pe_v, l1_nsa_wk1, l1_nsa_wk2, l1_nsa_wv1, l1_nsa_wv2, l1_w_out, l1_ffn2_pre, l1_ffn2_post, l1_ffn2_wg, l1_ffn2_wu, l1_ffn2_wd):
    b, s, d = x.shape
    m = b * s
    x2 = x.reshape(m, d)
    bias_a, bias_c, bias_d = _bias_tiles(rel_bias_table, ((A_SLOT, 2), (C_SLOT, BIAS_TILES), (D_SLOT, BIAS_TILES)))

    x2 = _ffn(x2, l0_ffn1_pre, l0_ffn1_post, l0_ffn1_wg, l0_ffn1_wu, l0_ffn1_wd, MACARON_WEIGHT)
    qa, ka, vat, qb, kb, vbt = _proj0(x2, s, l0_mix_pre, l0_w_in, l0_mla_q_norm, l0_mla_w_uq,
                                      l0_mla_kv_norm, l0_mla_w_ukv)
    shp = lambda t: t.reshape(b, s, t.shape[-1])
    o_a = _swa(shp(qa), shp(ka), vat, bias_a, l0_sinks)
    o_b = _mla(shp(qb), shp(kb), vbt)
    x2 = _outproj_ffn(x2, o_a.reshape(m, -1), o_b.reshape(m, -1), l0_w_out, l0_mix_post,
                      l0_ffn2_pre, l0_ffn2_post, l0_ffn2_wg, l0_ffn2_wu, l0_ffn2_wd, MACARON_WEIGHT)

    x2 = _ffn(x2, l1_ffn1_pre, l1_ffn1_post, l1_ffn1_wg, l1_ffn1_wu, l1_ffn1_wd, MACARON_WEIGHT)
    qc, kc_hm, vc_hm, ks, vst, kw, vwt, gates, qd, kd, vdt, kmean = _proj1(x2, b, s, l1_mix_pre, l1_w_in)
    kcmp, vcmpt = _compress(kc_hm, vc_hm, l1_nsa_pe_k, l1_nsa_pe_v, l1_nsa_wk1, l1_nsa_wk2, l1_nsa_wv1, l1_nsa_wv2)
    o_c = _nsa(shp(qc), kcmp, vcmpt, shp(ks), vst, shp(kw), vwt, shp(gates), bias_c)
    o_d = _moba(shp(qd), shp(kd), vdt, kmean, bias_d)
    x2 = _outproj_ffn(x2, o_c.reshape(m, -1), o_d.reshape(m, -1), l1_w_out, l1_mix_post,
                      l1_ffn2_pre, l1_ffn2_post, l1_ffn2_wg, l1_ffn2_wu, l1_ffn2_wd, MACARON_WEIGHT)
    return x2.reshape(b, s, d)
```
